```python
import math
import jax
import jax.numpy as jnp
from jax import lax
import numpy as np


D_MODEL = 1024
BATCH = 8
SEQ = 4096
DEPTH = 1

D_MIX = D_MODEL
HEAD_DIM = 64
NSA_HEADS = 8
NSA_KV_HEADS = 2
NSA_GROUP = NSA_HEADS // NSA_KV_HEADS
RWKV_HEADS = 8
D_NSA = NSA_HEADS * HEAD_DIM
D_RWKV = RWKV_HEADS * HEAD_DIM
D_KV = NSA_KV_HEADS * HEAD_DIM
CMP_LEN = 32
CMP_STRIDE = 16
CMP_HIDDEN = 128
SEL_LEN = 64
SEL_TOPN = 16
WINDOW = 512
Q_CHUNK = 64
RANK_W = 64
RANK_A = 64
RANK_G = 128
D_RWKV_IN = 3 * D_RWKV + RANK_W + RANK_A + RANK_G
RWKV_SIZES = (D_RWKV, D_RWKV, D_RWKV, RANK_W, RANK_A, RANK_G)
COL_SIZES = (D_NSA, D_KV, D_KV, D_KV, D_KV, D_KV, D_KV, 3 * NSA_HEADS, D_RWKV_IN)
D_IN = D_NSA + 6 * D_KV + 3 * NSA_HEADS + D_RWKV_IN
PEER_HEADS = 8
PEER_N_KEYS = 128
PEER_N_EXPERTS = PEER_N_KEYS * PEER_N_KEYS
PEER_D_QUERY = 256
PEER_TOPK = 16
PEER_CHUNK = 128
EPS = 1e-6
GN_EPS = 64e-5

kernel_name = 'hymba_nsa_rwkv7_peer_block'


def rms_norm(x, g):
    xf = x.astype(jnp.float32)
    y = xf * lax.rsqrt(jnp.mean(xf * xf, axis=-1, keepdims=True) + EPS)
    return (y * g.astype(jnp.float32)).astype(x.dtype)


def masked_softmax(s, mask):
    s = jnp.where(mask, s.astype(jnp.float32), -1e30)
    s = s - jnp.max(s, axis=-1, keepdims=True)
    p = jnp.where(mask, jnp.exp(s), 0.0)
    return p / jnp.maximum(jnp.sum(p, axis=-1, keepdims=True), 1e-30)


def alibi_slopes(n):
    return jnp.exp2(-8.0 * jnp.arange(1, n + 1, dtype=jnp.float32) / n)


def split_cols(p, sizes):
    return jnp.split(p, np.cumsum(np.array(sizes))[:-1].tolist(), axis=-1)


def compress_blocks(k, idx, pe, w1, w2):
    b, _, g, d = k.shape
    blk = k[:, idx] + pe[None, None, :, None, :]
    flat = blk.transpose(0, 1, 3, 2, 4).reshape(b, idx.shape[0], g, CMP_LEN * d)
    return jax.nn.gelu(flat @ w1) @ w2


def nsa_mixer(q, kc, vc, ks, vs, kw, vw, gates, k_pe, k_w1, k_w2, v_pe, v_w1, v_w2):
    b, t_len = q.shape[0], q.shape[1]
    G, R, d = NSA_KV_HEADS, NSA_GROUP, HEAD_DIM
    scale = d ** -0.5
    slopes = alibi_slopes(NSA_HEADS).reshape(G, R)
    n_cmp = (t_len - CMP_LEN) // CMP_STRIDE + 1
    cmp_idx = np.arange(n_cmp)[:, None] * CMP_STRIDE + np.arange(CMP_LEN)[None, :]
    kc_blk = compress_blocks(kc, cmp_idx, k_pe, k_w1, k_w2)
    vc_blk = compress_blocks(vc, cmp_idx, v_pe, v_w1, v_w2)
    cmp_end = jnp.asarray(cmp_idx[:, -1], dtype=jnp.int32)
    n_sel = t_len // SEL_LEN
    top_n = min(SEL_TOPN, n_sel)
    ks_blk = ks.reshape(b, n_sel, SEL_LEN, G, d).transpose(0, 3, 1, 2, 4)
    vs_blk = vs.reshape(b, n_sel, SEL_LEN, G, d).transpose(0, 3, 1, 2, 4)
    cmp_start = cmp_idx[:, 0]
    sel_start = np.arange(n_sel) * SEL_LEN
    overlap = np.clip(np.minimum(cmp_start[:, None] + CMP_LEN, sel_start[None, :] + SEL_LEN)
                      - np.maximum(cmp_start[:, None], sel_start[None, :]), 0, None)
    overlap = jnp.asarray(overlap / CMP_LEN, dtype=jnp.float32)
    blk_ids = jnp.arange(n_sel)
    bi = jnp.arange(b)[:, None, None, None]
    gi = jnp.arange(G)[None, :, None, None]
    kw_pad = jnp.pad(kw, ((0, 0), (WINDOW, 0), (0, 0), (0, 0)))
    vw_pad = jnp.pad(vw, ((0, 0), (WINDOW, 0), (0, 0), (0, 0)))
    n_q = t_len // Q_CHUNK
    q_ch = q.reshape(b, n_q, Q_CHUNK, G, R, d).transpose(1, 0, 3, 4, 2, 5)
    g_ch = gates.reshape(b, n_q, Q_CHUNK, G, R, 3).transpose(1, 0, 3, 4, 2, 5)

    def chunk(args):
        ci, qc, gc = args
        q0 = ci * Q_CHUNK
        t = q0 + jnp.arange(Q_CHUNK)
        dist = t[:, None] - cmp_end[None, :]
        s = jnp.einsum('bgrqd,bngd->bgrqn', qc, kc_blk).astype(jnp.float32) * scale \
            - slopes[:, :, None, None] * dist
        p_cmp = masked_softmax(s, dist >= 0)
        o_cmp = jnp.einsum('bgrqn,bngd->bgrqd', p_cmp.astype(vc_blk.dtype), vc_blk)
        imp = jnp.einsum('bgrqn,ns->bgqs', p_cmp, overlap)
        cur = t // SEL_LEN
        forced = (blk_ids[None] == 0) | (blk_ids[None] == cur[:, None]) | (blk_ids[None] == cur[:, None] - 1)
        valid = blk_ids[None] * SEL_LEN <= t[:, None]
        imp = jnp.where(forced, 1e6, jnp.where(valid, imp, -1.0))
        _, sel = lax.top_k(imp, top_n)
        k_sel = ks_blk[bi, gi, sel].reshape(b, G, Q_CHUNK, top_n * SEL_LEN, d)
        v_sel = vs_blk[bi, gi, sel].reshape(b, G, Q_CHUNK, top_n * SEL_LEN, d)
        pos = (sel[..., None] * SEL_LEN + jnp.arange(SEL_LEN)).reshape(b, G, Q_CHUNK, top_n * SEL_LEN)
        dist = t[None, None, :, None] - pos
        s = jnp.einsum('bgrqd,bgqkd->bgrqk', qc, k_sel).astype(jnp.float32) * scale \
            - slopes[None, :, :, None, None] * dist[:, :, None]
        p = masked_softmax(s, (dist >= 0)[:, :, None])
        o_slc = jnp.einsum('bgrqk,bgqkd->bgrqd', p.astype(v_sel.dtype), v_sel)
        k_win = lax.dynamic_slice_in_dim(kw_pad, q0, Q_CHUNK + WINDOW, axis=1)
        v_win = lax.dynamic_slice_in_dim(vw_pad, q0, Q_CHUNK + WINDOW, axis=1)
        pos = q0 - WINDOW + jnp.arange(Q_CHUNK + WINDOW)
        dist = t[:, None] - pos[None, :]
        mask = (dist >= 0) & (dist < WINDOW) & (pos[None, :] >= 0)
        s = jnp.einsum('bgrqd,bkgd->bgrqk', qc, k_win).astype(jnp.float32) * scale \
            - slopes[:, :, None, None] * dist
        p = masked_softmax(s, mask)
        o_win = jnp.einsum('bgrqk,bkgd->bgrqd', p.astype(v_win.dtype), v_win)
        return gc[..., 0:1] * o_cmp + gc[..., 1:2] * o_slc + gc[..., 2:3] * o_win

    out = lax.map(chunk, (jnp.arange(n_q), q_ch, g_ch))
    return out.transpose(1, 0, 4, 2, 3, 5).reshape(b, t_len, NSA_HEADS * d)


def rwkv7_mixer(p, mu, w0, w2, a0, a2, g2, k_k, k_a, r_k, ln_w, ln_b):
    b, t_len, _ = p.shape
    H, d = RWKV_HEADS, HEAD_DIM
    prev = jnp.pad(p, ((0, 0), (1, 0), (0, 0)))[:, :-1]
    p = p + (prev - p) * mu
    r, k, v, xw, xa, xg = split_cols(p, RWKV_SIZES)
    w = -jax.nn.softplus(-(w0 + jnp.tanh(xw) @ w2)) - 0.5
    decay = jnp.exp(-jnp.exp(w.astype(jnp.float32)))
    a = jax.nn.sigmoid(a0 + xa @ a2)
    g = jax.nn.sigmoid(xg) @ g2
    kk = (k * k_k).reshape(b, t_len, H, d).astype(jnp.float32)
    kk = kk / jnp.maximum(jnp.sqrt(jnp.sum(kk * kk, axis=-1, keepdims=True)), 1e-12)
    k = k * (1.0 + (a - 1.0) * k_a)
    r_h = r.reshape(b, t_len, H, d).astype(jnp.float32)
    k_h = k.reshape(b, t_len, H, d).astype(jnp.float32)
    v_h = v.reshape(b, t_len, H, d).astype(jnp.float32)
    w_h = decay.reshape(b, t_len, H, d)
    a_h = a.reshape(b, t_len, H, d).astype(jnp.float32)
    xs = tuple(z.transpose(1, 0, 2, 3) for z in (r_h, w_h, k_h, v_h, kk, a_h))

    def step(S, inp):
        r_t, w_t, k_t, v_t, kk_t, a_t = inp
        sa = jnp.einsum('bhvk,bhk->bhv', S, -kk_t)
        S = S * w_t[:, :, None, :] + sa[..., None] * (kk_t * a_t)[:, :, None, :] \
            + v_t[..., None] * k_t[:, :, None, :]
        return S, jnp.einsum('bhvk,bhk->bhv', S, r_t)

    S0 = jnp.zeros((b, H, d, d), jnp.float32)
    _, o = lax.scan(step, S0, xs)
    o = o.transpose(1, 0, 2, 3)
    mean = jnp.mean(o, axis=-1, keepdims=True)
    var = jnp.mean(jnp.square(o - mean), axis=-1, keepdims=True)
    o = (o - mean) * lax.rsqrt(var + GN_EPS) * ln_w + ln_b
    o = o + jnp.sum(r_h * k_h * r_k, axis=-1, keepdims=True) * v_h
    return (o.reshape(b, t_len, H * d) * g).astype(p.dtype)


def peer_ffn(h, w_q, sub_keys, u, v):
    b, t_len, dm = h.shape
    H, K, nk = PEER_HEADS, PEER_TOPK, PEER_N_KEYS
    q = (h @ w_q).reshape(b, t_len, H, 2, PEER_D_QUERY // 2)
    s = jnp.einsum('bthcd,hckd->bthck', q, sub_keys).astype(jnp.float32)
    s_top, i_top = lax.top_k(s, K)
    cand = (s_top[..., 0, :, None] + s_top[..., 1, None, :]).reshape(b, t_len, H, K * K)
    best, j = lax.top_k(cand, K)
    i1 = jnp.take_along_axis(i_top[..., 0, :], j // K, axis=-1)
    i2 = jnp.take_along_axis(i_top[..., 1, :], j % K, axis=-1)
    experts = (i1 * nk + i2).reshape(-1, H * K)
    gates = jax.nn.softmax(best, axis=-1).reshape(-1, H * K)
    n_ch = (b * t_len) // PEER_CHUNK
    xs = (h.reshape(n_ch, PEER_CHUNK, dm),
          experts.reshape(n_ch, PEER_CHUNK, H * K),
          gates.reshape(n_ch, PEER_CHUNK, H * K))

    def chunk(args):
        xc, ec, gc = args
        act = jax.nn.gelu(jnp.einsum('cd,ckd->ck', xc, u[ec]))
        return jnp.einsum('ck,ckd->cd', (act * gc).astype(v.dtype), v[ec])

    return lax.map(chunk, xs).reshape(b, t_len, dm)


def setup_inputs(seed: int = 0) -> dict:
    key = jax.random.key(seed)
    ks = jax.random.split(key, 32)
    L = DEPTH
    f32 = jnp.float32

    def nrm(k, shape, scale):
        return jax.random.normal(k, shape, f32) * scale

    return {
        'x': nrm(ks[0], (BATCH, SEQ, D_MODEL), 1.0),
        'norm_mix_g': 1.0 + nrm(ks[1], (L, D_MODEL), 0.02),
        'w_in': nrm(ks[2], (L, D_MODEL, D_IN), D_MODEL ** -0.5),
        'cmp_k_pe': nrm(ks[3], (L, CMP_LEN, HEAD_DIM), 0.1),
        'cmp_k_w1': nrm(ks[4], (L, CMP_LEN * HEAD_DIM, CMP_HIDDEN), (CMP_LEN * HEAD_DIM) ** -0.5),
        'cmp_k_w2': nrm(ks[5], (L, CMP_HIDDEN, HEAD_DIM), CMP_HIDDEN ** -0.5),
        'cmp_v_pe': nrm(ks[6], (L, CMP_LEN, HEAD_DIM), 0.1),
        'cmp_v_w1': nrm(ks[7], (L, CMP_LEN * HEAD_DIM, CMP_HIDDEN), (CMP_LEN * HEAD_DIM) ** -0.5),
        'cmp_v_w2': nrm(ks[8], (L, CMP_HIDDEN, HEAD_DIM), CMP_HIDDEN ** -0.5),
        'rwkv_mu': jax.random.uniform(ks[9], (L, D_RWKV_IN), f32),
        'rwkv_w0': jax.random.uniform(ks[10], (L, D_RWKV), f32, minval=-4.0, maxval=1.0),
        'rwkv_w2': nrm(ks[11], (L, RANK_W, D_RWKV), RANK_W ** -0.5),
        'rwkv_a0': nrm(ks[12], (L, D_RWKV), 0.5),
        'rwkv_a2': nrm(ks[13], (L, RANK_A, D_RWKV), RANK_A ** -0.5),
        'rwkv_g2': nrm(ks[14], (L, RANK_G, D_RWKV), RANK_G ** -0.5),
        'rwkv_k_k': 0.85 + nrm(ks[15], (L, D_RWKV), 0.05),
        'rwkv_k_a': 1.0 + nrm(ks[16], (L, D_RWKV), 0.05),
        'rwkv_r_k': nrm(ks[17], (L, RWKV_HEADS, HEAD_DIM), 0.1),
        'rwkv_ln_w': 1.0 + nrm(ks[18], (L, RWKV_HEADS, HEAD_DIM), 0.02),
        'rwkv_ln_b': nrm(ks[19], (L, RWKV_HEADS, HEAD_DIM), 0.02),
        'w_out': nrm(ks[20], (L, D_MIX, D_MODEL), D_MIX ** -0.5),
        'norm_ffn_g': 1.0 + nrm(ks[21], (L, D_MODEL), 0.02),
        'peer_w_q': nrm(ks[22], (L, D_MODEL, PEER_HEADS * PEER_D_QUERY), D_MODEL ** -0.5),
        'peer_sub_keys': nrm(ks[23], (L, PEER_HEADS, 2, PEER_N_KEYS, PEER_D_QUERY // 2), (PEER_D_QUERY // 2) ** -0.5),
        'peer_u': nrm(ks[24], (L, PEER_N_EXPERTS, D_MODEL), D_MODEL ** -0.5),
        'peer_v': nrm(ks[25], (L, PEER_N_EXPERTS, D_MODEL), PEER_HEADS ** -0.5),
        'norm_final_g': 1.0 + nrm(ks[26], (D_MODEL,), 0.02),
    }


def reference(x, norm_mix_g, w_in, cmp_k_pe, cmp_k_w1, cmp_k_w2, cmp_v_pe, cmp_v_w1, cmp_v_w2,
              rwkv_mu, rwkv_w0, rwkv_w2, rwkv_a0, rwkv_a2, rwkv_g2, rwkv_k_k, rwkv_k_a, rwkv_r_k,
              rwkv_ln_w, rwkv_ln_b, w_out, norm_ffn_g, peer_w_q, peer_sub_keys, peer_u, peer_v,
              norm_final_g):
    b, t_len, _ = x.shape
    for l in range(DEPTH):
        h = rms_norm(x, norm_mix_g[l])
        p = h @ w_in[l]
        q, kc, vc, ks_, vs_, kw, vw, gl, p_rwkv = split_cols(p, COL_SIZES)
        kv = lambda z: z.reshape(b, t_len, NSA_KV_HEADS, HEAD_DIM)
        gates = jax.nn.sigmoid(gl).reshape(b, t_len, NSA_HEADS, 3)
        y_nsa = nsa_mixer(q, kv(kc), kv(vc), kv(ks_), kv(vs_), kv(kw), kv(vw), gates,
                          cmp_k_pe[l], cmp_k_w1[l], cmp_k_w2[l], cmp_v_pe[l], cmp_v_w1[l], cmp_v_w2[l])
        y_rwkv = rwkv7_mixer(p_rwkv, rwkv_mu[l], rwkv_w0[l], rwkv_w2[l], rwkv_a0[l], rwkv_a2[l],
                             rwkv_g2[l], rwkv_k_k[l], rwkv_k_a[l], rwkv_r_k[l], rwkv_ln_w[l], rwkv_ln_b[l])
        y = jnp.concatenate([y_nsa.astype(x.dtype), y_rwkv.astype(x.dtype)], axis=-1)
        x = x + y @ w_out[l]
        h = rms_norm(x, norm_ffn_g[l])
        x = x + peer_ffn(h, peer_w_q[l], peer_sub_keys[l], peer_u[l], peer_v[l]).astype(x.dtype)
    return rms_norm(x, norm_final_g)
```

```python
import functools

import numpy as np
import jax
import jax.numpy as jnp
from jax import lax
from jax.experimental import pallas as pl
from jax.experimental.pallas import tpu as pltpu

F32 = jnp.float32
BF16 = jnp.bfloat16

D_MODEL = 1024
HEAD_DIM = 64
NSA_HEADS = 8
NSA_KV_HEADS = 2
NSA_GROUP = NSA_HEADS // NSA_KV_HEADS
RWKV_HEADS = 8
D_NSA = NSA_HEADS * HEAD_DIM
D_RWKV = RWKV_HEADS * HEAD_DIM
D_KV = NSA_KV_HEADS * HEAD_DIM
CMP_LEN = 32
CMP_STRIDE = 16
CMP_HIDDEN = 128
SEL_LEN = 64
SEL_TOPN = 16
WINDOW = 512
RANK_W = 64
RANK_A = 64
RANK_G = 128
D_RWKV_IN = 3 * D_RWKV + RANK_W + RANK_A + RANK_G
PEER_HEADS = 8
PEER_N_KEYS = 128
PEER_D_QUERY = 256
PEER_TOPK = 16
EPS = 1e-6
GN_EPS = 64e-5

LANES = 128
SUBLANES = 8
VMEM_LIMIT = 56 * 1024 * 1024

NEG_BIG = -1e30
Q_EXP = NSA_HEADS * LANES
NA_COLS = Q_EXP + 4 * D_KV
NB_COLS = 3 * LANES


def _cparams(sem):
    return pltpu.CompilerParams(dimension_semantics=sem, vmem_limit_bytes=VMEM_LIMIT)


def _dot(a, b):
    return jnp.dot(a, b, preferred_element_type=F32)


def _dot_nt(a, b):
    return lax.dot_general(a, b, (((1,), (1,)), ((), ())), preferred_element_type=F32)


def _split2(x):
    hi = x.astype(BF16)
    lo = (x - hi.astype(F32)).astype(BF16)
    return hi, lo


def _dot_x2(x, e):
    hi, lo = _split2(x)
    return _dot(hi, e) + _dot(lo, e)


def _gelu(x):
    return 0.5 * x * (1.0 + jnp.tanh(0.7978845608028654 * (x + 0.044715 * (x * x * x))))


def _sigmoid(x):
    return 1.0 / (1.0 + jnp.exp(-x))


def _in_proj_kernel(x_ref, g_ref, w_ref, oa_ref, ob_ref, oc_ref):
    x = x_ref[...]
    h = x * lax.rsqrt(jnp.mean(x * x, axis=-1, keepdims=True) + EPS) * g_ref[...]
    hb = h.astype(BF16)
    oa_ref[...] = _dot(hb, w_ref[:, :NA_COLS]).astype(BF16)
    ob_ref[...] = _dot(hb, w_ref[:, NA_COLS:NA_COLS + NB_COLS])
    oc_ref[...] = _dot(hb, w_ref[:, NA_COLS + NB_COLS:])


def _in_proj(x2d, g, w_pad, tm=256):
    n = x2d.shape[0]
    ncols = w_pad.shape[1]
    nc = ncols - NA_COLS - NB_COLS
    return pl.pallas_call(
        _in_proj_kernel,
        grid=(n // tm,),
        in_specs=[
            pl.BlockSpec((tm, D_MODEL), lambda i: (i, 0)),
            pl.BlockSpec((1, D_MODEL), lambda i: (0, 0)),
            pl.BlockSpec((D_MODEL, ncols), lambda i: (0, 0)),
        ],
        out_specs=[
            pl.BlockSpec((tm, NA_COLS), lambda i: (i, 0)),
            pl.BlockSpec((tm, NB_COLS), lambda i: (i, 0)),
            pl.BlockSpec((tm, nc), lambda i: (i, 0)),
        ],
        out_shape=[
            jax.ShapeDtypeStruct((n, NA_COLS), BF16),
            jax.ShapeDtypeStruct((n, NB_COLS), F32),
            jax.ShapeDtypeStruct((n, nc), F32),
        ],
        compiler_params=_cparams(("parallel",)),
        name="in_proj",
    )(x2d, g, w_pad)


def _pad_w_in(w_in):
    sizes = (D_NSA, D_KV, D_KV, D_KV, D_KV, D_KV, D_KV, 3 * NSA_HEADS, D_RWKV_IN)
    offs = np.cumsum((0,) + sizes)
    q, kc, vc, ks, vs, kw, vw, gl, rw = (w_in[:, offs[i]:offs[i + 1]] for i in range(9))
    onehot = (np.arange(NSA_HEADS)[:, None] // NSA_GROUP == np.arange(NSA_KV_HEADS)[None, :])
    onehot = jnp.asarray(onehot, F32)
    qe = q.reshape(D_MODEL, NSA_HEADS, 1, HEAD_DIM) * onehot[None, :, :, None]
    qe = qe.reshape(D_MODEL, Q_EXP)
    glp = jnp.pad(gl, ((0, 0), (0, LANES - 3 * NSA_HEADS)))
    return jnp.concatenate([qe, ks, vs, kw, vw, kc, vc, glp, rw], axis=1).astype(BF16)


def _compress_kernel(rk_ref, rv_ref, pek_ref, pev_ref, w1k_ref, w1v_ref, w2k_ref, w2v_ref,
                     ok_ref, ov_ref):
    for r_ref, pe_ref, w1_ref, w2_ref, o_ref in (
            (rk_ref, pek_ref, w1k_ref, w2k_ref, ok_ref),
            (rv_ref, pev_ref, w1v_ref, w2v_ref, ov_ref)):
        rows = r_ref[0]
        nxt = pltpu.roll(rows, rows.shape[0] - 1, axis=0)
        a = (rows + pe_ref[0:1, :]).astype(BF16)
        b = (nxt + pe_ref[1:2, :]).astype(BF16)
        hid = _dot(a, w1_ref[0]) + _dot(b, w1_ref[1])
        o_ref[0] = _dot(_gelu(hid).astype(BF16), w2_ref[...]).astype(BF16)


def _expand_cmp_weights(pe, w1, w2):
    half = CMP_LEN // 2
    eye = jnp.eye(NSA_KV_HEADS, dtype=F32)
    w1r = w1.reshape(2, half, HEAD_DIM, CMP_HIDDEN)
    w1e = w1r[:, :, None, :, None, :] * eye[None, None, :, None, :, None]
    w1e = w1e.reshape(2, half * D_KV, NSA_KV_HEADS * CMP_HIDDEN).astype(BF16)
    pee = jnp.broadcast_to(pe.reshape(2, half, 1, HEAD_DIM), (2, half, NSA_KV_HEADS, HEAD_DIM))
    pee = pee.reshape(2, half * D_KV)
    w2e = (eye[:, None, :, None] * w2[None, :, None, :]).reshape(
        NSA_KV_HEADS * CMP_HIDDEN, D_KV).astype(BF16)
    return pee, w1e, w2e


def _compress(rk, rv, wk, wv):
    b, nb, width = rk.shape
    full2 = lambda a: pl.BlockSpec(a.shape, lambda i: (0, 0))
    full3 = lambda a: pl.BlockSpec(a.shape, lambda i: (0, 0, 0))
    row = pl.BlockSpec((1, nb, width), lambda i: (i, 0, 0))
    out = pl.BlockSpec((1, nb, D_KV), lambda i: (i, 0, 0))
    return pl.pallas_call(
        _compress_kernel,
        grid=(b,),
        in_specs=[row, row, full2(wk[0]), full2(wv[0]), full3(wk[1]), full3(wv[1]),
                  full2(wk[2]), full2(wv[2])],
        out_specs=[out, out],
        out_shape=[jax.ShapeDtypeStruct((b, nb, D_KV), BF16)] * 2,
        compiler_params=_cparams(("parallel",)),
        name="nsa_compress",
    )(rk, rv, wk[0], wv[0], wk[1], wv[1], wk[2], wv[2])


NSA_TQ = 128
NSA_TK = 256
NSA_TKW = 128


def _masked_softmax(s, mask):
    s = jnp.where(mask, s, NEG_BIG)
    s = s - jnp.max(s, axis=-1, keepdims=True)
    p = jnp.where(mask, jnp.exp(s), 0.0)
    return p / jnp.maximum(jnp.sum(p, axis=-1, keepdims=True), 1e-30)


def _flash_step(carry, s, mask, v):
    m, l, acc = carry
    s = jnp.where(mask, s, NEG_BIG)
    m_new = jnp.maximum(m, jnp.max(s, axis=-1, keepdims=True))
    alpha = jnp.exp(m - m_new)
    p = jnp.where(mask, jnp.exp(s - m_new), 0.0)
    l = alpha * l + jnp.sum(p, axis=-1, keepdims=True)
    acc = alpha * acc + _dot(p.astype(BF16), v)
    return m_new, l, acc


def _nsa_kernel(q_ref, ks_ref, vs_ref, kw_ref, vw_ref, ck_ref, cv_ref, ovl_ref, e_ref, gl_ref,
                eg_ref, o_ref):
    tq = NSA_TQ
    R = NSA_GROUP
    i = pl.program_id(1)
    q0 = i * tq
    t_col = q0 + lax.broadcasted_iota(jnp.int32, (tq, 1), 0)
    t_rows = jnp.concatenate([t_col] * R, axis=0)
    n_cmp = ck_ref.shape[1]
    n_sel = ovl_ref.shape[1]

    gate = _sigmoid(gl_ref[...])
    g_hi, g_lo = _split2(gate)
    gexp = [_dot(g_hi, eg_ref[j]) + _dot(g_lo, eg_ref[j]) for j in range(3)]

    cmp_end = lax.broadcasted_iota(jnp.int32, (1, n_cmp), 1) * CMP_STRIDE + (CMP_LEN - 1)
    dist_c = t_col - cmp_end
    mask_c = dist_c >= 0
    dist_cf = dist_c.astype(F32)
    ids = lax.broadcasted_iota(jnp.int32, (1, n_sel), 1)
    idsf = ids.astype(F32)
    cur = t_col // SEL_LEN
    forced = (ids == 0) | (ids == cur) | (ids == cur - 1)
    valid = ids * SEL_LEN <= t_col

    for g in range(NSA_KV_HEADS):
        heads = [g * R + r for r in range(R)]
        slopes = [2.0 ** (-(h + 1)) for h in heads]
        qh = [q_ref[:, h * LANES:(h + 1) * LANES] * 0.125 for h in heads]
        ck = ck_ref[0]
        cv = cv_ref[0]

        o_cmp = []
        imp = jnp.zeros((tq, n_sel), F32)
        for r in range(R):
            s = _dot_nt(qh[r], ck) - slopes[r] * dist_cf
            p = _masked_softmax(s, mask_c)
            o_cmp.append(_dot(p.astype(BF16), cv))
            imp = imp + jnp.dot(p, ovl_ref[...], preferred_element_type=F32,
                                precision=lax.Precision.HIGHEST)
        imp = jnp.where(forced, 1e6, jnp.where(valid, imp, -1.0))

        sel = jnp.zeros((tq, n_sel), F32)
        for _ in range(min(SEL_TOPN, n_sel)):
            m = jnp.max(imp, axis=-1, keepdims=True)
            first = jnp.min(jnp.where(imp == m, idsf, float(n_sel)), axis=-1, keepdims=True)
            hit = idsf == first
            sel = jnp.where(hit, 1.0, sel)
            imp = jnp.where(hit, -3e38, imp)
        sel_b = sel.astype(BF16)

        q_st = jnp.concatenate(qh, axis=0)
        slope_rows = jnp.concatenate(
            [jnp.full((tq, 1), sl, F32) for sl in slopes], axis=0)
        init = (jnp.full((R * tq, 1), NEG_BIG, F32), jnp.zeros((R * tq, 1), F32),
                jnp.zeros((R * tq, LANES), F32))

        def slc_body(j, carry):
            k0 = pl.multiple_of(j * NSA_TK, NSA_TK)
            k = ks_ref[pl.ds(k0, NSA_TK), :]
            v = vs_ref[pl.ds(k0, NSA_TK), :]
            pos = k0 + lax.broadcasted_iota(jnp.int32, (1, NSA_TK), 1)
            dist = t_rows - pos
            selm = _dot(sel_b, e_ref[:, pl.ds(k0, NSA_TK)])
            selm = jnp.concatenate([selm] * R, axis=0)
            mask = (dist >= 0) & (selm > 0.5)
            s = _dot_nt(q_st, k) - slope_rows * dist.astype(F32)
            return _flash_step(carry, s, mask, v)

        n_kv = (q0 + tq + NSA_TK - 1) // NSA_TK
        _, l_s, acc_s = lax.fori_loop(0, n_kv, slc_body, init)
        o_slc = acc_s / l_s

        def win_body(j, carry):
            k0 = pl.multiple_of(j * NSA_TKW, NSA_TKW)
            k = kw_ref[pl.ds(k0, NSA_TKW), :]
            v = vw_ref[pl.ds(k0, NSA_TKW), :]
            pos = k0 + lax.broadcasted_iota(jnp.int32, (1, NSA_TKW), 1)
            dist = t_rows - pos
            mask = (dist >= 0) & (dist < WINDOW)
            s = _dot_nt(q_st, k) - slope_rows * dist.astype(F32)
            return _flash_step(carry, s, mask, v)

        lo = jnp.maximum(q0 - WINDOW, 0) // NSA_TKW
        hi = (q0 + tq + NSA_TKW - 1) // NSA_TKW
        _, l_w, acc_w = lax.fori_loop(lo, hi, win_body, init)
        o_win = acc_w / l_w

        for r, h in enumerate(heads):
            cols = slice(h * LANES, (h + 1) * LANES)
            rows = slice(r * tq, (r + 1) * tq)
            out = (gexp[0][:, cols] * o_cmp[r] + gexp[1][:, cols] * o_slc[rows]
                   + gexp[2][:, cols] * o_win[rows])
            o_ref[:, cols] = out.astype(BF16)


def _nsa_consts(t_len, n_cmp_pad):
    n_cmp = (t_len - CMP_LEN) // CMP_STRIDE + 1
    n_sel = t_len // SEL_LEN
    cmp_start = np.arange(n_cmp) * CMP_STRIDE
    sel_start = np.arange(n_sel) * SEL_LEN
    ovl = np.clip(np.minimum(cmp_start[:, None] + CMP_LEN, sel_start[None, :] + SEL_LEN)
                  - np.maximum(cmp_start[:, None], sel_start[None, :]), 0, None) / CMP_LEN
    ovl_pad = np.zeros((n_cmp_pad, n_sel), np.float32)
    ovl_pad[:n_cmp] = ovl
    expand = (np.arange(t_len)[None, :] // SEL_LEN == np.arange(n_sel)[:, None])
    eg = np.zeros((3, LANES, Q_EXP), np.float32)
    for h in range(NSA_HEADS):
        for j in range(3):
            eg[j, h * 3 + j, h * LANES:(h + 1) * LANES] = 1.0
    return jnp.asarray(ovl_pad), jnp.asarray(expand, BF16), jnp.asarray(eg, BF16)


def _nsa_attention(oa, ob, cmp_k, cmp_v, b, t_len):
    n_cmp_pad = cmp_k.shape[1]
    ovl, expand, eg = _nsa_consts(t_len, n_cmp_pad)
    nq = t_len // NSA_TQ
    kv_col0 = Q_EXP // D_KV
    kv = lambda c: pl.BlockSpec((t_len, D_KV), lambda bi, i: (bi, kv_col0 + c))
    cmp_spec = pl.BlockSpec((1, n_cmp_pad, D_KV), lambda bi, i: (bi, 0, 0))
    return pl.pallas_call(
        _nsa_kernel,
        grid=(b, nq),
        in_specs=[
            pl.BlockSpec((NSA_TQ, Q_EXP), lambda bi, i: (bi * nq + i, 0)),
            kv(0), kv(1), kv(2), kv(3),
            cmp_spec, cmp_spec,
            pl.BlockSpec(ovl.shape, lambda bi, i: (0, 0)),
            pl.BlockSpec(expand.shape, lambda bi, i: (0, 0)),
            pl.BlockSpec((NSA_TQ, LANES), lambda bi, i: (bi * nq + i, 2)),
            pl.BlockSpec(eg.shape, lambda bi, i: (0, 0, 0)),
        ],
        out_specs=pl.BlockSpec((NSA_TQ, Q_EXP), lambda bi, i: (bi * nq + i, 0)),
        out_shape=jax.ShapeDtypeStruct((b * t_len, Q_EXP), BF16),
        compiler_params=_cparams(("parallel", "arbitrary")),
        name="nsa_attention",
    )(oa, oa, oa, oa, oa, cmp_k, cmp_v, ovl, expand, ob, eg)


def _dot_x3(x, e):
    hi = x.astype(BF16)
    r1 = x - hi.astype(F32)
    mid = r1.astype(BF16)
    lo = (r1 - mid.astype(F32)).astype(BF16)
    return _dot(hi, e) + _dot(mid, e) + _dot(lo, e)


def _head_sum_matrix():
    ids = np.arange(D_RWKV) // HEAD_DIM
    return jnp.asarray(ids[:, None] == ids[None, :], BF16)


def _rwkv_pre_kernel(p_ref, hp_ref, mu_ref, w0_ref, w2_ref, a0_ref, a2_ref, g2_ref, kk_ref,
                     ka_ref, bd_ref, r_o, lw_o, k_o, v_o, kkn_o, a_o, g_o, *, tiles_per_seq):
    i = pl.program_id(0)
    p = p_ref[...]
    tm = p.shape[0]
    keep = jnp.where(i % tiles_per_seq == 0, 0.0, 1.0)
    halo = hp_ref[SUBLANES - 1:SUBLANES, :] * keep
    prev = pltpu.roll(p, 1, axis=0)
    row0 = lax.broadcasted_iota(jnp.int32, (tm, 1), 0) == 0
    prev = jnp.where(row0, halo, prev)
    ps = p + (prev - p) * mu_ref[...]
    d = D_RWKV
    r = ps[:, 0:d]
    k = ps[:, d:2 * d]
    v = ps[:, 2 * d:3 * d]
    xw = ps[:, 3 * d:3 * d + RANK_W]
    xa = ps[:, 3 * d + RANK_W:3 * d + RANK_W + RANK_A]
    xg = ps[:, 3 * d + RANK_W + RANK_A:]
    z = -(w0_ref[...] + _dot(jnp.tanh(xw).astype(BF16), w2_ref[...]))
    softplus = jnp.maximum(z, 0.0) + jnp.log(1.0 + jnp.exp(-jnp.abs(z)))
    w = -softplus - 0.5
    a = _sigmoid(a0_ref[...] + _dot(xa.astype(BF16), a2_ref[...]))
    g = _dot(_sigmoid(xg).astype(BF16), g2_ref[...])
    kk = k * kk_ref[...]
    ss = _dot_x3(kk * kk, bd_ref[...])
    kk = kk / jnp.maximum(jnp.sqrt(ss), 1e-12)
    r_o[...] = r
    lw_o[...] = -jnp.exp(w)
    k_o[...] = k * (1.0 + (a - 1.0) * ka_ref[...])
    v_o[...] = v
    kkn_o[...] = kk
    a_o[...] = a
    g_o[...] = g


def _rwkv_pre(oc, t_len, mu, w0, w2, a0, a2, g2, k_k, k_a, bd, tm=256):
    n = oc.shape[0]
    tiles_per_seq = t_len // tm
    halo_blocks = tm // SUBLANES
    full = lambda a: pl.BlockSpec(a.shape, lambda i: (0, 0))
    outs = pl.BlockSpec((tm, D_RWKV), lambda i: (i, 0))
    params = (mu, w0, w2, a0, a2, g2, k_k, k_a, bd)
    return pl.pallas_call(
        functools.partial(_rwkv_pre_kernel, tiles_per_seq=tiles_per_seq),
        grid=(n // tm,),
        in_specs=[
            pl.BlockSpec((tm, D_RWKV_IN), lambda i: (i, 0)),
            pl.BlockSpec((SUBLANES, D_RWKV_IN),
                         lambda i: (jnp.maximum(i * halo_blocks - 1, 0), 0)),
        ] + [full(a) for a in params],
        out_specs=[outs] * 7,
        out_shape=[jax.ShapeDtypeStruct((n, D_RWKV), F32)] * 7,
        compiler_params=_cparams(("parallel",)),
        name="rwkv_pre",
    )(oc, oc, *params)


RWKV_C = 64
RWKV_CB = 128
RWKV_PASSES = 3


def _bmm(eq, a, b):
    ein = lambda x, y: jnp.einsum(eq, x, y, preferred_element_type=F32)
    if RWKV_PASSES == 1:
        return ein(a.astype(BF16), b.astype(BF16))
    ah, al = _split2(a)
    bh, bl = _split2(b)
    return ein(ah, bh) + ein(ah, bl) + ein(al, bh)


def _cumsum_rows(x2d, seg):
    rows = lax.broadcasted_iota(jnp.int32, (x2d.shape[0], 1), 0) % seg
    step = 1
    while step < seg:
        shifted = pltpu.roll(x2d, step, axis=0)
        x2d = x2d + jnp.where(rows >= step, shifted, 0.0)
        step *= 2
    return x2d


def _rwkv_rec_kernel(r_ref, lw_ref, k_ref, kk_ref, a_ref, vt_ref, ot_ref, s_ref):
    H = r_ref.shape[1]
    C = RWKV_C

    @pl.when(pl.program_id(1) == 0)
    def _():
        s_ref[...] = jnp.zeros_like(s_ref)

    ri = lax.broadcasted_iota(jnp.int32, (C, C), 0)
    ci = lax.broadcasted_iota(jnp.int32, (C, C), 1)
    strict = (ri > ci)[None]
    incl = (ri >= ci)[None]
    eye = (ri == ci).astype(F32)[None]

    for sub in range(RWKV_CB // C):
        rows = slice(sub * C, (sub + 1) * C)
        r = r_ref[0, :, rows, :]
        lw = lw_ref[0, :, rows, :]
        k = k_ref[0, :, rows, :]
        kk = kk_ref[0, :, rows, :]
        a = a_ref[0, :, rows, :]
        vt = vt_ref[0, :, :, rows]
        s0 = s_ref[...]

        cum = _cumsum_rows(lw.reshape(H * C, HEAD_DIM), C).reshape(H, C, HEAD_DIM)
        cum_last = cum[:, C - 1:C, :]
        p_inv = jnp.exp(-cum)
        w_last = jnp.exp(cum_last - cum)
        bm = kk * a
        at = -kk * jnp.exp(cum - lw)
        rt = r * jnp.exp(cum)
        bt = bm * p_inv
        kt = k * p_inv

        nt = 'hik,hjk->hij'
        m_ab = jnp.where(strict, _bmm(nt, at, bt), 0.0)
        m_ak = jnp.where(strict, _bmm(nt, at, kt), 0.0)
        n_rb = jnp.where(incl, _bmm(nt, rt, bt), 0.0)
        n_rk = jnp.where(incl, _bmm(nt, rt, kt), 0.0)

        tinv = eye + m_ab
        mp = m_ab
        step = 1
        while 2 * step < C:
            mp = _bmm('hij,hjk->hik', mp, mp)
            tinv = tinv + _bmm('hij,hjk->hik', tinv, mp)
            step *= 2

        rhs_t = _bmm('hvk,hik->hvi', s0, at) + _bmm('hvj,hij->hvi', vt, m_ak)
        ut = _bmm('hvj,hij->hvi', rhs_t, tinv)
        ot = (_bmm('hvk,hik->hvi', s0, rt) + _bmm('hvj,hij->hvi', ut, n_rb)
              + _bmm('hvj,hij->hvi', vt, n_rk))
        ot_ref[0, :, :, rows] = ot
        s_ref[...] = (s0 * jnp.exp(cum_last) + _bmm('hvj,hjk->hvk', ut, bm * w_last)
                      + _bmm('hvj,hjk->hvk', vt, k * w_last))


def _rwkv_recurrence(r, lw, k, kk, a, vt):
    b, h, t_len, d = r.shape
    tok = pl.BlockSpec((1, h, RWKV_CB, d), lambda bi, c: (bi, 0, c, 0))
    tr = pl.BlockSpec((1, h, d, RWKV_CB), lambda bi, c: (bi, 0, 0, c))
    return pl.pallas_call(
        _rwkv_rec_kernel,
        grid=(b, t_len // RWKV_CB),
        in_specs=[tok] * 5 + [tr],
        out_specs=tr,
        out_shape=jax.ShapeDtypeStruct((b, h, d, t_len), F32),
        scratch_shapes=[pltpu.VMEM((h, d, d), F32)],
        compiler_params=_cparams(("parallel", "arbitrary")),
        name="rwkv_recurrence",
    )(r, lw, k, kk, a, vt)


def _rwkv_post_kernel(o_ref, r_ref, k_ref, v_ref, g_ref, lnw_ref, lnb_ref, rk_ref, bd_ref, y_ref):
    o = o_ref[...]
    bd = bd_ref[...]
    inv = 1.0 / HEAD_DIM
    mean = _dot_x3(o, bd) * inv
    d = o - mean
    var = _dot_x3(d * d, bd) * inv
    on = d * lax.rsqrt(var + GN_EPS) * lnw_ref[...] + lnb_ref[...]
    bonus = _dot_x3(r_ref[...] * k_ref[...] * rk_ref[...], bd)
    y_ref[...] = ((on + bonus * v_ref[...]) * g_ref[...]).astype(BF16)


def _rwkv_post(o, r, k, v, g, ln_w, ln_b, r_k, bd, tm=512):
    n = o.shape[0]
    tile = pl.BlockSpec((tm, D_RWKV), lambda i: (i, 0))
    full = lambda a: pl.BlockSpec(a.shape, lambda i: (0, 0))
    return pl.pallas_call(
        _rwkv_post_kernel,
        grid=(n // tm,),
        in_specs=[tile] * 5 + [full(ln_w), full(ln_b), full(r_k), full(bd)],
        out_specs=tile,
        out_shape=jax.ShapeDtypeStruct((n, D_RWKV), BF16),
        compiler_params=_cparams(("parallel",)),
        name="rwkv_post",
    )(o, r, k, v, g, ln_w, ln_b, r_k, bd)


def _rwkv_mixer(oc, b, t_len, mu, w0, w2, a0, a2, g2, k_k, k_a, r_k, ln_w, ln_b):
    bd = _head_sum_matrix()
    row = lambda z: z.reshape(1, -1)
    r, lw, k, v, kk, a, g = _rwkv_pre(
        oc, t_len, row(mu), row(w0), w2.astype(BF16), row(a0), a2.astype(BF16),
        g2.astype(BF16), row(k_k), row(k_a), bd)
    hm = lambda z: z.reshape(b, t_len, RWKV_HEADS, HEAD_DIM).transpose(0, 2, 1, 3)
    vt = v.reshape(b, t_len, RWKV_HEADS, HEAD_DIM).transpose(0, 2, 3, 1)
    ot = _rwkv_recurrence(hm(r), hm(lw), hm(k), hm(kk), hm(a), vt)
    o = ot.transpose(0, 3, 1, 2).reshape(b * t_len, D_RWKV)
    return _rwkv_post(o, r, k, v, g, row(ln_w), row(ln_b), row(r_k), bd)


def _out_proj_kernel(x_ref, yn_ref, yr_ref, wn_ref, wr_ref, g_ref, x1_ref, h2_ref):
    x1 = x_ref[...] + _dot(yn_ref[...], wn_ref[...]) + _dot(yr_ref[...], wr_ref[...])
    x1_ref[...] = x1
    h2_ref[...] = x1 * lax.rsqrt(jnp.mean(x1 * x1, axis=-1, keepdims=True) + EPS) * g_ref[...]


def _expand_w_out(w_out):
    wn = w_out[:D_NSA].reshape(NSA_HEADS, 1, HEAD_DIM, D_MODEL)
    onehot = (np.arange(NSA_HEADS)[:, None] // NSA_GROUP == np.arange(NSA_KV_HEADS)[None, :])
    wn = wn * jnp.asarray(onehot, F32)[:, :, None, None]
    return wn.reshape(Q_EXP, D_MODEL).astype(BF16), w_out[D_NSA:].astype(BF16)


def _out_proj(x2d, y_nsa, y_rwkv, wn, wr, g, tm=512):
    n = x2d.shape[0]
    tile = lambda a: pl.BlockSpec((tm, a.shape[1]), lambda i: (i, 0))
    full = lambda a: pl.BlockSpec(a.shape, lambda i: (0, 0))
    return pl.pallas_call(
        _out_proj_kernel,
        grid=(n // tm,),
        in_specs=[tile(x2d), tile(y_nsa), tile(y_rwkv), full(wn), full(wr), full(g)],
        out_specs=[tile(x2d), tile(x2d)],
        out_shape=[jax.ShapeDtypeStruct((n, D_MODEL), F32)] * 2,
        compiler_params=_cparams(("parallel",)),
        name="out_proj",
    )(x2d, y_nsa, y_rwkv, wn, wr, g)


def _topk_rows(s, k, payload=None):
    nrows = s.shape[0]
    rid = lax.broadcasted_iota(jnp.int32, s.shape, 0).astype(F32)
    vals, outs = [], []
    for _ in range(k):
        m = jnp.max(s, axis=0, keepdims=True)
        first = jnp.min(jnp.where(s == m, rid, float(nrows)), axis=0, keepdims=True)
        hit = rid == first
        vals.append(m)
        if payload is None:
            outs.append(first)
        else:
            outs.append(jnp.sum(jnp.where(hit, payload, 0.0), axis=0, keepdims=True))
        s = jnp.where(hit, -jnp.inf, s)
    return jnp.concatenate(vals, axis=0), jnp.concatenate(outs, axis=0)


def _peer_route_kernel(h_ref, wq_ref, sk_ref, e_ref, g_ref):
    K = PEER_TOPK
    half = PEER_D_QUERY // 2
    q = _dot(h_ref[...].astype(BF16), wq_ref[...]).astype(BF16)
    for h in range(PEER_HEADS):
        top = []
        for c in range(2):
            col = (h * 2 + c) * half
            s_t = _dot_nt(sk_ref[h, c], q[:, col:col + half])
            top.append(_topk_rows(s_t, K))
        (v0, i0), (v1, i1) = top
        cand = jnp.concatenate([v0[i:i + 1] + v1 for i in range(K)], axis=0)
        eid = jnp.concatenate([i0[i:i + 1] * float(PEER_N_KEYS) + i1 for i in range(K)], axis=0)
        best, experts = _topk_rows(cand, K, payload=eid)
        p = jnp.exp(best - jnp.max(best, axis=0, keepdims=True))
        e_ref[h * K:(h + 1) * K, :] = experts.astype(jnp.int32)
        g_ref[h * K:(h + 1) * K, :] = p / jnp.sum(p, axis=0, keepdims=True)


def _peer_route(h2, wq, sk, tm=256):
    n = h2.shape[0]
    hk = PEER_HEADS * PEER_TOPK
    out = pl.BlockSpec((hk, tm), lambda i: (0, i))
    return pl.pallas_call(
        _peer_route_kernel,
        grid=(n // tm,),
        in_specs=[
            pl.BlockSpec((tm, D_MODEL), lambda i: (i, 0)),
            pl.BlockSpec(wq.shape, lambda i: (0, 0)),
            pl.BlockSpec(sk.shape, lambda i: (0, 0, 0, 0)),
        ],
        out_specs=[out, out],
        out_shape=[jax.ShapeDtypeStruct((hk, n), jnp.int32), jax.ShapeDtypeStruct((hk, n), F32)],
        compiler_params=_cparams(("parallel",)),
        name="peer_route",
    )(h2, wq, sk)


PEER_TT = 64
HALF_EXPERTS = PEER_N_KEYS * PEER_N_KEYS // 2
HALF_SHIFT = HALF_EXPERTS.bit_length() - 1
SUB_ORDER = (0, 4, 2, 6, 1, 5, 3, 7)


def _pack_table(w):
    bits = lax.bitcast_convert_type(w.astype(BF16), jnp.uint16).astype(jnp.uint32)
    packed = bits[:HALF_EXPERTS] | (bits[HALF_EXPERTS:] << 16)
    return packed.reshape(HALF_EXPERTS, SUBLANES, LANES)


def _gather_row(tbl_ref, e):
    word = tbl_ref[e & (HALF_EXPERTS - 1)]
    shift = (16 - ((e >> HALF_SHIFT) << 4)).astype(jnp.uint32)
    return pltpu.bitcast((word << shift) & jnp.uint32(0xFFFF0000), F32)


def _sublane_tree(ps):
    sub = lax.broadcasted_iota(jnp.int32, (SUBLANES, LANES), 0)
    m4 = sub < 4
    c = []
    for a, b in zip(ps[0::2], ps[1::2]):
        c.append(jnp.where(m4, a, b) + pltpu.roll(jnp.where(m4, b, a), 4, axis=0))
    m2 = (sub % 4) < 2
    d = []
    for x, y in zip(c[0::2], c[1::2]):
        d.append(jnp.where(m2, x + pltpu.roll(x, 6, axis=0), y + pltpu.roll(y, 2, axis=0)))
    m1 = (sub % 2) == 0
    x, y = d
    return jnp.where(m1, x + pltpu.roll(x, 7, axis=0), y + pltpu.roll(y, 1, axis=0))


def _peer_act_kernel(e_ref, h_ref, tbl_ref, gate_ref, ones_ref, w_ref, part_ref):
    hk = PEER_HEADS * PEER_TOPK
    tt = h_ref.shape[0]

    def token(t, carry):
        hrow = h_ref[t]
        for m in range(hk // SUBLANES):
            ps = [_gather_row(tbl_ref, e_ref[t, m * SUBLANES + j]) * hrow for j in SUB_ORDER]
            row0 = pl.multiple_of((t * (hk // SUBLANES) + m) * SUBLANES, SUBLANES)
            part_ref[pl.ds(row0, SUBLANES), :] = _sublane_tree(ps)
        return carry

    lax.fori_loop(0, tt, token, 0)
    sums = _dot_x2(part_ref[...], ones_ref[...]).reshape(tt, hk, LANES)
    eye = (lax.broadcasted_iota(jnp.int32, (hk, LANES), 0)
           == lax.broadcasted_iota(jnp.int32, (hk, LANES), 1))
    act = jnp.sum(jnp.where(eye[None], sums, 0.0), axis=1)
    w_ref[...] = _gelu(act) * gate_ref[...]


def _table_spec(tbl):
    return pl.BlockSpec(tbl.shape, lambda i: (0, 0, 0), pipeline_mode=pl.Buffered(1))


def _peer_act(experts, h2_tiles, tbl, gates):
    n, hk = experts.shape
    ones = jnp.ones((LANES, LANES), BF16)
    return pl.pallas_call(
        _peer_act_kernel,
        grid=(n // PEER_TT,),
        in_specs=[
            pl.BlockSpec((PEER_TT, hk), lambda i: (i, 0), memory_space=pltpu.SMEM),
            pl.BlockSpec((PEER_TT, SUBLANES, LANES), lambda i: (i, 0, 0)),
            _table_spec(tbl),
            pl.BlockSpec((PEER_TT, hk), lambda i: (i, 0)),
            pl.BlockSpec(ones.shape, lambda i: (0, 0)),
        ],
        out_specs=pl.BlockSpec((PEER_TT, hk), lambda i: (i, 0)),
        out_shape=jax.ShapeDtypeStruct((n, hk), F32),
        scratch_shapes=[pltpu.VMEM((PEER_TT * hk, LANES), F32)],
        compiler_params=_cparams(("arbitrary",)),
        name="peer_act",
    )(experts, h2_tiles, tbl, gates, ones)


PEER_ACCS = 4


def _peer_out_kernel(e_ref, w_ref, x_ref, tbl_ref, o_ref):
    hk = PEER_HEADS * PEER_TOPK
    tt = x_ref.shape[0]

    def token(t, carry):
        accs = [jnp.zeros((SUBLANES, LANES), F32) for _ in range(PEER_ACCS)]
        for k in range(hk):
            accs[k % PEER_ACCS] = accs[k % PEER_ACCS] + w_ref[t, k] * _gather_row(tbl_ref, e_ref[t, k])
        o_ref[t] = x_ref[t] + ((accs[0] + accs[1]) + (accs[2] + accs[3]))
        return carry

    lax.fori_loop(0, tt, token, 0)


def _peer_out(experts, w, x1_tiles, tbl):
    n, hk = experts.shape
    smem = pl.BlockSpec((PEER_TT, hk), lambda i: (i, 0), memory_space=pltpu.SMEM)
    tile = pl.BlockSpec((PEER_TT, SUBLANES, LANES), lambda i: (i, 0, 0))
    return pl.pallas_call(
        _peer_out_kernel,
        grid=(n // PEER_TT,),
        in_specs=[smem, smem, tile, _table_spec(tbl)],
        out_specs=tile,
        out_shape=jax.ShapeDtypeStruct(x1_tiles.shape, F32),
        compiler_params=_cparams(("arbitrary",)),
        name="peer_out",
    )(experts, w, x1_tiles, tbl)


def _peer_ffn_residual(x1, h2, wq, sk, tbl_u, tbl_v):
    n = x1.shape[0]
    e_t, g_t = _peer_route(h2, wq, sk)
    experts, gates = e_t.T, g_t.T
    tiles = lambda z: z.reshape(n, SUBLANES, LANES)
    w = _peer_act(experts, tiles(h2), tbl_u, gates)
    return _peer_out(experts, w, tiles(x1), tbl_v).reshape(n, D_MODEL)


def _rms_kernel(x_ref, g_ref, o_ref):
    x = x_ref[...]
    o_ref[...] = x * lax.rsqrt(jnp.mean(x * x, axis=-1, keepdims=True) + EPS) * g_ref[...]


def _final_norm(x2d, g, tm=512):
    n = x2d.shape[0]
    tile = pl.BlockSpec((tm, D_MODEL), lambda i: (i, 0))
    return pl.pallas_call(
        _rms_kernel,
        grid=(n // tm,),
        in_specs=[tile, pl.BlockSpec(g.shape, lambda i: (0, 0))],
        out_specs=tile,
        out_shape=jax.ShapeDtypeStruct((n, D_MODEL), F32),
        compiler_params=_cparams(("parallel",)),
        name="final_norm",
    )(x2d, g)


def kernel(x, norm_mix_g, w_in, cmp_k_pe, cmp_k_w1, cmp_k_w2, cmp_v_pe, cmp_v_w1, cmp_v_w2,
           rwkv_mu, rwkv_w0, rwkv_w2, rwkv_a0, rwkv_a2, rwkv_g2, rwkv_k_k, rwkv_k_a, rwkv_r_k,
           rwkv_ln_w, rwkv_ln_b, w_out, norm_ffn_g, peer_w_q, peer_sub_keys, peer_u, peer_v,
           norm_final_g):
    b, t_len, _ = x.shape
    n = b * t_len
    row = lambda z: z.reshape(1, -1)
    x2d = x.reshape(n, D_MODEL)
    for l in range(w_in.shape[0]):
        oa, ob, oc = _in_proj(x2d, row(norm_mix_g[l]), _pad_w_in(w_in[l]))
        nb = t_len // CMP_STRIDE
        rk = ob[:, 0:D_KV].reshape(b, nb, CMP_STRIDE * D_KV)
        rv = ob[:, D_KV:2 * D_KV].reshape(b, nb, CMP_STRIDE * D_KV)
        cmp_k, cmp_v = _compress(
            rk, rv, _expand_cmp_weights(cmp_k_pe[l], cmp_k_w1[l], cmp_k_w2[l]),
            _expand_cmp_weights(cmp_v_pe[l], cmp_v_w1[l], cmp_v_w2[l]))
        y_nsa = _nsa_attention(oa, ob, cmp_k, cmp_v, b, t_len)
        y_rwkv = _rwkv_mixer(oc, b, t_len, rwkv_mu[l], rwkv_w0[l], rwkv_w2[l], rwkv_a0[l],
                             rwkv_a2[l], rwkv_g2[l], rwkv_k_k[l], rwkv_k_a[l], rwkv_r_k[l],
                             rwkv_ln_w[l], rwkv_ln_b[l])
        wn, wr = _expand_w_out(w_out[l])
        x1, h2 = _out_proj(x2d, y_nsa, y_rwkv, wn, wr, row(norm_ffn_g[l]))
        x2d = _peer_ffn_residual(x1, h2, peer_w_q[l].astype(BF16), peer_sub_keys[l].astype(BF16),
                                 _pack_table(peer_u[l]), _pack_table(peer_v[l]))
    return _final_norm(x2d, row(norm_final_g)).reshape(b, t_len, D_MODEL)
```

```python
import functools

import numpy as np
import jax
import jax.numpy as jnp
from jax import lax
from jax.experimental import pallas as pl
from jax.experimental.pallas import tpu as pltpu

F32 = jnp.float32
BF16 = jnp.bfloat16

D_MODEL = 1024
HEAD_DIM = 64
NSA_HEADS = 8
NSA_KV_HEADS = 2
NSA_GROUP = NSA_HEADS // NSA_KV_HEADS
RWKV_HEADS = 8
D_NSA = NSA_HEADS * HEAD_DIM
D_RWKV = RWKV_HEADS * HEAD_DIM
D_KV = NSA_KV_HEADS * HEAD_DIM
CMP_LEN = 32
CMP_STRIDE = 16
CMP_HIDDEN = 128
SEL_LEN = 64
SEL_TOPN = 16
WINDOW = 512
RANK_W = 64
RANK_A = 64
RANK_G = 128
D_RWKV_IN = 3 * D_RWKV + RANK_W + RANK_A + RANK_G
PEER_HEADS = 8
PEER_N_KEYS = 128
PEER_D_QUERY = 256
PEER_TOPK = 16
EPS = 1e-6
GN_EPS = 64e-5

LANES = 128
SUBLANES = 8
VMEM_LIMIT = 56 * 1024 * 1024

NEG_BIG = -1e30
Q_EXP = NSA_HEADS * LANES
NA_COLS = Q_EXP + 4 * D_KV
NB_COLS = 3 * LANES


def _cparams(sem):
    return pltpu.CompilerParams(dimension_semantics=sem, vmem_limit_bytes=VMEM_LIMIT)


def _dot(a, b):
    return jnp.dot(a, b, preferred_element_type=F32)


def _dot_nt(a, b):
    return lax.dot_general(a, b, (((1,), (1,)), ((), ())), preferred_element_type=F32)


def _split2(x):
    hi = x.astype(BF16)
    lo = (x - hi.astype(F32)).astype(BF16)
    return hi, lo


def _dot_x2(x, e):
    hi, lo = _split2(x)
    return _dot(hi, e) + _dot(lo, e)


def _gelu(x):
    return 0.5 * x * (1.0 + jnp.tanh(0.7978845608028654 * (x + 0.044715 * (x * x * x))))


def _sigmoid(x):
    return 1.0 / (1.0 + jnp.exp(-x))


def _in_proj_kernel(x_ref, g_ref, w_ref, oa_ref, ob_ref, oc_ref):
    x = x_ref[...]
    h = x * lax.rsqrt(jnp.mean(x * x, axis=-1, keepdims=True) + EPS) * g_ref[...]
    hb = h.astype(BF16)
    oa_ref[...] = _dot(hb, w_ref[:, :NA_COLS]).astype(BF16)
    ob_ref[...] = _dot(hb, w_ref[:, NA_COLS:NA_COLS + NB_COLS])
    oc_ref[...] = _dot(hb, w_ref[:, NA_COLS + NB_COLS:])


def _in_proj(x2d, g, w_pad, tm=256):
    n = x2d.shape[0]
    ncols = w_pad.shape[1]
    nc = ncols - NA_COLS - NB_COLS
    return pl.pallas_call(
        _in_proj_kernel,
        grid=(n // tm,),
        in_specs=[
            pl.BlockSpec((tm, D_MODEL), lambda i: (i, 0)),
            pl.BlockSpec((1, D_MODEL), lambda i: (0, 0)),
            pl.BlockSpec((D_MODEL, ncols), lambda i: (0, 0)),
        ],
        out_specs=[
            pl.BlockSpec((tm, NA_COLS), lambda i: (i, 0)),
            pl.BlockSpec((tm, NB_COLS), lambda i: (i, 0)),
            pl.BlockSpec((tm, nc), lambda i: (i, 0)),
        ],
        out_shape=[
            jax.ShapeDtypeStruct((n, NA_COLS), BF16),
            jax.ShapeDtypeStruct((n, NB_COLS), F32),
            jax.ShapeDtypeStruct((n, nc), F32),
        ],
        compiler_params=_cparams(("parallel",)),
        name="in_proj",
    )(x2d, g, w_pad)


def _pad_w_in(w_in):
    sizes = (D_NSA, D_KV, D_KV, D_KV, D_KV, D_KV, D_KV, 3 * NSA_HEADS, D_RWKV_IN)
    offs = np.cumsum((0,) + sizes)
    q, kc, vc, ks, vs, kw, vw, gl, rw = (w_in[:, offs[i]:offs[i + 1]] for i in range(9))
    onehot = (np.arange(NSA_HEADS)[:, None] // NSA_GROUP == np.arange(NSA_KV_HEADS)[None, :])
    onehot = jnp.asarray(onehot, F32)
    qe = q.reshape(D_MODEL, NSA_HEADS, 1, HEAD_DIM) * onehot[None, :, :, None]
    qe = qe.reshape(D_MODEL, Q_EXP)
    glp = jnp.pad(gl, ((0, 0), (0, LANES - 3 * NSA_HEADS)))
    return jnp.concatenate([qe, ks, vs, kw, vw, kc, vc, glp, rw], axis=1).astype(BF16)


def _compress_kernel(rk_ref, rv_ref, pek_ref, pev_ref, w1k_ref, w1v_ref, w2k_ref, w2v_ref,
                     ok_ref, ov_ref):
    for r_ref, pe_ref, w1_ref, w2_ref, o_ref in (
            (rk_ref, pek_ref, w1k_ref, w2k_ref, ok_ref),
            (rv_ref, pev_ref, w1v_ref, w2v_ref, ov_ref)):
        rows = r_ref[0]
        nxt = pltpu.roll(rows, rows.shape[0] - 1, axis=0)
        a = (rows + pe_ref[0:1, :]).astype(BF16)
        b = (nxt + pe_ref[1:2, :]).astype(BF16)
        hid = _dot(a, w1_ref[0]) + _dot(b, w1_ref[1])
        o_ref[0] = _dot(_gelu(hid).astype(BF16), w2_ref[...]).astype(BF16)


def _expand_cmp_weights(pe, w1, w2):
    half = CMP_LEN // 2
    eye = jnp.eye(NSA_KV_HEADS, dtype=F32)
    w1r = w1.reshape(2, half, HEAD_DIM, CMP_HIDDEN)
    w1e = w1r[:, :, None, :, None, :] * eye[None, None, :, None, :, None]
    w1e = w1e.reshape(2, half * D_KV, NSA_KV_HEADS * CMP_HIDDEN).astype(BF16)
    pee = jnp.broadcast_to(pe.reshape(2, half, 1, HEAD_DIM), (2, half, NSA_KV_HEADS, HEAD_DIM))
    pee = pee.reshape(2, half * D_KV)
    w2e = (eye[:, None, :, None] * w2[None, :, None, :]).reshape(
        NSA_KV_HEADS * CMP_HIDDEN, D_KV).astype(BF16)
    return pee, w1e, w2e


def _compress(rk, rv, wk, wv):
    b, nb, width = rk.shape
    full2 = lambda a: pl.BlockSpec(a.shape, lambda i: (0, 0))
    full3 = lambda a: pl.BlockSpec(a.shape, lambda i: (0, 0, 0))
    row = pl.BlockSpec((1, nb, width), lambda i: (i, 0, 0))
    out = pl.BlockSpec((1, nb, D_KV), lambda i: (i, 0, 0))
    return pl.pallas_call(
        _compress_kernel,
        grid=(b,),
        in_specs=[row, row, full2(wk[0]), full2(wv[0]), full3(wk[1]), full3(wv[1]),
                  full2(wk[2]), full2(wv[2])],
        out_specs=[out, out],
        out_shape=[jax.ShapeDtypeStruct((b, nb, D_KV), BF16)] * 2,
        compiler_params=_cparams(("parallel",)),
        name="nsa_compress",
    )(rk, rv, wk[0], wv[0], wk[1], wv[1], wk[2], wv[2])


NSA_TQ = 128
NSA_TK = 256
NSA_TKW = 128


def _masked_softmax(s, mask):
    s = jnp.where(mask, s, NEG_BIG)
    s = s - jnp.max(s, axis=-1, keepdims=True)
    p = jnp.where(mask, jnp.exp(s), 0.0)
    return p / jnp.maximum(jnp.sum(p, axis=-1, keepdims=True), 1e-30)


def _flash_step(carry, s, mask, v):
    m, l, acc = carry
    s = jnp.where(mask, s, NEG_BIG)
    m_new = jnp.maximum(m, jnp.max(s, axis=-1, keepdims=True))
    alpha = jnp.exp(m - m_new)
    p = jnp.where(mask, jnp.exp(s - m_new), 0.0)
    l = alpha * l + jnp.sum(p, axis=-1, keepdims=True)
    acc = alpha * acc + _dot(p.astype(BF16), v)
    return m_new, l, acc


def _nsa_kernel(q_ref, ks_ref, vs_ref, kw_ref, vw_ref, ck_ref, cv_ref, ovl_ref, e_ref, gl_ref,
                eg_ref, o_ref):
    tq = NSA_TQ
    R = NSA_GROUP
    i = pl.program_id(1)
    q0 = i * tq
    t_col = q0 + lax.broadcasted_iota(jnp.int32, (tq, 1), 0)
    t_rows = jnp.concatenate([t_col] * R, axis=0)
    n_cmp = ck_ref.shape[1]
    n_sel = ovl_ref.shape[1]

    gate = _sigmoid(gl_ref[...])
    g_hi, g_lo = _split2(gate)
    gexp = [_dot(g_hi, eg_ref[j]) + _dot(g_lo, eg_ref[j]) for j in range(3)]

    cmp_end = lax.broadcasted_iota(jnp.int32, (1, n_cmp), 1) * CMP_STRIDE + (CMP_LEN - 1)
    dist_c = t_col - cmp_end
    mask_c = dist_c >= 0
    dist_cf = dist_c.astype(F32)
    ids = lax.broadcasted_iota(jnp.int32, (1, n_sel), 1)
    idsf = ids.astype(F32)
    cur = t_col // SEL_LEN
    forced = (ids == 0) | (ids == cur) | (ids == cur - 1)
    valid = ids * SEL_LEN <= t_col

    for g in range(NSA_KV_HEADS):
        heads = [g * R + r for r in range(R)]
        slopes = [2.0 ** (-(h + 1)) for h in heads]
        qh = [q_ref[:, h * LANES:(h + 1) * LANES] * 0.125 for h in heads]
        ck = ck_ref[0]
        cv = cv_ref[0]

        o_cmp = []
        imp = jnp.zeros((tq, n_sel), F32)
        for r in range(R):
            s = _dot_nt(qh[r], ck) - slopes[r] * dist_cf
            p = _masked_softmax(s, mask_c)
            o_cmp.append(_dot(p.astype(BF16), cv))
            imp = imp + jnp.dot(p, ovl_ref[...], preferred_element_type=F32,
                                precision=lax.Precision.HIGHEST)
        imp = jnp.where(forced, 1e6, jnp.where(valid, imp, -1.0))

        sel = jnp.zeros((tq, n_sel), F32)
        for _ in range(min(SEL_TOPN, n_sel)):
            m = jnp.max(imp, axis=-1, keepdims=True)
            first = jnp.min(jnp.where(imp == m, idsf, float(n_sel)), axis=-1, keepdims=True)
            hit = idsf == first
            sel = jnp.where(hit, 1.0, sel)
            imp = jnp.where(hit, -3e38, imp)
        sel_b = sel.astype(BF16)

        q_st = jnp.concatenate(qh, axis=0)
        slope_rows = jnp.concatenate(
            [jnp.full((tq, 1), sl, F32) for sl in slopes], axis=0)
        init = (jnp.full((R * tq, 1), NEG_BIG, F32), jnp.zeros((R * tq, 1), F32),
                jnp.zeros((R * tq, LANES), F32))

        def slc_body(j, carry):
            k0 = pl.multiple_of(j * NSA_TK, NSA_TK)
            k = ks_ref[pl.ds(k0, NSA_TK), :]
            v = vs_ref[pl.ds(k0, NSA_TK), :]
            pos = k0 + lax.broadcasted_iota(jnp.int32, (1, NSA_TK), 1)
            dist = t_rows - pos
            selm = _dot(sel_b, e_ref[:, pl.ds(k0, NSA_TK)])
            selm = jnp.concatenate([selm] * R, axis=0)
            mask = (dist >= 0) & (selm > 0.5)
            s = _dot_nt(q_st, k) - slope_rows * dist.astype(F32)
            return _flash_step(carry, s, mask, v)

        n_kv = (q0 + tq + NSA_TK - 1) // NSA_TK
        _, l_s, acc_s = lax.fori_loop(0, n_kv, slc_body, init)
        o_slc = acc_s / l_s

        def win_body(j, carry):
            k0 = pl.multiple_of(j * NSA_TKW, NSA_TKW)
            k = kw_ref[pl.ds(k0, NSA_TKW), :]
            v = vw_ref[pl.ds(k0, NSA_TKW), :]
            pos = k0 + lax.broadcasted_iota(jnp.int32, (1, NSA_TKW), 1)
            dist = t_rows - pos
            mask = (dist >= 0) & (dist < WINDOW)
            s = _dot_nt(q_st, k) - slope_rows * dist.astype(F32)
            return _flash_step(carry, s, mask, v)

        lo = jnp.maximum(q0 - WINDOW, 0) // NSA_TKW
        hi = (q0 + tq + NSA_TKW - 1) // NSA_TKW
        _, l_w, acc_w = lax.fori_loop(lo, hi, win_body, init)
        o_win = acc_w / l_w

        for r, h in enumerate(heads):
            cols = slice(h * LANES, (h + 1) * LANES)
            rows = slice(r * tq, (r + 1) * tq)
            out = (gexp[0][:, cols] * o_cmp[r] + gexp[1][:, cols] * o_slc[rows]
                   + gexp[2][:, cols] * o_win[rows])
            o_ref[:, cols] = out.astype(BF16)


def _nsa_consts(t_len, n_cmp_pad):
    n_cmp = (t_len - CMP_LEN) // CMP_STRIDE + 1
    n_sel = t_len // SEL_LEN
    cmp_start = np.arange(n_cmp) * CMP_STRIDE
    sel_start = np.arange(n_sel) * SEL_LEN
    ovl = np.clip(np.minimum(cmp_start[:, None] + CMP_LEN, sel_start[None, :] + SEL_LEN)
                  - np.maximum(cmp_start[:, None], sel_start[None, :]), 0, None) / CMP_LEN
    ovl_pad = np.zeros((n_cmp_pad, n_sel), np.float32)
    ovl_pad[:n_cmp] = ovl
    expand = (np.arange(t_len)[None, :] // SEL_LEN == np.arange(n_sel)[:, None])
    eg = np.zeros((3, LANES, Q_EXP), np.float32)
    for h in range(NSA_HEADS):
        for j in range(3):
            eg[j, h * 3 + j, h * LANES:(h + 1) * LANES] = 1.0
    return jnp.asarray(ovl_pad), jnp.asarray(expand, BF16), jnp.asarray(eg, BF16)


def _nsa_attention(oa, ob, cmp_k, cmp_v, b, t_len):
    n_cmp_pad = cmp_k.shape[1]
    ovl, expand, eg = _nsa_consts(t_len, n_cmp_pad)
    nq = t_len // NSA_TQ
    kv_col0 = Q_EXP // D_KV
    kv = lambda c: pl.BlockSpec((t_len, D_KV), lambda bi, i: (bi, kv_col0 + c))
    cmp_spec = pl.BlockSpec((1, n_cmp_pad, D_KV), lambda bi, i: (bi, 0, 0))
    return pl.pallas_call(
        _nsa_kernel,
        grid=(b, nq),
        in_specs=[
            pl.BlockSpec((NSA_TQ, Q_EXP), lambda bi, i: (bi * nq + i, 0)),
            kv(0), kv(1), kv(2), kv(3),
            cmp_spec, cmp_spec,
            pl.BlockSpec(ovl.shape, lambda bi, i: (0, 0)),
            pl.BlockSpec(expand.shape, lambda bi, i: (0, 0)),
            pl.BlockSpec((NSA_TQ, LANES), lambda bi, i: (bi * nq + i, 2)),
            pl.BlockSpec(eg.shape, lambda bi, i: (0, 0, 0)),
        ],
        out_specs=pl.BlockSpec((NSA_TQ, Q_EXP), lambda bi, i: (bi * nq + i, 0)),
        out_shape=jax.ShapeDtypeStruct((b * t_len, Q_EXP), BF16),
        compiler_params=_cparams(("parallel", "arbitrary")),
        name="nsa_attention",
    )(oa, oa, oa, oa, oa, cmp_k, cmp_v, ovl, expand, ob, eg)


def _dot_x3(x, e):
    hi = x.astype(BF16)
    r1 = x - hi.astype(F32)
    mid = r1.astype(BF16)
    lo = (r1 - mid.astype(F32)).astype(BF16)
    return _dot(hi, e) + _dot(mid, e) + _dot(lo, e)


def _head_sum_matrix():
    ids = np.arange(D_RWKV) // HEAD_DIM
    return jnp.asarray(ids[:, None] == ids[None, :], BF16)


def _rwkv_pre_kernel(p_ref, hp_ref, mu_ref, w0_ref, w2_ref, a0_ref, a2_ref, g2_ref, kk_ref,
                     ka_ref, bd_ref, r_o, lw_o, k_o, v_o, kkn_o, a_o, g_o, *, tiles_per_seq):
    i = pl.program_id(0)
    p = p_ref[...]
    tm = p.shape[0]
    keep = jnp.where(i % tiles_per_seq == 0, 0.0, 1.0)
    halo = hp_ref[SUBLANES - 1:SUBLANES, :] * keep
    prev = pltpu.roll(p, 1, axis=0)
    row0 = lax.broadcasted_iota(jnp.int32, (tm, 1), 0) == 0
    prev = jnp.where(row0, halo, prev)
    ps = p + (prev - p) * mu_ref[...]
    d = D_RWKV
    r = ps[:, 0:d]
    k = ps[:, d:2 * d]
    v = ps[:, 2 * d:3 * d]
    xw = ps[:, 3 * d:3 * d + RANK_W]
    xa = ps[:, 3 * d + RANK_W:3 * d + RANK_W + RANK_A]
    xg = ps[:, 3 * d + RANK_W + RANK_A:]
    z = -(w0_ref[...] + _dot(jnp.tanh(xw).astype(BF16), w2_ref[...]))
    softplus = jnp.maximum(z, 0.0) + jnp.log(1.0 + jnp.exp(-jnp.abs(z)))
    w = -softplus - 0.5
    a = _sigmoid(a0_ref[...] + _dot(xa.astype(BF16), a2_ref[...]))
    g = _dot(_sigmoid(xg).astype(BF16), g2_ref[...])
    kk = k * kk_ref[...]
    ss = _dot_x3(kk * kk, bd_ref[...])
    kk = kk / jnp.maximum(jnp.sqrt(ss), 1e-12)
    r_o[...] = r
    lw_o[...] = -jnp.exp(w)
    k_o[...] = k * (1.0 + (a - 1.0) * ka_ref[...])
    v_o[...] = v
    kkn_o[...] = kk
    a_o[...] = a
    g_o[...] = g


def _rwkv_pre(oc, t_len, mu, w0, w2, a0, a2, g2, k_k, k_a, bd, tm=256):
    n = oc.shape[0]
    tiles_per_seq = t_len // tm
    halo_blocks = tm // SUBLANES
    full = lambda a: pl.BlockSpec(a.shape, lambda i: (0, 0))
    outs = pl.BlockSpec((tm, D_RWKV), lambda i: (i, 0))
    params = (mu, w0, w2, a0, a2, g2, k_k, k_a, bd)
    return pl.pallas_call(
        functools.partial(_rwkv_pre_kernel, tiles_per_seq=tiles_per_seq),
        grid=(n // tm,),
        in_specs=[
            pl.BlockSpec((tm, D_RWKV_IN), lambda i: (i, 0)),
            pl.BlockSpec((SUBLANES, D_RWKV_IN),
                         lambda i: (jnp.maximum(i * halo_blocks - 1, 0), 0)),
        ] + [full(a) for a in params],
        out_specs=[outs] * 7,
        out_shape=[jax.ShapeDtypeStruct((n, D_RWKV), F32)] * 7,
        compiler_params=_cparams(("parallel",)),
        name="rwkv_pre",
    )(oc, oc, *params)


RWKV_C = 64
RWKV_CB = 128
RWKV_PASSES = 3


def _bmm(eq, a, b):
    ein = lambda x, y: jnp.einsum(eq, x, y, preferred_element_type=F32)
    if RWKV_PASSES == 1:
        return ein(a.astype(BF16), b.astype(BF16))
    ah, al = _split2(a)
    bh, bl = _split2(b)
    return ein(ah, bh) + ein(ah, bl) + ein(al, bh)


def _cumsum_rows(x2d, seg):
    rows = lax.broadcasted_iota(jnp.int32, (x2d.shape[0], 1), 0) % seg
    step = 1
    while step < seg:
        shifted = pltpu.roll(x2d, step, axis=0)
        x2d = x2d + jnp.where(rows >= step, shifted, 0.0)
        step *= 2
    return x2d


def _rwkv_rec_kernel(r_ref, lw_ref, k_ref, kk_ref, a_ref, vt_ref, ot_ref, s_ref):
    H = r_ref.shape[1]
    C = RWKV_C

    @pl.when(pl.program_id(1) == 0)
    def _():
        s_ref[...] = jnp.zeros_like(s_ref)

    ri = lax.broadcasted_iota(jnp.int32, (C, C), 0)
    ci = lax.broadcasted_iota(jnp.int32, (C, C), 1)
    strict = (ri > ci)[None]
    incl = (ri >= ci)[None]
    eye = (ri == ci).astype(F32)[None]

    for sub in range(RWKV_CB // C):
        rows = slice(sub * C, (sub + 1) * C)
        r = r_ref[0, :, rows, :]
        lw = lw_ref[0, :, rows, :]
        k = k_ref[0, :, rows, :]
        kk = kk_ref[0, :, rows, :]
        a = a_ref[0, :, rows, :]
        vt = vt_ref[0, :, :, rows]
        s0 = s_ref[...]

        cum = _cumsum_rows(lw.reshape(H * C, HEAD_DIM), C).reshape(H, C, HEAD_DIM)
        cum_last = cum[:, C - 1:C, :]
        p_inv = jnp.exp(-cum)
        w_last = jnp.exp(cum_last - cum)
        bm = kk * a
        at = -kk * jnp.exp(cum - lw)
        rt = r * jnp.exp(cum)
        bt = bm * p_inv
        kt = k * p_inv

        nt = 'hik,hjk->hij'
        m_ab = jnp.where(strict, _bmm(nt, at, bt), 0.0)
        m_ak = jnp.where(strict, _bmm(nt, at, kt), 0.0)
        n_rb = jnp.where(incl, _bmm(nt, rt, bt), 0.0)
        n_rk = jnp.where(incl, _bmm(nt, rt, kt), 0.0)

        tinv = eye + m_ab
        mp = m_ab
        step = 1
        while 2 * step < C:
            mp = _bmm('hij,hjk->hik', mp, mp)
            tinv = tinv + _bmm('hij,hjk->hik', tinv, mp)
            step *= 2

        rhs_t = _bmm('hvk,hik->hvi', s0, at) + _bmm('hvj,hij->hvi', vt, m_ak)
        ut = _bmm('hvj,hij->hvi', rhs_t, tinv)
        ot = (_bmm('hvk,hik->hvi', s0, rt) + _bmm('hvj,hij->hvi', ut, n_rb)
              + _bmm('hvj,hij->hvi', vt, n_rk))
        ot_ref[0, :, :, rows] = ot
        s_ref[...] = (s0 * jnp.exp(cum_last) + _bmm('hvj,hjk->hvk', ut, bm * w_last)
                      + _bmm('hvj,hjk->hvk', vt, k * w_last))


def _rwkv_recurrence(r, lw, k, kk, a, vt):
    b, h, t_len, d = r.shape
    tok = pl.BlockSpec((1, h, RWKV_CB, d), lambda bi, c: (bi, 0, c, 0))
    tr = pl.BlockSpec((1, h, d, RWKV_CB), lambda bi, c: (bi, 0, 0, c))
    return pl.pallas_call(
        _rwkv_rec_kernel,
        grid=(b, t_len // RWKV_CB),
        in_specs=[tok] * 5 + [tr],
        out_specs=tr,
        out_shape=jax.ShapeDtypeStruct((b, h, d, t_len), F32),
        scratch_shapes=[pltpu.VMEM((h, d, d), F32)],
        compiler_params=_cparams(("parallel", "arbitrary")),
        name="rwkv_recurrence",
    )(r, lw, k, kk, a, vt)


def _rwkv_post_kernel(o_ref, r_ref, k_ref, v_ref, g_ref, lnw_ref, lnb_ref, rk_ref, bd_ref, y_ref):
    o = o_ref[...]
    bd = bd_ref[...]
    inv = 1.0 / HEAD_DIM
    mean = _dot_x3(o, bd) * inv
    d = o - mean
    var = _dot_x3(d * d, bd) * inv
    on = d * lax.rsqrt(var + GN_EPS) * lnw_ref[...] + lnb_ref[...]
    bonus = _dot_x3(r_ref[...] * k_ref[...] * rk_ref[...], bd)
    y_ref[...] = ((on + bonus * v_ref[...]) * g_ref[...]).astype(BF16)


def _rwkv_post(o, r, k, v, g, ln_w, ln_b, r_k, bd, tm=512):
    n = o.shape[0]
    tile = pl.BlockSpec((tm, D_RWKV), lambda i: (i, 0))
    full = lambda a: pl.BlockSpec(a.shape, lambda i: (0, 0))
    return pl.pallas_call(
        _rwkv_post_kernel,
        grid=(n // tm,),
        in_specs=[tile] * 5 + [full(ln_w), full(ln_b), full(r_k), full(bd)],
        out_specs=tile,
        out_shape=jax.ShapeDtypeStruct((n, D_RWKV), BF16),
        compiler_params=_cparams(("parallel",)),
        name="rwkv_post",
    )(o, r, k, v, g, ln_w, ln_b, r_k, bd)


def _rwkv_mixer(oc, b, t_len, mu, w0, w2, a0, a2, g2, k_k, k_a, r_k, ln_w, ln_b):
    bd = _head_sum_matrix()
    row = lambda z: z.reshape(1, -1)
    r, lw, k, v, kk, a, g = _rwkv_pre(
        oc, t_len, row(mu), row(w0), w2.astype(BF16), row(a0), a2.astype(BF16),
        g2.astype(BF16), row(k_k), row(k_a), bd)
    hm = lambda z: z.reshape(b, t_len, RWKV_HEADS, HEAD_DIM).transpose(0, 2, 1, 3)
    vt = v.reshape(b, t_len, RWKV_HEADS, HEAD_DIM).transpose(0, 2, 3, 1)
    ot = _rwkv_recurrence(hm(r), hm(lw), hm(k), hm(kk), hm(a), vt)
    o = ot.transpose(0, 3, 1, 2).reshape(b * t_len, D_RWKV)
    return _rwkv_post(o, r, k, v, g, row(ln_w), row(ln_b), row(r_k), bd)


def _out_proj_kernel(x_ref, yn_ref, yr_ref, wn_ref, wr_ref, g_ref, x1_ref, h2_ref):
    x1 = x_ref[...] + _dot(yn_ref[...], wn_ref[...]) + _dot(yr_ref[...], wr_ref[...])
    x1_ref[...] = x1
    h2_ref[...] = x1 * lax.rsqrt(jnp.mean(x1 * x1, axis=-1, keepdims=True) + EPS) * g_ref[...]


def _expand_w_out(w_out):
    wn = w_out[:D_NSA].reshape(NSA_HEADS, 1, HEAD_DIM, D_MODEL)
    onehot = (np.arange(NSA_HEADS)[:, None] // NSA_GROUP == np.arange(NSA_KV_HEADS)[None, :])
    wn = wn * jnp.asarray(onehot, F32)[:, :, None, None]
    return wn.reshape(Q_EXP, D_MODEL).astype(BF16), w_out[D_NSA:].astype(BF16)


def _out_proj(x2d, y_nsa, y_rwkv, wn, wr, g, tm=512):
    n = x2d.shape[0]
    tile = lambda a: pl.BlockSpec((tm, a.shape[1]), lambda i: (i, 0))
    full = lambda a: pl.BlockSpec(a.shape, lambda i: (0, 0))
    return pl.pallas_call(
        _out_proj_kernel,
        grid=(n // tm,),
        in_specs=[tile(x2d), tile(y_nsa), tile(y_rwkv), full(wn), full(wr), full(g)],
        out_specs=[tile(x2d), tile(x2d)],
        out_shape=[jax.ShapeDtypeStruct((n, D_MODEL), F32)] * 2,
        compiler_params=_cparams(("parallel",)),
        name="out_proj",
    )(x2d, y_nsa, y_rwkv, wn, wr, g)


def _topk_rows(s, k, payload=None):
    nrows = s.shape[0]
    rid = lax.broadcasted_iota(jnp.int32, s.shape, 0).astype(F32)
    vals, outs = [], []
    for _ in range(k):
        m = jnp.max(s, axis=0, keepdims=True)
        first = jnp.min(jnp.where(s == m, rid, float(nrows)), axis=0, keepdims=True)
        hit = rid == first
        vals.append(m)
        if payload is None:
            outs.append(first)
        else:
            outs.append(jnp.sum(jnp.where(hit, payload, 0.0), axis=0, keepdims=True))
        s = jnp.where(hit, -jnp.inf, s)
    return jnp.concatenate(vals, axis=0), jnp.concatenate(outs, axis=0)


def _peer_route_kernel(h_ref, wq_ref, sk_ref, e_ref, g_ref):
    K = PEER_TOPK
    half = PEER_D_QUERY // 2
    q = _dot(h_ref[...].astype(BF16), wq_ref[...]).astype(BF16)
    for h in range(PEER_HEADS):
        top = []
        for c in range(2):
            col = (h * 2 + c) * half
            s_t = _dot_nt(sk_ref[h, c], q[:, col:col + half])
            top.append(_topk_rows(s_t, K))
        (v0, i0), (v1, i1) = top
        cand = jnp.concatenate([v0[i:i + 1] + v1 for i in range(K)], axis=0)
        eid = jnp.concatenate([i0[i:i + 1] * float(PEER_N_KEYS) + i1 for i in range(K)], axis=0)
        best, experts = _topk_rows(cand, K, payload=eid)
        p = jnp.exp(best - jnp.max(best, axis=0, keepdims=True))
        e_ref[h * K:(h + 1) * K, :] = experts.astype(jnp.int32)
        g_ref[h * K:(h + 1) * K, :] = p / jnp.sum(p, axis=0, keepdims=True)


def _peer_route(h2, wq, sk, tm=256):
    n = h2.shape[0]
    hk = PEER_HEADS * PEER_TOPK
    out = pl.BlockSpec((hk, tm), lambda i: (0, i))
    return pl.pallas_call(
        _peer_route_kernel,
        grid=(n // tm,),
        in_specs=[
            pl.BlockSpec((tm, D_MODEL), lambda i: (i, 0)),
            pl.BlockSpec(wq.shape, lambda i: (0, 0)),
            pl.BlockSpec(sk.shape, lambda i: (0, 0, 0, 0)),
        ],
        out_specs=[out, out],
        out_shape=[jax.ShapeDtypeStruct((hk, n), jnp.int32), jax.ShapeDtypeStruct((hk, n), F32)],
        compiler_params=_cparams(("parallel",)),
        name="peer_route",
    )(h2, wq, sk)


PEER_TT = 64
HALF_ROWS = SUBLANES // 2
SUB_ORDER = (0, 4, 2, 6, 1, 5, 3, 7)
HIGH_MASK = 0xFFFF0000


def _pack_table(w):
    bits = lax.bitcast_convert_type(w.astype(BF16), jnp.uint16).astype(jnp.uint32)
    half = w.shape[1] // 2
    packed = bits[:, :half] | (bits[:, half:] << 16)
    return packed.reshape(w.shape[0] * HALF_ROWS, LANES)


def _table_row(tbl_ref, row0):
    return tbl_ref[pl.ds(pl.multiple_of(row0, HALF_ROWS), HALF_ROWS), :]


def _unpack_words(word):
    lo = pltpu.bitcast(word << 16, F32)
    hi = pltpu.bitcast(word & jnp.uint32(HIGH_MASK), F32)
    return lo, hi


def _sublane_tree(c):
    sub = lax.broadcasted_iota(jnp.int32, (SUBLANES, LANES), 0)
    m2 = (sub % 4) < 2
    d = []
    for x, y in zip(c[0::2], c[1::2]):
        d.append(jnp.where(m2, x + pltpu.roll(x, 6, axis=0), y + pltpu.roll(y, 2, axis=0)))
    m1 = (sub % 2) == 0
    x, y = d
    return jnp.where(m1, x + pltpu.roll(x, 7, axis=0), y + pltpu.roll(y, 1, axis=0))


def _peer_act_kernel(e_ref, h_ref, tbl_ref, gate_ref, ones_ref, w_ref, part_ref):
    hk = PEER_HEADS * PEER_TOPK
    tt = h_ref.shape[0]

    def token(t, carry):
        hrow = h_ref[t]
        h_lo = jnp.concatenate([hrow[:HALF_ROWS]] * 2, axis=0)
        h_hi = jnp.concatenate([hrow[HALF_ROWS:]] * 2, axis=0)
        for m in range(hk // SUBLANES):
            es = [e_ref[t, m * SUBLANES + j] for j in SUB_ORDER]
            c = []
            for ea, eb in zip(es[0::2], es[1::2]):
                lo, hi = _unpack_words(jnp.concatenate(
                    [_table_row(tbl_ref, ea), _table_row(tbl_ref, eb)], axis=0))
                c.append(lo * h_lo + hi * h_hi)
            row0 = pl.multiple_of((t * (hk // SUBLANES) + m) * SUBLANES, SUBLANES)
            part_ref[pl.ds(row0, SUBLANES), :] = _sublane_tree(c)
        return carry

    lax.fori_loop(0, tt, token, 0)
    sums = _dot_x2(part_ref[...], ones_ref[...]).reshape(tt, hk, LANES)
    eye = (lax.broadcasted_iota(jnp.int32, (hk, LANES), 0)
           == lax.broadcasted_iota(jnp.int32, (hk, LANES), 1))
    act = jnp.sum(jnp.where(eye[None], sums, 0.0), axis=1)
    w_ref[...] = _gelu(act) * gate_ref[...]


def _table_spec(tbl):
    return pl.BlockSpec(tbl.shape, lambda i: (0, 0), pipeline_mode=pl.Buffered(1))


def _peer_act(experts, h2_tiles, tbl, gates):
    n, hk = experts.shape
    ones = jnp.ones((LANES, LANES), BF16)
    return pl.pallas_call(
        _peer_act_kernel,
        grid=(n // PEER_TT,),
        in_specs=[
            pl.BlockSpec((PEER_TT, hk), lambda i: (i, 0), memory_space=pltpu.SMEM),
            pl.BlockSpec((PEER_TT, SUBLANES, LANES), lambda i: (i, 0, 0)),
            _table_spec(tbl),
            pl.BlockSpec((PEER_TT, hk), lambda i: (i, 0)),
            pl.BlockSpec(ones.shape, lambda i: (0, 0)),
        ],
        out_specs=pl.BlockSpec((PEER_TT, hk), lambda i: (i, 0)),
        out_shape=jax.ShapeDtypeStruct((n, hk), F32),
        scratch_shapes=[pltpu.VMEM((PEER_TT * hk, LANES), F32)],
        compiler_params=_cparams(("arbitrary",)),
        name="peer_act",
    )(experts, h2_tiles, tbl, gates, ones)


PEER_ACCS = 4
PEER_GROUP = 64


def _peer_out_kernel(e_ref, w_ref, x_ref, tbl_ref, o_ref):
    hk = PEER_HEADS * PEER_TOPK
    tt = x_ref.shape[0]

    def token(t, carry):
        def group(gi, accs):
            acc_lo, acc_hi = list(accs[0]), list(accs[1])
            base = t * hk + gi * PEER_GROUP
            for j in range(PEER_GROUP):
                lo, hi = _unpack_words(_table_row(tbl_ref, e_ref[base + j]))
                wk = w_ref[base + j]
                acc_lo[j % PEER_ACCS] = acc_lo[j % PEER_ACCS] + wk * lo
                acc_hi[j % PEER_ACCS] = acc_hi[j % PEER_ACCS] + wk * hi
            return tuple(acc_lo), tuple(acc_hi)

        zeros = tuple(jnp.zeros((HALF_ROWS, LANES), F32) for _ in range(PEER_ACCS))
        acc_lo, acc_hi = lax.fori_loop(0, hk // PEER_GROUP, group, (zeros, zeros))
        tree = lambda a: (a[0] + a[1]) + (a[2] + a[3])
        o_ref[t] = x_ref[t] + jnp.concatenate([tree(acc_lo), tree(acc_hi)], axis=0)
        return carry

    lax.fori_loop(0, tt, token, 0)


def _peer_out(experts, w, x1_tiles, tbl):
    n, hk = experts.shape
    smem = pl.BlockSpec((PEER_TT * hk,), lambda i: (i,), memory_space=pltpu.SMEM)
    tile = pl.BlockSpec((PEER_TT, SUBLANES, LANES), lambda i: (i, 0, 0))
    return pl.pallas_call(
        _peer_out_kernel,
        grid=(n // PEER_TT,),
        in_specs=[smem, smem, tile, _table_spec(tbl)],
        out_specs=tile,
        out_shape=jax.ShapeDtypeStruct(x1_tiles.shape, F32),
        compiler_params=_cparams(("arbitrary",)),
        name="peer_out",
    )(experts.reshape(-1), w.reshape(-1), x1_tiles, tbl)


def _peer_ffn_residual(x1, h2, wq, sk, tbl_u, tbl_v):
    n = x1.shape[0]
    e_t, g_t = _peer_route(h2, wq, sk)
    experts, gates = e_t.T * HALF_ROWS, g_t.T
    tiles = lambda z: z.reshape(n, SUBLANES, LANES)
    w = _peer_act(experts, tiles(h2), tbl_u, gates)
    return _peer_out(experts, w, tiles(x1), tbl_v).reshape(n, D_MODEL)


def _rms_kernel(x_ref, g_ref, o_ref):
    x = x_ref[...]
    o_ref[...] = x * lax.rsqrt(jnp.mean(x * x, axis=-1, keepdims=True) + EPS) * g_ref[...]


def _final_norm(x2d, g, tm=512):
    n = x2d.shape[0]
    tile = pl.BlockSpec((tm, D_MODEL), lambda i: (i, 0))
    return pl.pallas_call(
        _rms_kernel,
        grid=(n // tm,),
        in_specs=[tile, pl.BlockSpec(g.shape, lambda i: (0, 0))],
        out_specs=tile,
        out_shape=jax.ShapeDtypeStruct((n, D_MODEL), F32),
        compiler_params=_cparams(("parallel",)),
        name="final_norm",
    )(x2d, g)


def kernel(x, norm_mix_g, w_in, cmp_k_pe, cmp_k_w1, cmp_k_w2, cmp_v_pe, cmp_v_w1, cmp_v_w2,
           rwkv_mu, rwkv_w0, rwkv_w2, rwkv_a0, rwkv_a2, rwkv_g2, rwkv_k_k, rwkv_k_a, rwkv_r_k,
           rwkv_ln_w, rwkv_ln_b, w_out, norm_ffn_g, peer_w_q, peer_sub_keys, peer_u, peer_v,
           norm_final_g):
    b, t_len, _ = x.shape
    n = b * t_len
    row = lambda z: z.reshape(1, -1)
    x2d = x.reshape(n, D_MODEL)
    for l in range(w_in.shape[0]):
        oa, ob, oc = _in_proj(x2d, row(norm_mix_g[l]), _pad_w_in(w_in[l]))
        nb = t_len // CMP_STRIDE
        rk = ob[:, 0:D_KV].reshape(b, nb, CMP_STRIDE * D_KV)
        rv = ob[:, D_KV:2 * D_KV].reshape(b, nb, CMP_STRIDE * D_KV)
        cmp_k, cmp_v = _compress(
            rk, rv, _expand_cmp_weights(cmp_k_pe[l], cmp_k_w1[l], cmp_k_w2[l]),
            _expand_cmp_weights(cmp_v_pe[l], cmp_v_w1[l], cmp_v_w2[l]))
        y_nsa = _nsa_attention(oa, ob, cmp_k, cmp_v, b, t_len)
        y_rwkv = _rwkv_mixer(oc, b, t_len, rwkv_mu[l], rwkv_w0[l], rwkv_w2[l], rwkv_a0[l],
                             rwkv_a2[l], rwkv_g2[l], rwkv_k_k[l], rwkv_k_a[l], rwkv_r_k[l],
                             rwkv_ln_w[l], rwkv_ln_b[l])
        wn, wr = _expand_w_out(w_out[l])
        x1, h2 = _out_proj(x2d, y_nsa, y_rwkv, wn, wr, row(norm_ffn_g[l]))
        x2d = _peer_ffn_residual(x1, h2, peer_w_q[l].astype(BF16), peer_sub_keys[l].astype(BF16),
                                 _pack_table(peer_u[l]), _pack_table(peer_v[l]))
    return _final_norm(x2d, row(norm_final_g)).reshape(b, t_len, D_MODEL)
```

```python
import functools

import numpy as np
import jax
import jax.numpy as jnp
from jax import lax
from jax.experimental import pallas as pl
from jax.experimental.pallas import tpu as pltpu

F32 = jnp.float32
BF16 = jnp.bfloat16

D_MODEL = 1024
HEAD_DIM = 64
NSA_HEADS = 8
NSA_KV_HEADS = 2
NSA_GROUP = NSA_HEADS // NSA_KV_HEADS
RWKV_HEADS = 8
D_NSA = NSA_HEADS * HEAD_DIM
D_RWKV = RWKV_HEADS * HEAD_DIM
D_KV = NSA_KV_HEADS * HEAD_DIM
CMP_LEN = 32
CMP_STRIDE = 16
CMP_HIDDEN = 128
SEL_LEN = 64
SEL_TOPN = 16
WINDOW = 512
RANK_W = 64
RANK_A = 64
RANK_G = 128
D_RWKV_IN = 3 * D_RWKV + RANK_W + RANK_A + RANK_G
PEER_HEADS = 8
PEER_N_KEYS = 128
PEER_D_QUERY = 256
PEER_TOPK = 16
EPS = 1e-6
GN_EPS = 64e-5

LANES = 128
SUBLANES = 8
VMEM_LIMIT = 56 * 1024 * 1024

NEG_BIG = -1e30
Q_EXP = NSA_HEADS * LANES
K_AUG = NSA_KV_HEADS * LANES
NA_COLS = Q_EXP + 2 * K_AUG + 2 * D_KV
NB_COLS = 3 * LANES
FEAT0 = HEAD_DIM
POS_SPLIT = 64


def _cparams(sem):
    return pltpu.CompilerParams(dimension_semantics=sem, vmem_limit_bytes=VMEM_LIMIT)


def _dot(a, b):
    return jnp.dot(a, b, preferred_element_type=F32)


def _dot_nt(a, b):
    return lax.dot_general(a, b, (((1,), (1,)), ((), ())), preferred_element_type=F32)


def _split2(x):
    hi = x.astype(BF16)
    lo = (x - hi.astype(F32)).astype(BF16)
    return hi, lo


def _dot_x2(x, e):
    hi, lo = _split2(x)
    return _dot(hi, e) + _dot(lo, e)


def _gelu(x):
    return 0.5 * x * (1.0 + jnp.tanh(0.7978845608028654 * (x + 0.044715 * (x * x * x))))


def _sigmoid(x):
    return 1.0 / (1.0 + jnp.exp(-x))


def _key_features(pos, lane):
    hi = (pos // POS_SPLIT).astype(F32)
    lo = (pos % POS_SPLIT).astype(F32)
    return jnp.where(lane == FEAT0, hi, jnp.where(lane == FEAT0 + 1, lo,
                     jnp.where((lane == FEAT0 + 2) | (lane == FEAT0 + 3), 1.0, 0.0)))


def _query_features(t, lane, slope):
    hi = (t // POS_SPLIT).astype(F32)
    lo = (t % POS_SPLIT).astype(F32)
    return jnp.where(lane == FEAT0, slope * POS_SPLIT, jnp.where(
        lane == FEAT0 + 1, slope, jnp.where(
            lane == FEAT0 + 2, -slope * POS_SPLIT * hi, jnp.where(
                lane == FEAT0 + 3, -slope * lo, 0.0))))


def _in_proj_kernel(x_ref, g_ref, w_ref, oa_ref, ob_ref, oc_ref, *, t_len):
    x = x_ref[...]
    tm = x.shape[0]
    h = x * lax.rsqrt(jnp.mean(x * x, axis=-1, keepdims=True) + EPS) * g_ref[...]
    hb = h.astype(BF16)
    t = (pl.program_id(0) * tm + lax.broadcasted_iota(jnp.int32, (tm, 1), 0)) % t_len
    col = lax.broadcasted_iota(jnp.int32, (1, Q_EXP), 1)
    slope = jnp.zeros((1, Q_EXP), F32)
    for hd in range(NSA_HEADS):
        slope = jnp.where(col // LANES == hd, 2.0 ** -(hd + 1), slope)
    q = _dot(hb, w_ref[:, :Q_EXP]) * (HEAD_DIM ** -0.5) + _query_features(t, col % LANES, slope)
    oa_ref[:, :Q_EXP] = q.astype(BF16)
    colk = lax.broadcasted_iota(jnp.int32, (1, 2 * K_AUG), 1)
    k = _dot(hb, w_ref[:, Q_EXP:Q_EXP + 2 * K_AUG]) + _key_features(t, colk % LANES)
    oa_ref[:, Q_EXP:Q_EXP + 2 * K_AUG] = k.astype(BF16)
    oa_ref[:, Q_EXP + 2 * K_AUG:] = _dot(hb, w_ref[:, Q_EXP + 2 * K_AUG:NA_COLS]).astype(BF16)
    ob_ref[...] = _dot(hb, w_ref[:, NA_COLS:NA_COLS + NB_COLS])
    oc_ref[...] = _dot(hb, w_ref[:, NA_COLS + NB_COLS:])


def _in_proj(x2d, g, w_pad, t_len, tm=256):
    n = x2d.shape[0]
    ncols = w_pad.shape[1]
    nc = ncols - NA_COLS - NB_COLS
    return pl.pallas_call(
        functools.partial(_in_proj_kernel, t_len=t_len),
        grid=(n // tm,),
        in_specs=[
            pl.BlockSpec((tm, D_MODEL), lambda i: (i, 0)),
            pl.BlockSpec((1, D_MODEL), lambda i: (0, 0)),
            pl.BlockSpec((D_MODEL, ncols), lambda i: (0, 0)),
        ],
        out_specs=[
            pl.BlockSpec((tm, NA_COLS), lambda i: (i, 0)),
            pl.BlockSpec((tm, NB_COLS), lambda i: (i, 0)),
            pl.BlockSpec((tm, nc), lambda i: (i, 0)),
        ],
        out_shape=[
            jax.ShapeDtypeStruct((n, NA_COLS), BF16),
            jax.ShapeDtypeStruct((n, NB_COLS), F32),
            jax.ShapeDtypeStruct((n, nc), F32),
        ],
        compiler_params=_cparams(("parallel",)),
        name="in_proj",
    )(x2d, g, w_pad)


def _pad_w_in(w_in):
    sizes = (D_NSA, D_KV, D_KV, D_KV, D_KV, D_KV, D_KV, 3 * NSA_HEADS, D_RWKV_IN)
    offs = np.cumsum((0,) + sizes)
    q, kc, vc, ks, vs, kw, vw, gl, rw = (w_in[:, offs[i]:offs[i + 1]] for i in range(9))
    def lane_pad(w, groups):
        w = w.reshape(D_MODEL, groups, HEAD_DIM)
        return jnp.pad(w, ((0, 0), (0, 0), (0, LANES - HEAD_DIM))).reshape(D_MODEL, groups * LANES)

    glp = jnp.pad(gl, ((0, 0), (0, LANES - 3 * NSA_HEADS)))
    return jnp.concatenate(
        [lane_pad(q, NSA_HEADS), lane_pad(ks, NSA_KV_HEADS), lane_pad(kw, NSA_KV_HEADS),
         vs, vw, kc, vc, glp, rw], axis=1).astype(BF16)


def _compress_kernel(rk_ref, rv_ref, pek_ref, pev_ref, w1k_ref, w1v_ref, w2k_ref, w2v_ref,
                     ok_ref, ov_ref):
    for r_ref, pe_ref, w1_ref, w2_ref, o_ref in (
            (rk_ref, pek_ref, w1k_ref, w2k_ref, ok_ref),
            (rv_ref, pev_ref, w1v_ref, w2v_ref, ov_ref)):
        rows = r_ref[0]
        nxt = pltpu.roll(rows, rows.shape[0] - 1, axis=0)
        a = (rows + pe_ref[0:1, :]).astype(BF16)
        b = (nxt + pe_ref[1:2, :]).astype(BF16)
        hid = _dot(a, w1_ref[0]) + _dot(b, w1_ref[1])
        out = _dot(_gelu(hid).astype(BF16), w2_ref[...])
        if o_ref is ok_ref:
            nb = rows.shape[0]
            end = lax.broadcasted_iota(jnp.int32, (nb, 1), 0) * CMP_STRIDE + (CMP_LEN - 1)
            lane = lax.broadcasted_iota(jnp.int32, (1, K_AUG), 1) % LANES
            out = out + _key_features(end, lane)
        o_ref[0] = out.astype(BF16)


def _expand_cmp_weights(pe, w1, w2, out_lanes):
    half = CMP_LEN // 2
    eye = jnp.eye(NSA_KV_HEADS, dtype=F32)
    w1r = w1.reshape(2, half, HEAD_DIM, CMP_HIDDEN)
    w1e = w1r[:, :, None, :, None, :] * eye[None, None, :, None, :, None]
    w1e = w1e.reshape(2, half * D_KV, NSA_KV_HEADS * CMP_HIDDEN).astype(BF16)
    pee = jnp.broadcast_to(pe.reshape(2, half, 1, HEAD_DIM), (2, half, NSA_KV_HEADS, HEAD_DIM))
    pee = pee.reshape(2, half * D_KV)
    w2p = jnp.pad(w2, ((0, 0), (0, out_lanes - HEAD_DIM)))
    w2e = (eye[:, None, :, None] * w2p[None, :, None, :]).reshape(
        NSA_KV_HEADS * CMP_HIDDEN, NSA_KV_HEADS * out_lanes).astype(BF16)
    return pee, w1e, w2e


def _compress(rk, rv, wk, wv):
    b, nb, width = rk.shape
    full2 = lambda a: pl.BlockSpec(a.shape, lambda i: (0, 0))
    full3 = lambda a: pl.BlockSpec(a.shape, lambda i: (0, 0, 0))
    row = pl.BlockSpec((1, nb, width), lambda i: (i, 0, 0))
    out = lambda lanes: pl.BlockSpec((1, nb, lanes), lambda i: (i, 0, 0))
    return pl.pallas_call(
        _compress_kernel,
        grid=(b,),
        in_specs=[row, row, full2(wk[0]), full2(wv[0]), full3(wk[1]), full3(wv[1]),
                  full2(wk[2]), full2(wv[2])],
        out_specs=[out(K_AUG), out(D_KV)],
        out_shape=[jax.ShapeDtypeStruct((b, nb, K_AUG), BF16),
                   jax.ShapeDtypeStruct((b, nb, D_KV), BF16)],
        compiler_params=_cparams(("parallel",)),
        name="nsa_compress",
    )(rk, rv, wk[0], wv[0], wk[1], wv[1], wk[2], wv[2])


NSA_TQ = 128
NSA_TK = 256
NSA_TKW = 128


def _masked_softmax(s, mask):
    s = jnp.where(mask, s, NEG_BIG)
    s = s - jnp.max(s, axis=-1, keepdims=True)
    p = jnp.where(mask, jnp.exp(s), 0.0)
    return p / jnp.maximum(jnp.sum(p, axis=-1, keepdims=True), 1e-30)


def _flash_update(carry, s, v):
    m, l, acc = carry
    m_new = jnp.maximum(m, jnp.max(s, axis=-1, keepdims=True))
    alpha = jnp.exp(m - m_new)
    p = jnp.exp(s - m_new)
    l = alpha * l + jnp.sum(p, axis=-1, keepdims=True)
    acc = alpha * acc + _dot(p.astype(BF16), v)
    return m_new, l, acc


def _nsa_kernel(q_ref, ks_ref, kw_ref, vs_ref, vw_ref, ck_ref, cv_ref, ovl_ref, oh_ref, gl_ref,
                eg_ref, o_ref):
    tq = NSA_TQ
    R = NSA_GROUP
    i = pl.program_id(1)
    q0 = i * tq
    t_col = q0 + lax.broadcasted_iota(jnp.int32, (tq, 1), 0)
    t_rows = jnp.concatenate([t_col] * R, axis=0)
    t_lane = q0 + lax.broadcasted_iota(jnp.int32, (1, tq), 1)
    n_cmp = ck_ref.shape[1]
    n_sel = ovl_ref.shape[0]

    gate = _sigmoid(gl_ref[...])
    g_hi, g_lo = _split2(gate)
    gexp = [_dot(g_hi, eg_ref[j]) + _dot(g_lo, eg_ref[j]) for j in range(3)]

    cmp_end = lax.broadcasted_iota(jnp.int32, (1, n_cmp), 1) * CMP_STRIDE + (CMP_LEN - 1)
    mask_c = t_col >= cmp_end
    ids = lax.broadcasted_iota(jnp.int32, (n_sel, 1), 0)
    idsf = ids.astype(F32)
    cur = t_lane // SEL_LEN
    forced = (ids == 0) | (ids == cur) | (ids == cur - 1)
    valid = ids * SEL_LEN <= t_lane

    for g in range(NSA_KV_HEADS):
        heads = [g * R + r for r in range(R)]
        lanes_g = slice(g * LANES, (g + 1) * LANES)
        qh = [q_ref[:, h * LANES:(h + 1) * LANES] for h in heads]
        ck = ck_ref[0, :, lanes_g]
        cv = cv_ref[0]

        o_cmp = []
        imp = jnp.zeros((n_sel, tq), F32)
        for r in range(R):
            p = _masked_softmax(_dot_nt(qh[r], ck), mask_c)
            o_cmp.append(_dot(p.astype(BF16), cv))
            p_hi, p_lo = _split2(p)
            imp = imp + _dot_nt(ovl_ref[...], p_hi) + _dot_nt(ovl_ref[...], p_lo)
        imp = jnp.where(forced, 1e6, jnp.where(valid, imp, -1.0))

        sel = jnp.zeros((n_sel, tq), F32)
        for _ in range(min(SEL_TOPN, n_sel)):
            m = jnp.max(imp, axis=0, keepdims=True)
            first = jnp.min(jnp.where(imp == m, idsf, float(n_sel)), axis=0, keepdims=True)
            hit = idsf == first
            sel = jnp.where(hit, 1.0, sel)
            imp = jnp.where(hit, -3e38, imp)
        bias_t = jnp.where(sel > 0.5, 0.0, NEG_BIG)
        if n_sel < LANES:
            bias_t = jnp.concatenate([bias_t, jnp.zeros((LANES - n_sel, tq), F32)], axis=0)
        bias = bias_t.T.astype(BF16)

        q_st = jnp.concatenate(qh, axis=0)
        q_slc = jnp.concatenate([q_st, jnp.concatenate([bias] * R, axis=0)], axis=1)
        init = (jnp.full((R * tq, 1), -3e38, F32), jnp.zeros((R * tq, 1), F32),
                jnp.zeros((R * tq, LANES), F32))

        def slc_tile(j, carry, diagonal):
            k0 = pl.multiple_of(j * NSA_TK, NSA_TK)
            k = jnp.concatenate([ks_ref[pl.ds(k0, NSA_TK), lanes_g],
                                 oh_ref[pl.ds(k0, NSA_TK), :]], axis=1)
            s = _dot_nt(q_slc, k)
            if diagonal:
                pos = k0 + lax.broadcasted_iota(jnp.int32, (1, NSA_TK), 1)
                s = jnp.where(t_rows >= pos, s, NEG_BIG)
            return _flash_update(carry, s, vs_ref[pl.ds(k0, NSA_TK), :])

        n_off = q0 // NSA_TK
        carry = lax.fori_loop(0, n_off, functools.partial(slc_tile, diagonal=False), init)
        _, l_s, acc_s = slc_tile(n_off, carry, True)
        o_slc = acc_s / l_s

        def win_tile(j, carry, mode):
            k0 = pl.multiple_of(j * NSA_TKW, NSA_TKW)
            s = _dot_nt(q_st, kw_ref[pl.ds(k0, NSA_TKW), lanes_g])
            pos = k0 + lax.broadcasted_iota(jnp.int32, (1, NSA_TKW), 1)
            if mode == "first":
                s = jnp.where(pos > t_rows - WINDOW, s, NEG_BIG)
            elif mode == "diagonal":
                s = jnp.where(t_rows >= pos, s, NEG_BIG)
            return _flash_update(carry, s, vw_ref[pl.ds(k0, NSA_TKW), :])

        nw = WINDOW // NSA_TKW
        first_lo = jnp.maximum(i - nw, 0)
        first_hi = jnp.where(i >= nw, first_lo + 1, first_lo)
        carry = lax.fori_loop(first_lo, first_hi, functools.partial(win_tile, mode="first"), init)
        carry = lax.fori_loop(jnp.maximum(i - nw + 1, 0), i,
                              functools.partial(win_tile, mode="full"), carry)
        _, l_w, acc_w = win_tile(i, carry, "diagonal")
        o_win = acc_w / l_w

        for r, h in enumerate(heads):
            cols = slice(h * LANES, (h + 1) * LANES)
            rows = slice(r * tq, (r + 1) * tq)
            out = (gexp[0][:, cols] * o_cmp[r] + gexp[1][:, cols] * o_slc[rows]
                   + gexp[2][:, cols] * o_win[rows])
            o_ref[:, cols] = out.astype(BF16)


def _nsa_consts(t_len, n_cmp_pad):
    n_cmp = (t_len - CMP_LEN) // CMP_STRIDE + 1
    n_sel = t_len // SEL_LEN
    cmp_start = np.arange(n_cmp) * CMP_STRIDE
    sel_start = np.arange(n_sel) * SEL_LEN
    ovl = np.clip(np.minimum(cmp_start[:, None] + CMP_LEN, sel_start[None, :] + SEL_LEN)
                  - np.maximum(cmp_start[:, None], sel_start[None, :]), 0, None) / CMP_LEN
    ovl_t = np.zeros((n_sel, n_cmp_pad), np.float32)
    ovl_t[:, :n_cmp] = ovl.T
    onehot = np.zeros((t_len, LANES), np.float32)
    onehot[np.arange(t_len), np.arange(t_len) // SEL_LEN] = 1.0
    eg = np.zeros((3, LANES, Q_EXP), np.float32)
    for h in range(NSA_HEADS):
        for j in range(3):
            eg[j, h * 3 + j, h * LANES:(h + 1) * LANES] = 1.0
    return jnp.asarray(ovl_t, BF16), jnp.asarray(onehot, BF16), jnp.asarray(eg, BF16)


def _nsa_attention(oa, ob, cmp_k, cmp_v, b, t_len):
    assert NSA_TQ == NSA_TKW and NSA_TK % NSA_TQ == 0 and t_len // SEL_LEN <= LANES
    n_cmp_pad = cmp_k.shape[1]
    ovl_t, onehot, eg = _nsa_consts(t_len, n_cmp_pad)
    nq = t_len // NSA_TQ
    k_col0 = Q_EXP // K_AUG
    v_col0 = (Q_EXP + 2 * K_AUG) // D_KV
    kspec = lambda c: pl.BlockSpec((t_len, K_AUG), lambda bi, i: (bi, k_col0 + c))
    vspec = lambda c: pl.BlockSpec((t_len, D_KV), lambda bi, i: (bi, v_col0 + c))
    const2 = lambda a: pl.BlockSpec(a.shape, lambda bi, i: (0, 0))
    return pl.pallas_call(
        _nsa_kernel,
        grid=(b, nq),
        in_specs=[
            pl.BlockSpec((NSA_TQ, Q_EXP), lambda bi, i: (bi * nq + i, 0)),
            kspec(0), kspec(1), vspec(0), vspec(1),
            pl.BlockSpec((1, n_cmp_pad, K_AUG), lambda bi, i: (bi, 0, 0)),
            pl.BlockSpec((1, n_cmp_pad, D_KV), lambda bi, i: (bi, 0, 0)),
            const2(ovl_t), const2(onehot),
            pl.BlockSpec((NSA_TQ, LANES), lambda bi, i: (bi * nq + i, 2)),
            pl.BlockSpec(eg.shape, lambda bi, i: (0, 0, 0)),
        ],
        out_specs=pl.BlockSpec((NSA_TQ, Q_EXP), lambda bi, i: (bi * nq + i, 0)),
        out_shape=jax.ShapeDtypeStruct((b * t_len, Q_EXP), BF16),
        compiler_params=_cparams(("parallel", "arbitrary")),
        name="nsa_attention",
    )(oa, oa, oa, oa, oa, cmp_k, cmp_v, ovl_t, onehot, ob, eg)


def _dot_x3(x, e):
    hi = x.astype(BF16)
    r1 = x - hi.astype(F32)
    mid = r1.astype(BF16)
    lo = (r1 - mid.astype(F32)).astype(BF16)
    return _dot(hi, e) + _dot(mid, e) + _dot(lo, e)


def _head_sum_matrix():
    ids = np.arange(D_RWKV) // HEAD_DIM
    return jnp.asarray(ids[:, None] == ids[None, :], BF16)


def _rwkv_pre_kernel(p_ref, hp_ref, mu_ref, w0_ref, w2_ref, a0_ref, a2_ref, g2_ref, kk_ref,
                     ka_ref, bd_ref, r_o, lw_o, k_o, v_o, kkn_o, a_o, g_o, *, tiles_per_seq):
    i = pl.program_id(0)
    p = p_ref[...]
    tm = p.shape[0]
    keep = jnp.where(i % tiles_per_seq == 0, 0.0, 1.0)
    halo = hp_ref[SUBLANES - 1:SUBLANES, :] * keep
    prev = pltpu.roll(p, 1, axis=0)
    row0 = lax.broadcasted_iota(jnp.int32, (tm, 1), 0) == 0
    prev = jnp.where(row0, halo, prev)
    ps = p + (prev - p) * mu_ref[...]
    d = D_RWKV
    r = ps[:, 0:d]
    k = ps[:, d:2 * d]
    v = ps[:, 2 * d:3 * d]
    xw = ps[:, 3 * d:3 * d + RANK_W]
    xa = ps[:, 3 * d + RANK_W:3 * d + RANK_W + RANK_A]
    xg = ps[:, 3 * d + RANK_W + RANK_A:]
    z = -(w0_ref[...] + _dot(jnp.tanh(xw).astype(BF16), w2_ref[...]))
    softplus = jnp.maximum(z, 0.0) + jnp.log(1.0 + jnp.exp(-jnp.abs(z)))
    w = -softplus - 0.5
    a = _sigmoid(a0_ref[...] + _dot(xa.astype(BF16), a2_ref[...]))
    g = _dot(_sigmoid(xg).astype(BF16), g2_ref[...])
    kk = k * kk_ref[...]
    ss = _dot_x3(kk * kk, bd_ref[...])
    kk = kk / jnp.maximum(jnp.sqrt(ss), 1e-12)
    r_o[...] = r
    lw_o[...] = -jnp.exp(w)
    k_o[...] = k * (1.0 + (a - 1.0) * ka_ref[...])
    v_o[...] = v
    kkn_o[...] = kk
    a_o[...] = a
    g_o[...] = g


def _rwkv_pre(oc, t_len, mu, w0, w2, a0, a2, g2, k_k, k_a, bd, tm=256):
    n = oc.shape[0]
    tiles_per_seq = t_len // tm
    halo_blocks = tm // SUBLANES
    full = lambda a: pl.BlockSpec(a.shape, lambda i: (0, 0))
    outs = pl.BlockSpec((tm, D_RWKV), lambda i: (i, 0))
    params = (mu, w0, w2, a0, a2, g2, k_k, k_a, bd)
    return pl.pallas_call(
        functools.partial(_rwkv_pre_kernel, tiles_per_seq=tiles_per_seq),
        grid=(n // tm,),
        in_specs=[
            pl.BlockSpec((tm, D_RWKV_IN), lambda i: (i, 0)),
            pl.BlockSpec((SUBLANES, D_RWKV_IN),
                         lambda i: (jnp.maximum(i * halo_blocks - 1, 0), 0)),
        ] + [full(a) for a in params],
        out_specs=[outs] * 7,
        out_shape=[jax.ShapeDtypeStruct((n, D_RWKV), F32)] * 7,
        compiler_params=_cparams(("parallel",)),
        name="rwkv_pre",
    )(oc, oc, *params)


RWKV_C = 64
RWKV_CB = 128
RWKV_PASSES = 1


def _bmm(eq, a, b):
    ein = lambda x, y: jnp.einsum(eq, x, y, preferred_element_type=F32)
    if RWKV_PASSES == 1:
        return ein(a.astype(BF16), b.astype(BF16))
    ah, al = _split2(a)
    bh, bl = _split2(b)
    return ein(ah, bh) + ein(ah, bl) + ein(al, bh)


def _cumsum_rows(x2d, seg):
    rows = lax.broadcasted_iota(jnp.int32, (x2d.shape[0], 1), 0) % seg
    step = 1
    while step < seg:
        shifted = pltpu.roll(x2d, step, axis=0)
        x2d = x2d + jnp.where(rows >= step, shifted, 0.0)
        step *= 2
    return x2d


def _rwkv_rec_kernel(r_ref, lw_ref, k_ref, kk_ref, a_ref, vt_ref, ot_ref, s_ref):
    H = r_ref.shape[1]
    C = RWKV_C

    @pl.when(pl.program_id(1) == 0)
    def _():
        s_ref[...] = jnp.zeros_like(s_ref)

    ri = lax.broadcasted_iota(jnp.int32, (C, C), 0)
    ci = lax.broadcasted_iota(jnp.int32, (C, C), 1)
    strict = (ri > ci)[None]
    incl = (ri >= ci)[None]
    eye = (ri == ci).astype(F32)[None]

    for sub in range(RWKV_CB // C):
        rows = slice(sub * C, (sub + 1) * C)
        r = r_ref[0, :, rows, :]
        lw = lw_ref[0, :, rows, :]
        k = k_ref[0, :, rows, :]
        kk = kk_ref[0, :, rows, :]
        a = a_ref[0, :, rows, :]
        vt = vt_ref[0, :, :, rows]
        s0 = s_ref[...]

        cum = _cumsum_rows(lw.reshape(H * C, HEAD_DIM), C).reshape(H, C, HEAD_DIM)
        cum_last = cum[:, C - 1:C, :]
        p_inv = jnp.exp(-cum)
        w_last = jnp.exp(cum_last - cum)
        bm = kk * a
        at = -kk * jnp.exp(cum - lw)
        rt = r * jnp.exp(cum)
        bt = bm * p_inv
        kt = k * p_inv

        nt = 'hik,hjk->hij'
        m_ab = jnp.where(strict, _bmm(nt, at, bt), 0.0)
        m_ak = jnp.where(strict, _bmm(nt, at, kt), 0.0)
        n_rb = jnp.where(incl, _bmm(nt, rt, bt), 0.0)
        n_rk = jnp.where(incl, _bmm(nt, rt, kt), 0.0)

        tinv = eye + m_ab
        mp = m_ab
        step = 1
        while 2 * step < C:
            mp = _bmm('hij,hjk->hik', mp, mp)
            tinv = tinv + _bmm('hij,hjk->hik', tinv, mp)
            step *= 2

        rhs_t = _bmm('hvk,hik->hvi', s0, at) + _bmm('hvj,hij->hvi', vt, m_ak)
        ut = _bmm('hvj,hij->hvi', rhs_t, tinv)
        ot = (_bmm('hvk,hik->hvi', s0, rt) + _bmm('hvj,hij->hvi', ut, n_rb)
              + _bmm('hvj,hij->hvi', vt, n_rk))
        ot_ref[0, :, :, rows] = ot
        s_ref[...] = (s0 * jnp.exp(cum_last) + _bmm('hvj,hjk->hvk', ut, bm * w_last)
                      + _bmm('hvj,hjk->hvk', vt, k * w_last))


def _rwkv_recurrence(r, lw, k, kk, a, vt):
    b, h, t_len, d = r.shape
    tok = pl.BlockSpec((1, h, RWKV_CB, d), lambda bi, c: (bi, 0, c, 0))
    tr = pl.BlockSpec((1, h, d, RWKV_CB), lambda bi, c: (bi, 0, 0, c))
    return pl.pallas_call(
        _rwkv_rec_kernel,
        grid=(b, t_len // RWKV_CB),
        in_specs=[tok] * 5 + [tr],
        out_specs=tr,
        out_shape=jax.ShapeDtypeStruct((b, h, d, t_len), F32),
        scratch_shapes=[pltpu.VMEM((h, d, d), F32)],
        compiler_params=_cparams(("parallel", "arbitrary")),
        name="rwkv_recurrence",
    )(r, lw, k, kk, a, vt)


def _rwkv_post_kernel(o_ref, r_ref, k_ref, v_ref, g_ref, lnw_ref, lnb_ref, rk_ref, bd_ref, y_ref):
    o = o_ref[...]
    bd = bd_ref[...]
    inv = 1.0 / HEAD_DIM
    mean = _dot_x3(o, bd) * inv
    d = o - mean
    var = _dot_x3(d * d, bd) * inv
    on = d * lax.rsqrt(var + GN_EPS) * lnw_ref[...] + lnb_ref[...]
    bonus = _dot_x3(r_ref[...] * k_ref[...] * rk_ref[...], bd)
    y_ref[...] = ((on + bonus * v_ref[...]) * g_ref[...]).astype(BF16)


def _rwkv_post(o, r, k, v, g, ln_w, ln_b, r_k, bd, tm=512):
    n = o.shape[0]
    tile = pl.BlockSpec((tm, D_RWKV), lambda i: (i, 0))
    full = lambda a: pl.BlockSpec(a.shape, lambda i: (0, 0))
    return pl.pallas_call(
        _rwkv_post_kernel,
        grid=(n // tm,),
        in_specs=[tile] * 5 + [full(ln_w), full(ln_b), full(r_k), full(bd)],
        out_specs=tile,
        out_shape=jax.ShapeDtypeStruct((n, D_RWKV), BF16),
        compiler_params=_cparams(("parallel",)),
        name="rwkv_post",
    )(o, r, k, v, g, ln_w, ln_b, r_k, bd)


def _rwkv_mixer(oc, b, t_len, mu, w0, w2, a0, a2, g2, k_k, k_a, r_k, ln_w, ln_b):
    bd = _head_sum_matrix()
    row = lambda z: z.reshape(1, -1)
    r, lw, k, v, kk, a, g = _rwkv_pre(
        oc, t_len, row(mu), row(w0), w2.astype(BF16), row(a0), a2.astype(BF16),
        g2.astype(BF16), row(k_k), row(k_a), bd)
    hm = lambda z: z.reshape(b, t_len, RWKV_HEADS, HEAD_DIM).transpose(0, 2, 1, 3)
    vt = v.reshape(b, t_len, RWKV_HEADS, HEAD_DIM).transpose(0, 2, 3, 1)
    ot = _rwkv_recurrence(hm(r), hm(lw), hm(k), hm(kk), hm(a), vt)
    o = ot.transpose(0, 3, 1, 2).reshape(b * t_len, D_RWKV)
    return _rwkv_post(o, r, k, v, g, row(ln_w), row(ln_b), row(r_k), bd)


def _out_proj_kernel(x_ref, yn_ref, yr_ref, wn_ref, wr_ref, g_ref, x1_ref, h2_ref):
    x1 = x_ref[...] + _dot(yn_ref[...], wn_ref[...]) + _dot(yr_ref[...], wr_ref[...])
    x1_ref[...] = x1
    h2_ref[...] = x1 * lax.rsqrt(jnp.mean(x1 * x1, axis=-1, keepdims=True) + EPS) * g_ref[...]


def _expand_w_out(w_out):
    wn = w_out[:D_NSA].reshape(NSA_HEADS, 1, HEAD_DIM, D_MODEL)
    onehot = (np.arange(NSA_HEADS)[:, None] // NSA_GROUP == np.arange(NSA_KV_HEADS)[None, :])
    wn = wn * jnp.asarray(onehot, F32)[:, :, None, None]
    return wn.reshape(Q_EXP, D_MODEL).astype(BF16), w_out[D_NSA:].astype(BF16)


def _out_proj(x2d, y_nsa, y_rwkv, wn, wr, g, tm=512):
    n = x2d.shape[0]
    tile = lambda a: pl.BlockSpec((tm, a.shape[1]), lambda i: (i, 0))
    full = lambda a: pl.BlockSpec(a.shape, lambda i: (0, 0))
    return pl.pallas_call(
        _out_proj_kernel,
        grid=(n // tm,),
        in_specs=[tile(x2d), tile(y_nsa), tile(y_rwkv), full(wn), full(wr), full(g)],
        out_specs=[tile(x2d), tile(x2d)],
        out_shape=[jax.ShapeDtypeStruct((n, D_MODEL), F32)] * 2,
        compiler_params=_cparams(("parallel",)),
        name="out_proj",
    )(x2d, y_nsa, y_rwkv, wn, wr, g)


def _topk_rows(s, k, rid):
    vals, ids = [], []
    for _ in range(k):
        m = jnp.max(s, axis=0, keepdims=True)
        first = jnp.min(jnp.where(s == m, rid, jnp.inf), axis=0, keepdims=True)
        vals.append(m)
        ids.append(first)
        s = jnp.where(rid == first, -jnp.inf, s)
    return jnp.concatenate(vals, axis=0), jnp.concatenate(ids, axis=0)


def _take_rows(table, idx):
    rows = lax.broadcasted_iota(jnp.int32, (table.shape[0], 1), 0).astype(F32)
    out = [jnp.sum(jnp.where(rows == idx[r:r + 1], table, 0.0), axis=0, keepdims=True)
           for r in range(idx.shape[0])]
    return jnp.concatenate(out, axis=0)


def _pair_candidates():
    k = PEER_TOPK
    flat = [0 * k + j for j in range(k)]
    for i in range(1, SUBLANES):
        flat += [i * k + j for j in range(SUBLANES)]
    flat += [i * k for i in range(SUBLANES, k)]
    return np.asarray(flat, np.float32).reshape(-1, 1)


def _peer_route_kernel(h_ref, wq_ref, sk_ref, flat_ref, e_ref, g_ref):
    K = PEER_TOPK
    half = PEER_D_QUERY // 2
    q = _dot(h_ref[...].astype(BF16), wq_ref[...]).astype(BF16)
    key_ids = lax.broadcasted_iota(jnp.int32, (PEER_N_KEYS, 1), 0).astype(F32)
    flat = flat_ref[...]
    for h in range(PEER_HEADS):
        top = []
        for c in range(2):
            col = (h * 2 + c) * half
            s_t = _dot_nt(sk_ref[h, c], q[:, col:col + half])
            top.append(_topk_rows(s_t, K, key_ids))
        (v0, i0), (v1, i1) = top
        cand = jnp.concatenate(
            [v0[0:1] + v1]
            + [v0[i:i + 1] + v1[:SUBLANES] for i in range(1, SUBLANES)]
            + [v0[SUBLANES:] + v1[0:1]], axis=0)
        best, pair = _topk_rows(cand, K, flat)
        pi = jnp.floor(pair * (1.0 / K))
        pj = pair - pi * K
        experts = _take_rows(i0, pi) * float(PEER_N_KEYS) + _take_rows(i1, pj)
        p = jnp.exp(best - jnp.max(best, axis=0, keepdims=True))
        e_ref[h * K:(h + 1) * K, :] = experts.astype(jnp.int32)
        g_ref[h * K:(h + 1) * K, :] = p / jnp.sum(p, axis=0, keepdims=True)


def _peer_route(h2, wq, sk, tm=256):
    n = h2.shape[0]
    hk = PEER_HEADS * PEER_TOPK
    flat = jnp.asarray(_pair_candidates())
    out = pl.BlockSpec((hk, tm), lambda i: (0, i))
    return pl.pallas_call(
        _peer_route_kernel,
        grid=(n // tm,),
        in_specs=[
            pl.BlockSpec((tm, D_MODEL), lambda i: (i, 0)),
            pl.BlockSpec(wq.shape, lambda i: (0, 0)),
            pl.BlockSpec(sk.shape, lambda i: (0, 0, 0, 0)),
            pl.BlockSpec(flat.shape, lambda i: (0, 0)),
        ],
        out_specs=[out, out],
        out_shape=[jax.ShapeDtypeStruct((hk, n), jnp.int32), jax.ShapeDtypeStruct((hk, n), F32)],
        compiler_params=_cparams(("parallel",)),
        name="peer_route",
    )(h2, wq, sk, flat)


PEER_TT = 64
HALF_ROWS = SUBLANES // 2
SUB_ORDER = (0, 4, 2, 6, 1, 5, 3, 7)
HIGH_MASK = 0xFFFF0000


def _pack_table(w):
    bits = lax.bitcast_convert_type(w.astype(BF16), jnp.uint16).astype(jnp.uint32)
    half = w.shape[1] // 2
    packed = bits[:, :half] | (bits[:, half:] << 16)
    return packed.reshape(w.shape[0] * HALF_ROWS, LANES)


def _table_row(tbl_ref, row0):
    return tbl_ref[pl.ds(pl.multiple_of(row0, HALF_ROWS), HALF_ROWS), :]


def _unpack_words(word):
    lo = pltpu.bitcast(word << 16, F32)
    hi = pltpu.bitcast(word & jnp.uint32(HIGH_MASK), F32)
    return lo, hi


def _sublane_tree(c):
    sub = lax.broadcasted_iota(jnp.int32, (SUBLANES, LANES), 0)
    m2 = (sub % 4) < 2
    d = []
    for x, y in zip(c[0::2], c[1::2]):
        d.append(jnp.where(m2, x + pltpu.roll(x, 6, axis=0), y + pltpu.roll(y, 2, axis=0)))
    m1 = (sub % 2) == 0
    x, y = d
    return jnp.where(m1, x + pltpu.roll(x, 7, axis=0), y + pltpu.roll(y, 1, axis=0))


def _peer_act_kernel(e_ref, h_ref, tbl_ref, gate_ref, ones_ref, w_ref, part_ref):
    hk = PEER_HEADS * PEER_TOPK
    tt = h_ref.shape[0]

    def token(t, carry):
        hrow = h_ref[t]
        h_lo = jnp.concatenate([hrow[:HALF_ROWS]] * 2, axis=0)
        h_hi = jnp.concatenate([hrow[HALF_ROWS:]] * 2, axis=0)
        for m in range(hk // SUBLANES):
            es = [e_ref[t, m * SUBLANES + j] for j in SUB_ORDER]
            c = []
            for ea, eb in zip(es[0::2], es[1::2]):
                lo, hi = _unpack_words(jnp.concatenate(
                    [_table_row(tbl_ref, ea), _table_row(tbl_ref, eb)], axis=0))
                c.append(lo * h_lo + hi * h_hi)
            row0 = pl.multiple_of((t * (hk // SUBLANES) + m) * SUBLANES, SUBLANES)
            part_ref[pl.ds(row0, SUBLANES), :] = _sublane_tree(c)
        return carry

    lax.fori_loop(0, tt, token, 0)
    sums = _dot_x2(part_ref[...], ones_ref[...]).reshape(tt, hk, LANES)
    eye = (lax.broadcasted_iota(jnp.int32, (hk, LANES), 0)
           == lax.broadcasted_iota(jnp.int32, (hk, LANES), 1))
    act = jnp.sum(jnp.where(eye[None], sums, 0.0), axis=1)
    w_ref[...] = _gelu(act) * gate_ref[...]


def _table_spec(tbl):
    return pl.BlockSpec(tbl.shape, lambda i: (0, 0), pipeline_mode=pl.Buffered(1))


def _peer_act(experts, h2_tiles, tbl, gates):
    n, hk = experts.shape
    ones = jnp.ones((LANES, LANES), BF16)
    return pl.pallas_call(
        _peer_act_kernel,
        grid=(n // PEER_TT,),
        in_specs=[
            pl.BlockSpec((PEER_TT, hk), lambda i: (i, 0), memory_space=pltpu.SMEM),
            pl.BlockSpec((PEER_TT, SUBLANES, LANES), lambda i: (i, 0, 0)),
            _table_spec(tbl),
            pl.BlockSpec((PEER_TT, hk), lambda i: (i, 0)),
            pl.BlockSpec(ones.shape, lambda i: (0, 0)),
        ],
        out_specs=pl.BlockSpec((PEER_TT, hk), lambda i: (i, 0)),
        out_shape=jax.ShapeDtypeStruct((n, hk), F32),
        scratch_shapes=[pltpu.VMEM((PEER_TT * hk, LANES), F32)],
        compiler_params=_cparams(("arbitrary",)),
        name="peer_act",
    )(experts, h2_tiles, tbl, gates, ones)


PEER_ACCS = 4
PEER_GROUP = 64


def _peer_out_kernel(e_ref, w_ref, x_ref, tbl_ref, o_ref):
    hk = PEER_HEADS * PEER_TOPK
    tt = x_ref.shape[0]

    def token(t, carry):
        def group(gi, accs):
            acc_lo, acc_hi = list(accs[0]), list(accs[1])
            base = t * hk + gi * PEER_GROUP
            for j in range(PEER_GROUP):
                lo, hi = _unpack_words(_table_row(tbl_ref, e_ref[base + j]))
                wk = w_ref[base + j]
                acc_lo[j % PEER_ACCS] = acc_lo[j % PEER_ACCS] + wk * lo
                acc_hi[j % PEER_ACCS] = acc_hi[j % PEER_ACCS] + wk * hi
            return tuple(acc_lo), tuple(acc_hi)

        zeros = tuple(jnp.zeros((HALF_ROWS, LANES), F32) for _ in range(PEER_ACCS))
        acc_lo, acc_hi = lax.fori_loop(0, hk // PEER_GROUP, group, (zeros, zeros))
        tree = lambda a: (a[0] + a[1]) + (a[2] + a[3])
        o_ref[t] = x_ref[t] + jnp.concatenate([tree(acc_lo), tree(acc_hi)], axis=0)
        return carry

    lax.fori_loop(0, tt, token, 0)


def _peer_out(experts, w, x1_tiles, tbl):
    n, hk = experts.shape
    smem = pl.BlockSpec((PEER_TT * hk,), lambda i: (i,), memory_space=pltpu.SMEM)
    tile = pl.BlockSpec((PEER_TT, SUBLANES, LANES), lambda i: (i, 0, 0))
    return pl.pallas_call(
        _peer_out_kernel,
        grid=(n // PEER_TT,),
        in_specs=[smem, smem, tile, _table_spec(tbl)],
        out_specs=tile,
        out_shape=jax.ShapeDtypeStruct(x1_tiles.shape, F32),
        compiler_params=_cparams(("arbitrary",)),
        name="peer_out",
    )(experts.reshape(-1), w.reshape(-1), x1_tiles, tbl)


def _peer_ffn_residual(x1, h2, wq, sk, tbl_u, tbl_v):
    n = x1.shape[0]
    e_t, g_t = _peer_route(h2, wq, sk)
    experts, gates = e_t.T * HALF_ROWS, g_t.T
    tiles = lambda z: z.reshape(n, SUBLANES, LANES)
    w = _peer_act(experts, tiles(h2), tbl_u, gates)
    return _peer_out(experts, w, tiles(x1), tbl_v).reshape(n, D_MODEL)


def _rms_kernel(x_ref, g_ref, o_ref):
    x = x_ref[...]
    o_ref[...] = x * lax.rsqrt(jnp.mean(x * x, axis=-1, keepdims=True) + EPS) * g_ref[...]


def _final_norm(x2d, g, tm=512):
    n = x2d.shape[0]
    tile = pl.BlockSpec((tm, D_MODEL), lambda i: (i, 0))
    return pl.pallas_call(
        _rms_kernel,
        grid=(n // tm,),
        in_specs=[tile, pl.BlockSpec(g.shape, lambda i: (0, 0))],
        out_specs=tile,
        out_shape=jax.ShapeDtypeStruct((n, D_MODEL), F32),
        compiler_params=_cparams(("parallel",)),
        name="final_norm",
    )(x2d, g)


def kernel(x, norm_mix_g, w_in, cmp_k_pe, cmp_k_w1, cmp_k_w2, cmp_v_pe, cmp_v_w1, cmp_v_w2,
           rwkv_mu, rwkv_w0, rwkv_w2, rwkv_a0, rwkv_a2, rwkv_g2, rwkv_k_k, rwkv_k_a, rwkv_r_k,
           rwkv_ln_w, rwkv_ln_b, w_out, norm_ffn_g, peer_w_q, peer_sub_keys, peer_u, peer_v,
           norm_final_g):
    b, t_len, _ = x.shape
    n = b * t_len
    row = lambda z: z.reshape(1, -1)
    x2d = x.reshape(n, D_MODEL)
    for l in range(w_in.shape[0]):
        oa, ob, oc = _in_proj(x2d, row(norm_mix_g[l]), _pad_w_in(w_in[l]), t_len)
        nb = t_len // CMP_STRIDE
        rk = ob[:, 0:D_KV].reshape(b, nb, CMP_STRIDE * D_KV)
        rv = ob[:, D_KV:2 * D_KV].reshape(b, nb, CMP_STRIDE * D_KV)
        cmp_k, cmp_v = _compress(
            rk, rv, _expand_cmp_weights(cmp_k_pe[l], cmp_k_w1[l], cmp_k_w2[l], LANES),
            _expand_cmp_weights(cmp_v_pe[l], cmp_v_w1[l], cmp_v_w2[l], HEAD_DIM))
        y_nsa = _nsa_attention(oa, ob, cmp_k, cmp_v, b, t_len)
        y_rwkv = _rwkv_mixer(oc, b, t_len, rwkv_mu[l], rwkv_w0[l], rwkv_w2[l], rwkv_a0[l],
                             rwkv_a2[l], rwkv_g2[l], rwkv_k_k[l], rwkv_k_a[l], rwkv_r_k[l],
                             rwkv_ln_w[l], rwkv_ln_b[l])
        wn, wr = _expand_w_out(w_out[l])
        x1, h2 = _out_proj(x2d, y_nsa, y_rwkv, wn, wr, row(norm_ffn_g[l]))
        x2d = _peer_ffn_residual(x1, h2, peer_w_q[l].astype(BF16), peer_sub_keys[l].astype(BF16),
                                 _pack_table(peer_u[l]), _pack_table(peer_v[l]))
    return _final_norm(x2d, row(norm_final_g)).reshape(b, t_len, D_MODEL)
```

```python
import functools

import numpy as np
import jax
import jax.numpy as jnp
from jax import lax
from jax.experimental import pallas as pl
from jax.experimental.pallas import tpu as pltpu

F32 = jnp.float32
BF16 = jnp.bfloat16

D_MODEL = 1024
HEAD_DIM = 64
NSA_HEADS = 8
NSA_KV_HEADS = 2
NSA_GROUP = NSA_HEADS // NSA_KV_HEADS
RWKV_HEADS = 8
D_NSA = NSA_HEADS * HEAD_DIM
D_RWKV = RWKV_HEADS * HEAD_DIM
D_KV = NSA_KV_HEADS * HEAD_DIM
CMP_LEN = 32
CMP_STRIDE = 16
CMP_HIDDEN = 128
SEL_LEN = 64
SEL_TOPN = 16
WINDOW = 512
RANK_W = 64
RANK_A = 64
RANK_G = 128
D_RWKV_IN = 3 * D_RWKV + RANK_W + RANK_A + RANK_G
PEER_HEADS = 8
PEER_N_KEYS = 128
PEER_D_QUERY = 256
PEER_TOPK = 16
EPS = 1e-6
GN_EPS = 64e-5

LANES = 128
SUBLANES = 8
VMEM_LIMIT = 56 * 1024 * 1024

NEG_BIG = -1e30
Q_EXP = NSA_HEADS * LANES
K_AUG = NSA_KV_HEADS * LANES
NA_COLS = Q_EXP + 2 * K_AUG + 2 * D_KV
NB_COLS = 3 * LANES
FEAT0 = HEAD_DIM
POS_SPLIT = 64


def _cparams(sem):
    return pltpu.CompilerParams(dimension_semantics=sem, vmem_limit_bytes=VMEM_LIMIT)


def _dot(a, b):
    return jnp.dot(a, b, preferred_element_type=F32)


def _dot_nt(a, b):
    return lax.dot_general(a, b, (((1,), (1,)), ((), ())), preferred_element_type=F32)


def _split2(x):
    hi = x.astype(BF16)
    lo = (x - hi.astype(F32)).astype(BF16)
    return hi, lo


def _dot_x2(x, e):
    hi, lo = _split2(x)
    return _dot(hi, e) + _dot(lo, e)


def _gelu(x):
    return 0.5 * x * (1.0 + jnp.tanh(0.7978845608028654 * (x + 0.044715 * (x * x * x))))


def _sigmoid(x):
    return 1.0 / (1.0 + jnp.exp(-x))


def _key_features(pos, lane):
    hi = (pos // POS_SPLIT).astype(F32)
    lo = (pos % POS_SPLIT).astype(F32)
    return jnp.where(lane == FEAT0, hi, jnp.where(lane == FEAT0 + 1, lo,
                     jnp.where((lane == FEAT0 + 2) | (lane == FEAT0 + 3), 1.0, 0.0)))


def _query_features(t, lane, slope):
    hi = (t // POS_SPLIT).astype(F32)
    lo = (t % POS_SPLIT).astype(F32)
    return jnp.where(lane == FEAT0, slope * POS_SPLIT, jnp.where(
        lane == FEAT0 + 1, slope, jnp.where(
            lane == FEAT0 + 2, -slope * POS_SPLIT * hi, jnp.where(
                lane == FEAT0 + 3, -slope * lo, 0.0))))


def _in_proj_kernel(x_ref, g_ref, w_ref, oa_ref, ob_ref, oc_ref, *, t_len):
    x = x_ref[...]
    tm = x.shape[0]
    h = x * lax.rsqrt(jnp.mean(x * x, axis=-1, keepdims=True) + EPS) * g_ref[...]
    hb = h.astype(BF16)
    t = (pl.program_id(0) * tm + lax.broadcasted_iota(jnp.int32, (tm, 1), 0)) % t_len
    col = lax.broadcasted_iota(jnp.int32, (1, Q_EXP), 1)
    slope = jnp.zeros((1, Q_EXP), F32)
    for hd in range(NSA_HEADS):
        slope = jnp.where(col // LANES == hd, 2.0 ** -(hd + 1), slope)
    q = _dot(hb, w_ref[:, :Q_EXP]) * (HEAD_DIM ** -0.5) + _query_features(t, col % LANES, slope)
    oa_ref[:, :Q_EXP] = q.astype(BF16)
    colk = lax.broadcasted_iota(jnp.int32, (1, 2 * K_AUG), 1)
    k = _dot(hb, w_ref[:, Q_EXP:Q_EXP + 2 * K_AUG]) + _key_features(t, colk % LANES)
    oa_ref[:, Q_EXP:Q_EXP + 2 * K_AUG] = k.astype(BF16)
    oa_ref[:, Q_EXP + 2 * K_AUG:] = _dot(hb, w_ref[:, Q_EXP + 2 * K_AUG:NA_COLS]).astype(BF16)
    ob_ref[...] = _dot(hb, w_ref[:, NA_COLS:NA_COLS + NB_COLS])
    oc_ref[...] = _dot(hb, w_ref[:, NA_COLS + NB_COLS:])


def _in_proj(x2d, g, w_pad, t_len, tm=256):
    n = x2d.shape[0]
    ncols = w_pad.shape[1]
    nc = ncols - NA_COLS - NB_COLS
    return pl.pallas_call(
        functools.partial(_in_proj_kernel, t_len=t_len),
        grid=(n // tm,),
        in_specs=[
            pl.BlockSpec((tm, D_MODEL), lambda i: (i, 0)),
            pl.BlockSpec((1, D_MODEL), lambda i: (0, 0)),
            pl.BlockSpec((D_MODEL, ncols), lambda i: (0, 0)),
        ],
        out_specs=[
            pl.BlockSpec((tm, NA_COLS), lambda i: (i, 0)),
            pl.BlockSpec((tm, NB_COLS), lambda i: (i, 0)),
            pl.BlockSpec((tm, nc), lambda i: (i, 0)),
        ],
        out_shape=[
            jax.ShapeDtypeStruct((n, NA_COLS), BF16),
            jax.ShapeDtypeStruct((n, NB_COLS), F32),
            jax.ShapeDtypeStruct((n, nc), F32),
        ],
        compiler_params=_cparams(("parallel",)),
        name="in_proj",
    )(x2d, g, w_pad)


def _pad_w_in(w_in):
    sizes = (D_NSA, D_KV, D_KV, D_KV, D_KV, D_KV, D_KV, 3 * NSA_HEADS, D_RWKV_IN)
    offs = np.cumsum((0,) + sizes)
    q, kc, vc, ks, vs, kw, vw, gl, rw = (w_in[:, offs[i]:offs[i + 1]] for i in range(9))
    def lane_pad(w, groups):
        w = w.reshape(D_MODEL, groups, HEAD_DIM)
        return jnp.pad(w, ((0, 0), (0, 0), (0, LANES - HEAD_DIM))).reshape(D_MODEL, groups * LANES)

    glp = jnp.pad(gl, ((0, 0), (0, LANES - 3 * NSA_HEADS)))
    return jnp.concatenate(
        [lane_pad(q, NSA_HEADS), lane_pad(ks, NSA_KV_HEADS), lane_pad(kw, NSA_KV_HEADS),
         vs, vw, kc, vc, glp, rw], axis=1).astype(BF16)


def _compress_kernel(rk_ref, rv_ref, pek_ref, pev_ref, w1k_ref, w1v_ref, w2k_ref, w2v_ref,
                     ok_ref, ov_ref):
    for r_ref, pe_ref, w1_ref, w2_ref, o_ref in (
            (rk_ref, pek_ref, w1k_ref, w2k_ref, ok_ref),
            (rv_ref, pev_ref, w1v_ref, w2v_ref, ov_ref)):
        rows = r_ref[0]
        nxt = pltpu.roll(rows, rows.shape[0] - 1, axis=0)
        a = (rows + pe_ref[0:1, :]).astype(BF16)
        b = (nxt + pe_ref[1:2, :]).astype(BF16)
        hid = _dot(a, w1_ref[0]) + _dot(b, w1_ref[1])
        out = _dot(_gelu(hid).astype(BF16), w2_ref[...])
        if o_ref is ok_ref:
            nb = rows.shape[0]
            end = lax.broadcasted_iota(jnp.int32, (nb, 1), 0) * CMP_STRIDE + (CMP_LEN - 1)
            lane = lax.broadcasted_iota(jnp.int32, (1, K_AUG), 1) % LANES
            out = out + _key_features(end, lane)
        o_ref[0] = out.astype(BF16)


def _expand_cmp_weights(pe, w1, w2, out_lanes):
    half = CMP_LEN // 2
    eye = jnp.eye(NSA_KV_HEADS, dtype=F32)
    w1r = w1.reshape(2, half, HEAD_DIM, CMP_HIDDEN)
    w1e = w1r[:, :, None, :, None, :] * eye[None, None, :, None, :, None]
    w1e = w1e.reshape(2, half * D_KV, NSA_KV_HEADS * CMP_HIDDEN).astype(BF16)
    pee = jnp.broadcast_to(pe.reshape(2, half, 1, HEAD_DIM), (2, half, NSA_KV_HEADS, HEAD_DIM))
    pee = pee.reshape(2, half * D_KV)
    w2p = jnp.pad(w2, ((0, 0), (0, out_lanes - HEAD_DIM)))
    w2e = (eye[:, None, :, None] * w2p[None, :, None, :]).reshape(
        NSA_KV_HEADS * CMP_HIDDEN, NSA_KV_HEADS * out_lanes).astype(BF16)
    return pee, w1e, w2e


def _compress(rk, rv, wk, wv):
    b, nb, width = rk.shape
    full2 = lambda a: pl.BlockSpec(a.shape, lambda i: (0, 0))
    full3 = lambda a: pl.BlockSpec(a.shape, lambda i: (0, 0, 0))
    row = pl.BlockSpec((1, nb, width), lambda i: (i, 0, 0))
    out = lambda lanes: pl.BlockSpec((1, nb, lanes), lambda i: (i, 0, 0))
    return pl.pallas_call(
        _compress_kernel,
        grid=(b,),
        in_specs=[row, row, full2(wk[0]), full2(wv[0]), full3(wk[1]), full3(wv[1]),
                  full2(wk[2]), full2(wv[2])],
        out_specs=[out(K_AUG), out(D_KV)],
        out_shape=[jax.ShapeDtypeStruct((b, nb, K_AUG), BF16),
                   jax.ShapeDtypeStruct((b, nb, D_KV), BF16)],
        compiler_params=_cparams(("parallel",)),
        name="nsa_compress",
    )(rk, rv, wk[0], wv[0], wk[1], wv[1], wk[2], wv[2])


NSA_TQ = 128
NSA_TK = 256
NSA_TKW = 128


def _masked_softmax(s, mask):
    s = jnp.where(mask, s, NEG_BIG)
    s = s - jnp.max(s, axis=-1, keepdims=True)
    p = jnp.where(mask, jnp.exp(s), 0.0)
    return p / jnp.maximum(jnp.sum(p, axis=-1, keepdims=True), 1e-30)


def _flash_update(carry, s, v_t):
    m, l, acc = carry
    m_new = jnp.maximum(m, jnp.max(s, axis=0, keepdims=True))
    alpha = jnp.exp(m - m_new)
    p = jnp.exp(s - m_new)
    l = alpha * l + jnp.sum(p, axis=0, keepdims=True)
    acc = alpha * acc + _dot(v_t, p.astype(BF16))
    return m_new, l, acc


def _nsa_kernel(q_ref, ks_ref, kw_ref, vst_ref, vwt_ref, ck_ref, cv_ref, ovl_ref, oh_ref, gl_ref,
                eg_ref, o_ref):
    tq = NSA_TQ
    R = NSA_GROUP
    i = pl.program_id(1)
    q0 = i * tq
    t_col = q0 + lax.broadcasted_iota(jnp.int32, (tq, 1), 0)
    t_lane = q0 + lax.broadcasted_iota(jnp.int32, (1, tq), 1)
    t_lanes = jnp.concatenate([t_lane] * R, axis=1)
    n_cmp = ck_ref.shape[1]
    n_sel = ovl_ref.shape[0]

    gate = _sigmoid(gl_ref[...])
    g_hi, g_lo = _split2(gate)
    gexp = [_dot(g_hi, eg_ref[j]) + _dot(g_lo, eg_ref[j]) for j in range(3)]

    cmp_end = lax.broadcasted_iota(jnp.int32, (1, n_cmp), 1) * CMP_STRIDE + (CMP_LEN - 1)
    mask_c = t_col >= cmp_end
    ids = lax.broadcasted_iota(jnp.int32, (n_sel, 1), 0)
    idsf = ids.astype(F32)
    cur = t_lane // SEL_LEN
    forced = (ids == 0) | (ids == cur) | (ids == cur - 1)
    valid = ids * SEL_LEN <= t_lane

    G = NSA_KV_HEADS
    lanes_of = [slice(g * LANES, (g + 1) * LANES) for g in range(G)]
    q_st, q_slc, o_cmp = [], [], []
    for g in range(G):
        qh = [q_ref[:, h * LANES:(h + 1) * LANES] for h in range(g * R, (g + 1) * R)]
        ck = ck_ref[0, :, lanes_of[g]]
        cv = cv_ref[0]

        imp = jnp.zeros((n_sel, tq), F32)
        for r in range(R):
            p = _masked_softmax(_dot_nt(qh[r], ck), mask_c)
            o_cmp.append(_dot(p.astype(BF16), cv))
            p_hi, p_lo = _split2(p)
            imp = imp + _dot_nt(ovl_ref[...], p_hi) + _dot_nt(ovl_ref[...], p_lo)
        imp = jnp.where(forced, 1e6, jnp.where(valid, imp, -1.0))

        sel = jnp.zeros((n_sel, tq), F32)
        for _ in range(min(SEL_TOPN, n_sel)):
            m = jnp.max(imp, axis=0, keepdims=True)
            first = jnp.min(jnp.where(imp == m, idsf, float(n_sel)), axis=0, keepdims=True)
            hit = idsf == first
            sel = jnp.where(hit, 1.0, sel)
            imp = jnp.where(hit, -3e38, imp)
        bias_t = jnp.where(sel > 0.5, 0.0, NEG_BIG)
        if n_sel < LANES:
            bias_t = jnp.concatenate([bias_t, jnp.zeros((LANES - n_sel, tq), F32)], axis=0)
        bias = bias_t.T.astype(BF16)

        q_st.append(jnp.concatenate(qh, axis=0))
        q_slc.append(jnp.concatenate([q_st[g], jnp.concatenate([bias] * R, axis=0)], axis=1))

    init = tuple((jnp.full((1, R * tq), -3e38, F32), jnp.zeros((1, R * tq), F32),
                  jnp.zeros((LANES, R * tq), F32)) for _ in range(G))

    def slc_tile(j, carry, diagonal):
        k0 = pl.multiple_of(j * NSA_TK, NSA_TK)
        onehot = oh_ref[pl.ds(k0, NSA_TK), :]
        v_t = vst_ref[0, :, pl.ds(k0, NSA_TK)]
        out = []
        for g in range(G):
            k = jnp.concatenate([ks_ref[pl.ds(k0, NSA_TK), lanes_of[g]], onehot], axis=1)
            s = _dot_nt(k, q_slc[g])
            if diagonal:
                pos = k0 + lax.broadcasted_iota(jnp.int32, (NSA_TK, 1), 0)
                s = jnp.where(t_lanes >= pos, s, NEG_BIG)
            out.append(_flash_update(carry[g], s, v_t))
        return tuple(out)

    n_off = q0 // NSA_TK
    carry = lax.fori_loop(0, n_off, functools.partial(slc_tile, diagonal=False), init)
    slc = slc_tile(n_off, carry, True)

    def win_tile(j, carry, mode):
        k0 = pl.multiple_of(j * NSA_TKW, NSA_TKW)
        v_t = vwt_ref[0, :, pl.ds(k0, NSA_TKW)]
        pos = k0 + lax.broadcasted_iota(jnp.int32, (NSA_TKW, 1), 0)
        out = []
        for g in range(G):
            s = _dot_nt(kw_ref[pl.ds(k0, NSA_TKW), lanes_of[g]], q_st[g])
            if mode == "first":
                s = jnp.where(pos > t_lanes - WINDOW, s, NEG_BIG)
            elif mode == "diagonal":
                s = jnp.where(t_lanes >= pos, s, NEG_BIG)
            out.append(_flash_update(carry[g], s, v_t))
        return tuple(out)

    nw = WINDOW // NSA_TKW
    first_lo = jnp.maximum(i - nw, 0)
    first_hi = jnp.where(i >= nw, first_lo + 1, first_lo)
    carry = lax.fori_loop(first_lo, first_hi, functools.partial(win_tile, mode="first"), init)
    carry = lax.fori_loop(jnp.maximum(i - nw + 1, 0), i,
                          functools.partial(win_tile, mode="full"), carry)
    win = win_tile(i, carry, "diagonal")

    for h in range(NSA_HEADS):
        g, r = divmod(h, R)
        cols = slice(h * LANES, (h + 1) * LANES)
        rows = slice(r * tq, (r + 1) * tq)
        o_slc = (slc[g][2][:, rows] / slc[g][1][:, rows]).T
        o_win = (win[g][2][:, rows] / win[g][1][:, rows]).T
        out = gexp[0][:, cols] * o_cmp[h] + gexp[1][:, cols] * o_slc + gexp[2][:, cols] * o_win
        o_ref[:, cols] = out.astype(BF16)


def _nsa_consts(t_len, n_cmp_pad):
    n_cmp = (t_len - CMP_LEN) // CMP_STRIDE + 1
    n_sel = t_len // SEL_LEN
    cmp_start = np.arange(n_cmp) * CMP_STRIDE
    sel_start = np.arange(n_sel) * SEL_LEN
    ovl = np.clip(np.minimum(cmp_start[:, None] + CMP_LEN, sel_start[None, :] + SEL_LEN)
                  - np.maximum(cmp_start[:, None], sel_start[None, :]), 0, None) / CMP_LEN
    ovl_t = np.zeros((n_sel, n_cmp_pad), np.float32)
    ovl_t[:, :n_cmp] = ovl.T
    onehot = np.zeros((t_len, LANES), np.float32)
    onehot[np.arange(t_len), np.arange(t_len) // SEL_LEN] = 1.0
    eg = np.zeros((3, LANES, Q_EXP), np.float32)
    for h in range(NSA_HEADS):
        for j in range(3):
            eg[j, h * 3 + j, h * LANES:(h + 1) * LANES] = 1.0
    return jnp.asarray(ovl_t, BF16), jnp.asarray(onehot, BF16), jnp.asarray(eg, BF16)


def _nsa_attention(oa, ob, vs_t, vw_t, cmp_k, cmp_v, b, t_len):
    assert NSA_TQ == NSA_TKW and NSA_TK % NSA_TQ == 0 and t_len // SEL_LEN <= LANES
    n_cmp_pad = cmp_k.shape[1]
    ovl_t, onehot, eg = _nsa_consts(t_len, n_cmp_pad)
    nq = t_len // NSA_TQ
    k_col0 = Q_EXP // K_AUG
    kspec = lambda c: pl.BlockSpec((t_len, K_AUG), lambda bi, i: (bi, k_col0 + c))
    vspec = pl.BlockSpec((1, D_KV, t_len), lambda bi, i: (bi, 0, 0))
    const2 = lambda a: pl.BlockSpec(a.shape, lambda bi, i: (0, 0))
    return pl.pallas_call(
        _nsa_kernel,
        grid=(b, nq),
        in_specs=[
            pl.BlockSpec((NSA_TQ, Q_EXP), lambda bi, i: (bi * nq + i, 0)),
            kspec(0), kspec(1), vspec, vspec,
            pl.BlockSpec((1, n_cmp_pad, K_AUG), lambda bi, i: (bi, 0, 0)),
            pl.BlockSpec((1, n_cmp_pad, D_KV), lambda bi, i: (bi, 0, 0)),
            const2(ovl_t), const2(onehot),
            pl.BlockSpec((NSA_TQ, LANES), lambda bi, i: (bi * nq + i, 2)),
            pl.BlockSpec(eg.shape, lambda bi, i: (0, 0, 0)),
        ],
        out_specs=pl.BlockSpec((NSA_TQ, Q_EXP), lambda bi, i: (bi * nq + i, 0)),
        out_shape=jax.ShapeDtypeStruct((b * t_len, Q_EXP), BF16),
        compiler_params=_cparams(("parallel", "arbitrary")),
        name="nsa_attention",
    )(oa, oa, oa, vs_t, vw_t, cmp_k, cmp_v, ovl_t, onehot, ob, eg)


def _dot_x3(x, e):
    hi = x.astype(BF16)
    r1 = x - hi.astype(F32)
    mid = r1.astype(BF16)
    lo = (r1 - mid.astype(F32)).astype(BF16)
    return _dot(hi, e) + _dot(mid, e) + _dot(lo, e)


def _head_sum_matrix():
    ids = np.arange(D_RWKV) // HEAD_DIM
    return jnp.asarray(ids[:, None] == ids[None, :], BF16)


def _rwkv_pre_kernel(p_ref, hp_ref, mu_ref, w0_ref, w2_ref, a0_ref, a2_ref, g2_ref, kk_ref,
                     ka_ref, bd_ref, r_o, lw_o, k_o, v_o, kkn_o, a_o, g_o, *, tiles_per_seq):
    i = pl.program_id(0)
    p = p_ref[...]
    tm = p.shape[0]
    keep = jnp.where(i % tiles_per_seq == 0, 0.0, 1.0)
    halo = hp_ref[SUBLANES - 1:SUBLANES, :] * keep
    prev = pltpu.roll(p, 1, axis=0)
    row0 = lax.broadcasted_iota(jnp.int32, (tm, 1), 0) == 0
    prev = jnp.where(row0, halo, prev)
    ps = p + (prev - p) * mu_ref[...]
    d = D_RWKV
    r = ps[:, 0:d]
    k = ps[:, d:2 * d]
    v = ps[:, 2 * d:3 * d]
    xw = ps[:, 3 * d:3 * d + RANK_W]
    xa = ps[:, 3 * d + RANK_W:3 * d + RANK_W + RANK_A]
    xg = ps[:, 3 * d + RANK_W + RANK_A:]
    z = -(w0_ref[...] + _dot(jnp.tanh(xw).astype(BF16), w2_ref[...]))
    softplus = jnp.maximum(z, 0.0) + jnp.log(1.0 + jnp.exp(-jnp.abs(z)))
    w = -softplus - 0.5
    a = _sigmoid(a0_ref[...] + _dot(xa.astype(BF16), a2_ref[...]))
    g = _dot(_sigmoid(xg).astype(BF16), g2_ref[...])
    kk = k * kk_ref[...]
    ss = _dot_x3(kk * kk, bd_ref[...])
    kk = kk / jnp.maximum(jnp.sqrt(ss), 1e-12)
    r_o[...] = r
    lw_o[...] = -jnp.exp(w)
    k_o[...] = k * (1.0 + (a - 1.0) * ka_ref[...])
    v_o[...] = v
    kkn_o[...] = kk
    a_o[...] = a
    g_o[...] = g


def _rwkv_pre(oc, t_len, mu, w0, w2, a0, a2, g2, k_k, k_a, bd, tm=256):
    n = oc.shape[0]
    tiles_per_seq = t_len // tm
    halo_blocks = tm // SUBLANES
    full = lambda a: pl.BlockSpec(a.shape, lambda i: (0, 0))
    outs = pl.BlockSpec((tm, D_RWKV), lambda i: (i, 0))
    params = (mu, w0, w2, a0, a2, g2, k_k, k_a, bd)
    return pl.pallas_call(
        functools.partial(_rwkv_pre_kernel, tiles_per_seq=tiles_per_seq),
        grid=(n // tm,),
        in_specs=[
            pl.BlockSpec((tm, D_RWKV_IN), lambda i: (i, 0)),
            pl.BlockSpec((SUBLANES, D_RWKV_IN),
                         lambda i: (jnp.maximum(i * halo_blocks - 1, 0), 0)),
        ] + [full(a) for a in params],
        out_specs=[outs] * 7,
        out_shape=[jax.ShapeDtypeStruct((n, D_RWKV), F32)] * 7,
        compiler_params=_cparams(("parallel",)),
        name="rwkv_pre",
    )(oc, oc, *params)


RWKV_C = 64
RWKV_CB = 128
RWKV_PASSES = 1


def _bmm(eq, a, b):
    ein = lambda x, y: jnp.einsum(eq, x, y, preferred_element_type=F32)
    if RWKV_PASSES == 1:
        return ein(a.astype(BF16), b.astype(BF16))
    ah, al = _split2(a)
    bh, bl = _split2(b)
    return ein(ah, bh) + ein(ah, bl) + ein(al, bh)


def _cumsum_rows(x2d, seg):
    rows = lax.broadcasted_iota(jnp.int32, (x2d.shape[0], 1), 0) % seg
    step = 1
    while step < seg:
        shifted = pltpu.roll(x2d, step, axis=0)
        x2d = x2d + jnp.where(rows >= step, shifted, 0.0)
        step *= 2
    return x2d


def _rwkv_rec_kernel(r_ref, lw_ref, k_ref, kk_ref, a_ref, vt_ref, ot_ref, s_ref):
    H = r_ref.shape[1]
    C = RWKV_C

    @pl.when(pl.program_id(1) == 0)
    def _():
        s_ref[...] = jnp.zeros_like(s_ref)

    ri = lax.broadcasted_iota(jnp.int32, (C, C), 0)
    ci = lax.broadcasted_iota(jnp.int32, (C, C), 1)
    strict = (ri > ci)[None]
    incl = (ri >= ci)[None]
    eye = (ri == ci).astype(F32)[None]

    for sub in range(RWKV_CB // C):
        rows = slice(sub * C, (sub + 1) * C)
        r = r_ref[0, :, rows, :]
        lw = lw_ref[0, :, rows, :]
        k = k_ref[0, :, rows, :]
        kk = kk_ref[0, :, rows, :]
        a = a_ref[0, :, rows, :]
        vt = vt_ref[0, :, :, rows]
        s0 = s_ref[...]

        cum = _cumsum_rows(lw.reshape(H * C, HEAD_DIM), C).reshape(H, C, HEAD_DIM)
        cum_last = cum[:, C - 1:C, :]
        p_inv = jnp.exp(-cum)
        w_last = jnp.exp(cum_last - cum)
        bm = kk * a
        at = -kk * jnp.exp(cum - lw)
        rt = r * jnp.exp(cum)
        bt = bm * p_inv
        kt = k * p_inv

        nt = 'hik,hjk->hij'
        m_ab = jnp.where(strict, _bmm(nt, at, bt), 0.0)
        m_ak = jnp.where(strict, _bmm(nt, at, kt), 0.0)
        n_rb = jnp.where(incl, _bmm(nt, rt, bt), 0.0)
        n_rk = jnp.where(incl, _bmm(nt, rt, kt), 0.0)

        tinv = eye + m_ab
        mp = m_ab
        step = 1
        while 2 * step < C:
            mp = _bmm('hij,hjk->hik', mp, mp)
            tinv = tinv + _bmm('hij,hjk->hik', tinv, mp)
            step *= 2

        rhs_t = _bmm('hvk,hik->hvi', s0, at) + _bmm('hvj,hij->hvi', vt, m_ak)
        ut = _bmm('hvj,hij->hvi', rhs_t, tinv)
        ot = (_bmm('hvk,hik->hvi', s0, rt) + _bmm('hvj,hij->hvi', ut, n_rb)
              + _bmm('hvj,hij->hvi', vt, n_rk))
        ot_ref[0, :, :, rows] = ot
        s_ref[...] = (s0 * jnp.exp(cum_last) + _bmm('hvj,hjk->hvk', ut, bm * w_last)
                      + _bmm('hvj,hjk->hvk', vt, k * w_last))


def _rwkv_recurrence(r, lw, k, kk, a, vt):
    b, h, t_len, d = r.shape
    tok = pl.BlockSpec((1, h, RWKV_CB, d), lambda bi, c: (bi, 0, c, 0))
    tr = pl.BlockSpec((1, h, d, RWKV_CB), lambda bi, c: (bi, 0, 0, c))
    return pl.pallas_call(
        _rwkv_rec_kernel,
        grid=(b, t_len // RWKV_CB),
        in_specs=[tok] * 5 + [tr],
        out_specs=tr,
        out_shape=jax.ShapeDtypeStruct((b, h, d, t_len), F32),
        scratch_shapes=[pltpu.VMEM((h, d, d), F32)],
        compiler_params=_cparams(("parallel", "arbitrary")),
        name="rwkv_recurrence",
    )(r, lw, k, kk, a, vt)


def _rwkv_post_kernel(o_ref, r_ref, k_ref, v_ref, g_ref, lnw_ref, lnb_ref, rk_ref, bd_ref, y_ref):
    o = o_ref[...]
    bd = bd_ref[...]
    inv = 1.0 / HEAD_DIM
    mean = _dot_x3(o, bd) * inv
    d = o - mean
    var = _dot_x3(d * d, bd) * inv
    on = d * lax.rsqrt(var + GN_EPS) * lnw_ref[...] + lnb_ref[...]
    bonus = _dot_x3(r_ref[...] * k_ref[...] * rk_ref[...], bd)
    y_ref[...] = ((on + bonus * v_ref[...]) * g_ref[...]).astype(BF16)


def _rwkv_post(o, r, k, v, g, ln_w, ln_b, r_k, bd, tm=512):
    n = o.shape[0]
    tile = pl.BlockSpec((tm, D_RWKV), lambda i: (i, 0))
    full = lambda a: pl.BlockSpec(a.shape, lambda i: (0, 0))
    return pl.pallas_call(
        _rwkv_post_kernel,
        grid=(n // tm,),
        in_specs=[tile] * 5 + [full(ln_w), full(ln_b), full(r_k), full(bd)],
        out_specs=tile,
        out_shape=jax.ShapeDtypeStruct((n, D_RWKV), BF16),
        compiler_params=_cparams(("parallel",)),
        name="rwkv_post",
    )(o, r, k, v, g, ln_w, ln_b, r_k, bd)


def _rwkv_mixer(oc, b, t_len, mu, w0, w2, a0, a2, g2, k_k, k_a, r_k, ln_w, ln_b):
    bd = _head_sum_matrix()
    row = lambda z: z.reshape(1, -1)
    r, lw, k, v, kk, a, g = _rwkv_pre(
        oc, t_len, row(mu), row(w0), w2.astype(BF16), row(a0), a2.astype(BF16),
        g2.astype(BF16), row(k_k), row(k_a), bd)
    hm = lambda z: z.reshape(b, t_len, RWKV_HEADS, HEAD_DIM).transpose(0, 2, 1, 3)
    vt = v.reshape(b, t_len, RWKV_HEADS, HEAD_DIM).transpose(0, 2, 3, 1)
    ot = _rwkv_recurrence(hm(r), hm(lw), hm(k), hm(kk), hm(a), vt)
    o = ot.transpose(0, 3, 1, 2).reshape(b * t_len, D_RWKV)
    return _rwkv_post(o, r, k, v, g, row(ln_w), row(ln_b), row(r_k), bd)


def _out_proj_kernel(x_ref, yn_ref, yr_ref, wn_ref, wr_ref, g_ref, x1_ref, h2_ref):
    x1 = x_ref[...] + _dot(yn_ref[...], wn_ref[...]) + _dot(yr_ref[...], wr_ref[...])
    x1_ref[...] = x1
    h2_ref[...] = x1 * lax.rsqrt(jnp.mean(x1 * x1, axis=-1, keepdims=True) + EPS) * g_ref[...]


def _expand_w_out(w_out):
    wn = w_out[:D_NSA].reshape(NSA_HEADS, 1, HEAD_DIM, D_MODEL)
    onehot = (np.arange(NSA_HEADS)[:, None] // NSA_GROUP == np.arange(NSA_KV_HEADS)[None, :])
    wn = wn * jnp.asarray(onehot, F32)[:, :, None, None]
    return wn.reshape(Q_EXP, D_MODEL).astype(BF16), w_out[D_NSA:].astype(BF16)


def _out_proj(x2d, y_nsa, y_rwkv, wn, wr, g, tm=512):
    n = x2d.shape[0]
    tile = lambda a: pl.BlockSpec((tm, a.shape[1]), lambda i: (i, 0))
    full = lambda a: pl.BlockSpec(a.shape, lambda i: (0, 0))
    return pl.pallas_call(
        _out_proj_kernel,
        grid=(n // tm,),
        in_specs=[tile(x2d), tile(y_nsa), tile(y_rwkv), full(wn), full(wr), full(g)],
        out_specs=[tile(x2d), tile(x2d)],
        out_shape=[jax.ShapeDtypeStruct((n, D_MODEL), F32)] * 2,
        compiler_params=_cparams(("parallel",)),
        name="out_proj",
    )(x2d, y_nsa, y_rwkv, wn, wr, g)


def _topk_rows(s, k, rid):
    vals, ids = [], []
    for _ in range(k):
        m = jnp.max(s, axis=0, keepdims=True)
        first = jnp.min(jnp.where(s == m, rid, jnp.inf), axis=0, keepdims=True)
        vals.append(m)
        ids.append(first)
        s = jnp.where(rid == first, -jnp.inf, s)
    return jnp.concatenate(vals, axis=0), jnp.concatenate(ids, axis=0)


def _take_rows(table, idx):
    rows = lax.broadcasted_iota(jnp.int32, (table.shape[0], 1), 0).astype(F32)
    out = [jnp.sum(jnp.where(rows == idx[r:r + 1], table, 0.0), axis=0, keepdims=True)
           for r in range(idx.shape[0])]
    return jnp.concatenate(out, axis=0)


def _pair_candidates():
    k = PEER_TOPK
    flat = [0 * k + j for j in range(k)]
    for i in range(1, SUBLANES):
        flat += [i * k + j for j in range(SUBLANES)]
    flat += [i * k for i in range(SUBLANES, k)]
    return np.asarray(flat, np.float32).reshape(-1, 1)


def _peer_route_kernel(h_ref, wq_ref, sk_ref, flat_ref, e_ref, g_ref):
    K = PEER_TOPK
    half = PEER_D_QUERY // 2
    q = _dot(h_ref[...].astype(BF16), wq_ref[...]).astype(BF16)
    key_ids = lax.broadcasted_iota(jnp.int32, (PEER_N_KEYS, 1), 0).astype(F32)
    flat = flat_ref[...]
    for h in range(PEER_HEADS):
        top = []
        for c in range(2):
            col = (h * 2 + c) * half
            s_t = _dot_nt(sk_ref[h, c], q[:, col:col + half])
            top.append(_topk_rows(s_t, K, key_ids))
        (v0, i0), (v1, i1) = top
        cand = jnp.concatenate(
            [v0[0:1] + v1]
            + [v0[i:i + 1] + v1[:SUBLANES] for i in range(1, SUBLANES)]
            + [v0[SUBLANES:] + v1[0:1]], axis=0)
        best, pair = _topk_rows(cand, K, flat)
        pi = jnp.floor(pair * (1.0 / K))
        pj = pair - pi * K
        experts = _take_rows(i0, pi) * float(PEER_N_KEYS) + _take_rows(i1, pj)
        p = jnp.exp(best - jnp.max(best, axis=0, keepdims=True))
        e_ref[h * K:(h + 1) * K, :] = experts.astype(jnp.int32)
        g_ref[h * K:(h + 1) * K, :] = p / jnp.sum(p, axis=0, keepdims=True)


def _peer_route(h2, wq, sk, tm=256):
    n = h2.shape[0]
    hk = PEER_HEADS * PEER_TOPK
    flat = jnp.asarray(_pair_candidates())
    out = pl.BlockSpec((hk, tm), lambda i: (0, i))
    return pl.pallas_call(
        _peer_route_kernel,
        grid=(n // tm,),
        in_specs=[
            pl.BlockSpec((tm, D_MODEL), lambda i: (i, 0)),
            pl.BlockSpec(wq.shape, lambda i: (0, 0)),
            pl.BlockSpec(sk.shape, lambda i: (0, 0, 0, 0)),
            pl.BlockSpec(flat.shape, lambda i: (0, 0)),
        ],
        out_specs=[out, out],
        out_shape=[jax.ShapeDtypeStruct((hk, n), jnp.int32), jax.ShapeDtypeStruct((hk, n), F32)],
        compiler_params=_cparams(("parallel",)),
        name="peer_route",
    )(h2, wq, sk, flat)


PEER_TT = 64
HALF_ROWS = SUBLANES // 2
SUB_ORDER = (0, 4, 2, 6, 1, 5, 3, 7)
HIGH_MASK = 0xFFFF0000


def _pack_table(w):
    bits = lax.bitcast_convert_type(w.astype(BF16), jnp.uint16).astype(jnp.uint32)
    half = w.shape[1] // 2
    packed = bits[:, :half] | (bits[:, half:] << 16)
    return packed.reshape(w.shape[0] * HALF_ROWS, LANES)


def _table_row(tbl_ref, row0):
    return tbl_ref[pl.ds(pl.multiple_of(row0, HALF_ROWS), HALF_ROWS), :]


def _unpack_words(word):
    lo = pltpu.bitcast(word << 16, F32)
    hi = pltpu.bitcast(word & jnp.uint32(HIGH_MASK), F32)
    return lo, hi


def _sublane_tree(c):
    sub = lax.broadcasted_iota(jnp.int32, (SUBLANES, LANES), 0)
    m2 = (sub % 4) < 2
    d = []
    for x, y in zip(c[0::2], c[1::2]):
        d.append(jnp.where(m2, x + pltpu.roll(x, 6, axis=0), y + pltpu.roll(y, 2, axis=0)))
    m1 = (sub % 2) == 0
    x, y = d
    return jnp.where(m1, x + pltpu.roll(x, 7, axis=0), y + pltpu.roll(y, 1, axis=0))


def _peer_act_kernel(e_ref, h_ref, tbl_ref, gate_ref, ones_ref, w_ref, part_ref):
    hk = PEER_HEADS * PEER_TOPK
    tt = h_ref.shape[0]

    def token(t, carry):
        hrow = h_ref[t]
        h_lo = jnp.concatenate([hrow[:HALF_ROWS]] * 2, axis=0)
        h_hi = jnp.concatenate([hrow[HALF_ROWS:]] * 2, axis=0)
        for m in range(hk // SUBLANES):
            es = [e_ref[t, m * SUBLANES + j] for j in SUB_ORDER]
            c = []
            for ea, eb in zip(es[0::2], es[1::2]):
                lo, hi = _unpack_words(jnp.concatenate(
                    [_table_row(tbl_ref, ea), _table_row(tbl_ref, eb)], axis=0))
                c.append(lo * h_lo + hi * h_hi)
            row0 = pl.multiple_of((t * (hk // SUBLANES) + m) * SUBLANES, SUBLANES)
            part_ref[pl.ds(row0, SUBLANES), :] = _sublane_tree(c)
        return carry

    lax.fori_loop(0, tt, token, 0)
    hi, lo = _split2(part_ref[...])
    sums = _dot(jnp.concatenate([hi, lo], axis=1), ones_ref[...])
    sums = sums.reshape(tt, hk, LANES)
    eye = (lax.broadcasted_iota(jnp.int32, (hk, LANES), 0)
           == lax.broadcasted_iota(jnp.int32, (hk, LANES), 1))
    act = jnp.sum(jnp.where(eye[None], sums, 0.0), axis=1)
    w_ref[...] = _gelu(act) * gate_ref[...]


def _table_spec(tbl):
    return pl.BlockSpec(tbl.shape, lambda i: (0, 0), pipeline_mode=pl.Buffered(1))


def _peer_act(experts, h2_tiles, tbl, gates):
    n, hk = experts.shape
    ones = jnp.ones((2 * LANES, LANES), BF16)
    return pl.pallas_call(
        _peer_act_kernel,
        grid=(n // PEER_TT,),
        in_specs=[
            pl.BlockSpec((PEER_TT, hk), lambda i: (i, 0), memory_space=pltpu.SMEM),
            pl.BlockSpec((PEER_TT, SUBLANES, LANES), lambda i: (i, 0, 0)),
            _table_spec(tbl),
            pl.BlockSpec((PEER_TT, hk), lambda i: (i, 0)),
            pl.BlockSpec(ones.shape, lambda i: (0, 0)),
        ],
        out_specs=pl.BlockSpec((PEER_TT, hk), lambda i: (i, 0)),
        out_shape=jax.ShapeDtypeStruct((n, hk), F32),
        scratch_shapes=[pltpu.VMEM((PEER_TT * hk, LANES), F32)],
        compiler_params=_cparams(("arbitrary",)),
        name="peer_act",
    )(experts, h2_tiles, tbl, gates, ones)


PEER_ACCS = 4
PEER_GROUP = 64


def _peer_out_kernel(e_ref, w_ref, x_ref, tbl_ref, o_ref):
    hk = PEER_HEADS * PEER_TOPK
    tt = x_ref.shape[0]

    def token(t, carry):
        def group(gi, accs):
            acc_lo, acc_hi = list(accs[0]), list(accs[1])
            base = t * hk + gi * PEER_GROUP
            for j in range(PEER_GROUP):
                lo, hi = _unpack_words(_table_row(tbl_ref, e_ref[base + j]))
                wk = w_ref[base + j]
                acc_lo[j % PEER_ACCS] = acc_lo[j % PEER_ACCS] + wk * lo
                acc_hi[j % PEER_ACCS] = acc_hi[j % PEER_ACCS] + wk * hi
            return tuple(acc_lo), tuple(acc_hi)

        zeros = tuple(jnp.zeros((HALF_ROWS, LANES), F32) for _ in range(PEER_ACCS))
        acc_lo, acc_hi = lax.fori_loop(0, hk // PEER_GROUP, group, (zeros, zeros))
        tree = lambda a: (a[0] + a[1]) + (a[2] + a[3])
        o_ref[t] = x_ref[t] + jnp.concatenate([tree(acc_lo), tree(acc_hi)], axis=0)
        return carry

    lax.fori_loop(0, tt, token, 0)


def _peer_out(experts, w, x1_tiles, tbl):
    n, hk = experts.shape
    smem = pl.BlockSpec((PEER_TT * hk,), lambda i: (i,), memory_space=pltpu.SMEM)
    tile = pl.BlockSpec((PEER_TT, SUBLANES, LANES), lambda i: (i, 0, 0))
    return pl.pallas_call(
        _peer_out_kernel,
        grid=(n // PEER_TT,),
        in_specs=[smem, smem, tile, _table_spec(tbl)],
        out_specs=tile,
        out_shape=jax.ShapeDtypeStruct(x1_tiles.shape, F32),
        compiler_params=_cparams(("arbitrary",)),
        name="peer_out",
    )(experts.reshape(-1), w.reshape(-1), x1_tiles, tbl)


def _peer_ffn_residual(x1, h2, wq, sk, tbl_u, tbl_v):
    n = x1.shape[0]
    e_t, g_t = _peer_route(h2, wq, sk)
    experts, gates = e_t.T * HALF_ROWS, g_t.T
    tiles = lambda z: z.reshape(n, SUBLANES, LANES)
    w = _peer_act(experts, tiles(h2), tbl_u, gates)
    return _peer_out(experts, w, tiles(x1), tbl_v).reshape(n, D_MODEL)


def _rms_kernel(x_ref, g_ref, o_ref):
    x = x_ref[...]
    o_ref[...] = x * lax.rsqrt(jnp.mean(x * x, axis=-1, keepdims=True) + EPS) * g_ref[...]


def _final_norm(x2d, g, tm=512):
    n = x2d.shape[0]
    tile = pl.BlockSpec((tm, D_MODEL), lambda i: (i, 0))
    return pl.pallas_call(
        _rms_kernel,
        grid=(n // tm,),
        in_specs=[tile, pl.BlockSpec(g.shape, lambda i: (0, 0))],
        out_specs=tile,
        out_shape=jax.ShapeDtypeStruct((n, D_MODEL), F32),
        compiler_params=_cparams(("parallel",)),
        name="final_norm",
    )(x2d, g)


def kernel(x, norm_mix_g, w_in, cmp_k_pe, cmp_k_w1, cmp_k_w2, cmp_v_pe, cmp_v_w1, cmp_v_w2,
           rwkv_mu, rwkv_w0, rwkv_w2, rwkv_a0, rwkv_a2, rwkv_g2, rwkv_k_k, rwkv_k_a, rwkv_r_k,
           rwkv_ln_w, rwkv_ln_b, w_out, norm_ffn_g, peer_w_q, peer_sub_keys, peer_u, peer_v,
           norm_final_g):
    b, t_len, _ = x.shape
    n = b * t_len
    row = lambda z: z.reshape(1, -1)
    x2d = x.reshape(n, D_MODEL)
    for l in range(w_in.shape[0]):
        oa, ob, oc = _in_proj(x2d, row(norm_mix_g[l]), _pad_w_in(w_in[l]), t_len)
        nb = t_len // CMP_STRIDE
        rk = ob[:, 0:D_KV].reshape(b, nb, CMP_STRIDE * D_KV)
        rv = ob[:, D_KV:2 * D_KV].reshape(b, nb, CMP_STRIDE * D_KV)
        cmp_k, cmp_v = _compress(
            rk, rv, _expand_cmp_weights(cmp_k_pe[l], cmp_k_w1[l], cmp_k_w2[l], LANES),
            _expand_cmp_weights(cmp_v_pe[l], cmp_v_w1[l], cmp_v_w2[l], HEAD_DIM))
        v_col0 = Q_EXP + 2 * K_AUG
        v_t = lambda c: oa[:, v_col0 + c * D_KV:v_col0 + (c + 1) * D_KV].reshape(
            b, t_len, D_KV).transpose(0, 2, 1)
        y_nsa = _nsa_attention(oa, ob, v_t(0), v_t(1), cmp_k, cmp_v, b, t_len)
        y_rwkv = _rwkv_mixer(oc, b, t_len, rwkv_mu[l], rwkv_w0[l], rwkv_w2[l], rwkv_a0[l],
                             rwkv_a2[l], rwkv_g2[l], rwkv_k_k[l], rwkv_k_a[l], rwkv_r_k[l],
                             rwkv_ln_w[l], rwkv_ln_b[l])
        wn, wr = _expand_w_out(w_out[l])
        x1, h2 = _out_proj(x2d, y_nsa, y_rwkv, wn, wr, row(norm_ffn_g[l]))
        x2d = _peer_ffn_residual(x1, h2, peer_w_q[l].astype(BF16), peer_sub_keys[l].astype(BF16),
                                 _pack_table(peer_u[l]), _pack_table(peer_v[l]))
    return _final_norm(x2d, row(norm_final_g)).reshape(b, t_len, D_MODEL)
```

```python
import functools

import numpy as np
import jax
import jax.numpy as jnp
from jax import lax
from jax.experimental import pallas as pl
from jax.experimental.pallas import tpu as pltpu

F32 = jnp.float32
BF16 = jnp.bfloat16

D_MODEL = 1024
HEAD_DIM = 64
NSA_HEADS = 8
NSA_KV_HEADS = 2
NSA_GROUP = NSA_HEADS // NSA_KV_HEADS
RWKV_HEADS = 8
D_NSA = NSA_HEADS * HEAD_DIM
D_RWKV = RWKV_HEADS * HEAD_DIM
D_KV = NSA_KV_HEADS * HEAD_DIM
CMP_LEN = 32
CMP_STRIDE = 16
CMP_HIDDEN = 128
SEL_LEN = 64
SEL_TOPN = 16
WINDOW = 512
RANK_W = 64
RANK_A = 64
RANK_G = 128
D_RWKV_IN = 3 * D_RWKV + RANK_W + RANK_A + RANK_G
PEER_HEADS = 8
PEER_N_KEYS = 128
PEER_D_QUERY = 256
PEER_TOPK = 16
EPS = 1e-6
GN_EPS = 64e-5

LANES = 128
SUBLANES = 8
VMEM_LIMIT = 56 * 1024 * 1024

NEG_BIG = -1e30
Q_EXP = NSA_HEADS * LANES
K_AUG = NSA_KV_HEADS * LANES
NA_COLS = Q_EXP + 2 * K_AUG + 2 * D_KV
NB_COLS = 3 * LANES
FEAT0 = HEAD_DIM
POS_SPLIT = 64


def _cparams(sem):
    return pltpu.CompilerParams(dimension_semantics=sem, vmem_limit_bytes=VMEM_LIMIT)


def _dot(a, b):
    return jnp.dot(a, b, preferred_element_type=F32)


def _dot_nt(a, b):
    return lax.dot_general(a, b, (((1,), (1,)), ((), ())), preferred_element_type=F32)


def _split2(x):
    hi = x.astype(BF16)
    lo = (x - hi.astype(F32)).astype(BF16)
    return hi, lo


def _dot_x2(x, e):
    hi, lo = _split2(x)
    return _dot(hi, e) + _dot(lo, e)


def _gelu(x):
    return 0.5 * x * (1.0 + jnp.tanh(0.7978845608028654 * (x + 0.044715 * (x * x * x))))


def _sigmoid(x):
    return 1.0 / (1.0 + jnp.exp(-x))


def _key_features(pos, lane):
    hi = (pos // POS_SPLIT).astype(F32)
    lo = (pos % POS_SPLIT).astype(F32)
    return jnp.where(lane == FEAT0, hi, jnp.where(lane == FEAT0 + 1, lo,
                     jnp.where((lane == FEAT0 + 2) | (lane == FEAT0 + 3), 1.0, 0.0)))


def _query_features(t, lane, slope):
    hi = (t // POS_SPLIT).astype(F32)
    lo = (t % POS_SPLIT).astype(F32)
    return jnp.where(lane == FEAT0, slope * POS_SPLIT, jnp.where(
        lane == FEAT0 + 1, slope, jnp.where(
            lane == FEAT0 + 2, -slope * POS_SPLIT * hi, jnp.where(
                lane == FEAT0 + 3, -slope * lo, 0.0))))


def _in_proj_kernel(x_ref, g_ref, w_ref, oa_ref, ob_ref, oc_ref, *, t_len):
    x = x_ref[...]
    tm = x.shape[0]
    h = x * lax.rsqrt(jnp.mean(x * x, axis=-1, keepdims=True) + EPS) * g_ref[...]
    hb = h.astype(BF16)
    t = (pl.program_id(0) * tm + lax.broadcasted_iota(jnp.int32, (tm, 1), 0)) % t_len
    col = lax.broadcasted_iota(jnp.int32, (1, Q_EXP), 1)
    slope = jnp.zeros((1, Q_EXP), F32)
    for hd in range(NSA_HEADS):
        slope = jnp.where(col // LANES == hd, 2.0 ** -(hd + 1), slope)
    q = _dot(hb, w_ref[:, :Q_EXP]) * (HEAD_DIM ** -0.5) + _query_features(t, col % LANES, slope)
    oa_ref[:, :Q_EXP] = q.astype(BF16)
    colk = lax.broadcasted_iota(jnp.int32, (1, 2 * K_AUG), 1)
    k = _dot(hb, w_ref[:, Q_EXP:Q_EXP + 2 * K_AUG]) + _key_features(t, colk % LANES)
    oa_ref[:, Q_EXP:Q_EXP + 2 * K_AUG] = k.astype(BF16)
    oa_ref[:, Q_EXP + 2 * K_AUG:] = _dot(hb, w_ref[:, Q_EXP + 2 * K_AUG:NA_COLS]).astype(BF16)
    ob_ref[...] = _dot(hb, w_ref[:, NA_COLS:NA_COLS + NB_COLS])
    oc_ref[...] = _dot(hb, w_ref[:, NA_COLS + NB_COLS:])


def _in_proj(x2d, g, w_pad, t_len, tm=256):
    n = x2d.shape[0]
    ncols = w_pad.shape[1]
    nc = ncols - NA_COLS - NB_COLS
    return pl.pallas_call(
        functools.partial(_in_proj_kernel, t_len=t_len),
        grid=(n // tm,),
        in_specs=[
            pl.BlockSpec((tm, D_MODEL), lambda i: (i, 0)),
            pl.BlockSpec((1, D_MODEL), lambda i: (0, 0)),
            pl.BlockSpec((D_MODEL, ncols), lambda i: (0, 0)),
        ],
        out_specs=[
            pl.BlockSpec((tm, NA_COLS), lambda i: (i, 0)),
            pl.BlockSpec((tm, NB_COLS), lambda i: (i, 0)),
            pl.BlockSpec((tm, nc), lambda i: (i, 0)),
        ],
        out_shape=[
            jax.ShapeDtypeStruct((n, NA_COLS), BF16),
            jax.ShapeDtypeStruct((n, NB_COLS), F32),
            jax.ShapeDtypeStruct((n, nc), F32),
        ],
        compiler_params=_cparams(("parallel",)),
        name="in_proj",
    )(x2d, g, w_pad)


def _pad_w_in(w_in):
    sizes = (D_NSA, D_KV, D_KV, D_KV, D_KV, D_KV, D_KV, 3 * NSA_HEADS, D_RWKV_IN)
    offs = np.cumsum((0,) + sizes)
    q, kc, vc, ks, vs, kw, vw, gl, rw = (w_in[:, offs[i]:offs[i + 1]] for i in range(9))
    def lane_pad(w, groups):
        w = w.reshape(D_MODEL, groups, HEAD_DIM)
        return jnp.pad(w, ((0, 0), (0, 0), (0, LANES - HEAD_DIM))).reshape(D_MODEL, groups * LANES)

    glp = jnp.pad(gl, ((0, 0), (0, LANES - 3 * NSA_HEADS)))
    return jnp.concatenate(
        [lane_pad(q, NSA_HEADS), lane_pad(ks, NSA_KV_HEADS), lane_pad(kw, NSA_KV_HEADS),
         vs, vw, kc, vc, glp, rw], axis=1).astype(BF16)


def _compress_kernel(rk_ref, rv_ref, pek_ref, pev_ref, w1k_ref, w1v_ref, w2k_ref, w2v_ref,
                     ok_ref, ov_ref):
    for r_ref, pe_ref, w1_ref, w2_ref, o_ref in (
            (rk_ref, pek_ref, w1k_ref, w2k_ref, ok_ref),
            (rv_ref, pev_ref, w1v_ref, w2v_ref, ov_ref)):
        rows = r_ref[0]
        nxt = pltpu.roll(rows, rows.shape[0] - 1, axis=0)
        a = (rows + pe_ref[0:1, :]).astype(BF16)
        b = (nxt + pe_ref[1:2, :]).astype(BF16)
        hid = _dot(a, w1_ref[0]) + _dot(b, w1_ref[1])
        out = _dot(_gelu(hid).astype(BF16), w2_ref[...])
        if o_ref is ok_ref:
            nb = rows.shape[0]
            end = lax.broadcasted_iota(jnp.int32, (nb, 1), 0) * CMP_STRIDE + (CMP_LEN - 1)
            lane = lax.broadcasted_iota(jnp.int32, (1, K_AUG), 1) % LANES
            out = out + _key_features(end, lane)
        o_ref[0] = out.astype(BF16)


def _expand_cmp_weights(pe, w1, w2, out_lanes):
    half = CMP_LEN // 2
    eye = jnp.eye(NSA_KV_HEADS, dtype=F32)
    w1r = w1.reshape(2, half, HEAD_DIM, CMP_HIDDEN)
    w1e = w1r[:, :, None, :, None, :] * eye[None, None, :, None, :, None]
    w1e = w1e.reshape(2, half * D_KV, NSA_KV_HEADS * CMP_HIDDEN).astype(BF16)
    pee = jnp.broadcast_to(pe.reshape(2, half, 1, HEAD_DIM), (2, half, NSA_KV_HEADS, HEAD_DIM))
    pee = pee.reshape(2, half * D_KV)
    w2p = jnp.pad(w2, ((0, 0), (0, out_lanes - HEAD_DIM)))
    w2e = (eye[:, None, :, None] * w2p[None, :, None, :]).reshape(
        NSA_KV_HEADS * CMP_HIDDEN, NSA_KV_HEADS * out_lanes).astype(BF16)
    return pee, w1e, w2e


def _compress(rk, rv, wk, wv):
    b, nb, width = rk.shape
    full2 = lambda a: pl.BlockSpec(a.shape, lambda i: (0, 0))
    full3 = lambda a: pl.BlockSpec(a.shape, lambda i: (0, 0, 0))
    row = pl.BlockSpec((1, nb, width), lambda i: (i, 0, 0))
    out = lambda lanes: pl.BlockSpec((1, nb, lanes), lambda i: (i, 0, 0))
    return pl.pallas_call(
        _compress_kernel,
        grid=(b,),
        in_specs=[row, row, full2(wk[0]), full2(wv[0]), full3(wk[1]), full3(wv[1]),
                  full2(wk[2]), full2(wv[2])],
        out_specs=[out(K_AUG), out(D_KV)],
        out_shape=[jax.ShapeDtypeStruct((b, nb, K_AUG), BF16),
                   jax.ShapeDtypeStruct((b, nb, D_KV), BF16)],
        compiler_params=_cparams(("parallel",)),
        name="nsa_compress",
    )(rk, rv, wk[0], wv[0], wk[1], wv[1], wk[2], wv[2])


NSA_TQ = 128
NSA_TK = 256
NSA_TKW = 128


def _masked_softmax(s, mask):
    s = jnp.where(mask, s, NEG_BIG)
    s = s - jnp.max(s, axis=-1, keepdims=True)
    p = jnp.where(mask, jnp.exp(s), 0.0)
    return p / jnp.maximum(jnp.sum(p, axis=-1, keepdims=True), 1e-30)


def _flash_update(carry, s, v_t):
    m, l, acc = carry
    m_new = jnp.maximum(m, jnp.max(s, axis=0, keepdims=True))
    alpha = jnp.exp(m - m_new)
    p = jnp.exp(s - m_new)
    l = alpha * l + jnp.sum(p, axis=0, keepdims=True)
    acc = alpha * acc + _dot(v_t, p.astype(BF16))
    return m_new, l, acc


def _nsa_kernel(q_ref, ks_ref, kw_ref, vst_ref, vwt_ref, ck_ref, cv_ref, ovl_ref, oh_ref, gl_ref,
                eg_ref, o_ref):
    tq = NSA_TQ
    R = NSA_GROUP
    i = pl.program_id(1)
    q0 = i * tq
    t_col = q0 + lax.broadcasted_iota(jnp.int32, (tq, 1), 0)
    t_lane = q0 + lax.broadcasted_iota(jnp.int32, (1, tq), 1)
    t_lanes = jnp.concatenate([t_lane] * R, axis=1)
    n_cmp = ck_ref.shape[1]
    n_sel = ovl_ref.shape[0]

    gate = _sigmoid(gl_ref[...])
    g_hi, g_lo = _split2(gate)
    gexp = [_dot(g_hi, eg_ref[j]) + _dot(g_lo, eg_ref[j]) for j in range(3)]

    cmp_end = lax.broadcasted_iota(jnp.int32, (1, n_cmp), 1) * CMP_STRIDE + (CMP_LEN - 1)
    mask_c = t_col >= cmp_end
    ids = lax.broadcasted_iota(jnp.int32, (n_sel, 1), 0)
    idsf = ids.astype(F32)
    cur = t_lane // SEL_LEN
    forced = (ids == 0) | (ids == cur) | (ids == cur - 1)
    valid = ids * SEL_LEN <= t_lane

    G = NSA_KV_HEADS
    lanes_of = [slice(g * LANES, (g + 1) * LANES) for g in range(G)]
    q_st, q_slc, o_cmp = [], [], []
    for g in range(G):
        qh = [q_ref[:, h * LANES:(h + 1) * LANES] for h in range(g * R, (g + 1) * R)]
        ck = ck_ref[0, :, lanes_of[g]]
        cv = cv_ref[0]

        imp = jnp.zeros((n_sel, tq), F32)
        for r in range(R):
            p = _masked_softmax(_dot_nt(qh[r], ck), mask_c)
            o_cmp.append(_dot(p.astype(BF16), cv))
            p_hi, p_lo = _split2(p)
            imp = imp + _dot_nt(ovl_ref[...], p_hi) + _dot_nt(ovl_ref[...], p_lo)
        imp = jnp.where(forced, 1e6, jnp.where(valid, imp, -1.0))

        sel = jnp.zeros((n_sel, tq), F32)
        for _ in range(min(SEL_TOPN, n_sel)):
            m = jnp.max(imp, axis=0, keepdims=True)
            first = jnp.min(jnp.where(imp == m, idsf, float(n_sel)), axis=0, keepdims=True)
            hit = idsf == first
            sel = jnp.where(hit, 1.0, sel)
            imp = jnp.where(hit, -3e38, imp)
        bias_t = jnp.where(sel > 0.5, 0.0, NEG_BIG)
        if n_sel < LANES:
            bias_t = jnp.concatenate([bias_t, jnp.zeros((LANES - n_sel, tq), F32)], axis=0)
        bias = bias_t.T.astype(BF16)

        q_st.append(jnp.concatenate(qh, axis=0))
        q_slc.append(jnp.concatenate([q_st[g], jnp.concatenate([bias] * R, axis=0)], axis=1))

    init = tuple((jnp.full((1, R * tq), -3e38, F32), jnp.zeros((1, R * tq), F32),
                  jnp.zeros((LANES, R * tq), F32)) for _ in range(G))

    def slc_tile(j, carry, diagonal):
        k0 = pl.multiple_of(j * NSA_TK, NSA_TK)
        onehot = oh_ref[pl.ds(k0, NSA_TK), :]
        v_t = vst_ref[0, :, pl.ds(k0, NSA_TK)]
        out = []
        for g in range(G):
            k = jnp.concatenate([ks_ref[pl.ds(k0, NSA_TK), lanes_of[g]], onehot], axis=1)
            s = _dot_nt(k, q_slc[g])
            if diagonal:
                pos = k0 + lax.broadcasted_iota(jnp.int32, (NSA_TK, 1), 0)
                s = jnp.where(t_lanes >= pos, s, NEG_BIG)
            out.append(_flash_update(carry[g], s, v_t))
        return tuple(out)

    n_off = q0 // NSA_TK
    carry = lax.fori_loop(0, n_off, functools.partial(slc_tile, diagonal=False), init)
    slc = slc_tile(n_off, carry, True)

    def win_tile(j, carry, mode):
        k0 = pl.multiple_of(j * NSA_TKW, NSA_TKW)
        v_t = vwt_ref[0, :, pl.ds(k0, NSA_TKW)]
        pos = k0 + lax.broadcasted_iota(jnp.int32, (NSA_TKW, 1), 0)
        out = []
        for g in range(G):
            s = _dot_nt(kw_ref[pl.ds(k0, NSA_TKW), lanes_of[g]], q_st[g])
            if mode == "first":
                s = jnp.where(pos > t_lanes - WINDOW, s, NEG_BIG)
            elif mode == "diagonal":
                s = jnp.where(t_lanes >= pos, s, NEG_BIG)
            out.append(_flash_update(carry[g], s, v_t))
        return tuple(out)

    nw = WINDOW // NSA_TKW
    first_lo = jnp.maximum(i - nw, 0)
    first_hi = jnp.where(i >= nw, first_lo + 1, first_lo)
    carry = lax.fori_loop(first_lo, first_hi, functools.partial(win_tile, mode="first"), init)
    carry = lax.fori_loop(jnp.maximum(i - nw + 1, 0), i,
                          functools.partial(win_tile, mode="full"), carry)
    win = win_tile(i, carry, "diagonal")

    for h in range(NSA_HEADS):
        g, r = divmod(h, R)
        cols = slice(h * LANES, (h + 1) * LANES)
        rows = slice(r * tq, (r + 1) * tq)
        o_slc = (slc[g][2][:, rows] / slc[g][1][:, rows]).T
        o_win = (win[g][2][:, rows] / win[g][1][:, rows]).T
        out = gexp[0][:, cols] * o_cmp[h] + gexp[1][:, cols] * o_slc + gexp[2][:, cols] * o_win
        o_ref[:, cols] = out.astype(BF16)


def _nsa_consts(t_len, n_cmp_pad):
    n_cmp = (t_len - CMP_LEN) // CMP_STRIDE + 1
    n_sel = t_len // SEL_LEN
    cmp_start = np.arange(n_cmp) * CMP_STRIDE
    sel_start = np.arange(n_sel) * SEL_LEN
    ovl = np.clip(np.minimum(cmp_start[:, None] + CMP_LEN, sel_start[None, :] + SEL_LEN)
                  - np.maximum(cmp_start[:, None], sel_start[None, :]), 0, None) / CMP_LEN
    ovl_t = np.zeros((n_sel, n_cmp_pad), np.float32)
    ovl_t[:, :n_cmp] = ovl.T
    onehot = np.zeros((t_len, LANES), np.float32)
    onehot[np.arange(t_len), np.arange(t_len) // SEL_LEN] = 1.0
    eg = np.zeros((3, LANES, Q_EXP), np.float32)
    for h in range(NSA_HEADS):
        for j in range(3):
            eg[j, h * 3 + j, h * LANES:(h + 1) * LANES] = 1.0
    return jnp.asarray(ovl_t, BF16), jnp.asarray(onehot, BF16), jnp.asarray(eg, BF16)


def _nsa_attention(oa, ob, vs_t, vw_t, cmp_k, cmp_v, b, t_len):
    assert NSA_TQ == NSA_TKW and NSA_TK % NSA_TQ == 0 and t_len // SEL_LEN <= LANES
    n_cmp_pad = cmp_k.shape[1]
    ovl_t, onehot, eg = _nsa_consts(t_len, n_cmp_pad)
    nq = t_len // NSA_TQ
    k_col0 = Q_EXP // K_AUG
    kspec = lambda c: pl.BlockSpec((t_len, K_AUG), lambda bi, i: (bi, k_col0 + c))
    vspec = pl.BlockSpec((1, D_KV, t_len), lambda bi, i: (bi, 0, 0))
    const2 = lambda a: pl.BlockSpec(a.shape, lambda bi, i: (0, 0))
    return pl.pallas_call(
        _nsa_kernel,
        grid=(b, nq),
        in_specs=[
            pl.BlockSpec((NSA_TQ, Q_EXP), lambda bi, i: (bi * nq + i, 0)),
            kspec(0), kspec(1), vspec, vspec,
            pl.BlockSpec((1, n_cmp_pad, K_AUG), lambda bi, i: (bi, 0, 0)),
            pl.BlockSpec((1, n_cmp_pad, D_KV), lambda bi, i: (bi, 0, 0)),
            const2(ovl_t), const2(onehot),
            pl.BlockSpec((NSA_TQ, LANES), lambda bi, i: (bi * nq + i, 2)),
            pl.BlockSpec(eg.shape, lambda bi, i: (0, 0, 0)),
        ],
        out_specs=pl.BlockSpec((NSA_TQ, Q_EXP), lambda bi, i: (bi * nq + i, 0)),
        out_shape=jax.ShapeDtypeStruct((b * t_len, Q_EXP), BF16),
        compiler_params=_cparams(("parallel", "arbitrary")),
        name="nsa_attention",
    )(oa, oa, oa, vs_t, vw_t, cmp_k, cmp_v, ovl_t, onehot, ob, eg)


def _dot_x3(x, e):
    hi = x.astype(BF16)
    r1 = x - hi.astype(F32)
    mid = r1.astype(BF16)
    lo = (r1 - mid.astype(F32)).astype(BF16)
    return _dot(hi, e) + _dot(mid, e) + _dot(lo, e)


def _head_sum_matrix():
    ids = np.arange(D_RWKV) // HEAD_DIM
    return jnp.asarray(ids[:, None] == ids[None, :], BF16)


def _rwkv_pre_kernel(p_ref, hp_ref, mu_ref, w0_ref, w2_ref, a0_ref, a2_ref, g2_ref, kk_ref,
                     ka_ref, rk_ref, bd_ref, r_o, lw_o, k_o, kkn_o, a_o, vt_o, bv_o, g_o, *,
                     tiles_per_seq):
    i = pl.program_id(0)
    p = p_ref[...]
    tm = p.shape[0]
    keep = jnp.where(i % tiles_per_seq == 0, 0.0, 1.0)
    halo = hp_ref[SUBLANES - 1:SUBLANES, :] * keep
    prev = pltpu.roll(p, 1, axis=0)
    row0 = lax.broadcasted_iota(jnp.int32, (tm, 1), 0) == 0
    prev = jnp.where(row0, halo, prev)
    ps = p + (prev - p) * mu_ref[...]
    d = D_RWKV
    r = ps[:, 0:d]
    k = ps[:, d:2 * d]
    v = ps[:, 2 * d:3 * d]
    xw = ps[:, 3 * d:3 * d + RANK_W]
    xa = ps[:, 3 * d + RANK_W:3 * d + RANK_W + RANK_A]
    xg = ps[:, 3 * d + RANK_W + RANK_A:]
    z = -(w0_ref[...] + _dot(jnp.tanh(xw).astype(BF16), w2_ref[...]))
    softplus = jnp.maximum(z, 0.0) + jnp.log(1.0 + jnp.exp(-jnp.abs(z)))
    w = -softplus - 0.5
    a = _sigmoid(a0_ref[...] + _dot(xa.astype(BF16), a2_ref[...]))
    g = _dot(_sigmoid(xg).astype(BF16), g2_ref[...])
    kk = k * kk_ref[...]
    ss = _dot_x3(kk * kk, bd_ref[...])
    kk = kk / jnp.maximum(jnp.sqrt(ss), 1e-12)
    k_mod = k * (1.0 + (a - 1.0) * ka_ref[...])
    for o_ref, val in ((r_o, r), (lw_o, -jnp.exp(w)), (k_o, k_mod), (kkn_o, kk), (a_o, a)):
        for hd in range(RWKV_HEADS):
            o_ref[0, hd] = val[:, hd * HEAD_DIM:(hd + 1) * HEAD_DIM]
    vt_o[0] = v.T.reshape(RWKV_HEADS, HEAD_DIM, tm)
    bv_o[...] = _dot_x3(r * k_mod * rk_ref[...], bd_ref[...]) * v
    g_o[...] = g


def _rwkv_pre(oc, b, t_len, mu, w0, w2, a0, a2, g2, k_k, k_a, r_k, bd, tm=256):
    n = oc.shape[0]
    tps = t_len // tm
    halo_blocks = tm // SUBLANES
    full = lambda a: pl.BlockSpec(a.shape, lambda i: (0, 0))
    tok = pl.BlockSpec((tm, D_RWKV), lambda i: (i, 0))
    hm = pl.BlockSpec((1, RWKV_HEADS, tm, HEAD_DIM), lambda i: (i // tps, 0, i % tps, 0))
    hm_t = pl.BlockSpec((1, RWKV_HEADS, HEAD_DIM, tm), lambda i: (i // tps, 0, 0, i % tps))
    params = (mu, w0, w2, a0, a2, g2, k_k, k_a, r_k, bd)
    hm_shape = jax.ShapeDtypeStruct((b, RWKV_HEADS, t_len, HEAD_DIM), F32)
    return pl.pallas_call(
        functools.partial(_rwkv_pre_kernel, tiles_per_seq=tps),
        grid=(n // tm,),
        in_specs=[
            pl.BlockSpec((tm, D_RWKV_IN), lambda i: (i, 0)),
            pl.BlockSpec((SUBLANES, D_RWKV_IN),
                         lambda i: (jnp.maximum(i * halo_blocks - 1, 0), 0)),
        ] + [full(a) for a in params],
        out_specs=[hm] * 5 + [hm_t, tok, tok],
        out_shape=[hm_shape] * 5
        + [jax.ShapeDtypeStruct((b, RWKV_HEADS, HEAD_DIM, t_len), F32)]
        + [jax.ShapeDtypeStruct((n, D_RWKV), F32)] * 2,
        compiler_params=_cparams(("parallel",)),
        name="rwkv_pre",
    )(oc, oc, *params)


RWKV_C = 64
RWKV_CB = 128
RWKV_PASSES = 1


def _bmm(eq, a, b):
    ein = lambda x, y: jnp.einsum(eq, x, y, preferred_element_type=F32)
    if RWKV_PASSES == 1:
        return ein(a.astype(BF16), b.astype(BF16))
    ah, al = _split2(a)
    bh, bl = _split2(b)
    return ein(ah, bh) + ein(ah, bl) + ein(al, bh)


def _cumsum_rows(x2d, seg):
    rows = lax.broadcasted_iota(jnp.int32, (x2d.shape[0], 1), 0) % seg
    step = 1
    while step < seg:
        shifted = pltpu.roll(x2d, step, axis=0)
        x2d = x2d + jnp.where(rows >= step, shifted, 0.0)
        step *= 2
    return x2d


def _rwkv_rec_kernel(r_ref, lw_ref, k_ref, kk_ref, a_ref, vt_ref, ot_ref, s_ref):
    H = r_ref.shape[1]
    C = RWKV_C

    @pl.when(pl.program_id(1) == 0)
    def _():
        s_ref[...] = jnp.zeros_like(s_ref)

    ri = lax.broadcasted_iota(jnp.int32, (C, C), 0)
    ci = lax.broadcasted_iota(jnp.int32, (C, C), 1)
    strict = (ri > ci)[None]
    incl = (ri >= ci)[None]
    eye = (ri == ci).astype(F32)[None]

    for sub in range(RWKV_CB // C):
        rows = slice(sub * C, (sub + 1) * C)
        r = r_ref[0, :, rows, :]
        lw = lw_ref[0, :, rows, :]
        k = k_ref[0, :, rows, :]
        kk = kk_ref[0, :, rows, :]
        a = a_ref[0, :, rows, :]
        vt = vt_ref[0, :, :, rows]
        s0 = s_ref[...]

        cum = _cumsum_rows(lw.reshape(H * C, HEAD_DIM), C).reshape(H, C, HEAD_DIM)
        cum_last = cum[:, C - 1:C, :]
        p_inv = jnp.exp(-cum)
        w_last = jnp.exp(cum_last - cum)
        bm = kk * a
        at = -kk * jnp.exp(cum - lw)
        rt = r * jnp.exp(cum)
        bt = bm * p_inv
        kt = k * p_inv

        nt = 'hik,hjk->hij'
        m_ab = jnp.where(strict, _bmm(nt, at, bt), 0.0)
        m_ak = jnp.where(strict, _bmm(nt, at, kt), 0.0)
        n_rb = jnp.where(incl, _bmm(nt, rt, bt), 0.0)
        n_rk = jnp.where(incl, _bmm(nt, rt, kt), 0.0)

        tinv = eye + m_ab
        mp = m_ab
        step = 1
        while 2 * step < C:
            mp = _bmm('hij,hjk->hik', mp, mp)
            tinv = tinv + _bmm('hij,hjk->hik', tinv, mp)
            step *= 2

        rhs_t = _bmm('hvk,hik->hvi', s0, at) + _bmm('hvj,hij->hvi', vt, m_ak)
        ut = _bmm('hvj,hij->hvi', rhs_t, tinv)
        ot = (_bmm('hvk,hik->hvi', s0, rt) + _bmm('hvj,hij->hvi', ut, n_rb)
              + _bmm('hvj,hij->hvi', vt, n_rk))
        ot_ref[0, :, :, rows] = ot
        s_ref[...] = (s0 * jnp.exp(cum_last) + _bmm('hvj,hjk->hvk', ut, bm * w_last)
                      + _bmm('hvj,hjk->hvk', vt, k * w_last))


def _rwkv_recurrence(r, lw, k, kk, a, vt):
    b, h, t_len, d = r.shape
    tok = pl.BlockSpec((1, h, RWKV_CB, d), lambda bi, c: (bi, 0, c, 0))
    tr = pl.BlockSpec((1, h, d, RWKV_CB), lambda bi, c: (bi, 0, 0, c))
    return pl.pallas_call(
        _rwkv_rec_kernel,
        grid=(b, t_len // RWKV_CB),
        in_specs=[tok] * 5 + [tr],
        out_specs=tr,
        out_shape=jax.ShapeDtypeStruct((b, h, d, t_len), F32),
        scratch_shapes=[pltpu.VMEM((h, d, d), F32)],
        compiler_params=_cparams(("parallel", "arbitrary")),
        name="rwkv_recurrence",
    )(r, lw, k, kk, a, vt)


def _rwkv_post_kernel(ot_ref, bv_ref, g_ref, lnw_ref, lnb_ref, bd_ref, y_ref):
    tm = bv_ref.shape[0]
    o = ot_ref[0].reshape(D_RWKV, tm).T
    bd = bd_ref[...]
    inv = 1.0 / HEAD_DIM
    mean = _dot_x3(o, bd) * inv
    d = o - mean
    var = _dot_x3(d * d, bd) * inv
    on = d * lax.rsqrt(var + GN_EPS) * lnw_ref[...] + lnb_ref[...]
    y_ref[...] = ((on + bv_ref[...]) * g_ref[...]).astype(BF16)


def _rwkv_post(ot, bv, g, ln_w, ln_b, bd, tm=512):
    n = bv.shape[0]
    tps = ot.shape[-1] // tm
    tile = pl.BlockSpec((tm, D_RWKV), lambda i: (i, 0))
    hm_t = pl.BlockSpec((1, RWKV_HEADS, HEAD_DIM, tm), lambda i: (i // tps, 0, 0, i % tps))
    full = lambda a: pl.BlockSpec(a.shape, lambda i: (0, 0))
    return pl.pallas_call(
        _rwkv_post_kernel,
        grid=(n // tm,),
        in_specs=[hm_t, tile, tile, full(ln_w), full(ln_b), full(bd)],
        out_specs=tile,
        out_shape=jax.ShapeDtypeStruct((n, D_RWKV), BF16),
        compiler_params=_cparams(("parallel",)),
        name="rwkv_post",
    )(ot, bv, g, ln_w, ln_b, bd)


def _rwkv_mixer(oc, b, t_len, mu, w0, w2, a0, a2, g2, k_k, k_a, r_k, ln_w, ln_b):
    bd = _head_sum_matrix()
    row = lambda z: z.reshape(1, -1)
    r, lw, k, kk, a, vt, bv, g = _rwkv_pre(
        oc, b, t_len, row(mu), row(w0), w2.astype(BF16), row(a0), a2.astype(BF16),
        g2.astype(BF16), row(k_k), row(k_a), row(r_k), bd)
    ot = _rwkv_recurrence(r, lw, k, kk, a, vt)
    return _rwkv_post(ot, bv, g, row(ln_w), row(ln_b), bd)


def _out_proj_kernel(x_ref, yn_ref, yr_ref, wn_ref, wr_ref, g_ref, x1_ref, h2_ref):
    x1 = x_ref[...] + _dot(yn_ref[...], wn_ref[...]) + _dot(yr_ref[...], wr_ref[...])
    x1_ref[...] = x1
    h2_ref[...] = x1 * lax.rsqrt(jnp.mean(x1 * x1, axis=-1, keepdims=True) + EPS) * g_ref[...]


def _expand_w_out(w_out):
    wn = w_out[:D_NSA].reshape(NSA_HEADS, 1, HEAD_DIM, D_MODEL)
    onehot = (np.arange(NSA_HEADS)[:, None] // NSA_GROUP == np.arange(NSA_KV_HEADS)[None, :])
    wn = wn * jnp.asarray(onehot, F32)[:, :, None, None]
    return wn.reshape(Q_EXP, D_MODEL).astype(BF16), w_out[D_NSA:].astype(BF16)


def _out_proj(x2d, y_nsa, y_rwkv, wn, wr, g, tm=512):
    n = x2d.shape[0]
    tile = lambda a: pl.BlockSpec((tm, a.shape[1]), lambda i: (i, 0))
    full = lambda a: pl.BlockSpec(a.shape, lambda i: (0, 0))
    return pl.pallas_call(
        _out_proj_kernel,
        grid=(n // tm,),
        in_specs=[tile(x2d), tile(y_nsa), tile(y_rwkv), full(wn), full(wr), full(g)],
        out_specs=[tile(x2d), tile(x2d)],
        out_shape=[jax.ShapeDtypeStruct((n, D_MODEL), F32)] * 2,
        compiler_params=_cparams(("parallel",)),
        name="out_proj",
    )(x2d, y_nsa, y_rwkv, wn, wr, g)


def _topk_rows(s, k, rid):
    vals, ids = [], []
    for _ in range(k):
        m = jnp.max(s, axis=0, keepdims=True)
        first = jnp.min(jnp.where(s == m, rid, jnp.inf), axis=0, keepdims=True)
        vals.append(m)
        ids.append(first)
        s = jnp.where(rid == first, -jnp.inf, s)
    return jnp.concatenate(vals, axis=0), jnp.concatenate(ids, axis=0)


def _take_rows(table, idx):
    rows = lax.broadcasted_iota(jnp.int32, (table.shape[0], 1), 0).astype(F32)
    out = [jnp.sum(jnp.where(rows == idx[r:r + 1], table, 0.0), axis=0, keepdims=True)
           for r in range(idx.shape[0])]
    return jnp.concatenate(out, axis=0)


def _pair_candidates():
    k = PEER_TOPK
    flat = [0 * k + j for j in range(k)]
    for i in range(1, SUBLANES):
        flat += [i * k + j for j in range(SUBLANES)]
    flat += [i * k for i in range(SUBLANES, k)]
    return np.asarray(flat, np.float32).reshape(-1, 1)


def _peer_route_kernel(h_ref, wq_ref, sk_ref, flat_ref, e_ref, g_ref):
    K = PEER_TOPK
    half = PEER_D_QUERY // 2
    q = _dot(h_ref[...].astype(BF16), wq_ref[...]).astype(BF16)
    key_ids = lax.broadcasted_iota(jnp.int32, (PEER_N_KEYS, 1), 0).astype(F32)
    flat = flat_ref[...]
    rows_e, rows_g = [], []
    for h in range(PEER_HEADS):
        top = []
        for c in range(2):
            col = (h * 2 + c) * half
            s_t = _dot_nt(sk_ref[h, c], q[:, col:col + half])
            top.append(_topk_rows(s_t, K, key_ids))
        (v0, i0), (v1, i1) = top
        cand = jnp.concatenate(
            [v0[0:1] + v1]
            + [v0[i:i + 1] + v1[:SUBLANES] for i in range(1, SUBLANES)]
            + [v0[SUBLANES:] + v1[0:1]], axis=0)
        best, pair = _topk_rows(cand, K, flat)
        pi = jnp.floor(pair * (1.0 / K))
        pj = pair - pi * K
        experts = _take_rows(i0, pi) * float(PEER_N_KEYS) + _take_rows(i1, pj)
        p = jnp.exp(best - jnp.max(best, axis=0, keepdims=True))
        rows_e.append(experts * float(HALF_ROWS))
        rows_g.append(p / jnp.sum(p, axis=0, keepdims=True))
    e_ref[...] = jnp.concatenate(rows_e, axis=0).T.astype(jnp.int32)
    g_ref[...] = jnp.concatenate(rows_g, axis=0).T


def _peer_route(h2, wq, sk, tm=256):
    n = h2.shape[0]
    hk = PEER_HEADS * PEER_TOPK
    flat = jnp.asarray(_pair_candidates())
    out = pl.BlockSpec((tm, hk), lambda i: (i, 0))
    return pl.pallas_call(
        _peer_route_kernel,
        grid=(n // tm,),
        in_specs=[
            pl.BlockSpec((tm, D_MODEL), lambda i: (i, 0)),
            pl.BlockSpec(wq.shape, lambda i: (0, 0)),
            pl.BlockSpec(sk.shape, lambda i: (0, 0, 0, 0)),
            pl.BlockSpec(flat.shape, lambda i: (0, 0)),
        ],
        out_specs=[out, out],
        out_shape=[jax.ShapeDtypeStruct((n, hk), jnp.int32), jax.ShapeDtypeStruct((n, hk), F32)],
        compiler_params=_cparams(("parallel",)),
        name="peer_route",
    )(h2, wq, sk, flat)


PEER_TT = 64
HALF_ROWS = SUBLANES // 2
SUB_ORDER = (0, 4, 2, 6, 1, 5, 3, 7)
HIGH_MASK = 0xFFFF0000


def _pack_table(w):
    bits = lax.bitcast_convert_type(w.astype(BF16), jnp.uint16).astype(jnp.uint32)
    half = w.shape[1] // 2
    packed = bits[:, :half] | (bits[:, half:] << 16)
    return packed.reshape(w.shape[0] * HALF_ROWS, LANES)


def _table_row(tbl_ref, row0):
    return tbl_ref[pl.ds(pl.multiple_of(row0, HALF_ROWS), HALF_ROWS), :]


def _unpack_words(word):
    lo = pltpu.bitcast(word << 16, F32)
    hi = pltpu.bitcast(word & jnp.uint32(HIGH_MASK), F32)
    return lo, hi


def _sublane_tree(c):
    sub = lax.broadcasted_iota(jnp.int32, (SUBLANES, LANES), 0)
    m2 = (sub % 4) < 2
    d = []
    for x, y in zip(c[0::2], c[1::2]):
        d.append(jnp.where(m2, x + pltpu.roll(x, 6, axis=0), y + pltpu.roll(y, 2, axis=0)))
    m1 = (sub % 2) == 0
    x, y = d
    return jnp.where(m1, x + pltpu.roll(x, 7, axis=0), y + pltpu.roll(y, 1, axis=0))


def _peer_act_kernel(e_ref, h_ref, tbl_ref, gate_ref, ones_ref, w_ref, part_ref):
    hk = PEER_HEADS * PEER_TOPK
    tt = h_ref.shape[0]

    def token(t, carry):
        hrow = h_ref[t]
        h_lo = jnp.concatenate([hrow[:HALF_ROWS]] * 2, axis=0)
        h_hi = jnp.concatenate([hrow[HALF_ROWS:]] * 2, axis=0)
        for m in range(hk // SUBLANES):
            es = [e_ref[t, m * SUBLANES + j] for j in SUB_ORDER]
            c = []
            for ea, eb in zip(es[0::2], es[1::2]):
                lo, hi = _unpack_words(jnp.concatenate(
                    [_table_row(tbl_ref, ea), _table_row(tbl_ref, eb)], axis=0))
                c.append(lo * h_lo + hi * h_hi)
            row0 = pl.multiple_of((t * (hk // SUBLANES) + m) * SUBLANES, SUBLANES)
            part_ref[pl.ds(row0, SUBLANES), :] = _sublane_tree(c)
        return carry

    lax.fori_loop(0, tt, token, 0)
    hi, lo = _split2(part_ref[...])
    sums = _dot(jnp.concatenate([hi, lo], axis=1), ones_ref[...])
    sums = sums.reshape(tt, hk, LANES)
    eye = (lax.broadcasted_iota(jnp.int32, (hk, LANES), 0)
           == lax.broadcasted_iota(jnp.int32, (hk, LANES), 1))
    act = jnp.sum(jnp.where(eye[None], sums, 0.0), axis=1)
    w_ref[...] = _gelu(act) * gate_ref[...]


def _table_spec(tbl):
    return pl.BlockSpec(tbl.shape, lambda i: (0, 0), pipeline_mode=pl.Buffered(1))


def _peer_act(experts, h2_tiles, tbl, gates):
    n, hk = experts.shape
    ones = jnp.ones((2 * LANES, LANES), BF16)
    return pl.pallas_call(
        _peer_act_kernel,
        grid=(n // PEER_TT,),
        in_specs=[
            pl.BlockSpec((PEER_TT, hk), lambda i: (i, 0), memory_space=pltpu.SMEM),
            pl.BlockSpec((PEER_TT, SUBLANES, LANES), lambda i: (i, 0, 0)),
            _table_spec(tbl),
            pl.BlockSpec((PEER_TT, hk), lambda i: (i, 0)),
            pl.BlockSpec(ones.shape, lambda i: (0, 0)),
        ],
        out_specs=pl.BlockSpec((PEER_TT, hk), lambda i: (i, 0)),
        out_shape=jax.ShapeDtypeStruct((n, hk), F32),
        scratch_shapes=[pltpu.VMEM((PEER_TT * hk, LANES), F32)],
        compiler_params=_cparams(("arbitrary",)),
        name="peer_act",
    )(experts, h2_tiles, tbl, gates, ones)


PEER_ACCS = 4
PEER_GROUP = 64


def _peer_out_kernel(e_ref, w_ref, x_ref, tbl_ref, g_ref, o_ref, *, final_norm):
    hk = PEER_HEADS * PEER_TOPK
    tt = x_ref.shape[0]

    def token(t, carry):
        def group(gi, accs):
            acc_lo, acc_hi = list(accs[0]), list(accs[1])
            base = t * hk + gi * PEER_GROUP
            for j in range(PEER_GROUP):
                lo, hi = _unpack_words(_table_row(tbl_ref, e_ref[base + j]))
                wk = w_ref[base + j]
                acc_lo[j % PEER_ACCS] = acc_lo[j % PEER_ACCS] + wk * lo
                acc_hi[j % PEER_ACCS] = acc_hi[j % PEER_ACCS] + wk * hi
            return tuple(acc_lo), tuple(acc_hi)

        zeros = tuple(jnp.zeros((HALF_ROWS, LANES), F32) for _ in range(PEER_ACCS))
        acc_lo, acc_hi = lax.fori_loop(0, hk // PEER_GROUP, group, (zeros, zeros))
        tree = lambda a: (a[0] + a[1]) + (a[2] + a[3])
        o_ref[t] = x_ref[t] + jnp.concatenate([tree(acc_lo), tree(acc_hi)], axis=0)
        return carry

    lax.fori_loop(0, tt, token, 0)
    if final_norm:
        x2 = o_ref[...]
        ms = jnp.sum(jnp.sum(x2 * x2, axis=2, keepdims=True), axis=1, keepdims=True) / D_MODEL
        o_ref[...] = x2 * lax.rsqrt(ms + EPS) * g_ref[...]


def _peer_out(experts, w, x1_tiles, tbl, final_g, final_norm):
    n, hk = experts.shape
    smem = pl.BlockSpec((PEER_TT * hk,), lambda i: (i,), memory_space=pltpu.SMEM)
    tile = pl.BlockSpec((PEER_TT, SUBLANES, LANES), lambda i: (i, 0, 0))
    return pl.pallas_call(
        functools.partial(_peer_out_kernel, final_norm=final_norm),
        grid=(n // PEER_TT,),
        in_specs=[smem, smem, tile, _table_spec(tbl),
                  pl.BlockSpec(final_g.shape, lambda i: (0, 0, 0))],
        out_specs=tile,
        out_shape=jax.ShapeDtypeStruct(x1_tiles.shape, F32),
        compiler_params=_cparams(("arbitrary",)),
        name="peer_out",
    )(experts.reshape(-1), w.reshape(-1), x1_tiles, tbl, final_g)


def _peer_ffn_residual(x1, h2, wq, sk, tbl_u, tbl_v, final_g, final_norm):
    n = x1.shape[0]
    experts, gates = _peer_route(h2, wq, sk)
    tiles = lambda z: z.reshape(n, SUBLANES, LANES)
    w = _peer_act(experts, tiles(h2), tbl_u, gates)
    g_tile = final_g.reshape(1, SUBLANES, LANES)
    return _peer_out(experts, w, tiles(x1), tbl_v, g_tile, final_norm).reshape(n, D_MODEL)


def kernel(x, norm_mix_g, w_in, cmp_k_pe, cmp_k_w1, cmp_k_w2, cmp_v_pe, cmp_v_w1, cmp_v_w2,
           rwkv_mu, rwkv_w0, rwkv_w2, rwkv_a0, rwkv_a2, rwkv_g2, rwkv_k_k, rwkv_k_a, rwkv_r_k,
           rwkv_ln_w, rwkv_ln_b, w_out, norm_ffn_g, peer_w_q, peer_sub_keys, peer_u, peer_v,
           norm_final_g):
    b, t_len, _ = x.shape
    n = b * t_len
    row = lambda z: z.reshape(1, -1)
    x2d = x.reshape(n, D_MODEL)
    depth = w_in.shape[0]
    for l in range(depth):
        oa, ob, oc = _in_proj(x2d, row(norm_mix_g[l]), _pad_w_in(w_in[l]), t_len)
        nb = t_len // CMP_STRIDE
        rk = ob[:, 0:D_KV].reshape(b, nb, CMP_STRIDE * D_KV)
        rv = ob[:, D_KV:2 * D_KV].reshape(b, nb, CMP_STRIDE * D_KV)
        cmp_k, cmp_v = _compress(
            rk, rv, _expand_cmp_weights(cmp_k_pe[l], cmp_k_w1[l], cmp_k_w2[l], LANES),
            _expand_cmp_weights(cmp_v_pe[l], cmp_v_w1[l], cmp_v_w2[l], HEAD_DIM))
        v_col0 = Q_EXP + 2 * K_AUG
        v_t = lambda c: oa[:, v_col0 + c * D_KV:v_col0 + (c + 1) * D_KV].reshape(
            b, t_len, D_KV).transpose(0, 2, 1)
        y_nsa = _nsa_attention(oa, ob, v_t(0), v_t(1), cmp_k, cmp_v, b, t_len)
        y_rwkv = _rwkv_mixer(oc, b, t_len, rwkv_mu[l], rwkv_w0[l], rwkv_w2[l], rwkv_a0[l],
                             rwkv_a2[l], rwkv_g2[l], rwkv_k_k[l], rwkv_k_a[l], rwkv_r_k[l],
                             rwkv_ln_w[l], rwkv_ln_b[l])
        wn, wr = _expand_w_out(w_out[l])
        x1, h2 = _out_proj(x2d, y_nsa, y_rwkv, wn, wr, row(norm_ffn_g[l]))
        x2d = _peer_ffn_residual(x1, h2, peer_w_q[l].astype(BF16), peer_sub_keys[l].astype(BF16),
                                 _pack_table(peer_u[l]), _pack_table(peer_v[l]),
                                 norm_final_g, final_norm=(l == depth - 1))
    return x2d.reshape(b, t_len, D_MODEL)
```

```python
import functools

import numpy as np
import jax
import jax.numpy as jnp
from jax import lax
from jax.experimental import pallas as pl
from jax.experimental.pallas import tpu as pltpu

F32 = jnp.float32
BF16 = jnp.bfloat16

D_MODEL = 1024
HEAD_DIM = 64
NSA_HEADS = 8
NSA_KV_HEADS = 2
NSA_GROUP = NSA_HEADS // NSA_KV_HEADS
RWKV_HEADS = 8
D_NSA = NSA_HEADS * HEAD_DIM
D_RWKV = RWKV_HEADS * HEAD_DIM
D_KV = NSA_KV_HEADS * HEAD_DIM
CMP_LEN = 32
CMP_STRIDE = 16
CMP_HIDDEN = 128
SEL_LEN = 64
SEL_TOPN = 16
WINDOW = 512
RANK_W = 64
RANK_A = 64
RANK_G = 128
D_RWKV_IN = 3 * D_RWKV + RANK_W + RANK_A + RANK_G
PEER_HEADS = 8
PEER_N_KEYS = 128
PEER_D_QUERY = 256
PEER_TOPK = 16
EPS = 1e-6
GN_EPS = 64e-5

LANES = 128
SUBLANES = 8
VMEM_LIMIT = 56 * 1024 * 1024

NEG_BIG = -1e30
Q_EXP = NSA_HEADS * LANES
K_AUG = NSA_KV_HEADS * LANES
NA_COLS = Q_EXP + 2 * K_AUG + 2 * D_KV
NB_COLS = 3 * LANES
FEAT0 = HEAD_DIM
POS_SPLIT = 64


def _cparams(sem):
    return pltpu.CompilerParams(dimension_semantics=sem, vmem_limit_bytes=VMEM_LIMIT)


def _dot(a, b):
    return jnp.dot(a, b, preferred_element_type=F32)


def _dot_nt(a, b):
    return lax.dot_general(a, b, (((1,), (1,)), ((), ())), preferred_element_type=F32)


def _split2(x):
    hi = x.astype(BF16)
    lo = (x - hi.astype(F32)).astype(BF16)
    return hi, lo


def _dot_x2(x, e):
    hi, lo = _split2(x)
    return _dot(hi, e) + _dot(lo, e)


def _gelu(x):
    return 0.5 * x * (1.0 + jnp.tanh(0.7978845608028654 * (x + 0.044715 * (x * x * x))))


def _sigmoid(x):
    return 1.0 / (1.0 + jnp.exp(-x))


def _key_features(pos, lane):
    hi = (pos // POS_SPLIT).astype(F32)
    lo = (pos % POS_SPLIT).astype(F32)
    return jnp.where(lane == FEAT0, hi, jnp.where(lane == FEAT0 + 1, lo,
                     jnp.where((lane == FEAT0 + 2) | (lane == FEAT0 + 3), 1.0, 0.0)))


def _query_features(t, lane, slope):
    hi = (t // POS_SPLIT).astype(F32)
    lo = (t % POS_SPLIT).astype(F32)
    return jnp.where(lane == FEAT0, slope * POS_SPLIT, jnp.where(
        lane == FEAT0 + 1, slope, jnp.where(
            lane == FEAT0 + 2, -slope * POS_SPLIT * hi, jnp.where(
                lane == FEAT0 + 3, -slope * lo, 0.0))))


def _in_proj_kernel(x_ref, g_ref, w_ref, oa_ref, ob_ref, oc_ref, *, t_len):
    x = x_ref[...]
    tm = x.shape[0]
    h = x * lax.rsqrt(jnp.mean(x * x, axis=-1, keepdims=True) + EPS) * g_ref[...]
    hb = h.astype(BF16)
    t = (pl.program_id(0) * tm + lax.broadcasted_iota(jnp.int32, (tm, 1), 0)) % t_len
    col = lax.broadcasted_iota(jnp.int32, (1, Q_EXP), 1)
    slope = jnp.zeros((1, Q_EXP), F32)
    for hd in range(NSA_HEADS):
        slope = jnp.where(col // LANES == hd, 2.0 ** -(hd + 1), slope)
    q = _dot(hb, w_ref[:, :Q_EXP]) * (HEAD_DIM ** -0.5) + _query_features(t, col % LANES, slope)
    oa_ref[:, :Q_EXP] = q.astype(BF16)
    colk = lax.broadcasted_iota(jnp.int32, (1, 2 * K_AUG), 1)
    k = _dot(hb, w_ref[:, Q_EXP:Q_EXP + 2 * K_AUG]) + _key_features(t, colk % LANES)
    oa_ref[:, Q_EXP:Q_EXP + 2 * K_AUG] = k.astype(BF16)
    oa_ref[:, Q_EXP + 2 * K_AUG:] = _dot(hb, w_ref[:, Q_EXP + 2 * K_AUG:NA_COLS]).astype(BF16)
    ob_ref[...] = _dot(hb, w_ref[:, NA_COLS:NA_COLS + NB_COLS])
    oc_ref[...] = _dot(hb, w_ref[:, NA_COLS + NB_COLS:])


def _in_proj(x2d, g, w_pad, t_len, tm=256):
    n = x2d.shape[0]
    ncols = w_pad.shape[1]
    nc = ncols - NA_COLS - NB_COLS
    return pl.pallas_call(
        functools.partial(_in_proj_kernel, t_len=t_len),
        grid=(n // tm,),
        in_specs=[
            pl.BlockSpec((tm, D_MODEL), lambda i: (i, 0)),
            pl.BlockSpec((1, D_MODEL), lambda i: (0, 0)),
            pl.BlockSpec((D_MODEL, ncols), lambda i: (0, 0)),
        ],
        out_specs=[
            pl.BlockSpec((tm, NA_COLS), lambda i: (i, 0)),
            pl.BlockSpec((tm, NB_COLS), lambda i: (i, 0)),
            pl.BlockSpec((tm, nc), lambda i: (i, 0)),
        ],
        out_shape=[
            jax.ShapeDtypeStruct((n, NA_COLS), BF16),
            jax.ShapeDtypeStruct((n, NB_COLS), F32),
            jax.ShapeDtypeStruct((n, nc), F32),
        ],
        compiler_params=_cparams(("parallel",)),
        name="in_proj",
    )(x2d, g, w_pad)


def _pad_w_in(w_in):
    sizes = (D_NSA, D_KV, D_KV, D_KV, D_KV, D_KV, D_KV, 3 * NSA_HEADS, D_RWKV_IN)
    offs = np.cumsum((0,) + sizes)
    q, kc, vc, ks, vs, kw, vw, gl, rw = (w_in[:, offs[i]:offs[i + 1]] for i in range(9))
    def lane_pad(w, groups):
        w = w.reshape(D_MODEL, groups, HEAD_DIM)
        return jnp.pad(w, ((0, 0), (0, 0), (0, LANES - HEAD_DIM))).reshape(D_MODEL, groups * LANES)

    glp = jnp.pad(gl, ((0, 0), (0, LANES - 3 * NSA_HEADS)))
    return jnp.concatenate(
        [lane_pad(q, NSA_HEADS), lane_pad(ks, NSA_KV_HEADS), lane_pad(kw, NSA_KV_HEADS),
         vs, vw, kc, vc, glp, rw], axis=1).astype(BF16)


def _compress_kernel(rk_ref, rv_ref, pek_ref, pev_ref, w1k_ref, w1v_ref, w2k_ref, w2v_ref,
                     ok_ref, ov_ref):
    for r_ref, pe_ref, w1_ref, w2_ref, o_ref in (
            (rk_ref, pek_ref, w1k_ref, w2k_ref, ok_ref),
            (rv_ref, pev_ref, w1v_ref, w2v_ref, ov_ref)):
        rows = r_ref[0]
        nxt = pltpu.roll(rows, rows.shape[0] - 1, axis=0)
        a = (rows + pe_ref[0:1, :]).astype(BF16)
        b = (nxt + pe_ref[1:2, :]).astype(BF16)
        hid = _dot(a, w1_ref[0]) + _dot(b, w1_ref[1])
        out = _dot(_gelu(hid).astype(BF16), w2_ref[...])
        if o_ref is ok_ref:
            nb = rows.shape[0]
            end = lax.broadcasted_iota(jnp.int32, (nb, 1), 0) * CMP_STRIDE + (CMP_LEN - 1)
            lane = lax.broadcasted_iota(jnp.int32, (1, K_AUG), 1) % LANES
            out = out + _key_features(end, lane)
        o_ref[0] = out.astype(BF16)


def _expand_cmp_weights(pe, w1, w2, out_lanes):
    half = CMP_LEN // 2
    eye = jnp.eye(NSA_KV_HEADS, dtype=F32)
    w1r = w1.reshape(2, half, HEAD_DIM, CMP_HIDDEN)
    w1e = w1r[:, :, None, :, None, :] * eye[None, None, :, None, :, None]
    w1e = w1e.reshape(2, half * D_KV, NSA_KV_HEADS * CMP_HIDDEN).astype(BF16)
    pee = jnp.broadcast_to(pe.reshape(2, half, 1, HEAD_DIM), (2, half, NSA_KV_HEADS, HEAD_DIM))
    pee = pee.reshape(2, half * D_KV)
    w2p = jnp.pad(w2, ((0, 0), (0, out_lanes - HEAD_DIM)))
    w2e = (eye[:, None, :, None] * w2p[None, :, None, :]).reshape(
        NSA_KV_HEADS * CMP_HIDDEN, NSA_KV_HEADS * out_lanes).astype(BF16)
    return pee, w1e, w2e


def _compress(rk, rv, wk, wv):
    b, nb, width = rk.shape
    full2 = lambda a: pl.BlockSpec(a.shape, lambda i: (0, 0))
    full3 = lambda a: pl.BlockSpec(a.shape, lambda i: (0, 0, 0))
    row = pl.BlockSpec((1, nb, width), lambda i: (i, 0, 0))
    out = lambda lanes: pl.BlockSpec((1, nb, lanes), lambda i: (i, 0, 0))
    return pl.pallas_call(
        _compress_kernel,
        grid=(b,),
        in_specs=[row, row, full2(wk[0]), full2(wv[0]), full3(wk[1]), full3(wv[1]),
                  full2(wk[2]), full2(wv[2])],
        out_specs=[out(K_AUG), out(D_KV)],
        out_shape=[jax.ShapeDtypeStruct((b, nb, K_AUG), BF16),
                   jax.ShapeDtypeStruct((b, nb, D_KV), BF16)],
        compiler_params=_cparams(("parallel",)),
        name="nsa_compress",
    )(rk, rv, wk[0], wv[0], wk[1], wv[1], wk[2], wv[2])


NSA_TQ = 128
NSA_TK = 1024


def _masked_softmax(s, mask):
    s = jnp.where(mask, s, NEG_BIG)
    s = s - jnp.max(s, axis=-1, keepdims=True)
    p = jnp.where(mask, jnp.exp(s), 0.0)
    return p / jnp.maximum(jnp.sum(p, axis=-1, keepdims=True), 1e-30)


def _flash_update(carry, s, v_t):
    m, l, acc = carry
    m_new = jnp.maximum(m, jnp.max(s, axis=0, keepdims=True))
    alpha = jnp.exp(m - m_new)
    p = jnp.exp(s - m_new)
    l = alpha * l + jnp.sum(p, axis=0, keepdims=True)
    acc = alpha * acc + _dot(v_t, p.astype(BF16))
    return m_new, l, acc


def _nsa_kernel(q_ref, ks_ref, kw_ref, vst_ref, vwt_ref, ck_ref, cv_ref, ovl_ref, oh_ref, gl_ref,
                eg_ref, o_ref):
    tq = NSA_TQ
    R = NSA_GROUP
    i = pl.program_id(1)
    q0 = i * tq
    t_col = q0 + lax.broadcasted_iota(jnp.int32, (tq, 1), 0)
    t_lane = q0 + lax.broadcasted_iota(jnp.int32, (1, tq), 1)
    t_lanes = jnp.concatenate([t_lane] * R, axis=1)
    n_cmp = ck_ref.shape[1]
    n_sel = ovl_ref.shape[0]

    gate = _sigmoid(gl_ref[...])
    g_hi, g_lo = _split2(gate)
    gexp = [_dot(g_hi, eg_ref[j]) + _dot(g_lo, eg_ref[j]) for j in range(3)]

    cmp_end = lax.broadcasted_iota(jnp.int32, (1, n_cmp), 1) * CMP_STRIDE + (CMP_LEN - 1)
    mask_c = t_col >= cmp_end
    ids = lax.broadcasted_iota(jnp.int32, (n_sel, 1), 0)
    idsf = ids.astype(F32)
    cur = t_lane // SEL_LEN
    forced = (ids == 0) | (ids == cur) | (ids == cur - 1)
    valid = ids * SEL_LEN <= t_lane

    G = NSA_KV_HEADS
    lanes_of = [slice(g * LANES, (g + 1) * LANES) for g in range(G)]
    q_st, q_slc, o_cmp = [], [], []
    for g in range(G):
        qh = [q_ref[:, h * LANES:(h + 1) * LANES] for h in range(g * R, (g + 1) * R)]
        ck = ck_ref[0, :, lanes_of[g]]
        cv = cv_ref[0]

        imp = jnp.zeros((n_sel, tq), F32)
        for r in range(R):
            p = _masked_softmax(_dot_nt(qh[r], ck), mask_c)
            o_cmp.append(_dot(p.astype(BF16), cv))
            p_hi, p_lo = _split2(p)
            imp = imp + _dot_nt(ovl_ref[...], p_hi) + _dot_nt(ovl_ref[...], p_lo)
        imp = jnp.where(forced, 1e6, jnp.where(valid, imp, -1.0))

        sel = jnp.zeros((n_sel, tq), F32)
        for _ in range(min(SEL_TOPN, n_sel)):
            m = jnp.max(imp, axis=0, keepdims=True)
            first = jnp.min(jnp.where(imp == m, idsf, float(n_sel)), axis=0, keepdims=True)
            hit = idsf == first
            sel = jnp.where(hit, 1.0, sel)
            imp = jnp.where(hit, -3e38, imp)
        bias_t = jnp.where(sel > 0.5, 0.0, NEG_BIG)
        if n_sel < LANES:
            bias_t = jnp.concatenate([bias_t, jnp.zeros((LANES - n_sel, tq), F32)], axis=0)
        bias = bias_t.T.astype(BF16)

        q_st.append(jnp.concatenate(qh, axis=0))
        q_slc.append(jnp.concatenate([q_st[g], jnp.concatenate([bias] * R, axis=0)], axis=1))

    init = tuple((jnp.full((1, R * tq), -3e38, F32), jnp.zeros((1, R * tq), F32),
                  jnp.zeros((LANES, R * tq), F32)) for _ in range(G))

    def slc_tile(j, carry, diagonal):
        k0 = pl.multiple_of(j * NSA_TK, NSA_TK)
        onehot = oh_ref[pl.ds(k0, NSA_TK), :]
        v_t = vst_ref[0, :, pl.ds(k0, NSA_TK)]
        out = []
        for g in range(G):
            k = jnp.concatenate([ks_ref[pl.ds(k0, NSA_TK), lanes_of[g]], onehot], axis=1)
            s = _dot_nt(k, q_slc[g])
            if diagonal:
                pos = k0 + lax.broadcasted_iota(jnp.int32, (NSA_TK, 1), 0)
                s = jnp.where(t_lanes >= pos, s, NEG_BIG)
            out.append(_flash_update(carry[g], s, v_t))
        return tuple(out)

    n_off = q0 // NSA_TK
    carry = lax.fori_loop(0, n_off, functools.partial(slc_tile, diagonal=False), init)
    slc = slc_tile(n_off, carry, True)

    span = WINDOW + tq
    w0 = pl.multiple_of(jnp.maximum(q0 - WINDOW, 0), tq)
    pos = w0 + lax.broadcasted_iota(jnp.int32, (span, 1), 0)
    visible = (t_lanes >= pos) & (pos > t_lanes - WINDOW)
    v_t = vwt_ref[0, :, pl.ds(w0, span)]
    win = [_flash_update(init[g], jnp.where(
        visible, _dot_nt(kw_ref[pl.ds(w0, span), lanes_of[g]], q_st[g]), NEG_BIG), v_t)
        for g in range(G)]

    for h in range(NSA_HEADS):
        g, r = divmod(h, R)
        cols = slice(h * LANES, (h + 1) * LANES)
        rows = slice(r * tq, (r + 1) * tq)
        o_slc = (slc[g][2][:, rows] / slc[g][1][:, rows]).T
        o_win = (win[g][2][:, rows] / win[g][1][:, rows]).T
        out = gexp[0][:, cols] * o_cmp[h] + gexp[1][:, cols] * o_slc + gexp[2][:, cols] * o_win
        o_ref[:, cols] = out.astype(BF16)


def _nsa_consts(t_len, n_cmp_pad):
    n_cmp = (t_len - CMP_LEN) // CMP_STRIDE + 1
    n_sel = t_len // SEL_LEN
    cmp_start = np.arange(n_cmp) * CMP_STRIDE
    sel_start = np.arange(n_sel) * SEL_LEN
    ovl = np.clip(np.minimum(cmp_start[:, None] + CMP_LEN, sel_start[None, :] + SEL_LEN)
                  - np.maximum(cmp_start[:, None], sel_start[None, :]), 0, None) / CMP_LEN
    ovl_t = np.zeros((n_sel, n_cmp_pad), np.float32)
    ovl_t[:, :n_cmp] = ovl.T
    onehot = np.zeros((t_len, LANES), np.float32)
    onehot[np.arange(t_len), np.arange(t_len) // SEL_LEN] = 1.0
    eg = np.zeros((3, LANES, Q_EXP), np.float32)
    for h in range(NSA_HEADS):
        for j in range(3):
            eg[j, h * 3 + j, h * LANES:(h + 1) * LANES] = 1.0
    return jnp.asarray(ovl_t, BF16), jnp.asarray(onehot, BF16), jnp.asarray(eg, BF16)


def _nsa_attention(oa, ob, vs_t, vw_t, cmp_k, cmp_v, b, t_len):
    assert NSA_TK % NSA_TQ == 0 and t_len % NSA_TK == 0 and WINDOW % NSA_TQ == 0
    assert t_len >= WINDOW + NSA_TQ and t_len // SEL_LEN <= LANES
    n_cmp_pad = cmp_k.shape[1]
    ovl_t, onehot, eg = _nsa_consts(t_len, n_cmp_pad)
    nq = t_len // NSA_TQ
    k_col0 = Q_EXP // K_AUG
    kspec = lambda c: pl.BlockSpec((t_len, K_AUG), lambda bi, i: (bi, k_col0 + c))
    vspec = pl.BlockSpec((1, D_KV, t_len), lambda bi, i: (bi, 0, 0))
    const2 = lambda a: pl.BlockSpec(a.shape, lambda bi, i: (0, 0))
    return pl.pallas_call(
        _nsa_kernel,
        grid=(b, nq),
        in_specs=[
            pl.BlockSpec((NSA_TQ, Q_EXP), lambda bi, i: (bi * nq + i, 0)),
            kspec(0), kspec(1), vspec, vspec,
            pl.BlockSpec((1, n_cmp_pad, K_AUG), lambda bi, i: (bi, 0, 0)),
            pl.BlockSpec((1, n_cmp_pad, D_KV), lambda bi, i: (bi, 0, 0)),
            const2(ovl_t), const2(onehot),
            pl.BlockSpec((NSA_TQ, LANES), lambda bi, i: (bi * nq + i, 2)),
            pl.BlockSpec(eg.shape, lambda bi, i: (0, 0, 0)),
        ],
        out_specs=pl.BlockSpec((NSA_TQ, Q_EXP), lambda bi, i: (bi * nq + i, 0)),
        out_shape=jax.ShapeDtypeStruct((b * t_len, Q_EXP), BF16),
        compiler_params=_cparams(("parallel", "arbitrary")),
        name="nsa_attention",
    )(oa, oa, oa, vs_t, vw_t, cmp_k, cmp_v, ovl_t, onehot, ob, eg)


def _dot_x3(x, e):
    hi = x.astype(BF16)
    r1 = x - hi.astype(F32)
    mid = r1.astype(BF16)
    lo = (r1 - mid.astype(F32)).astype(BF16)
    return _dot(hi, e) + _dot(mid, e) + _dot(lo, e)


def _head_sum_matrix():
    ids = np.arange(D_RWKV) // HEAD_DIM
    return jnp.asarray(ids[:, None] == ids[None, :], BF16)


def _rwkv_pre_kernel(p_ref, hp_ref, mu_ref, w0_ref, w2_ref, a0_ref, a2_ref, g2_ref, kk_ref,
                     ka_ref, rk_ref, bd_ref, r_o, lw_o, k_o, kkn_o, a_o, vt_o, bv_o, g_o, *,
                     tiles_per_seq):
    i = pl.program_id(0)
    p = p_ref[...]
    tm = p.shape[0]
    keep = jnp.where(i % tiles_per_seq == 0, 0.0, 1.0)
    halo = hp_ref[SUBLANES - 1:SUBLANES, :] * keep
    prev = pltpu.roll(p, 1, axis=0)
    row0 = lax.broadcasted_iota(jnp.int32, (tm, 1), 0) == 0
    prev = jnp.where(row0, halo, prev)
    ps = p + (prev - p) * mu_ref[...]
    d = D_RWKV
    r = ps[:, 0:d]
    k = ps[:, d:2 * d]
    v = ps[:, 2 * d:3 * d]
    xw = ps[:, 3 * d:3 * d + RANK_W]
    xa = ps[:, 3 * d + RANK_W:3 * d + RANK_W + RANK_A]
    xg = ps[:, 3 * d + RANK_W + RANK_A:]
    z = -(w0_ref[...] + _dot(jnp.tanh(xw).astype(BF16), w2_ref[...]))
    softplus = jnp.maximum(z, 0.0) + jnp.log(1.0 + jnp.exp(-jnp.abs(z)))
    w = -softplus - 0.5
    a = _sigmoid(a0_ref[...] + _dot(xa.astype(BF16), a2_ref[...]))
    g = _dot(_sigmoid(xg).astype(BF16), g2_ref[...])
    kk = k * kk_ref[...]
    ss = _dot_x3(kk * kk, bd_ref[...])
    kk = kk / jnp.maximum(jnp.sqrt(ss), 1e-12)
    k_mod = k * (1.0 + (a - 1.0) * ka_ref[...])
    for o_ref, val in ((r_o, r), (lw_o, -jnp.exp(w)), (k_o, k_mod), (kkn_o, kk), (a_o, a)):
        for hd in range(RWKV_HEADS):
            o_ref[0, hd] = val[:, hd * HEAD_DIM:(hd + 1) * HEAD_DIM]
    vt_o[0] = v.T.reshape(RWKV_HEADS, HEAD_DIM, tm)
    bv_o[...] = _dot_x3(r * k_mod * rk_ref[...], bd_ref[...]) * v
    g_o[...] = g


def _rwkv_pre(oc, b, t_len, mu, w0, w2, a0, a2, g2, k_k, k_a, r_k, bd, tm=256):
    n = oc.shape[0]
    tps = t_len // tm
    halo_blocks = tm // SUBLANES
    full = lambda a: pl.BlockSpec(a.shape, lambda i: (0, 0))
    tok = pl.BlockSpec((tm, D_RWKV), lambda i: (i, 0))
    hm = pl.BlockSpec((1, RWKV_HEADS, tm, HEAD_DIM), lambda i: (i // tps, 0, i % tps, 0))
    hm_t = pl.BlockSpec((1, RWKV_HEADS, HEAD_DIM, tm), lambda i: (i // tps, 0, 0, i % tps))
    params = (mu, w0, w2, a0, a2, g2, k_k, k_a, r_k, bd)
    hm_shape = jax.ShapeDtypeStruct((b, RWKV_HEADS, t_len, HEAD_DIM), F32)
    return pl.pallas_call(
        functools.partial(_rwkv_pre_kernel, tiles_per_seq=tps),
        grid=(n // tm,),
        in_specs=[
            pl.BlockSpec((tm, D_RWKV_IN), lambda i: (i, 0)),
            pl.BlockSpec((SUBLANES, D_RWKV_IN),
                         lambda i: (jnp.maximum(i * halo_blocks - 1, 0), 0)),
        ] + [full(a) for a in params],
        out_specs=[hm] * 5 + [hm_t, tok, tok],
        out_shape=[hm_shape] * 5
        + [jax.ShapeDtypeStruct((b, RWKV_HEADS, HEAD_DIM, t_len), F32)]
        + [jax.ShapeDtypeStruct((n, D_RWKV), F32)] * 2,
        compiler_params=_cparams(("parallel",)),
        name="rwkv_pre",
    )(oc, oc, *params)


RWKV_C = 64
RWKV_CB = 128
RWKV_PASSES = 1


def _bmm(eq, a, b):
    ein = lambda x, y: jnp.einsum(eq, x, y, preferred_element_type=F32)
    if RWKV_PASSES == 1:
        return ein(a.astype(BF16), b.astype(BF16))
    ah, al = _split2(a)
    bh, bl = _split2(b)
    return ein(ah, bh) + ein(ah, bl) + ein(al, bh)


def _cumsum_rows(x2d, seg):
    rows = lax.broadcasted_iota(jnp.int32, (x2d.shape[0], 1), 0) % seg
    step = 1
    while step < seg:
        shifted = pltpu.roll(x2d, step, axis=0)
        x2d = x2d + jnp.where(rows >= step, shifted, 0.0)
        step *= 2
    return x2d


def _rwkv_rec_kernel(r_ref, lw_ref, k_ref, kk_ref, a_ref, vt_ref, ot_ref, s_ref):
    H = r_ref.shape[1]
    C = RWKV_C

    @pl.when(pl.program_id(1) == 0)
    def _():
        s_ref[...] = jnp.zeros_like(s_ref)

    ri = lax.broadcasted_iota(jnp.int32, (C, C), 0)
    ci = lax.broadcasted_iota(jnp.int32, (C, C), 1)
    strict = (ri > ci)[None]
    incl = (ri >= ci)[None]
    eye = (ri == ci).astype(F32)[None]

    for sub in range(RWKV_CB // C):
        rows = slice(sub * C, (sub + 1) * C)
        r = r_ref[0, :, rows, :]
        lw = lw_ref[0, :, rows, :]
        k = k_ref[0, :, rows, :]
        kk = kk_ref[0, :, rows, :]
        a = a_ref[0, :, rows, :]
        vt = vt_ref[0, :, :, rows]
        s0 = s_ref[...]

        cum = _cumsum_rows(lw.reshape(H * C, HEAD_DIM), C).reshape(H, C, HEAD_DIM)
        cum_last = cum[:, C - 1:C, :]
        p_inv = jnp.exp(-cum)
        w_last = jnp.exp(cum_last - cum)
        bm = kk * a
        at = -kk * jnp.exp(cum - lw)
        rt = r * jnp.exp(cum)
        bt = bm * p_inv
        kt = k * p_inv

        nt = 'hik,hjk->hij'
        m_ab = jnp.where(strict, _bmm(nt, at, bt), 0.0)
        m_ak = jnp.where(strict, _bmm(nt, at, kt), 0.0)
        n_rb = jnp.where(incl, _bmm(nt, rt, bt), 0.0)
        n_rk = jnp.where(incl, _bmm(nt, rt, kt), 0.0)

        tinv = eye + m_ab
        mp = m_ab
        step = 1
        while 2 * step < C:
            mp = _bmm('hij,hjk->hik', mp, mp)
            tinv = tinv + _bmm('hij,hjk->hik', tinv, mp)
            step *= 2

        rhs_t = _bmm('hvk,hik->hvi', s0, at) + _bmm('hvj,hij->hvi', vt, m_ak)
        ut = _bmm('hvj,hij->hvi', rhs_t, tinv)
        ot = (_bmm('hvk,hik->hvi', s0, rt) + _bmm('hvj,hij->hvi', ut, n_rb)
              + _bmm('hvj,hij->hvi', vt, n_rk))
        ot_ref[0, :, :, rows] = ot
        s_ref[...] = (s0 * jnp.exp(cum_last) + _bmm('hvj,hjk->hvk', ut, bm * w_last)
                      + _bmm('hvj,hjk->hvk', vt, k * w_last))


def _rwkv_recurrence(r, lw, k, kk, a, vt):
    b, h, t_len, d = r.shape
    tok = pl.BlockSpec((1, h, RWKV_CB, d), lambda bi, c: (bi, 0, c, 0))
    tr = pl.BlockSpec((1, h, d, RWKV_CB), lambda bi, c: (bi, 0, 0, c))
    return pl.pallas_call(
        _rwkv_rec_kernel,
        grid=(b, t_len // RWKV_CB),
        in_specs=[tok] * 5 + [tr],
        out_specs=tr,
        out_shape=jax.ShapeDtypeStruct((b, h, d, t_len), F32),
        scratch_shapes=[pltpu.VMEM((h, d, d), F32)],
        compiler_params=_cparams(("parallel", "arbitrary")),
        name="rwkv_recurrence",
    )(r, lw, k, kk, a, vt)


def _rwkv_post_kernel(ot_ref, bv_ref, g_ref, lnw_ref, lnb_ref, bd_ref, y_ref):
    tm = bv_ref.shape[0]
    o = ot_ref[0].reshape(D_RWKV, tm).T
    bd = bd_ref[...]
    inv = 1.0 / HEAD_DIM
    mean = _dot_x3(o, bd) * inv
    d = o - mean
    var = _dot_x3(d * d, bd) * inv
    on = d * lax.rsqrt(var + GN_EPS) * lnw_ref[...] + lnb_ref[...]
    y_ref[...] = ((on + bv_ref[...]) * g_ref[...]).astype(BF16)


def _rwkv_post(ot, bv, g, ln_w, ln_b, bd, tm=512):
    n = bv.shape[0]
    tps = ot.shape[-1] // tm
    tile = pl.BlockSpec((tm, D_RWKV), lambda i: (i, 0))
    hm_t = pl.BlockSpec((1, RWKV_HEADS, HEAD_DIM, tm), lambda i: (i // tps, 0, 0, i % tps))
    full = lambda a: pl.BlockSpec(a.shape, lambda i: (0, 0))
    return pl.pallas_call(
        _rwkv_post_kernel,
        grid=(n // tm,),
        in_specs=[hm_t, tile, tile, full(ln_w), full(ln_b), full(bd)],
        out_specs=tile,
        out_shape=jax.ShapeDtypeStruct((n, D_RWKV), BF16),
        compiler_params=_cparams(("parallel",)),
        name="rwkv_post",
    )(ot, bv, g, ln_w, ln_b, bd)


def _rwkv_mixer(oc, b, t_len, mu, w0, w2, a0, a2, g2, k_k, k_a, r_k, ln_w, ln_b):
    bd = _head_sum_matrix()
    row = lambda z: z.reshape(1, -1)
    r, lw, k, kk, a, vt, bv, g = _rwkv_pre(
        oc, b, t_len, row(mu), row(w0), w2.astype(BF16), row(a0), a2.astype(BF16),
        g2.astype(BF16), row(k_k), row(k_a), row(r_k), bd)
    ot = _rwkv_recurrence(r, lw, k, kk, a, vt)
    return _rwkv_post(ot, bv, g, row(ln_w), row(ln_b), bd)


def _out_proj_kernel(x_ref, yn_ref, yr_ref, wn_ref, wr_ref, g_ref, x1_ref, h2_ref):
    x1 = x_ref[...] + _dot(yn_ref[...], wn_ref[...]) + _dot(yr_ref[...], wr_ref[...])
    x1_ref[...] = x1
    h2_ref[...] = x1 * lax.rsqrt(jnp.mean(x1 * x1, axis=-1, keepdims=True) + EPS) * g_ref[...]


def _expand_w_out(w_out):
    wn = w_out[:D_NSA].reshape(NSA_HEADS, 1, HEAD_DIM, D_MODEL)
    onehot = (np.arange(NSA_HEADS)[:, None] // NSA_GROUP == np.arange(NSA_KV_HEADS)[None, :])
    wn = wn * jnp.asarray(onehot, F32)[:, :, None, None]
    return wn.reshape(Q_EXP, D_MODEL).astype(BF16), w_out[D_NSA:].astype(BF16)


def _out_proj(x2d, y_nsa, y_rwkv, wn, wr, g, tm=512):
    n = x2d.shape[0]
    tile = lambda a: pl.BlockSpec((tm, a.shape[1]), lambda i: (i, 0))
    full = lambda a: pl.BlockSpec(a.shape, lambda i: (0, 0))
    return pl.pallas_call(
        _out_proj_kernel,
        grid=(n // tm,),
        in_specs=[tile(x2d), tile(y_nsa), tile(y_rwkv), full(wn), full(wr), full(g)],
        out_specs=[tile(x2d), tile(x2d)],
        out_shape=[jax.ShapeDtypeStruct((n, D_MODEL), F32)] * 2,
        compiler_params=_cparams(("parallel",)),
        name="out_proj",
    )(x2d, y_nsa, y_rwkv, wn, wr, g)


def _topk_rows(s, k, rid):
    vals, ids = [], []
    for _ in range(k):
        m = jnp.max(s, axis=0, keepdims=True)
        first = jnp.min(jnp.where(s == m, rid, jnp.inf), axis=0, keepdims=True)
        vals.append(m)
        ids.append(first)
        s = jnp.where(rid == first, -jnp.inf, s)
    return jnp.concatenate(vals, axis=0), jnp.concatenate(ids, axis=0)


def _take_rows(table, idx):
    rows = lax.broadcasted_iota(jnp.int32, (table.shape[0], 1), 0).astype(F32)
    out = [jnp.sum(jnp.where(rows == idx[r:r + 1], table, 0.0), axis=0, keepdims=True)
           for r in range(idx.shape[0])]
    return jnp.concatenate(out, axis=0)


def _pair_candidates():
    k = PEER_TOPK
    flat = [0 * k + j for j in range(k)]
    for i in range(1, SUBLANES):
        flat += [i * k + j for j in range(SUBLANES)]
    flat += [i * k for i in range(SUBLANES, k)]
    return np.asarray(flat, np.float32).reshape(-1, 1)


def _peer_route_kernel(h_ref, wq_ref, sk_ref, flat_ref, e_ref, g_ref):
    K = PEER_TOPK
    half = PEER_D_QUERY // 2
    q = _dot(h_ref[...].astype(BF16), wq_ref[...]).astype(BF16)
    key_ids = lax.broadcasted_iota(jnp.int32, (PEER_N_KEYS, 1), 0).astype(F32)
    flat = flat_ref[...]
    rows_e, rows_g = [], []
    for h in range(PEER_HEADS):
        top = []
        for c in range(2):
            col = (h * 2 + c) * half
            s_t = _dot_nt(sk_ref[h, c], q[:, col:col + half])
            top.append(_topk_rows(s_t, K, key_ids))
        (v0, i0), (v1, i1) = top
        cand = jnp.concatenate(
            [v0[0:1] + v1]
            + [v0[i:i + 1] + v1[:SUBLANES] for i in range(1, SUBLANES)]
            + [v0[SUBLANES:] + v1[0:1]], axis=0)
        best, pair = _topk_rows(cand, K, flat)
        pi = jnp.floor(pair * (1.0 / K))
        pj = pair - pi * K
        experts = _take_rows(i0, pi) * float(PEER_N_KEYS) + _take_rows(i1, pj)
        p = jnp.exp(best - jnp.max(best, axis=0, keepdims=True))
        rows_e.append(experts * float(HALF_ROWS))
        rows_g.append(p / jnp.sum(p, axis=0, keepdims=True))
    e_ref[...] = jnp.concatenate(rows_e, axis=0).T.astype(jnp.int32)
    g_ref[...] = jnp.concatenate(rows_g, axis=0).T


def _peer_route(h2, wq, sk, tm=256):
    n = h2.shape[0]
    hk = PEER_HEADS * PEER_TOPK
    flat = jnp.asarray(_pair_candidates())
    out = pl.BlockSpec((tm, hk), lambda i: (i, 0))
    return pl.pallas_call(
        _peer_route_kernel,
        grid=(n // tm,),
        in_specs=[
            pl.BlockSpec((tm, D_MODEL), lambda i: (i, 0)),
            pl.BlockSpec(wq.shape, lambda i: (0, 0)),
            pl.BlockSpec(sk.shape, lambda i: (0, 0, 0, 0)),
            pl.BlockSpec(flat.shape, lambda i: (0, 0)),
        ],
        out_specs=[out, out],
        out_shape=[jax.ShapeDtypeStruct((n, hk), jnp.int32), jax.ShapeDtypeStruct((n, hk), F32)],
        compiler_params=_cparams(("parallel",)),
        name="peer_route",
    )(h2, wq, sk, flat)


PEER_TT = 64
HALF_ROWS = SUBLANES // 2
SUB_ORDER = (0, 4, 2, 6, 1, 5, 3, 7)
HIGH_MASK = 0xFFFF0000


def _pack_table(w):
    bits = lax.bitcast_convert_type(w.astype(BF16), jnp.uint16).astype(jnp.uint32)
    half = w.shape[1] // 2
    packed = bits[:, :half] | (bits[:, half:] << 16)
    return packed.reshape(w.shape[0] * HALF_ROWS, LANES)


def _table_row(tbl_ref, row0):
    return tbl_ref[pl.ds(pl.multiple_of(row0, HALF_ROWS), HALF_ROWS), :]


def _unpack_words(word):
    lo = pltpu.bitcast(word << 16, F32)
    hi = pltpu.bitcast(word & jnp.uint32(HIGH_MASK), F32)
    return lo, hi


def _sublane_tree(c):
    sub = lax.broadcasted_iota(jnp.int32, (SUBLANES, LANES), 0)
    m2 = (sub % 4) < 2
    d = []
    for x, y in zip(c[0::2], c[1::2]):
        d.append(jnp.where(m2, x + pltpu.roll(x, 6, axis=0), y + pltpu.roll(y, 2, axis=0)))
    m1 = (sub % 2) == 0
    x, y = d
    return jnp.where(m1, x + pltpu.roll(x, 7, axis=0), y + pltpu.roll(y, 1, axis=0))


def _peer_act_kernel(e_ref, h_ref, tbl_ref, gate_ref, ones_ref, w_ref, part_ref):
    hk = PEER_HEADS * PEER_TOPK
    tt = h_ref.shape[0]

    def token(t, carry):
        hrow = h_ref[t]
        h_lo = jnp.concatenate([hrow[:HALF_ROWS]] * 2, axis=0)
        h_hi = jnp.concatenate([hrow[HALF_ROWS:]] * 2, axis=0)
        for m in range(hk // SUBLANES):
            es = [e_ref[t, m * SUBLANES + j] for j in SUB_ORDER]
            c = []
            for ea, eb in zip(es[0::2], es[1::2]):
                lo, hi = _unpack_words(jnp.concatenate(
                    [_table_row(tbl_ref, ea), _table_row(tbl_ref, eb)], axis=0))
                c.append(lo * h_lo + hi * h_hi)
            row0 = pl.multiple_of((t * (hk // SUBLANES) + m) * SUBLANES, SUBLANES)
            part_ref[pl.ds(row0, SUBLANES), :] = _sublane_tree(c)
        return carry

    lax.fori_loop(0, tt, token, 0)
    hi, lo = _split2(part_ref[...])
    sums = _dot(jnp.concatenate([hi, lo], axis=1), ones_ref[...])
    sums = sums.reshape(tt, hk, LANES)
    eye = (lax.broadcasted_iota(jnp.int32, (hk, LANES), 0)
           == lax.broadcasted_iota(jnp.int32, (hk, LANES), 1))
    act = jnp.sum(jnp.where(eye[None], sums, 0.0), axis=1)
    w_ref[...] = _gelu(act) * gate_ref[...]


def _table_spec(tbl):
    return pl.BlockSpec(tbl.shape, lambda i: (0, 0), pipeline_mode=pl.Buffered(1))


def _peer_act(experts, h2_tiles, tbl, gates):
    n, hk = experts.shape
    ones = jnp.ones((2 * LANES, LANES), BF16)
    return pl.pallas_call(
        _peer_act_kernel,
        grid=(n // PEER_TT,),
        in_specs=[
            pl.BlockSpec((PEER_TT, hk), lambda i: (i, 0), memory_space=pltpu.SMEM),
            pl.BlockSpec((PEER_TT, SUBLANES, LANES), lambda i: (i, 0, 0)),
            _table_spec(tbl),
            pl.BlockSpec((PEER_TT, hk), lambda i: (i, 0)),
            pl.BlockSpec(ones.shape, lambda i: (0, 0)),
        ],
        out_specs=pl.BlockSpec((PEER_TT, hk), lambda i: (i, 0)),
        out_shape=jax.ShapeDtypeStruct((n, hk), F32),
        scratch_shapes=[pltpu.VMEM((PEER_TT * hk, LANES), F32)],
        compiler_params=_cparams(("arbitrary",)),
        name="peer_act",
    )(experts, h2_tiles, tbl, gates, ones)


PEER_ACCS = 4
PEER_GROUP = 64


def _peer_out_kernel(e_ref, w_ref, x_ref, tbl_ref, g_ref, o_ref, *, final_norm):
    hk = PEER_HEADS * PEER_TOPK
    tt = x_ref.shape[0]

    def token(t, carry):
        def group(gi, accs):
            acc_lo, acc_hi = list(accs[0]), list(accs[1])
            base = t * hk + gi * PEER_GROUP
            for j in range(PEER_GROUP):
                lo, hi = _unpack_words(_table_row(tbl_ref, e_ref[base + j]))
                wk = w_ref[base + j]
                acc_lo[j % PEER_ACCS] = acc_lo[j % PEER_ACCS] + wk * lo
                acc_hi[j % PEER_ACCS] = acc_hi[j % PEER_ACCS] + wk * hi
            return tuple(acc_lo), tuple(acc_hi)

        zeros = tuple(jnp.zeros((HALF_ROWS, LANES), F32) for _ in range(PEER_ACCS))
        acc_lo, acc_hi = lax.fori_loop(0, hk // PEER_GROUP, group, (zeros, zeros))
        tree = lambda a: (a[0] + a[1]) + (a[2] + a[3])
        o_ref[t] = x_ref[t] + jnp.concatenate([tree(acc_lo), tree(acc_hi)], axis=0)
        return carry

    lax.fori_loop(0, tt, token, 0)
    if final_norm:
        x2 = o_ref[...]
        ms = jnp.sum(jnp.sum(x2 * x2, axis=2, keepdims=True), axis=1, keepdims=True) / D_MODEL
        o_ref[...] = x2 * lax.rsqrt(ms + EPS) * g_ref[...]


def _peer_out(experts, w, x1_tiles, tbl, final_g, final_norm):
    n, hk = experts.shape
    smem = pl.BlockSpec((PEER_TT * hk,), lambda i: (i,), memory_space=pltpu.SMEM)
    tile = pl.BlockSpec((PEER_TT, SUBLANES, LANES), lambda i: (i, 0, 0))
    return pl.pallas_call(
        functools.partial(_peer_out_kernel, final_norm=final_norm),
        grid=(n // PEER_TT,),
        in_specs=[smem, smem, tile, _table_spec(tbl),
                  pl.BlockSpec(final_g.shape, lambda i: (0, 0, 0))],
        out_specs=tile,
        out_shape=jax.ShapeDtypeStruct(x1_tiles.shape, F32),
        compiler_params=_cparams(("arbitrary",)),
        name="peer_out",
    )(experts.reshape(-1), w.reshape(-1), x1_tiles, tbl, final_g)


def _peer_ffn_residual(x1, h2, wq, sk, tbl_u, tbl_v, final_g, final_norm):
    n = x1.shape[0]
    experts, gates = _peer_route(h2, wq, sk)
    tiles = lambda z: z.reshape(n, SUBLANES, LANES)
    w = _peer_act(experts, tiles(h2), tbl_u, gates)
    g_tile = final_g.reshape(1, SUBLANES, LANES)
    return _peer_out(experts, w, tiles(x1), tbl_v, g_tile, final_norm).reshape(n, D_MODEL)


def kernel(x, norm_mix_g, w_in, cmp_k_pe, cmp_k_w1, cmp_k_w2, cmp_v_pe, cmp_v_w1, cmp_v_w2,
           rwkv_mu, rwkv_w0, rwkv_w2, rwkv_a0, rwkv_a2, rwkv_g2, rwkv_k_k, rwkv_k_a, rwkv_r_k,
           rwkv_ln_w, rwkv_ln_b, w_out, norm_ffn_g, peer_w_q, peer_sub_keys, peer_u, peer_v,
           norm_final_g):
    b, t_len, _ = x.shape
    n = b * t_len
    row = lambda z: z.reshape(1, -1)
    x2d = x.reshape(n, D_MODEL)
    depth = w_in.shape[0]
    for l in range(depth):
        oa, ob, oc = _in_proj(x2d, row(norm_mix_g[l]), _pad_w_in(w_in[l]), t_len)
        nb = t_len // CMP_STRIDE
        rk = ob[:, 0:D_KV].reshape(b, nb, CMP_STRIDE * D_KV)
        rv = ob[:, D_KV:2 * D_KV].reshape(b, nb, CMP_STRIDE * D_KV)
        cmp_k, cmp_v = _compress(
            rk, rv, _expand_cmp_weights(cmp_k_pe[l], cmp_k_w1[l], cmp_k_w2[l], LANES),
            _expand_cmp_weights(cmp_v_pe[l], cmp_v_w1[l], cmp_v_w2[l], HEAD_DIM))
        v_col0 = Q_EXP + 2 * K_AUG
        v_t = lambda c: oa[:, v_col0 + c * D_KV:v_col0 + (c + 1) * D_KV].reshape(
            b, t_len, D_KV).transpose(0, 2, 1)
        y_nsa = _nsa_attention(oa, ob, v_t(0), v_t(1), cmp_k, cmp_v, b, t_len)
        y_rwkv = _rwkv_mixer(oc, b, t_len, rwkv_mu[l], rwkv_w0[l], rwkv_w2[l], rwkv_a0[l],
                             rwkv_a2[l], rwkv_g2[l], rwkv_k_k[l], rwkv_k_a[l], rwkv_r_k[l],
                             rwkv_ln_w[l], rwkv_ln_b[l])
        wn, wr = _expand_w_out(w_out[l])
        x1, h2 = _out_proj(x2d, y_nsa, y_rwkv, wn, wr, row(norm_ffn_g[l]))
        x2d = _peer_ffn_residual(x1, h2, peer_w_q[l].astype(BF16), peer_sub_keys[l].astype(BF16),
                                 _pack_table(peer_u[l]), _pack_table(peer_v[l]),
                                 norm_final_g, final_norm=(l == depth - 1))
    return x2d.reshape(b, t_len, D_MODEL)
```

```python
import functools

import numpy as np
import jax
import jax.numpy as jnp
from jax import lax
from jax.experimental import pallas as pl
from jax.experimental.pallas import tpu as pltpu

F32 = jnp.float32
BF16 = jnp.bfloat16

D_MODEL = 1024
HEAD_DIM = 64
NSA_HEADS = 8
NSA_KV_HEADS = 2
NSA_GROUP = NSA_HEADS // NSA_KV_HEADS
RWKV_HEADS = 8
D_NSA = NSA_HEADS * HEAD_DIM
D_RWKV = RWKV_HEADS * HEAD_DIM
D_KV = NSA_KV_HEADS * HEAD_DIM
CMP_LEN = 32
CMP_STRIDE = 16
CMP_HIDDEN = 128
SEL_LEN = 64
SEL_TOPN = 16
WINDOW = 512
RANK_W = 64
RANK_A = 64
RANK_G = 128
D_RWKV_IN = 3 * D_RWKV + RANK_W + RANK_A + RANK_G
PEER_HEADS = 8
PEER_N_KEYS = 128
PEER_D_QUERY = 256
PEER_TOPK = 16
EPS = 1e-6
GN_EPS = 64e-5

LANES = 128
SUBLANES = 8
VMEM_LIMIT = 56 * 1024 * 1024

NEG_BIG = -1e30
Q_EXP = NSA_HEADS * LANES
K_AUG = NSA_KV_HEADS * LANES
NA_COLS = Q_EXP + 2 * K_AUG + 2 * D_KV
NB_COLS = 3 * LANES
FEAT0 = HEAD_DIM
POS_SPLIT = 64


def _cparams(sem):
    return pltpu.CompilerParams(dimension_semantics=sem, vmem_limit_bytes=VMEM_LIMIT)


def _dot(a, b):
    return jnp.dot(a, b, preferred_element_type=F32)


def _dot_nt(a, b):
    return lax.dot_general(a, b, (((1,), (1,)), ((), ())), preferred_element_type=F32)


def _split2(x):
    hi = x.astype(BF16)
    lo = (x - hi.astype(F32)).astype(BF16)
    return hi, lo


def _dot_x2(x, e):
    hi, lo = _split2(x)
    return _dot(hi, e) + _dot(lo, e)


def _gelu(x):
    return 0.5 * x * (1.0 + jnp.tanh(0.7978845608028654 * (x + 0.044715 * (x * x * x))))


def _sigmoid(x):
    return 1.0 / (1.0 + jnp.exp(-x))


def _key_features(pos, lane):
    hi = (pos // POS_SPLIT).astype(F32)
    lo = (pos % POS_SPLIT).astype(F32)
    return jnp.where(lane == FEAT0, hi, jnp.where(lane == FEAT0 + 1, lo,
                     jnp.where((lane == FEAT0 + 2) | (lane == FEAT0 + 3), 1.0, 0.0)))


def _query_features(t, lane, slope):
    hi = (t // POS_SPLIT).astype(F32)
    lo = (t % POS_SPLIT).astype(F32)
    return jnp.where(lane == FEAT0, slope * POS_SPLIT, jnp.where(
        lane == FEAT0 + 1, slope, jnp.where(
            lane == FEAT0 + 2, -slope * POS_SPLIT * hi, jnp.where(
                lane == FEAT0 + 3, -slope * lo, 0.0))))


def _in_proj_kernel(x_ref, g_ref, w_ref, oa_ref, ob_ref, oc_ref, *, t_len):
    x = x_ref[...]
    tm = x.shape[0]
    h = x * lax.rsqrt(jnp.mean(x * x, axis=-1, keepdims=True) + EPS) * g_ref[...]
    hb = h.astype(BF16)
    t = (pl.program_id(0) * tm + lax.broadcasted_iota(jnp.int32, (tm, 1), 0)) % t_len
    col = lax.broadcasted_iota(jnp.int32, (1, Q_EXP), 1)
    slope = jnp.zeros((1, Q_EXP), F32)
    for hd in range(NSA_HEADS):
        slope = jnp.where(col // LANES == hd, 2.0 ** -(hd + 1), slope)
    q = _dot(hb, w_ref[:, :Q_EXP]) * (HEAD_DIM ** -0.5) + _query_features(t, col % LANES, slope)
    oa_ref[:, :Q_EXP] = q.astype(BF16)
    colk = lax.broadcasted_iota(jnp.int32, (1, 2 * K_AUG), 1)
    k = _dot(hb, w_ref[:, Q_EXP:Q_EXP + 2 * K_AUG]) + _key_features(t, colk % LANES)
    oa_ref[:, Q_EXP:Q_EXP + 2 * K_AUG] = k.astype(BF16)
    oa_ref[:, Q_EXP + 2 * K_AUG:] = _dot(hb, w_ref[:, Q_EXP + 2 * K_AUG:NA_COLS]).astype(BF16)
    ob_ref[...] = _dot(hb, w_ref[:, NA_COLS:NA_COLS + NB_COLS])
    oc_ref[...] = _dot(hb, w_ref[:, NA_COLS + NB_COLS:])


def _in_proj(x2d, g, w_pad, t_len, tm=256):
    n = x2d.shape[0]
    ncols = w_pad.shape[1]
    nc = ncols - NA_COLS - NB_COLS
    return pl.pallas_call(
        functools.partial(_in_proj_kernel, t_len=t_len),
        grid=(n // tm,),
        in_specs=[
            pl.BlockSpec((tm, D_MODEL), lambda i: (i, 0)),
            pl.BlockSpec((1, D_MODEL), lambda i: (0, 0)),
            pl.BlockSpec((D_MODEL, ncols), lambda i: (0, 0)),
        ],
        out_specs=[
            pl.BlockSpec((tm, NA_COLS), lambda i: (i, 0)),
            pl.BlockSpec((tm, NB_COLS), lambda i: (i, 0)),
            pl.BlockSpec((tm, nc), lambda i: (i, 0)),
        ],
        out_shape=[
            jax.ShapeDtypeStruct((n, NA_COLS), BF16),
            jax.ShapeDtypeStruct((n, NB_COLS), F32),
            jax.ShapeDtypeStruct((n, nc), F32),
        ],
        compiler_params=_cparams(("parallel",)),
        name="in_proj",
    )(x2d, g, w_pad)


def _pad_w_in(w_in):
    sizes = (D_NSA, D_KV, D_KV, D_KV, D_KV, D_KV, D_KV, 3 * NSA_HEADS, D_RWKV_IN)
    offs = np.cumsum((0,) + sizes)
    q, kc, vc, ks, vs, kw, vw, gl, rw = (w_in[:, offs[i]:offs[i + 1]] for i in range(9))
    def lane_pad(w, groups):
        w = w.reshape(D_MODEL, groups, HEAD_DIM)
        return jnp.pad(w, ((0, 0), (0, 0), (0, LANES - HEAD_DIM))).reshape(D_MODEL, groups * LANES)

    glp = jnp.pad(gl, ((0, 0), (0, LANES - 3 * NSA_HEADS)))
    return jnp.concatenate(
        [lane_pad(q, NSA_HEADS), lane_pad(ks, NSA_KV_HEADS), lane_pad(kw, NSA_KV_HEADS),
         vs, vw, kc, vc, glp, rw], axis=1).astype(BF16)


def _compress_kernel(rk_ref, rv_ref, pek_ref, pev_ref, w1k_ref, w1v_ref, w2k_ref, w2v_ref,
                     ok_ref, ov_ref):
    for r_ref, pe_ref, w1_ref, w2_ref, o_ref in (
            (rk_ref, pek_ref, w1k_ref, w2k_ref, ok_ref),
            (rv_ref, pev_ref, w1v_ref, w2v_ref, ov_ref)):
        rows = r_ref[0]
        nxt = pltpu.roll(rows, rows.shape[0] - 1, axis=0)
        a = (rows + pe_ref[0:1, :]).astype(BF16)
        b = (nxt + pe_ref[1:2, :]).astype(BF16)
        hid = _dot(a, w1_ref[0]) + _dot(b, w1_ref[1])
        out = _dot(_gelu(hid).astype(BF16), w2_ref[...])
        if o_ref is ok_ref:
            nb = rows.shape[0]
            end = lax.broadcasted_iota(jnp.int32, (nb, 1), 0) * CMP_STRIDE + (CMP_LEN - 1)
            lane = lax.broadcasted_iota(jnp.int32, (1, K_AUG), 1) % LANES
            out = out + _key_features(end, lane)
        o_ref[0] = out.astype(BF16)


def _expand_cmp_weights(pe, w1, w2, out_lanes):
    half = CMP_LEN // 2
    eye = jnp.eye(NSA_KV_HEADS, dtype=F32)
    w1r = w1.reshape(2, half, HEAD_DIM, CMP_HIDDEN)
    w1e = w1r[:, :, None, :, None, :] * eye[None, None, :, None, :, None]
    w1e = w1e.reshape(2, half * D_KV, NSA_KV_HEADS * CMP_HIDDEN).astype(BF16)
    pee = jnp.broadcast_to(pe.reshape(2, half, 1, HEAD_DIM), (2, half, NSA_KV_HEADS, HEAD_DIM))
    pee = pee.reshape(2, half * D_KV)
    w2p = jnp.pad(w2, ((0, 0), (0, out_lanes - HEAD_DIM)))
    w2e = (eye[:, None, :, None] * w2p[None, :, None, :]).reshape(
        NSA_KV_HEADS * CMP_HIDDEN, NSA_KV_HEADS * out_lanes).astype(BF16)
    return pee, w1e, w2e


def _compress(rk, rv, wk, wv):
    b, nb, width = rk.shape
    full2 = lambda a: pl.BlockSpec(a.shape, lambda i: (0, 0))
    full3 = lambda a: pl.BlockSpec(a.shape, lambda i: (0, 0, 0))
    row = pl.BlockSpec((1, nb, width), lambda i: (i, 0, 0))
    out = lambda lanes: pl.BlockSpec((1, nb, lanes), lambda i: (i, 0, 0))
    return pl.pallas_call(
        _compress_kernel,
        grid=(b,),
        in_specs=[row, row, full2(wk[0]), full2(wv[0]), full3(wk[1]), full3(wv[1]),
                  full2(wk[2]), full2(wv[2])],
        out_specs=[out(K_AUG), out(D_KV)],
        out_shape=[jax.ShapeDtypeStruct((b, nb, K_AUG), BF16),
                   jax.ShapeDtypeStruct((b, nb, D_KV), BF16)],
        compiler_params=_cparams(("parallel",)),
        name="nsa_compress",
    )(rk, rv, wk[0], wv[0], wk[1], wv[1], wk[2], wv[2])


NSA_TQ = 128
NSA_TK = 1024


def _masked_softmax(s, mask):
    s = jnp.where(mask, s, NEG_BIG)
    s = s - jnp.max(s, axis=-1, keepdims=True)
    p = jnp.where(mask, jnp.exp(s), 0.0)
    return p / jnp.maximum(jnp.sum(p, axis=-1, keepdims=True), 1e-30)


def _flash_update(carry, s, v_t):
    m, l, acc = carry
    m_new = jnp.maximum(m, jnp.max(s, axis=0, keepdims=True))
    alpha = jnp.exp(m - m_new)
    p = jnp.exp(s - m_new)
    l = alpha * l + jnp.sum(p, axis=0, keepdims=True)
    acc = alpha * acc + _dot(v_t, p.astype(BF16))
    return m_new, l, acc


def _nsa_kernel(q_ref, ks_ref, kw_ref, vst_ref, vwt_ref, ck_ref, cv_ref, ovl_ref, oh_ref, gl_ref,
                eg_ref, o_ref):
    tq = NSA_TQ
    R = NSA_GROUP
    i = pl.program_id(1)
    q0 = i * tq
    t_col = q0 + lax.broadcasted_iota(jnp.int32, (tq, 1), 0)
    t_lane = q0 + lax.broadcasted_iota(jnp.int32, (1, tq), 1)
    t_lanes = jnp.concatenate([t_lane] * R, axis=1)
    n_cmp = ck_ref.shape[1]
    n_sel = ovl_ref.shape[0]

    gate = _sigmoid(gl_ref[...])
    g_hi, g_lo = _split2(gate)
    gexp = [_dot(g_hi, eg_ref[j]) + _dot(g_lo, eg_ref[j]) for j in range(3)]

    cmp_end = lax.broadcasted_iota(jnp.int32, (1, n_cmp), 1) * CMP_STRIDE + (CMP_LEN - 1)
    mask_c = t_col >= cmp_end
    ids = lax.broadcasted_iota(jnp.int32, (n_sel, 1), 0)
    idsf = ids.astype(F32)
    cur = t_lane // SEL_LEN
    forced = (ids == 0) | (ids == cur) | (ids == cur - 1)
    valid = ids * SEL_LEN <= t_lane

    G = NSA_KV_HEADS
    lanes_of = [slice(g * LANES, (g + 1) * LANES) for g in range(G)]
    q_st, q_slc, o_cmp = [], [], []
    for g in range(G):
        qh = [q_ref[:, h * LANES:(h + 1) * LANES] for h in range(g * R, (g + 1) * R)]
        ck = ck_ref[0, :, lanes_of[g]]
        cv = cv_ref[0]

        imp = jnp.zeros((n_sel, tq), F32)
        for r in range(R):
            p = _masked_softmax(_dot_nt(qh[r], ck), mask_c)
            o_cmp.append(_dot(p.astype(BF16), cv))
            p_hi, p_lo = _split2(p)
            imp = imp + _dot_nt(ovl_ref[...], p_hi) + _dot_nt(ovl_ref[...], p_lo)
        imp = jnp.where(forced, 1e6, jnp.where(valid, imp, -1.0))

        sel = jnp.zeros((n_sel, tq), F32)
        for _ in range(min(SEL_TOPN, n_sel)):
            m = jnp.max(imp, axis=0, keepdims=True)
            first = jnp.min(jnp.where(imp == m, idsf, float(n_sel)), axis=0, keepdims=True)
            hit = idsf == first
            sel = jnp.where(hit, 1.0, sel)
            imp = jnp.where(hit, -3e38, imp)
        bias_t = jnp.where(sel > 0.5, 0.0, NEG_BIG)
        if n_sel < LANES:
            bias_t = jnp.concatenate([bias_t, jnp.zeros((LANES - n_sel, tq), F32)], axis=0)
        bias = bias_t.T.astype(BF16)

        q_st.append(jnp.concatenate(qh, axis=0))
        q_slc.append(jnp.concatenate([q_st[g], jnp.concatenate([bias] * R, axis=0)], axis=1))

    init = tuple((jnp.full((1, R * tq), -3e38, F32), jnp.zeros((1, R * tq), F32),
                  jnp.zeros((LANES, R * tq), F32)) for _ in range(G))

    def slc_tile(j, carry, diagonal):
        k0 = pl.multiple_of(j * NSA_TK, NSA_TK)
        onehot = oh_ref[pl.ds(k0, NSA_TK), :]
        v_t = vst_ref[0, :, pl.ds(k0, NSA_TK)]
        out = []
        for g in range(G):
            k = jnp.concatenate([ks_ref[pl.ds(k0, NSA_TK), lanes_of[g]], onehot], axis=1)
            s = _dot_nt(k, q_slc[g])
            if diagonal:
                pos = k0 + lax.broadcasted_iota(jnp.int32, (NSA_TK, 1), 0)
                s = jnp.where(t_lanes >= pos, s, NEG_BIG)
            out.append(_flash_update(carry[g], s, v_t))
        return tuple(out)

    n_off = q0 // NSA_TK
    carry = lax.fori_loop(0, n_off, functools.partial(slc_tile, diagonal=False), init)
    slc = slc_tile(n_off, carry, True)

    span = WINDOW + tq
    w0 = pl.multiple_of(jnp.maximum(q0 - WINDOW, 0), tq)
    pos = w0 + lax.broadcasted_iota(jnp.int32, (span, 1), 0)
    visible = (t_lanes >= pos) & (pos > t_lanes - WINDOW)
    v_t = vwt_ref[0, :, pl.ds(w0, span)]
    win = [_flash_update(init[g], jnp.where(
        visible, _dot_nt(kw_ref[pl.ds(w0, span), lanes_of[g]], q_st[g]), NEG_BIG), v_t)
        for g in range(G)]

    for h in range(NSA_HEADS):
        g, r = divmod(h, R)
        cols = slice(h * LANES, (h + 1) * LANES)
        rows = slice(r * tq, (r + 1) * tq)
        o_slc = (slc[g][2][:, rows] / slc[g][1][:, rows]).T
        o_win = (win[g][2][:, rows] / win[g][1][:, rows]).T
        out = gexp[0][:, cols] * o_cmp[h] + gexp[1][:, cols] * o_slc + gexp[2][:, cols] * o_win
        o_ref[:, cols] = out.astype(BF16)


def _nsa_consts(t_len, n_cmp_pad):
    n_cmp = (t_len - CMP_LEN) // CMP_STRIDE + 1
    n_sel = t_len // SEL_LEN
    cmp_start = np.arange(n_cmp) * CMP_STRIDE
    sel_start = np.arange(n_sel) * SEL_LEN
    ovl = np.clip(np.minimum(cmp_start[:, None] + CMP_LEN, sel_start[None, :] + SEL_LEN)
                  - np.maximum(cmp_start[:, None], sel_start[None, :]), 0, None) / CMP_LEN
    ovl_t = np.zeros((n_sel, n_cmp_pad), np.float32)
    ovl_t[:, :n_cmp] = ovl.T
    onehot = np.zeros((t_len, LANES), np.float32)
    onehot[np.arange(t_len), np.arange(t_len) // SEL_LEN] = 1.0
    eg = np.zeros((3, LANES, Q_EXP), np.float32)
    for h in range(NSA_HEADS):
        for j in range(3):
            eg[j, h * 3 + j, h * LANES:(h + 1) * LANES] = 1.0
    return jnp.asarray(ovl_t, BF16), jnp.asarray(onehot, BF16), jnp.asarray(eg, BF16)


def _nsa_attention(oa, ob, vs_t, vw_t, cmp_k, cmp_v, b, t_len):
    assert NSA_TK % NSA_TQ == 0 and t_len % NSA_TK == 0 and WINDOW % NSA_TQ == 0
    assert t_len >= WINDOW + NSA_TQ and t_len // SEL_LEN <= LANES
    n_cmp_pad = cmp_k.shape[1]
    ovl_t, onehot, eg = _nsa_consts(t_len, n_cmp_pad)
    nq = t_len // NSA_TQ
    k_col0 = Q_EXP // K_AUG
    kspec = lambda c: pl.BlockSpec((t_len, K_AUG), lambda bi, i: (bi, k_col0 + c))
    vspec = pl.BlockSpec((1, D_KV, t_len), lambda bi, i: (bi, 0, 0))
    const2 = lambda a: pl.BlockSpec(a.shape, lambda bi, i: (0, 0))
    return pl.pallas_call(
        _nsa_kernel,
        grid=(b, nq),
        in_specs=[
            pl.BlockSpec((NSA_TQ, Q_EXP), lambda bi, i: (bi * nq + i, 0)),
            kspec(0), kspec(1), vspec, vspec,
            pl.BlockSpec((1, n_cmp_pad, K_AUG), lambda bi, i: (bi, 0, 0)),
            pl.BlockSpec((1, n_cmp_pad, D_KV), lambda bi, i: (bi, 0, 0)),
            const2(ovl_t), const2(onehot),
            pl.BlockSpec((NSA_TQ, LANES), lambda bi, i: (bi * nq + i, 2)),
            pl.BlockSpec(eg.shape, lambda bi, i: (0, 0, 0)),
        ],
        out_specs=pl.BlockSpec((NSA_TQ, Q_EXP), lambda bi, i: (bi * nq + i, 0)),
        out_shape=jax.ShapeDtypeStruct((b * t_len, Q_EXP), BF16),
        compiler_params=_cparams(("parallel", "arbitrary")),
        name="nsa_attention",
    )(oa, oa, oa, vs_t, vw_t, cmp_k, cmp_v, ovl_t, onehot, ob, eg)


def _dot_x3(x, e):
    hi = x.astype(BF16)
    r1 = x - hi.astype(F32)
    mid = r1.astype(BF16)
    lo = (r1 - mid.astype(F32)).astype(BF16)
    return _dot(hi, e) + _dot(mid, e) + _dot(lo, e)


def _head_sum_matrix():
    ids = np.arange(D_RWKV) // HEAD_DIM
    return jnp.asarray(ids[:, None] == ids[None, :], BF16)


def _rwkv_pre_kernel(p_ref, hp_ref, mu_ref, w0_ref, w2_ref, a0_ref, a2_ref, g2_ref, kk_ref,
                     ka_ref, rk_ref, bd_ref, r_o, lw_o, k_o, kkn_o, a_o, vt_o, bv_o, g_o, *,
                     tiles_per_seq):
    i = pl.program_id(0)
    p = p_ref[...]
    tm = p.shape[0]
    keep = jnp.where(i % tiles_per_seq == 0, 0.0, 1.0)
    halo = hp_ref[SUBLANES - 1:SUBLANES, :] * keep
    prev = pltpu.roll(p, 1, axis=0)
    row0 = lax.broadcasted_iota(jnp.int32, (tm, 1), 0) == 0
    prev = jnp.where(row0, halo, prev)
    ps = p + (prev - p) * mu_ref[...]
    d = D_RWKV
    r = ps[:, 0:d]
    k = ps[:, d:2 * d]
    v = ps[:, 2 * d:3 * d]
    xw = ps[:, 3 * d:3 * d + RANK_W]
    xa = ps[:, 3 * d + RANK_W:3 * d + RANK_W + RANK_A]
    xg = ps[:, 3 * d + RANK_W + RANK_A:]
    z = -(w0_ref[...] + _dot(jnp.tanh(xw).astype(BF16), w2_ref[...]))
    softplus = jnp.maximum(z, 0.0) + jnp.log(1.0 + jnp.exp(-jnp.abs(z)))
    w = -softplus - 0.5
    a = _sigmoid(a0_ref[...] + _dot(xa.astype(BF16), a2_ref[...]))
    g = _dot(_sigmoid(xg).astype(BF16), g2_ref[...])
    kk = k * kk_ref[...]
    ss = _dot_x3(kk * kk, bd_ref[...])
    kk = kk / jnp.maximum(jnp.sqrt(ss), 1e-12)
    k_mod = k * (1.0 + (a - 1.0) * ka_ref[...])
    for o_ref, val in ((r_o, r), (lw_o, -jnp.exp(w)), (k_o, k_mod), (kkn_o, kk), (a_o, a)):
        for hd in range(RWKV_HEADS):
            o_ref[0, hd] = val[:, hd * HEAD_DIM:(hd + 1) * HEAD_DIM]
    vt_o[0] = v.T.reshape(RWKV_HEADS, HEAD_DIM, tm)
    bv_o[...] = _dot_x3(r * k_mod * rk_ref[...], bd_ref[...]) * v
    g_o[...] = g


def _rwkv_pre(oc, b, t_len, mu, w0, w2, a0, a2, g2, k_k, k_a, r_k, bd, tm=256):
    n = oc.shape[0]
    tps = t_len // tm
    halo_blocks = tm // SUBLANES
    full = lambda a: pl.BlockSpec(a.shape, lambda i: (0, 0))
    tok = pl.BlockSpec((tm, D_RWKV), lambda i: (i, 0))
    hm = pl.BlockSpec((1, RWKV_HEADS, tm, HEAD_DIM), lambda i: (i // tps, 0, i % tps, 0))
    hm_t = pl.BlockSpec((1, RWKV_HEADS, HEAD_DIM, tm), lambda i: (i // tps, 0, 0, i % tps))
    params = (mu, w0, w2, a0, a2, g2, k_k, k_a, r_k, bd)
    hm_shape = jax.ShapeDtypeStruct((b, RWKV_HEADS, t_len, HEAD_DIM), F32)
    return pl.pallas_call(
        functools.partial(_rwkv_pre_kernel, tiles_per_seq=tps),
        grid=(n // tm,),
        in_specs=[
            pl.BlockSpec((tm, D_RWKV_IN), lambda i: (i, 0)),
            pl.BlockSpec((SUBLANES, D_RWKV_IN),
                         lambda i: (jnp.maximum(i * halo_blocks - 1, 0), 0)),
        ] + [full(a) for a in params],
        out_specs=[hm] * 5 + [hm_t, tok, tok],
        out_shape=[hm_shape] * 5
        + [jax.ShapeDtypeStruct((b, RWKV_HEADS, HEAD_DIM, t_len), F32)]
        + [jax.ShapeDtypeStruct((n, D_RWKV), F32)] * 2,
        compiler_params=_cparams(("parallel",)),
        name="rwkv_pre",
    )(oc, oc, *params)


RWKV_C = 64
RWKV_CB = 128
RWKV_PASSES = 1


def _bmm(eq, a, b):
    ein = lambda x, y: jnp.einsum(eq, x, y, preferred_element_type=F32)
    if RWKV_PASSES == 1:
        return ein(a.astype(BF16), b.astype(BF16))
    ah, al = _split2(a)
    bh, bl = _split2(b)
    return ein(ah, bh) + ein(ah, bl) + ein(al, bh)


def _cumsum_rows(x2d, seg):
    rows = lax.broadcasted_iota(jnp.int32, (x2d.shape[0], 1), 0) % seg
    step = 1
    while step < seg:
        shifted = pltpu.roll(x2d, step, axis=0)
        x2d = x2d + jnp.where(rows >= step, shifted, 0.0)
        step *= 2
    return x2d


def _rwkv_rec_kernel(r_ref, lw_ref, k_ref, kk_ref, a_ref, vt_ref, ot_ref, s_ref):
    H = r_ref.shape[1]
    C = RWKV_C

    @pl.when(pl.program_id(1) == 0)
    def _():
        s_ref[...] = jnp.zeros_like(s_ref)

    ri = lax.broadcasted_iota(jnp.int32, (C, C), 0)
    ci = lax.broadcasted_iota(jnp.int32, (C, C), 1)
    strict = (ri > ci)[None]
    incl = (ri >= ci)[None]
    eye = (ri == ci).astype(F32)[None]

    for sub in range(RWKV_CB // C):
        rows = slice(sub * C, (sub + 1) * C)
        r = r_ref[0, :, rows, :]
        lw = lw_ref[0, :, rows, :]
        k = k_ref[0, :, rows, :]
        kk = kk_ref[0, :, rows, :]
        a = a_ref[0, :, rows, :]
        vt = vt_ref[0, :, :, rows]
        s0 = s_ref[...]

        cum = _cumsum_rows(lw.reshape(H * C, HEAD_DIM), C).reshape(H, C, HEAD_DIM)
        cum_last = cum[:, C - 1:C, :]
        p_inv = jnp.exp(-cum)
        w_last = jnp.exp(cum_last - cum)
        bm = kk * a
        at = -kk * jnp.exp(cum - lw)
        rt = r * jnp.exp(cum)
        bt = bm * p_inv
        kt = k * p_inv

        nt = 'hik,hjk->hij'
        m_ab = jnp.where(strict, _bmm(nt, at, bt), 0.0)
        m_ak = jnp.where(strict, _bmm(nt, at, kt), 0.0)
        n_rb = jnp.where(incl, _bmm(nt, rt, bt), 0.0)
        n_rk = jnp.where(incl, _bmm(nt, rt, kt), 0.0)

        tinv = eye + m_ab
        mp = m_ab
        step = 1
        while 2 * step < C:
            mp = _bmm('hij,hjk->hik', mp, mp)
            tinv = tinv + _bmm('hij,hjk->hik', tinv, mp)
            step *= 2

        rhs_t = _bmm('hvk,hik->hvi', s0, at) + _bmm('hvj,hij->hvi', vt, m_ak)
        ut = _bmm('hvj,hij->hvi', rhs_t, tinv)
        ot = (_bmm('hvk,hik->hvi', s0, rt) + _bmm('hvj,hij->hvi', ut, n_rb)
              + _bmm('hvj,hij->hvi', vt, n_rk))
        ot_ref[0, :, :, rows] = ot
        s_ref[...] = (s0 * jnp.exp(cum_last) + _bmm('hvj,hjk->hvk', ut, bm * w_last)
                      + _bmm('hvj,hjk->hvk', vt, k * w_last))


def _rwkv_recurrence(r, lw, k, kk, a, vt):
    b, h, t_len, d = r.shape
    tok = pl.BlockSpec((1, h, RWKV_CB, d), lambda bi, c: (bi, 0, c, 0))
    tr = pl.BlockSpec((1, h, d, RWKV_CB), lambda bi, c: (bi, 0, 0, c))
    return pl.pallas_call(
        _rwkv_rec_kernel,
        grid=(b, t_len // RWKV_CB),
        in_specs=[tok] * 5 + [tr],
        out_specs=tr,
        out_shape=jax.ShapeDtypeStruct((b, h, d, t_len), F32),
        scratch_shapes=[pltpu.VMEM((h, d, d), F32)],
        compiler_params=_cparams(("parallel", "arbitrary")),
        name="rwkv_recurrence",
    )(r, lw, k, kk, a, vt)


def _rwkv_post_kernel(ot_ref, bv_ref, g_ref, lnw_ref, lnb_ref, bd_ref, y_ref):
    tm = bv_ref.shape[0]
    o = ot_ref[0].reshape(D_RWKV, tm).T
    bd = bd_ref[...]
    inv = 1.0 / HEAD_DIM
    mean = _dot_x3(o, bd) * inv
    d = o - mean
    var = _dot_x3(d * d, bd) * inv
    on = d * lax.rsqrt(var + GN_EPS) * lnw_ref[...] + lnb_ref[...]
    y_ref[...] = ((on + bv_ref[...]) * g_ref[...]).astype(BF16)


def _rwkv_post(ot, bv, g, ln_w, ln_b, bd, tm=512):
    n = bv.shape[0]
    tps = ot.shape[-1] // tm
    tile = pl.BlockSpec((tm, D_RWKV), lambda i: (i, 0))
    hm_t = pl.BlockSpec((1, RWKV_HEADS, HEAD_DIM, tm), lambda i: (i // tps, 0, 0, i % tps))
    full = lambda a: pl.BlockSpec(a.shape, lambda i: (0, 0))
    return pl.pallas_call(
        _rwkv_post_kernel,
        grid=(n // tm,),
        in_specs=[hm_t, tile, tile, full(ln_w), full(ln_b), full(bd)],
        out_specs=tile,
        out_shape=jax.ShapeDtypeStruct((n, D_RWKV), BF16),
        compiler_params=_cparams(("parallel",)),
        name="rwkv_post",
    )(ot, bv, g, ln_w, ln_b, bd)


def _rwkv_mixer(oc, b, t_len, mu, w0, w2, a0, a2, g2, k_k, k_a, r_k, ln_w, ln_b):
    bd = _head_sum_matrix()
    row = lambda z: z.reshape(1, -1)
    r, lw, k, kk, a, vt, bv, g = _rwkv_pre(
        oc, b, t_len, row(mu), row(w0), w2.astype(BF16), row(a0), a2.astype(BF16),
        g2.astype(BF16), row(k_k), row(k_a), row(r_k), bd)
    ot = _rwkv_recurrence(r, lw, k, kk, a, vt)
    return _rwkv_post(ot, bv, g, row(ln_w), row(ln_b), bd)


def _out_proj_kernel(x_ref, yn_ref, yr_ref, wn_ref, wr_ref, g_ref, x1_ref, h2_ref):
    x1 = x_ref[...] + _dot(yn_ref[...], wn_ref[...]) + _dot(yr_ref[...], wr_ref[...])
    x1_ref[...] = x1
    h2_ref[...] = x1 * lax.rsqrt(jnp.mean(x1 * x1, axis=-1, keepdims=True) + EPS) * g_ref[...]


def _expand_w_out(w_out):
    wn = w_out[:D_NSA].reshape(NSA_HEADS, 1, HEAD_DIM, D_MODEL)
    onehot = (np.arange(NSA_HEADS)[:, None] // NSA_GROUP == np.arange(NSA_KV_HEADS)[None, :])
    wn = wn * jnp.asarray(onehot, F32)[:, :, None, None]
    return wn.reshape(Q_EXP, D_MODEL).astype(BF16), w_out[D_NSA:].astype(BF16)


def _out_proj(x2d, y_nsa, y_rwkv, wn, wr, g, tm=512):
    n = x2d.shape[0]
    tile = lambda a: pl.BlockSpec((tm, a.shape[1]), lambda i: (i, 0))
    full = lambda a: pl.BlockSpec(a.shape, lambda i: (0, 0))
    return pl.pallas_call(
        _out_proj_kernel,
        grid=(n // tm,),
        in_specs=[tile(x2d), tile(y_nsa), tile(y_rwkv), full(wn), full(wr), full(g)],
        out_specs=[tile(x2d), tile(x2d)],
        out_shape=[jax.ShapeDtypeStruct((n, D_MODEL), F32)] * 2,
        compiler_params=_cparams(("parallel",)),
        name="out_proj",
    )(x2d, y_nsa, y_rwkv, wn, wr, g)


def _topk_rows(s, k, rid):
    vals, ids = [], []
    for _ in range(k):
        m = jnp.max(s, axis=0, keepdims=True)
        first = jnp.min(jnp.where(s == m, rid, jnp.inf), axis=0, keepdims=True)
        vals.append(m)
        ids.append(first)
        s = jnp.where(rid == first, -jnp.inf, s)
    return jnp.concatenate(vals, axis=0), jnp.concatenate(ids, axis=0)


def _take_rows(table, idx):
    rows = lax.broadcasted_iota(jnp.int32, (table.shape[0], 1), 0).astype(F32)
    out = [jnp.sum(jnp.where(rows == idx[r:r + 1], table, 0.0), axis=0, keepdims=True)
           for r in range(idx.shape[0])]
    return jnp.concatenate(out, axis=0)


def _pair_candidates():
    k = PEER_TOPK
    flat = [0 * k + j for j in range(k)]
    for i in range(1, SUBLANES):
        flat += [i * k + j for j in range(SUBLANES)]
    flat += [i * k for i in range(SUBLANES, k)]
    return np.asarray(flat, np.float32).reshape(-1, 1)


def _peer_route_kernel(h_ref, wq_ref, sk_ref, flat_ref, e_ref, g_ref):
    K = PEER_TOPK
    half = PEER_D_QUERY // 2
    q = _dot(h_ref[...].astype(BF16), wq_ref[...]).astype(BF16)
    key_ids = lax.broadcasted_iota(jnp.int32, (PEER_N_KEYS, 1), 0).astype(F32)
    flat = flat_ref[...]
    rows_e, rows_g = [], []
    for h in range(PEER_HEADS):
        top = []
        for c in range(2):
            col = (h * 2 + c) * half
            s_t = _dot_nt(sk_ref[h, c], q[:, col:col + half])
            top.append(_topk_rows(s_t, K, key_ids))
        (v0, i0), (v1, i1) = top
        cand = jnp.concatenate(
            [v0[0:1] + v1]
            + [v0[i:i + 1] + v1[:SUBLANES] for i in range(1, SUBLANES)]
            + [v0[SUBLANES:] + v1[0:1]], axis=0)
        best, pair = _topk_rows(cand, K, flat)
        pi = jnp.floor(pair * (1.0 / K))
        pj = pair - pi * K
        experts = _take_rows(i0, pi) * float(PEER_N_KEYS) + _take_rows(i1, pj)
        p = jnp.exp(best - jnp.max(best, axis=0, keepdims=True))
        rows_e.append(experts * float(HALF_ROWS))
        rows_g.append(p / jnp.sum(p, axis=0, keepdims=True))
    e_ref[...] = jnp.concatenate(rows_e, axis=0).T.astype(jnp.int32)
    g_ref[...] = jnp.concatenate(rows_g, axis=0).T


def _peer_route(h2, wq, sk, tm=256):
    n = h2.shape[0]
    hk = PEER_HEADS * PEER_TOPK
    flat = jnp.asarray(_pair_candidates())
    out = pl.BlockSpec((tm, hk), lambda i: (i, 0))
    return pl.pallas_call(
        _peer_route_kernel,
        grid=(n // tm,),
        in_specs=[
            pl.BlockSpec((tm, D_MODEL), lambda i: (i, 0)),
            pl.BlockSpec(wq.shape, lambda i: (0, 0)),
            pl.BlockSpec(sk.shape, lambda i: (0, 0, 0, 0)),
            pl.BlockSpec(flat.shape, lambda i: (0, 0)),
        ],
        out_specs=[out, out],
        out_shape=[jax.ShapeDtypeStruct((n, hk), jnp.int32), jax.ShapeDtypeStruct((n, hk), F32)],
        compiler_params=_cparams(("parallel",)),
        name="peer_route",
    )(h2, wq, sk, flat)


PEER_TT = 128
HALF_ROWS = SUBLANES // 2
SUB_ORDER = (0, 4, 2, 6, 1, 5, 3, 7)
HIGH_MASK = 0xFFFF0000


def _pack_table(w):
    bits = lax.bitcast_convert_type(w.astype(BF16), jnp.uint16).astype(jnp.uint32)
    half = w.shape[1] // 2
    packed = bits[:, :half] | (bits[:, half:] << 16)
    return packed.reshape(w.shape[0] * HALF_ROWS, LANES)


def _table_row(tbl_ref, row0):
    return tbl_ref[pl.ds(pl.multiple_of(row0, HALF_ROWS), HALF_ROWS), :]


def _unpack_words(word):
    lo = pltpu.bitcast(word << 16, F32)
    hi = pltpu.bitcast(word & jnp.uint32(HIGH_MASK), F32)
    return lo, hi


def _sublane_tree(c):
    sub = lax.broadcasted_iota(jnp.int32, (SUBLANES, LANES), 0)
    m2 = (sub % 4) < 2
    d = []
    for x, y in zip(c[0::2], c[1::2]):
        d.append(jnp.where(m2, x + pltpu.roll(x, 6, axis=0), y + pltpu.roll(y, 2, axis=0)))
    m1 = (sub % 2) == 0
    x, y = d
    return jnp.where(m1, x + pltpu.roll(x, 7, axis=0), y + pltpu.roll(y, 1, axis=0))


def _peer_act_kernel(e_ref, h_ref, tbl_ref, gate_ref, ones_ref, w_ref, part_ref):
    hk = PEER_HEADS * PEER_TOPK
    tt = h_ref.shape[0]

    def token(t, carry):
        hrow = h_ref[t]
        h_lo = jnp.concatenate([hrow[:HALF_ROWS]] * 2, axis=0)
        h_hi = jnp.concatenate([hrow[HALF_ROWS:]] * 2, axis=0)
        for m in range(hk // SUBLANES):
            es = [e_ref[t, m * SUBLANES + j] for j in SUB_ORDER]
            c = []
            for ea, eb in zip(es[0::2], es[1::2]):
                lo, hi = _unpack_words(jnp.concatenate(
                    [_table_row(tbl_ref, ea), _table_row(tbl_ref, eb)], axis=0))
                c.append(lo * h_lo + hi * h_hi)
            row0 = pl.multiple_of((t * (hk // SUBLANES) + m) * SUBLANES, SUBLANES)
            part_ref[pl.ds(row0, SUBLANES), :] = _sublane_tree(c)
        return carry

    lax.fori_loop(0, tt, token, 0)
    hi, lo = _split2(part_ref[...])
    sums = _dot(jnp.concatenate([hi, lo], axis=1), ones_ref[...])
    sums = sums.reshape(tt, hk, LANES)
    eye = (lax.broadcasted_iota(jnp.int32, (hk, LANES), 0)
           == lax.broadcasted_iota(jnp.int32, (hk, LANES), 1))
    act = jnp.sum(jnp.where(eye[None], sums, 0.0), axis=1)
    w_ref[...] = _gelu(act) * gate_ref[...]


def _table_spec(tbl):
    return pl.BlockSpec(tbl.shape, lambda i: (0, 0), pipeline_mode=pl.Buffered(1))


def _peer_act(experts, h2_tiles, tbl, gates):
    n, hk = experts.shape
    ones = jnp.ones((2 * LANES, LANES), BF16)
    return pl.pallas_call(
        _peer_act_kernel,
        grid=(n // PEER_TT,),
        in_specs=[
            pl.BlockSpec((PEER_TT, hk), lambda i: (i, 0), memory_space=pltpu.SMEM),
            pl.BlockSpec((PEER_TT, SUBLANES, LANES), lambda i: (i, 0, 0)),
            _table_spec(tbl),
            pl.BlockSpec((PEER_TT, hk), lambda i: (i, 0)),
            pl.BlockSpec(ones.shape, lambda i: (0, 0)),
        ],
        out_specs=pl.BlockSpec((PEER_TT, hk), lambda i: (i, 0)),
        out_shape=jax.ShapeDtypeStruct((n, hk), F32),
        scratch_shapes=[pltpu.VMEM((PEER_TT * hk, LANES), F32)],
        compiler_params=_cparams(("arbitrary",)),
        name="peer_act",
    )(experts, h2_tiles, tbl, gates, ones)


PEER_ACCS = 4
PEER_GROUP = 64


def _peer_out_kernel(e_ref, w_ref, x_ref, tbl_ref, g_ref, ones_ref, o_ref, wrep_ref, *,
                     final_norm):
    hk = PEER_HEADS * PEER_TOPK
    tt = x_ref.shape[0]

    eye = (lax.broadcasted_iota(jnp.int32, (hk, LANES), 0)
           == lax.broadcasted_iota(jnp.int32, (hk, LANES), 1))
    diag = jnp.where(eye[None], w_ref[...][:, None, :], 0.0).reshape(tt * hk, LANES)
    wrep_ref[...] = _dot(diag.astype(BF16), ones_ref[...])

    def token(t, carry):
        def group(gi, accs):
            acc_lo, acc_hi = list(accs[0]), list(accs[1])
            base = t * hk + gi * PEER_GROUP
            for j in range(PEER_GROUP):
                lo, hi = _unpack_words(_table_row(tbl_ref, e_ref[base + j]))
                wk = wrep_ref[pl.ds(base + j, 1), :]
                acc_lo[j % PEER_ACCS] = acc_lo[j % PEER_ACCS] + wk * lo
                acc_hi[j % PEER_ACCS] = acc_hi[j % PEER_ACCS] + wk * hi
            return tuple(acc_lo), tuple(acc_hi)

        zeros = tuple(jnp.zeros((HALF_ROWS, LANES), F32) for _ in range(PEER_ACCS))
        acc_lo, acc_hi = lax.fori_loop(0, hk // PEER_GROUP, group, (zeros, zeros))
        tree = lambda a: (a[0] + a[1]) + (a[2] + a[3])
        o_ref[t] = x_ref[t] + jnp.concatenate([tree(acc_lo), tree(acc_hi)], axis=0)
        return carry

    lax.fori_loop(0, tt, token, 0)
    if final_norm:
        x2 = o_ref[...]
        ms = jnp.sum(jnp.sum(x2 * x2, axis=2, keepdims=True), axis=1, keepdims=True) / D_MODEL
        o_ref[...] = x2 * lax.rsqrt(ms + EPS) * g_ref[...]


def _peer_out(experts, w, x1_tiles, tbl, final_g, final_norm):
    n, hk = experts.shape
    ones = jnp.ones((LANES, LANES), BF16)
    smem = pl.BlockSpec((PEER_TT * hk,), lambda i: (i,), memory_space=pltpu.SMEM)
    tile = pl.BlockSpec((PEER_TT, SUBLANES, LANES), lambda i: (i, 0, 0))
    return pl.pallas_call(
        functools.partial(_peer_out_kernel, final_norm=final_norm),
        grid=(n // PEER_TT,),
        in_specs=[smem, pl.BlockSpec((PEER_TT, hk), lambda i: (i, 0)), tile, _table_spec(tbl),
                  pl.BlockSpec(final_g.shape, lambda i: (0, 0, 0)),
                  pl.BlockSpec(ones.shape, lambda i: (0, 0))],
        out_specs=tile,
        out_shape=jax.ShapeDtypeStruct(x1_tiles.shape, F32),
        scratch_shapes=[pltpu.VMEM((PEER_TT * hk, LANES), F32)],
        compiler_params=_cparams(("arbitrary",)),
        name="peer_out",
    )(experts.reshape(-1), w, x1_tiles, tbl, final_g, ones)


def _peer_ffn_residual(x1, h2, wq, sk, tbl_u, tbl_v, final_g, final_norm):
    n = x1.shape[0]
    experts, gates = _peer_route(h2, wq, sk)
    tiles = lambda z: z.reshape(n, SUBLANES, LANES)
    w = _peer_act(experts, tiles(h2), tbl_u, gates)
    g_tile = final_g.reshape(1, SUBLANES, LANES)
    return _peer_out(experts, w, tiles(x1), tbl_v, g_tile, final_norm).reshape(n, D_MODEL)


def kernel(x, norm_mix_g, w_in, cmp_k_pe, cmp_k_w1, cmp_k_w2, cmp_v_pe, cmp_v_w1, cmp_v_w2,
           rwkv_mu, rwkv_w0, rwkv_w2, rwkv_a0, rwkv_a2, rwkv_g2, rwkv_k_k, rwkv_k_a, rwkv_r_k,
           rwkv_ln_w, rwkv_ln_b, w_out, norm_ffn_g, peer_w_q, peer_sub_keys, peer_u, peer_v,
           norm_final_g):
    b, t_len, _ = x.shape
    n = b * t_len
    row = lambda z: z.reshape(1, -1)
    x2d = x.reshape(n, D_MODEL)
    depth = w_in.shape[0]
    for l in range(depth):
        oa, ob, oc = _in_proj(x2d, row(norm_mix_g[l]), _pad_w_in(w_in[l]), t_len)
        nb = t_len // CMP_STRIDE
        rk = ob[:, 0:D_KV].reshape(b, nb, CMP_STRIDE * D_KV)
        rv = ob[:, D_KV:2 * D_KV].reshape(b, nb, CMP_STRIDE * D_KV)
        cmp_k, cmp_v = _compress(
            rk, rv, _expand_cmp_weights(cmp_k_pe[l], cmp_k_w1[l], cmp_k_w2[l], LANES),
            _expand_cmp_weights(cmp_v_pe[l], cmp_v_w1[l], cmp_v_w2[l], HEAD_DIM))
        v_col0 = Q_EXP + 2 * K_AUG
        v_t = lambda c: oa[:, v_col0 + c * D_KV:v_col0 + (c + 1) * D_KV].reshape(
            b, t_len, D_KV).transpose(0, 2, 1)
        y_nsa = _nsa_attention(oa, ob, v_t(0), v_t(1), cmp_k, cmp_v, b, t_len)
        y_rwkv = _rwkv_mixer(oc, b, t_len, rwkv_mu[l], rwkv_w0[l], rwkv_w2[l], rwkv_a0[l],
                             rwkv_a2[l], rwkv_g2[l], rwkv_k_k[l], rwkv_k_a[l], rwkv_r_k[l],
                             rwkv_ln_w[l], rwkv_ln_b[l])
        wn, wr = _expand_w_out(w_out[l])
        x1, h2 = _out_proj(x2d, y_nsa, y_rwkv, wn, wr, row(norm_ffn_g[l]))
        x2d = _peer_ffn_residual(x1, h2, peer_w_q[l].astype(BF16), peer_sub_keys[l].astype(BF16),
                                 _pack_table(peer_u[l]), _pack_table(peer_v[l]),
                                 norm_final_g, final_norm=(l == depth - 1))
    return x2d.reshape(b, t_len, D_MODEL)
```

```python
import functools

import numpy as np
import jax
import jax.numpy as jnp
from jax import lax
from jax.experimental import pallas as pl
from jax.experimental.pallas import tpu as pltpu

F32 = jnp.float32
BF16 = jnp.bfloat16

D_MODEL = 1024
HEAD_DIM = 64
NSA_HEADS = 8
NSA_KV_HEADS = 2
NSA_GROUP = NSA_HEADS // NSA_KV_HEADS
RWKV_HEADS = 8
D_NSA = NSA_HEADS * HEAD_DIM
D_RWKV = RWKV_HEADS * HEAD_DIM
D_KV = NSA_KV_HEADS * HEAD_DIM
CMP_LEN = 32
CMP_STRIDE = 16
CMP_HIDDEN = 128
SEL_LEN = 64
SEL_TOPN = 16
WINDOW = 512
RANK_W = 64
RANK_A = 64
RANK_G = 128
D_RWKV_IN = 3 * D_RWKV + RANK_W + RANK_A + RANK_G
PEER_HEADS = 8
PEER_N_KEYS = 128
PEER_D_QUERY = 256
PEER_TOPK = 16
EPS = 1e-6
GN_EPS = 64e-5

LANES = 128
SUBLANES = 8
VMEM_LIMIT = 56 * 1024 * 1024

NEG_BIG = -1e30
Q_EXP = NSA_HEADS * LANES
K_AUG = NSA_KV_HEADS * LANES
NA_COLS = Q_EXP + 2 * K_AUG + 2 * D_KV
NB_COLS = 3 * LANES
FEAT0 = HEAD_DIM
POS_SPLIT = 64


def _cparams(sem):
    return pltpu.CompilerParams(dimension_semantics=sem, vmem_limit_bytes=VMEM_LIMIT)


def _dot(a, b):
    return jnp.dot(a, b, preferred_element_type=F32)


def _dot_nt(a, b):
    return lax.dot_general(a, b, (((1,), (1,)), ((), ())), preferred_element_type=F32)


def _split2(x):
    hi = x.astype(BF16)
    lo = (x - hi.astype(F32)).astype(BF16)
    return hi, lo


def _dot_x2(x, e):
    hi, lo = _split2(x)
    return _dot(hi, e) + _dot(lo, e)


def _gelu(x):
    return 0.5 * x * (1.0 + jnp.tanh(0.7978845608028654 * (x + 0.044715 * (x * x * x))))


def _sigmoid(x):
    return 1.0 / (1.0 + jnp.exp(-x))


def _key_features(pos, lane):
    hi = (pos // POS_SPLIT).astype(F32)
    lo = (pos % POS_SPLIT).astype(F32)
    return jnp.where(lane == FEAT0, hi, jnp.where(lane == FEAT0 + 1, lo,
                     jnp.where((lane == FEAT0 + 2) | (lane == FEAT0 + 3), 1.0, 0.0)))


def _query_features(t, lane, slope):
    hi = (t // POS_SPLIT).astype(F32)
    lo = (t % POS_SPLIT).astype(F32)
    return jnp.where(lane == FEAT0, slope * POS_SPLIT, jnp.where(
        lane == FEAT0 + 1, slope, jnp.where(
            lane == FEAT0 + 2, -slope * POS_SPLIT * hi, jnp.where(
                lane == FEAT0 + 3, -slope * lo, 0.0))))


def _in_proj_kernel(x_ref, g_ref, w_ref, oa_ref, ob_ref, oc_ref, *, t_len):
    x = x_ref[...]
    tm = x.shape[0]
    h = x * lax.rsqrt(jnp.mean(x * x, axis=-1, keepdims=True) + EPS) * g_ref[...]
    hb = h.astype(BF16)
    t = (pl.program_id(0) * tm + lax.broadcasted_iota(jnp.int32, (tm, 1), 0)) % t_len
    col = lax.broadcasted_iota(jnp.int32, (1, Q_EXP), 1)
    slope = jnp.zeros((1, Q_EXP), F32)
    for hd in range(NSA_HEADS):
        slope = jnp.where(col // LANES == hd, 2.0 ** -(hd + 1), slope)
    q = _dot(hb, w_ref[:, :Q_EXP]) * (HEAD_DIM ** -0.5) + _query_features(t, col % LANES, slope)
    oa_ref[:, :Q_EXP] = q.astype(BF16)
    colk = lax.broadcasted_iota(jnp.int32, (1, 2 * K_AUG), 1)
    k = _dot(hb, w_ref[:, Q_EXP:Q_EXP + 2 * K_AUG]) + _key_features(t, colk % LANES)
    oa_ref[:, Q_EXP:Q_EXP + 2 * K_AUG] = k.astype(BF16)
    oa_ref[:, Q_EXP + 2 * K_AUG:] = _dot(hb, w_ref[:, Q_EXP + 2 * K_AUG:NA_COLS]).astype(BF16)
    ob_ref[...] = _dot(hb, w_ref[:, NA_COLS:NA_COLS + NB_COLS])
    oc_ref[...] = _dot(hb, w_ref[:, NA_COLS + NB_COLS:])


def _in_proj(x2d, g, w_pad, t_len, tm=256):
    n = x2d.shape[0]
    ncols = w_pad.shape[1]
    nc = ncols - NA_COLS - NB_COLS
    return pl.pallas_call(
        functools.partial(_in_proj_kernel, t_len=t_len),
        grid=(n // tm,),
        in_specs=[
            pl.BlockSpec((tm, D_MODEL), lambda i: (i, 0)),
            pl.BlockSpec((1, D_MODEL), lambda i: (0, 0)),
            pl.BlockSpec((D_MODEL, ncols), lambda i: (0, 0)),
        ],
        out_specs=[
            pl.BlockSpec((tm, NA_COLS), lambda i: (i, 0)),
            pl.BlockSpec((tm, NB_COLS), lambda i: (i, 0)),
            pl.BlockSpec((tm, nc), lambda i: (i, 0)),
        ],
        out_shape=[
            jax.ShapeDtypeStruct((n, NA_COLS), BF16),
            jax.ShapeDtypeStruct((n, NB_COLS), F32),
            jax.ShapeDtypeStruct((n, nc), F32),
        ],
        compiler_params=_cparams(("parallel",)),
        name="in_proj",
    )(x2d, g, w_pad)


def _pad_w_in(w_in):
    sizes = (D_NSA, D_KV, D_KV, D_KV, D_KV, D_KV, D_KV, 3 * NSA_HEADS, D_RWKV_IN)
    offs = np.cumsum((0,) + sizes)
    q, kc, vc, ks, vs, kw, vw, gl, rw = (w_in[:, offs[i]:offs[i + 1]] for i in range(9))
    def lane_pad(w, groups):
        w = w.reshape(D_MODEL, groups, HEAD_DIM)
        return jnp.pad(w, ((0, 0), (0, 0), (0, LANES - HEAD_DIM))).reshape(D_MODEL, groups * LANES)

    glp = jnp.pad(gl, ((0, 0), (0, LANES - 3 * NSA_HEADS)))
    return jnp.concatenate(
        [lane_pad(q, NSA_HEADS), lane_pad(ks, NSA_KV_HEADS), lane_pad(kw, NSA_KV_HEADS),
         vs, vw, kc, vc, glp, rw], axis=1).astype(BF16)


def _compress_kernel(rk_ref, rv_ref, pek_ref, pev_ref, w1k_ref, w1v_ref, w2k_ref, w2v_ref,
                     ok_ref, ov_ref):
    for r_ref, pe_ref, w1_ref, w2_ref, o_ref in (
            (rk_ref, pek_ref, w1k_ref, w2k_ref, ok_ref),
            (rv_ref, pev_ref, w1v_ref, w2v_ref, ov_ref)):
        rows = r_ref[0]
        nxt = pltpu.roll(rows, rows.shape[0] - 1, axis=0)
        a = (rows + pe_ref[0:1, :]).astype(BF16)
        b = (nxt + pe_ref[1:2, :]).astype(BF16)
        hid = _dot(a, w1_ref[0]) + _dot(b, w1_ref[1])
        out = _dot(_gelu(hid).astype(BF16), w2_ref[...])
        if o_ref is ok_ref:
            nb = rows.shape[0]
            end = lax.broadcasted_iota(jnp.int32, (nb, 1), 0) * CMP_STRIDE + (CMP_LEN - 1)
            lane = lax.broadcasted_iota(jnp.int32, (1, K_AUG), 1) % LANES
            out = out + _key_features(end, lane)
        else:
            out = out.T
        o_ref[0] = out.astype(BF16)


def _expand_cmp_weights(pe, w1, w2, out_lanes):
    half = CMP_LEN // 2
    eye = jnp.eye(NSA_KV_HEADS, dtype=F32)
    w1r = w1.reshape(2, half, HEAD_DIM, CMP_HIDDEN)
    w1e = w1r[:, :, None, :, None, :] * eye[None, None, :, None, :, None]
    w1e = w1e.reshape(2, half * D_KV, NSA_KV_HEADS * CMP_HIDDEN).astype(BF16)
    pee = jnp.broadcast_to(pe.reshape(2, half, 1, HEAD_DIM), (2, half, NSA_KV_HEADS, HEAD_DIM))
    pee = pee.reshape(2, half * D_KV)
    w2p = jnp.pad(w2, ((0, 0), (0, out_lanes - HEAD_DIM)))
    w2e = (eye[:, None, :, None] * w2p[None, :, None, :]).reshape(
        NSA_KV_HEADS * CMP_HIDDEN, NSA_KV_HEADS * out_lanes).astype(BF16)
    return pee, w1e, w2e


def _compress(rk, rv, wk, wv):
    b, nb, width = rk.shape
    full2 = lambda a: pl.BlockSpec(a.shape, lambda i: (0, 0))
    full3 = lambda a: pl.BlockSpec(a.shape, lambda i: (0, 0, 0))
    row = pl.BlockSpec((1, nb, width), lambda i: (i, 0, 0))
    out = lambda rows, lanes: pl.BlockSpec((1, rows, lanes), lambda i: (i, 0, 0))
    return pl.pallas_call(
        _compress_kernel,
        grid=(b,),
        in_specs=[row, row, full2(wk[0]), full2(wv[0]), full3(wk[1]), full3(wv[1]),
                  full2(wk[2]), full2(wv[2])],
        out_specs=[out(nb, K_AUG), out(D_KV, nb)],
        out_shape=[jax.ShapeDtypeStruct((b, nb, K_AUG), BF16),
                   jax.ShapeDtypeStruct((b, D_KV, nb), BF16)],
        compiler_params=_cparams(("parallel",)),
        name="nsa_compress",
    )(rk, rv, wk[0], wv[0], wk[1], wv[1], wk[2], wv[2])


NSA_TQ = 128
NSA_TK = 1024


def _masked_softmax(s, mask):
    s = jnp.where(mask, s, NEG_BIG)
    s = s - jnp.max(s, axis=-1, keepdims=True)
    p = jnp.where(mask, jnp.exp(s), 0.0)
    return p / jnp.maximum(jnp.sum(p, axis=-1, keepdims=True), 1e-30)


def _flash_update(carry, s, v_t):
    m, l, acc = carry
    m_new = jnp.maximum(m, jnp.max(s, axis=0, keepdims=True))
    alpha = jnp.exp(m - m_new)
    p = jnp.exp(s - m_new)
    l = alpha * l + jnp.sum(p, axis=0, keepdims=True)
    acc = alpha * acc + _dot(v_t, p.astype(BF16))
    return m_new, l, acc


def _nsa_kernel(q_ref, ks_ref, kw_ref, vst_ref, vwt_ref, ck_ref, cvt_ref, ovl_ref, oh_ref, gl_ref,
                o_ref):
    tq = NSA_TQ
    R = NSA_GROUP
    i = pl.program_id(1)
    q0 = i * tq
    t_lane = q0 + lax.broadcasted_iota(jnp.int32, (1, tq), 1)
    t_lanes = jnp.concatenate([t_lane] * R, axis=1)
    n_cmp = ck_ref.shape[1]
    n_sel = ovl_ref.shape[0]

    gate_t = _sigmoid(gl_ref[...]).T

    cmp_end = lax.broadcasted_iota(jnp.int32, (n_cmp, 1), 0) * CMP_STRIDE + (CMP_LEN - 1)
    mask_c = t_lanes >= cmp_end
    ids = lax.broadcasted_iota(jnp.int32, (n_sel, 1), 0)
    idsf = ids.astype(F32)
    cur = t_lane // SEL_LEN
    forced = (ids == 0) | (ids == cur) | (ids == cur - 1)
    valid = ids * SEL_LEN <= t_lane

    G = NSA_KV_HEADS
    lanes_of = [slice(g * LANES, (g + 1) * LANES) for g in range(G)]
    q_st, q_slc, o_cmp = [], [], []
    for g in range(G):
        qh = [q_ref[:, h * LANES:(h + 1) * LANES] for h in range(g * R, (g + 1) * R)]
        q_st.append(jnp.concatenate(qh, axis=0))

        s = jnp.where(mask_c, _dot_nt(ck_ref[0, :, lanes_of[g]], q_st[g]), NEG_BIG)
        s = s - jnp.max(s, axis=0, keepdims=True)
        p = jnp.where(mask_c, jnp.exp(s), 0.0)
        p = p / jnp.maximum(jnp.sum(p, axis=0, keepdims=True), 1e-30)
        o_cmp.append(_dot(cvt_ref[0], p.astype(BF16)))
        p_hi, p_lo = _split2(p)
        imp4 = _dot(ovl_ref[...], p_hi) + _dot(ovl_ref[...], p_lo)
        imp = imp4[:, 0:tq]
        for r in range(1, R):
            imp = imp + imp4[:, r * tq:(r + 1) * tq]
        imp = jnp.where(forced, 1e6, jnp.where(valid, imp, -1.0))

        sel = jnp.zeros((n_sel, tq), F32)
        for _ in range(min(SEL_TOPN, n_sel)):
            m = jnp.max(imp, axis=0, keepdims=True)
            first = jnp.min(jnp.where(imp == m, idsf, float(n_sel)), axis=0, keepdims=True)
            hit = idsf == first
            sel = jnp.where(hit, 1.0, sel)
            imp = jnp.where(hit, -3e38, imp)
        bias_t = jnp.where(sel > 0.5, 0.0, NEG_BIG)
        if n_sel < LANES:
            bias_t = jnp.concatenate([bias_t, jnp.zeros((LANES - n_sel, tq), F32)], axis=0)
        bias = bias_t.T.astype(BF16)

        q_slc.append(jnp.concatenate([q_st[g], jnp.concatenate([bias] * R, axis=0)], axis=1))

    init = tuple((jnp.full((1, R * tq), -3e38, F32), jnp.zeros((1, R * tq), F32),
                  jnp.zeros((LANES, R * tq), F32)) for _ in range(G))

    def slc_tile(j, carry, diagonal):
        k0 = pl.multiple_of(j * NSA_TK, NSA_TK)
        onehot = oh_ref[pl.ds(k0, NSA_TK), :]
        v_t = vst_ref[0, :, pl.ds(k0, NSA_TK)]
        out = []
        for g in range(G):
            k = jnp.concatenate([ks_ref[pl.ds(k0, NSA_TK), lanes_of[g]], onehot], axis=1)
            s = _dot_nt(k, q_slc[g])
            if diagonal:
                pos = k0 + lax.broadcasted_iota(jnp.int32, (NSA_TK, 1), 0)
                s = jnp.where(t_lanes >= pos, s, NEG_BIG)
            out.append(_flash_update(carry[g], s, v_t))
        return tuple(out)

    n_off = q0 // NSA_TK
    carry = lax.fori_loop(0, n_off, functools.partial(slc_tile, diagonal=False), init)
    slc = slc_tile(n_off, carry, True)

    span = WINDOW + tq
    w0 = pl.multiple_of(jnp.maximum(q0 - WINDOW, 0), tq)
    pos = w0 + lax.broadcasted_iota(jnp.int32, (span, 1), 0)
    visible = (t_lanes >= pos) & (pos > t_lanes - WINDOW)
    v_t = vwt_ref[0, :, pl.ds(w0, span)]
    win = [_flash_update(init[g], jnp.where(
        visible, _dot_nt(kw_ref[pl.ds(w0, span), lanes_of[g]], q_st[g]), NEG_BIG), v_t)
        for g in range(G)]

    for h in range(NSA_HEADS):
        g, r = divmod(h, R)
        rows = slice(r * tq, (r + 1) * tq)
        gate = lambda j: gate_t[h * 3 + j:h * 3 + j + 1, :]
        out_t = (gate(0) * o_cmp[g][:, rows]
                 + gate(1) * (slc[g][2][:, rows] / slc[g][1][:, rows])
                 + gate(2) * (win[g][2][:, rows] / win[g][1][:, rows]))
        o_ref[:, h * LANES:(h + 1) * LANES] = out_t.T.astype(BF16)


def _nsa_consts(t_len, n_cmp_pad):
    n_cmp = (t_len - CMP_LEN) // CMP_STRIDE + 1
    n_sel = t_len // SEL_LEN
    cmp_start = np.arange(n_cmp) * CMP_STRIDE
    sel_start = np.arange(n_sel) * SEL_LEN
    ovl = np.clip(np.minimum(cmp_start[:, None] + CMP_LEN, sel_start[None, :] + SEL_LEN)
                  - np.maximum(cmp_start[:, None], sel_start[None, :]), 0, None) / CMP_LEN
    ovl_t = np.zeros((n_sel, n_cmp_pad), np.float32)
    ovl_t[:, :n_cmp] = ovl.T
    onehot = np.zeros((t_len, LANES), np.float32)
    onehot[np.arange(t_len), np.arange(t_len) // SEL_LEN] = 1.0
    return jnp.asarray(ovl_t, BF16), jnp.asarray(onehot, BF16)


def _nsa_attention(oa, ob, vs_t, vw_t, cmp_k, cmp_vt, b, t_len):
    assert NSA_TK % NSA_TQ == 0 and t_len % NSA_TK == 0 and WINDOW % NSA_TQ == 0
    assert t_len >= WINDOW + NSA_TQ and t_len // SEL_LEN <= LANES
    n_cmp_pad = cmp_k.shape[1]
    ovl_t, onehot = _nsa_consts(t_len, n_cmp_pad)
    nq = t_len // NSA_TQ
    k_col0 = Q_EXP // K_AUG
    kspec = lambda c: pl.BlockSpec((t_len, K_AUG), lambda bi, i: (bi, k_col0 + c))
    vspec = pl.BlockSpec((1, D_KV, t_len), lambda bi, i: (bi, 0, 0))
    const2 = lambda a: pl.BlockSpec(a.shape, lambda bi, i: (0, 0))
    return pl.pallas_call(
        _nsa_kernel,
        grid=(b, nq),
        in_specs=[
            pl.BlockSpec((NSA_TQ, Q_EXP), lambda bi, i: (bi * nq + i, 0)),
            kspec(0), kspec(1), vspec, vspec,
            pl.BlockSpec((1, n_cmp_pad, K_AUG), lambda bi, i: (bi, 0, 0)),
            pl.BlockSpec((1, D_KV, n_cmp_pad), lambda bi, i: (bi, 0, 0)),
            const2(ovl_t), const2(onehot),
            pl.BlockSpec((NSA_TQ, LANES), lambda bi, i: (bi * nq + i, 2)),
        ],
        out_specs=pl.BlockSpec((NSA_TQ, Q_EXP), lambda bi, i: (bi * nq + i, 0)),
        out_shape=jax.ShapeDtypeStruct((b * t_len, Q_EXP), BF16),
        compiler_params=_cparams(("parallel", "arbitrary")),
        name="nsa_attention",
    )(oa, oa, oa, vs_t, vw_t, cmp_k, cmp_vt, ovl_t, onehot, ob)


def _dot_x3(x, e):
    hi = x.astype(BF16)
    r1 = x - hi.astype(F32)
    mid = r1.astype(BF16)
    lo = (r1 - mid.astype(F32)).astype(BF16)
    return _dot(hi, e) + _dot(mid, e) + _dot(lo, e)


def _head_sum_matrix():
    ids = np.arange(D_RWKV) // HEAD_DIM
    return jnp.asarray(ids[:, None] == ids[None, :], BF16)


def _rwkv_pre_kernel(p_ref, hp_ref, mu_ref, w0_ref, w2_ref, a0_ref, a2_ref, g2_ref, kk_ref,
                     ka_ref, rk_ref, bd_ref, r_o, lw_o, k_o, kkn_o, a_o, vt_o, bv_o, g_o, *,
                     tiles_per_seq):
    i = pl.program_id(0)
    p = p_ref[...]
    tm = p.shape[0]
    keep = jnp.where(i % tiles_per_seq == 0, 0.0, 1.0)
    halo = hp_ref[SUBLANES - 1:SUBLANES, :] * keep
    prev = pltpu.roll(p, 1, axis=0)
    row0 = lax.broadcasted_iota(jnp.int32, (tm, 1), 0) == 0
    prev = jnp.where(row0, halo, prev)
    ps = p + (prev - p) * mu_ref[...]
    d = D_RWKV
    r = ps[:, 0:d]
    k = ps[:, d:2 * d]
    v = ps[:, 2 * d:3 * d]
    xw = ps[:, 3 * d:3 * d + RANK_W]
    xa = ps[:, 3 * d + RANK_W:3 * d + RANK_W + RANK_A]
    xg = ps[:, 3 * d + RANK_W + RANK_A:]
    z = -(w0_ref[...] + _dot(jnp.tanh(xw).astype(BF16), w2_ref[...]))
    softplus = jnp.maximum(z, 0.0) + jnp.log(1.0 + jnp.exp(-jnp.abs(z)))
    w = -softplus - 0.5
    a = _sigmoid(a0_ref[...] + _dot(xa.astype(BF16), a2_ref[...]))
    g = _dot(_sigmoid(xg).astype(BF16), g2_ref[...])
    kk = k * kk_ref[...]
    ss = _dot_x3(kk * kk, bd_ref[...])
    kk = kk / jnp.maximum(jnp.sqrt(ss), 1e-12)
    k_mod = k * (1.0 + (a - 1.0) * ka_ref[...])
    for o_ref, val in ((r_o, r), (lw_o, -jnp.exp(w)), (k_o, k_mod), (kkn_o, kk), (a_o, a)):
        for hd in range(RWKV_HEADS):
            o_ref[0, hd] = val[:, hd * HEAD_DIM:(hd + 1) * HEAD_DIM]
    vt_o[0] = v.T.reshape(RWKV_HEADS, HEAD_DIM, tm)
    bv_o[...] = _dot_x3(r * k_mod * rk_ref[...], bd_ref[...]) * v
    g_o[...] = g


def _rwkv_pre(oc, b, t_len, mu, w0, w2, a0, a2, g2, k_k, k_a, r_k, bd, tm=256):
    n = oc.shape[0]
    tps = t_len // tm
    halo_blocks = tm // SUBLANES
    full = lambda a: pl.BlockSpec(a.shape, lambda i: (0, 0))
    tok = pl.BlockSpec((tm, D_RWKV), lambda i: (i, 0))
    hm = pl.BlockSpec((1, RWKV_HEADS, tm, HEAD_DIM), lambda i: (i // tps, 0, i % tps, 0))
    hm_t = pl.BlockSpec((1, RWKV_HEADS, HEAD_DIM, tm), lambda i: (i // tps, 0, 0, i % tps))
    params = (mu, w0, w2, a0, a2, g2, k_k, k_a, r_k, bd)
    hm_shape = jax.ShapeDtypeStruct((b, RWKV_HEADS, t_len, HEAD_DIM), F32)
    return pl.pallas_call(
        functools.partial(_rwkv_pre_kernel, tiles_per_seq=tps),
        grid=(n // tm,),
        in_specs=[
            pl.BlockSpec((tm, D_RWKV_IN), lambda i: (i, 0)),
            pl.BlockSpec((SUBLANES, D_RWKV_IN),
                         lambda i: (jnp.maximum(i * halo_blocks - 1, 0), 0)),
        ] + [full(a) for a in params],
        out_specs=[hm] * 5 + [hm_t, tok, tok],
        out_shape=[hm_shape] * 5
        + [jax.ShapeDtypeStruct((b, RWKV_HEADS, HEAD_DIM, t_len), F32)]
        + [jax.ShapeDtypeStruct((n, D_RWKV), F32)] * 2,
        compiler_params=_cparams(("parallel",)),
        name="rwkv_pre",
    )(oc, oc, *params)


RWKV_C = 64
RWKV_CB = 128
RWKV_PASSES = 1


def _bmm(eq, a, b):
    ein = lambda x, y: jnp.einsum(eq, x, y, preferred_element_type=F32)
    if RWKV_PASSES == 1:
        return ein(a.astype(BF16), b.astype(BF16))
    ah, al = _split2(a)
    bh, bl = _split2(b)
    return ein(ah, bh) + ein(ah, bl) + ein(al, bh)


def _cumsum_rows(x2d, seg):
    rows = lax.broadcasted_iota(jnp.int32, (x2d.shape[0], 1), 0) % seg
    step = 1
    while step < seg:
        shifted = pltpu.roll(x2d, step, axis=0)
        x2d = x2d + jnp.where(rows >= step, shifted, 0.0)
        step *= 2
    return x2d


def _rwkv_rec_kernel(r_ref, lw_ref, k_ref, kk_ref, a_ref, vt_ref, ot_ref, s_ref):
    H = r_ref.shape[1]
    C = RWKV_C

    @pl.when(pl.program_id(1) == 0)
    def _():
        s_ref[...] = jnp.zeros_like(s_ref)

    ri = lax.broadcasted_iota(jnp.int32, (C, C), 0)
    ci = lax.broadcasted_iota(jnp.int32, (C, C), 1)
    strict = (ri > ci)[None]
    incl = (ri >= ci)[None]
    eye = (ri == ci).astype(F32)[None]

    for sub in range(RWKV_CB // C):
        rows = slice(sub * C, (sub + 1) * C)
        r = r_ref[0, :, rows, :]
        lw = lw_ref[0, :, rows, :]
        k = k_ref[0, :, rows, :]
        kk = kk_ref[0, :, rows, :]
        a = a_ref[0, :, rows, :]
        vt = vt_ref[0, :, :, rows]
        s0 = s_ref[...]

        cum = _cumsum_rows(lw.reshape(H * C, HEAD_DIM), C).reshape(H, C, HEAD_DIM)
        cum_last = cum[:, C - 1:C, :]
        p_inv = jnp.exp(-cum)
        w_last = jnp.exp(cum_last - cum)
        bm = kk * a
        at = -kk * jnp.exp(cum - lw)
        rt = r * jnp.exp(cum)
        bt = bm * p_inv
        kt = k * p_inv

        nt = 'hik,hjk->hij'
        m_ab = jnp.where(strict, _bmm(nt, at, bt), 0.0)
        m_ak = jnp.where(strict, _bmm(nt, at, kt), 0.0)
        n_rb = jnp.where(incl, _bmm(nt, rt, bt), 0.0)
        n_rk = jnp.where(incl, _bmm(nt, rt, kt), 0.0)

        tinv = eye + m_ab
        mp = m_ab
        step = 1
        while 2 * step < C:
            mp = _bmm('hij,hjk->hik', mp, mp)
            tinv = tinv + _bmm('hij,hjk->hik', tinv, mp)
            step *= 2

        rhs_t = _bmm('hvk,hik->hvi', s0, at) + _bmm('hvj,hij->hvi', vt, m_ak)
        ut = _bmm('hvj,hij->hvi', rhs_t, tinv)
        ot = (_bmm('hvk,hik->hvi', s0, rt) + _bmm('hvj,hij->hvi', ut, n_rb)
              + _bmm('hvj,hij->hvi', vt, n_rk))
        ot_ref[0, :, :, rows] = ot
        s_ref[...] = (s0 * jnp.exp(cum_last) + _bmm('hvj,hjk->hvk', ut, bm * w_last)
                      + _bmm('hvj,hjk->hvk', vt, k * w_last))


def _rwkv_recurrence(r, lw, k, kk, a, vt):
    b, h, t_len, d = r.shape
    tok = pl.BlockSpec((1, h, RWKV_CB, d), lambda bi, c: (bi, 0, c, 0))
    tr = pl.BlockSpec((1, h, d, RWKV_CB), lambda bi, c: (bi, 0, 0, c))
    return pl.pallas_call(
        _rwkv_rec_kernel,
        grid=(b, t_len // RWKV_CB),
        in_specs=[tok] * 5 + [tr],
        out_specs=tr,
        out_shape=jax.ShapeDtypeStruct((b, h, d, t_len), F32),
        scratch_shapes=[pltpu.VMEM((h, d, d), F32)],
        compiler_params=_cparams(("parallel", "arbitrary")),
        name="rwkv_recurrence",
    )(r, lw, k, kk, a, vt)


def _rwkv_post_kernel(ot_ref, bv_ref, g_ref, lnw_ref, lnb_ref, bd_ref, y_ref):
    tm = bv_ref.shape[0]
    o = ot_ref[0].reshape(D_RWKV, tm).T
    bd = bd_ref[...]
    inv = 1.0 / HEAD_DIM
    mean = _dot_x3(o, bd) * inv
    d = o - mean
    var = _dot_x3(d * d, bd) * inv
    on = d * lax.rsqrt(var + GN_EPS) * lnw_ref[...] + lnb_ref[...]
    y_ref[...] = ((on + bv_ref[...]) * g_ref[...]).astype(BF16)


def _rwkv_post(ot, bv, g, ln_w, ln_b, bd, tm=512):
    n = bv.shape[0]
    tps = ot.shape[-1] // tm
    tile = pl.BlockSpec((tm, D_RWKV), lambda i: (i, 0))
    hm_t = pl.BlockSpec((1, RWKV_HEADS, HEAD_DIM, tm), lambda i: (i // tps, 0, 0, i % tps))
    full = lambda a: pl.BlockSpec(a.shape, lambda i: (0, 0))
    return pl.pallas_call(
        _rwkv_post_kernel,
        grid=(n // tm,),
        in_specs=[hm_t, tile, tile, full(ln_w), full(ln_b), full(bd)],
        out_specs=tile,
        out_shape=jax.ShapeDtypeStruct((n, D_RWKV), BF16),
        compiler_params=_cparams(("parallel",)),
        name="rwkv_post",
    )(ot, bv, g, ln_w, ln_b, bd)


def _rwkv_mixer(oc, b, t_len, mu, w0, w2, a0, a2, g2, k_k, k_a, r_k, ln_w, ln_b):
    bd = _head_sum_matrix()
    row = lambda z: z.reshape(1, -1)
    r, lw, k, kk, a, vt, bv, g = _rwkv_pre(
        oc, b, t_len, row(mu), row(w0), w2.astype(BF16), row(a0), a2.astype(BF16),
        g2.astype(BF16), row(k_k), row(k_a), row(r_k), bd)
    ot = _rwkv_recurrence(r, lw, k, kk, a, vt)
    return _rwkv_post(ot, bv, g, row(ln_w), row(ln_b), bd)


def _out_proj_kernel(x_ref, yn_ref, yr_ref, wn_ref, wr_ref, g_ref, x1_ref, h2_ref):
    x1 = x_ref[...] + _dot(yn_ref[...], wn_ref[...]) + _dot(yr_ref[...], wr_ref[...])
    x1_ref[...] = x1
    h2_ref[...] = x1 * lax.rsqrt(jnp.mean(x1 * x1, axis=-1, keepdims=True) + EPS) * g_ref[...]


def _expand_w_out(w_out):
    wn = w_out[:D_NSA].reshape(NSA_HEADS, 1, HEAD_DIM, D_MODEL)
    onehot = (np.arange(NSA_HEADS)[:, None] // NSA_GROUP == np.arange(NSA_KV_HEADS)[None, :])
    wn = wn * jnp.asarray(onehot, F32)[:, :, None, None]
    return wn.reshape(Q_EXP, D_MODEL).astype(BF16), w_out[D_NSA:].astype(BF16)


def _out_proj(x2d, y_nsa, y_rwkv, wn, wr, g, tm=512):
    n = x2d.shape[0]
    tile = lambda a: pl.BlockSpec((tm, a.shape[1]), lambda i: (i, 0))
    full = lambda a: pl.BlockSpec(a.shape, lambda i: (0, 0))
    return pl.pallas_call(
        _out_proj_kernel,
        grid=(n // tm,),
        in_specs=[tile(x2d), tile(y_nsa), tile(y_rwkv), full(wn), full(wr), full(g)],
        out_specs=[tile(x2d), tile(x2d)],
        out_shape=[jax.ShapeDtypeStruct((n, D_MODEL), F32)] * 2,
        compiler_params=_cparams(("parallel",)),
        name="out_proj",
    )(x2d, y_nsa, y_rwkv, wn, wr, g)


def _topk_rows(s, k, rid):
    vals, ids = [], []
    for _ in range(k):
        m = jnp.max(s, axis=0, keepdims=True)
        first = jnp.min(jnp.where(s == m, rid, jnp.inf), axis=0, keepdims=True)
        vals.append(m)
        ids.append(first)
        s = jnp.where(rid == first, -jnp.inf, s)
    return jnp.concatenate(vals, axis=0), jnp.concatenate(ids, axis=0)


def _take_rows(table, idx):
    rows = lax.broadcasted_iota(jnp.int32, (table.shape[0], 1), 0).astype(F32)
    out = [jnp.sum(jnp.where(rows == idx[r:r + 1], table, 0.0), axis=0, keepdims=True)
           for r in range(idx.shape[0])]
    return jnp.concatenate(out, axis=0)


def _pair_candidates():
    k = PEER_TOPK
    flat = [0 * k + j for j in range(k)]
    for i in range(1, SUBLANES):
        flat += [i * k + j for j in range(SUBLANES)]
    flat += [i * k for i in range(SUBLANES, k)]
    return np.asarray(flat, np.float32).reshape(-1, 1)


def _peer_route_kernel(h_ref, wq_ref, sk_ref, flat_ref, e_ref, g_ref):
    K = PEER_TOPK
    half = PEER_D_QUERY // 2
    q = _dot(h_ref[...].astype(BF16), wq_ref[...]).astype(BF16)
    key_ids = lax.broadcasted_iota(jnp.int32, (PEER_N_KEYS, 1), 0).astype(F32)
    flat = flat_ref[...]
    rows_e, rows_g = [], []
    for h in range(PEER_HEADS):
        top = []
        for c in range(2):
            col = (h * 2 + c) * half
            s_t = _dot_nt(sk_ref[h, c], q[:, col:col + half])
            top.append(_topk_rows(s_t, K, key_ids))
        (v0, i0), (v1, i1) = top
        cand = jnp.concatenate(
            [v0[0:1] + v1]
            + [v0[i:i + 1] + v1[:SUBLANES] for i in range(1, SUBLANES)]
            + [v0[SUBLANES:] + v1[0:1]], axis=0)
        best, pair = _topk_rows(cand, K, flat)
        pi = jnp.floor(pair * (1.0 / K))
        pj = pair - pi * K
        experts = _take_rows(i0, pi) * float(PEER_N_KEYS) + _take_rows(i1, pj)
        p = jnp.exp(best - jnp.max(best, axis=0, keepdims=True))
        rows_e.append(experts * float(HALF_ROWS))
        rows_g.append(p / jnp.sum(p, axis=0, keepdims=True))
    e_ref[...] = jnp.concatenate(rows_e, axis=0).T.astype(jnp.int32)
    g_ref[...] = jnp.concatenate(rows_g, axis=0).T


def _peer_route(h2, wq, sk, tm=256):
    n = h2.shape[0]
    hk = PEER_HEADS * PEER_TOPK
    flat = jnp.asarray(_pair_candidates())
    out = pl.BlockSpec((tm, hk), lambda i: (i, 0))
    return pl.pallas_call(
        _peer_route_kernel,
        grid=(n // tm,),
        in_specs=[
            pl.BlockSpec((tm, D_MODEL), lambda i: (i, 0)),
            pl.BlockSpec(wq.shape, lambda i: (0, 0)),
            pl.BlockSpec(sk.shape, lambda i: (0, 0, 0, 0)),
            pl.BlockSpec(flat.shape, lambda i: (0, 0)),
        ],
        out_specs=[out, out],
        out_shape=[jax.ShapeDtypeStruct((n, hk), jnp.int32), jax.ShapeDtypeStruct((n, hk), F32)],
        compiler_params=_cparams(("parallel",)),
        name="peer_route",
    )(h2, wq, sk, flat)


PEER_TT = 128
HALF_ROWS = SUBLANES // 2
SUB_ORDER = (0, 4, 2, 6, 1, 5, 3, 7)
HIGH_MASK = 0xFFFF0000


def _pack_table(w):
    bits = lax.bitcast_convert_type(w.astype(BF16), jnp.uint16).astype(jnp.uint32)
    half = w.shape[1] // 2
    packed = bits[:, :half] | (bits[:, half:] << 16)
    return packed.reshape(w.shape[0] * HALF_ROWS, LANES)


def _table_row(tbl_ref, row0):
    return tbl_ref[pl.ds(pl.multiple_of(row0, HALF_ROWS), HALF_ROWS), :]


def _unpack_words(word):
    lo = pltpu.bitcast(word << 16, F32)
    hi = pltpu.bitcast(word & jnp.uint32(HIGH_MASK), F32)
    return lo, hi


def _sublane_tree(c):
    sub = lax.broadcasted_iota(jnp.int32, (SUBLANES, LANES), 0)
    m2 = (sub % 4) < 2
    d = []
    for x, y in zip(c[0::2], c[1::2]):
        d.append(jnp.where(m2, x + pltpu.roll(x, 6, axis=0), y + pltpu.roll(y, 2, axis=0)))
    m1 = (sub % 2) == 0
    x, y = d
    return jnp.where(m1, x + pltpu.roll(x, 7, axis=0), y + pltpu.roll(y, 1, axis=0))


def _peer_act_kernel(e_ref, h_ref, tbl_ref, gate_ref, ones_ref, w_ref, part_ref):
    hk = PEER_HEADS * PEER_TOPK
    tt = h_ref.shape[0]

    def token(t, carry):
        hrow = h_ref[t]
        h_lo = jnp.concatenate([hrow[:HALF_ROWS]] * 2, axis=0)
        h_hi = jnp.concatenate([hrow[HALF_ROWS:]] * 2, axis=0)
        for m in range(hk // SUBLANES):
            es = [e_ref[t, m * SUBLANES + j] for j in SUB_ORDER]
            c = []
            for ea, eb in zip(es[0::2], es[1::2]):
                lo, hi = _unpack_words(jnp.concatenate(
                    [_table_row(tbl_ref, ea), _table_row(tbl_ref, eb)], axis=0))
                c.append(lo * h_lo + hi * h_hi)
            row0 = pl.multiple_of((t * (hk // SUBLANES) + m) * SUBLANES, SUBLANES)
            part_ref[pl.ds(row0, SUBLANES), :] = _sublane_tree(c)
        return carry

    lax.fori_loop(0, tt, token, 0)
    hi, lo = _split2(part_ref[...])
    sums = _dot(jnp.concatenate([hi, lo], axis=1), ones_ref[...])
    sums = sums.reshape(tt, hk, LANES)
    eye = (lax.broadcasted_iota(jnp.int32, (hk, LANES), 0)
           == lax.broadcasted_iota(jnp.int32, (hk, LANES), 1))
    act = jnp.sum(jnp.where(eye[None], sums, 0.0), axis=1)
    w_ref[...] = _gelu(act) * gate_ref[...]


def _table_spec(tbl):
    return pl.BlockSpec(tbl.shape, lambda i: (0, 0), pipeline_mode=pl.Buffered(1))


def _peer_act(experts, h2_tiles, tbl, gates):
    n, hk = experts.shape
    ones = jnp.ones((2 * LANES, LANES), BF16)
    return pl.pallas_call(
        _peer_act_kernel,
        grid=(n // PEER_TT,),
        in_specs=[
            pl.BlockSpec((PEER_TT, hk), lambda i: (i, 0), memory_space=pltpu.SMEM),
            pl.BlockSpec((PEER_TT, SUBLANES, LANES), lambda i: (i, 0, 0)),
            _table_spec(tbl),
            pl.BlockSpec((PEER_TT, hk), lambda i: (i, 0)),
            pl.BlockSpec(ones.shape, lambda i: (0, 0)),
        ],
        out_specs=pl.BlockSpec((PEER_TT, hk), lambda i: (i, 0)),
        out_shape=jax.ShapeDtypeStruct((n, hk), F32),
        scratch_shapes=[pltpu.VMEM((PEER_TT * hk, LANES), F32)],
        compiler_params=_cparams(("arbitrary",)),
        name="peer_act",
    )(experts, h2_tiles, tbl, gates, ones)


PEER_ACCS = 4
PEER_GROUP = 64


def _peer_out_kernel(e_ref, w_ref, x_ref, tbl_ref, g_ref, ones_ref, o_ref, wrep_ref, *,
                     final_norm):
    hk = PEER_HEADS * PEER_TOPK
    tt = x_ref.shape[0]

    eye = (lax.broadcasted_iota(jnp.int32, (hk, LANES), 0)
           == lax.broadcasted_iota(jnp.int32, (hk, LANES), 1))
    diag = jnp.where(eye[None], w_ref[...][:, None, :], 0.0).reshape(tt * hk, LANES)
    wrep_ref[...] = _dot(diag.astype(BF16), ones_ref[...])

    def token(t, carry):
        def group(gi, accs):
            acc_lo, acc_hi = list(accs[0]), list(accs[1])
            base = t * hk + gi * PEER_GROUP
            for j in range(PEER_GROUP):
                lo, hi = _unpack_words(_table_row(tbl_ref, e_ref[base + j]))
                wk = wrep_ref[pl.ds(base + j, 1), :]
                acc_lo[j % PEER_ACCS] = acc_lo[j % PEER_ACCS] + wk * lo
                acc_hi[j % PEER_ACCS] = acc_hi[j % PEER_ACCS] + wk * hi
            return tuple(acc_lo), tuple(acc_hi)

        zeros = tuple(jnp.zeros((HALF_ROWS, LANES), F32) for _ in range(PEER_ACCS))
        acc_lo, acc_hi = lax.fori_loop(0, hk // PEER_GROUP, group, (zeros, zeros))
        tree = lambda a: (a[0] + a[1]) + (a[2] + a[3])
        o_ref[t] = x_ref[t] + jnp.concatenate([tree(acc_lo), tree(acc_hi)], axis=0)
        return carry

    lax.fori_loop(0, tt, token, 0)
    if final_norm:
        x2 = o_ref[...]
        ms = jnp.sum(jnp.sum(x2 * x2, axis=2, keepdims=True), axis=1, keepdims=True) / D_MODEL
        o_ref[...] = x2 * lax.rsqrt(ms + EPS) * g_ref[...]


def _peer_out(experts, w, x1_tiles, tbl, final_g, final_norm):
    n, hk = experts.shape
    ones = jnp.ones((LANES, LANES), BF16)
    smem = pl.BlockSpec((PEER_TT * hk,), lambda i: (i,), memory_space=pltpu.SMEM)
    tile = pl.BlockSpec((PEER_TT, SUBLANES, LANES), lambda i: (i, 0, 0))
    return pl.pallas_call(
        functools.partial(_peer_out_kernel, final_norm=final_norm),
        grid=(n // PEER_TT,),
        in_specs=[smem, pl.BlockSpec((PEER_TT, hk), lambda i: (i, 0)), tile, _table_spec(tbl),
                  pl.BlockSpec(final_g.shape, lambda i: (0, 0, 0)),
                  pl.BlockSpec(ones.shape, lambda i: (0, 0))],
        out_specs=tile,
        out_shape=jax.ShapeDtypeStruct(x1_tiles.shape, F32),
        scratch_shapes=[pltpu.VMEM((PEER_TT * hk, LANES), F32)],
        compiler_params=_cparams(("arbitrary",)),
        name="peer_out",
    )(experts.reshape(-1), w, x1_tiles, tbl, final_g, ones)


def _peer_ffn_residual(x1, h2, wq, sk, tbl_u, tbl_v, final_g, final_norm):
    n = x1.shape[0]
    experts, gates = _peer_route(h2, wq, sk)
    tiles = lambda z: z.reshape(n, SUBLANES, LANES)
    w = _peer_act(experts, tiles(h2), tbl_u, gates)
    g_tile = final_g.reshape(1, SUBLANES, LANES)
    return _peer_out(experts, w, tiles(x1), tbl_v, g_tile, final_norm).reshape(n, D_MODEL)


def kernel(x, norm_mix_g, w_in, cmp_k_pe, cmp_k_w1, cmp_k_w2, cmp_v_pe, cmp_v_w1, cmp_v_w2,
           rwkv_mu, rwkv_w0, rwkv_w2, rwkv_a0, rwkv_a2, rwkv_g2, rwkv_k_k, rwkv_k_a, rwkv_r_k,
           rwkv_ln_w, rwkv_ln_b, w_out, norm_ffn_g, peer_w_q, peer_sub_keys, peer_u, peer_v,
           norm_final_g):
    b, t_len, _ = x.shape
    n = b * t_len
    row = lambda z: z.reshape(1, -1)
    x2d = x.reshape(n, D_MODEL)
    depth = w_in.shape[0]
    for l in range(depth):
        oa, ob, oc = _in_proj(x2d, row(norm_mix_g[l]), _pad_w_in(w_in[l]), t_len)
        nb = t_len // CMP_STRIDE
        rk = ob[:, 0:D_KV].reshape(b, nb, CMP_STRIDE * D_KV)
        rv = ob[:, D_KV:2 * D_KV].reshape(b, nb, CMP_STRIDE * D_KV)
        cmp_k, cmp_vt = _compress(
            rk, rv, _expand_cmp_weights(cmp_k_pe[l], cmp_k_w1[l], cmp_k_w2[l], LANES),
            _expand_cmp_weights(cmp_v_pe[l], cmp_v_w1[l], cmp_v_w2[l], HEAD_DIM))
        v_col0 = Q_EXP + 2 * K_AUG
        v_t = lambda c: oa[:, v_col0 + c * D_KV:v_col0 + (c + 1) * D_KV].reshape(
            b, t_len, D_KV).transpose(0, 2, 1)
        y_nsa = _nsa_attention(oa, ob, v_t(0), v_t(1), cmp_k, cmp_vt, b, t_len)
        y_rwkv = _rwkv_mixer(oc, b, t_len, rwkv_mu[l], rwkv_w0[l], rwkv_w2[l], rwkv_a0[l],
                             rwkv_a2[l], rwkv_g2[l], rwkv_k_k[l], rwkv_k_a[l], rwkv_r_k[l],
                             rwkv_ln_w[l], rwkv_ln_b[l])
        wn, wr = _expand_w_out(w_out[l])
        x1, h2 = _out_proj(x2d, y_nsa, y_rwkv, wn, wr, row(norm_ffn_g[l]))
        x2d = _peer_ffn_residual(x1, h2, peer_w_q[l].astype(BF16), peer_sub_keys[l].astype(BF16),
                                 _pack_table(peer_u[l]), _pack_table(peer_v[l]),
                                 norm_final_g, final_norm=(l == depth - 1))
    return x2d.reshape(b, t_len, D_MODEL)
```

```python
import functools

import numpy as np
import jax
import jax.numpy as jnp
from jax import lax
from jax.experimental import pallas as pl
from jax.experimental.pallas import tpu as pltpu

F32 = jnp.float32
BF16 = jnp.bfloat16

D_MODEL = 1024
HEAD_DIM = 64
NSA_HEADS = 8
NSA_KV_HEADS = 2
NSA_GROUP = NSA_HEADS // NSA_KV_HEADS
RWKV_HEADS = 8
D_NSA = NSA_HEADS * HEAD_DIM
D_RWKV = RWKV_HEADS * HEAD_DIM
D_KV = NSA_KV_HEADS * HEAD_DIM
CMP_LEN = 32
CMP_STRIDE = 16
CMP_HIDDEN = 128
SEL_LEN = 64
SEL_TOPN = 16
WINDOW = 512
RANK_W = 64
RANK_A = 64
RANK_G = 128
D_RWKV_IN = 3 * D_RWKV + RANK_W + RANK_A + RANK_G
PEER_HEADS = 8
PEER_N_KEYS = 128
PEER_D_QUERY = 256
PEER_TOPK = 16
EPS = 1e-6
GN_EPS = 64e-5

LANES = 128
SUBLANES = 8
VMEM_LIMIT = 56 * 1024 * 1024

NEG_BIG = -1e30
Q_EXP = NSA_HEADS * LANES
K_AUG = NSA_KV_HEADS * LANES
NA_COLS = Q_EXP + 2 * K_AUG + 2 * D_KV
NB_COLS = 3 * LANES
FEAT0 = HEAD_DIM
POS_SPLIT = 64


def _cparams(sem):
    return pltpu.CompilerParams(dimension_semantics=sem, vmem_limit_bytes=VMEM_LIMIT)


def _dot(a, b):
    return jnp.dot(a, b, preferred_element_type=F32)


def _dot_nt(a, b):
    return lax.dot_general(a, b, (((1,), (1,)), ((), ())), preferred_element_type=F32)


def _split2(x):
    hi = x.astype(BF16)
    lo = (x - hi.astype(F32)).astype(BF16)
    return hi, lo


def _dot_x2(x, e):
    hi, lo = _split2(x)
    return _dot(hi, e) + _dot(lo, e)


def _gelu(x):
    return 0.5 * x * (1.0 + jnp.tanh(0.7978845608028654 * (x + 0.044715 * (x * x * x))))


def _sigmoid(x):
    return 1.0 / (1.0 + jnp.exp(-x))


def _key_features(pos, lane):
    hi = (pos // POS_SPLIT).astype(F32)
    lo = (pos % POS_SPLIT).astype(F32)
    return jnp.where(lane == FEAT0, hi, jnp.where(lane == FEAT0 + 1, lo,
                     jnp.where((lane == FEAT0 + 2) | (lane == FEAT0 + 3), 1.0, 0.0)))


def _query_features(t, lane, slope):
    hi = (t // POS_SPLIT).astype(F32)
    lo = (t % POS_SPLIT).astype(F32)
    return jnp.where(lane == FEAT0, slope * POS_SPLIT, jnp.where(
        lane == FEAT0 + 1, slope, jnp.where(
            lane == FEAT0 + 2, -slope * POS_SPLIT * hi, jnp.where(
                lane == FEAT0 + 3, -slope * lo, 0.0))))


def _in_proj_kernel(x_ref, g_ref, w_ref, oa_ref, ob_ref, oc_ref, *, t_len):
    x = x_ref[...]
    tm = x.shape[0]
    h = x * lax.rsqrt(jnp.mean(x * x, axis=-1, keepdims=True) + EPS) * g_ref[...]
    hb = h.astype(BF16)
    t = (pl.program_id(0) * tm + lax.broadcasted_iota(jnp.int32, (tm, 1), 0)) % t_len
    col = lax.broadcasted_iota(jnp.int32, (1, Q_EXP), 1)
    slope = jnp.zeros((1, Q_EXP), F32)
    for hd in range(NSA_HEADS):
        slope = jnp.where(col // LANES == hd, 2.0 ** -(hd + 1), slope)
    q = _dot(hb, w_ref[:, :Q_EXP]) * (HEAD_DIM ** -0.5) + _query_features(t, col % LANES, slope)
    oa_ref[:, :Q_EXP] = q.astype(BF16)
    colk = lax.broadcasted_iota(jnp.int32, (1, 2 * K_AUG), 1)
    k = _dot(hb, w_ref[:, Q_EXP:Q_EXP + 2 * K_AUG]) + _key_features(t, colk % LANES)
    oa_ref[:, Q_EXP:Q_EXP + 2 * K_AUG] = k.astype(BF16)
    oa_ref[:, Q_EXP + 2 * K_AUG:] = _dot(hb, w_ref[:, Q_EXP + 2 * K_AUG:NA_COLS]).astype(BF16)
    ob_ref[...] = _dot(hb, w_ref[:, NA_COLS:NA_COLS + NB_COLS])
    oc_ref[...] = _dot(hb, w_ref[:, NA_COLS + NB_COLS:])


def _in_proj(x2d, g, w_pad, t_len, tm=256):
    n = x2d.shape[0]
    ncols = w_pad.shape[1]
    nc = ncols - NA_COLS - NB_COLS
    return pl.pallas_call(
        functools.partial(_in_proj_kernel, t_len=t_len),
        grid=(n // tm,),
        in_specs=[
            pl.BlockSpec((tm, D_MODEL), lambda i: (i, 0)),
            pl.BlockSpec((1, D_MODEL), lambda i: (0, 0)),
            pl.BlockSpec((D_MODEL, ncols), lambda i: (0, 0)),
        ],
        out_specs=[
            pl.BlockSpec((tm, NA_COLS), lambda i: (i, 0)),
            pl.BlockSpec((tm, NB_COLS), lambda i: (i, 0)),
            pl.BlockSpec((tm, nc), lambda i: (i, 0)),
        ],
        out_shape=[
            jax.ShapeDtypeStruct((n, NA_COLS), BF16),
            jax.ShapeDtypeStruct((n, NB_COLS), F32),
            jax.ShapeDtypeStruct((n, nc), F32),
        ],
        compiler_params=_cparams(("parallel",)),
        name="in_proj",
    )(x2d, g, w_pad)


def _pad_w_in(w_in):
    sizes = (D_NSA, D_KV, D_KV, D_KV, D_KV, D_KV, D_KV, 3 * NSA_HEADS, D_RWKV_IN)
    offs = np.cumsum((0,) + sizes)
    q, kc, vc, ks, vs, kw, vw, gl, rw = (w_in[:, offs[i]:offs[i + 1]] for i in range(9))
    def lane_pad(w, groups):
        w = w.reshape(D_MODEL, groups, HEAD_DIM)
        return jnp.pad(w, ((0, 0), (0, 0), (0, LANES - HEAD_DIM))).reshape(D_MODEL, groups * LANES)

    glp = jnp.pad(gl, ((0, 0), (0, LANES - 3 * NSA_HEADS)))
    return jnp.concatenate(
        [lane_pad(q, NSA_HEADS), lane_pad(ks, NSA_KV_HEADS), lane_pad(kw, NSA_KV_HEADS),
         vs, vw, kc, vc, glp, rw], axis=1).astype(BF16)


def _compress_kernel(rk_ref, rv_ref, pek_ref, pev_ref, w1k_ref, w1v_ref, w2k_ref, w2v_ref,
                     ok_ref, ov_ref):
    for r_ref, pe_ref, w1_ref, w2_ref, o_ref in (
            (rk_ref, pek_ref, w1k_ref, w2k_ref, ok_ref),
            (rv_ref, pev_ref, w1v_ref, w2v_ref, ov_ref)):
        rows = r_ref[0]
        nxt = pltpu.roll(rows, rows.shape[0] - 1, axis=0)
        a = (rows + pe_ref[0:1, :]).astype(BF16)
        b = (nxt + pe_ref[1:2, :]).astype(BF16)
        hid = _dot(a, w1_ref[0]) + _dot(b, w1_ref[1])
        out = _dot(_gelu(hid).astype(BF16), w2_ref[...])
        if o_ref is ok_ref:
            nb = rows.shape[0]
            end = lax.broadcasted_iota(jnp.int32, (nb, 1), 0) * CMP_STRIDE + (CMP_LEN - 1)
            lane = lax.broadcasted_iota(jnp.int32, (1, K_AUG), 1) % LANES
            out = out + _key_features(end, lane)
        else:
            out = out.T
        o_ref[0] = out.astype(BF16)


def _expand_cmp_weights(pe, w1, w2, out_lanes):
    half = CMP_LEN // 2
    eye = jnp.eye(NSA_KV_HEADS, dtype=F32)
    w1r = w1.reshape(2, half, HEAD_DIM, CMP_HIDDEN)
    w1e = w1r[:, :, None, :, None, :] * eye[None, None, :, None, :, None]
    w1e = w1e.reshape(2, half * D_KV, NSA_KV_HEADS * CMP_HIDDEN).astype(BF16)
    pee = jnp.broadcast_to(pe.reshape(2, half, 1, HEAD_DIM), (2, half, NSA_KV_HEADS, HEAD_DIM))
    pee = pee.reshape(2, half * D_KV)
    w2p = jnp.pad(w2, ((0, 0), (0, out_lanes - HEAD_DIM)))
    w2e = (eye[:, None, :, None] * w2p[None, :, None, :]).reshape(
        NSA_KV_HEADS * CMP_HIDDEN, NSA_KV_HEADS * out_lanes).astype(BF16)
    return pee, w1e, w2e


def _compress(rk, rv, wk, wv):
    b, nb, width = rk.shape
    full2 = lambda a: pl.BlockSpec(a.shape, lambda i: (0, 0))
    full3 = lambda a: pl.BlockSpec(a.shape, lambda i: (0, 0, 0))
    row = pl.BlockSpec((1, nb, width), lambda i: (i, 0, 0))
    out = lambda rows, lanes: pl.BlockSpec((1, rows, lanes), lambda i: (i, 0, 0))
    return pl.pallas_call(
        _compress_kernel,
        grid=(b,),
        in_specs=[row, row, full2(wk[0]), full2(wv[0]), full3(wk[1]), full3(wv[1]),
                  full2(wk[2]), full2(wv[2])],
        out_specs=[out(nb, K_AUG), out(D_KV, nb)],
        out_shape=[jax.ShapeDtypeStruct((b, nb, K_AUG), BF16),
                   jax.ShapeDtypeStruct((b, D_KV, nb), BF16)],
        compiler_params=_cparams(("parallel",)),
        name="nsa_compress",
    )(rk, rv, wk[0], wv[0], wk[1], wv[1], wk[2], wv[2])


NSA_TQ = 128
NSA_TK = 1024


def _masked_softmax(s, mask):
    s = jnp.where(mask, s, NEG_BIG)
    s = s - jnp.max(s, axis=-1, keepdims=True)
    p = jnp.where(mask, jnp.exp(s), 0.0)
    return p / jnp.maximum(jnp.sum(p, axis=-1, keepdims=True), 1e-30)


def _flash_update(carry, s, v_t):
    m, l, acc = carry
    m_new = jnp.maximum(m, jnp.max(s, axis=0, keepdims=True))
    alpha = jnp.exp(m - m_new)
    p = jnp.exp(s - m_new)
    l = alpha * l + jnp.sum(p, axis=0, keepdims=True)
    acc = alpha * acc + _dot(v_t, p.astype(BF16))
    return m_new, l, acc


def _nsa_kernel(q_ref, ks_ref, kw_ref, vst_ref, vwt_ref, ck_ref, cvt_ref, ovl_ref, oh_ref, gl_ref,
                o_ref):
    tq = NSA_TQ
    R = NSA_GROUP
    i = pl.program_id(1)
    q0 = i * tq
    t_lane = q0 + lax.broadcasted_iota(jnp.int32, (1, tq), 1)
    t_lanes = jnp.concatenate([t_lane] * R, axis=1)
    n_cmp = ck_ref.shape[1]
    n_sel = ovl_ref.shape[0]

    gate_t = _sigmoid(gl_ref[...]).T

    cmp_end = lax.broadcasted_iota(jnp.int32, (n_cmp, 1), 0) * CMP_STRIDE + (CMP_LEN - 1)
    mask_c = t_lanes >= cmp_end
    ids = lax.broadcasted_iota(jnp.int32, (n_sel, 1), 0)
    idsf = ids.astype(F32)
    cur = t_lane // SEL_LEN
    forced = (ids == 0) | (ids == cur) | (ids == cur - 1)
    valid = ids * SEL_LEN <= t_lane

    G = NSA_KV_HEADS
    lanes_of = [slice(g * LANES, (g + 1) * LANES) for g in range(G)]
    q_st, q_slc, o_cmp = [], [], []
    for g in range(G):
        qh = [q_ref[:, h * LANES:(h + 1) * LANES] for h in range(g * R, (g + 1) * R)]
        q_st.append(jnp.concatenate(qh, axis=0))

        s = jnp.where(mask_c, _dot_nt(ck_ref[0, :, lanes_of[g]], q_st[g]), NEG_BIG)
        s = s - jnp.max(s, axis=0, keepdims=True)
        p = jnp.where(mask_c, jnp.exp(s), 0.0)
        p = p / jnp.maximum(jnp.sum(p, axis=0, keepdims=True), 1e-30)
        o_cmp.append(_dot(cvt_ref[0], p.astype(BF16)))
        p_hi, p_lo = _split2(p)
        imp4 = _dot(ovl_ref[...], p_hi) + _dot(ovl_ref[...], p_lo)
        imp = imp4[:, 0:tq]
        for r in range(1, R):
            imp = imp + imp4[:, r * tq:(r + 1) * tq]
        imp = jnp.where(forced, 1e6, jnp.where(valid, imp, -1.0))

        sel = jnp.zeros((n_sel, tq), F32)
        for _ in range(min(SEL_TOPN, n_sel)):
            m = jnp.max(imp, axis=0, keepdims=True)
            first = jnp.min(jnp.where(imp == m, idsf, float(n_sel)), axis=0, keepdims=True)
            hit = idsf == first
            sel = jnp.where(hit, 1.0, sel)
            imp = jnp.where(hit, -3e38, imp)
        bias_t = jnp.where(sel > 0.5, 0.0, NEG_BIG)
        if n_sel < LANES:
            bias_t = jnp.concatenate([bias_t, jnp.zeros((LANES - n_sel, tq), F32)], axis=0)
        bias = bias_t.T.astype(BF16)

        q_slc.append(jnp.concatenate([q_st[g], jnp.concatenate([bias] * R, axis=0)], axis=1))

    init = tuple((jnp.full((1, R * tq), -3e38, F32), jnp.zeros((1, R * tq), F32),
                  jnp.zeros((LANES, R * tq), F32)) for _ in range(G))

    def slc_tile(j, carry, diagonal):
        k0 = pl.multiple_of(j * NSA_TK, NSA_TK)
        onehot = oh_ref[pl.ds(k0, NSA_TK), :]
        v_t = vst_ref[0, :, pl.ds(k0, NSA_TK)]
        out = []
        for g in range(G):
            k = jnp.concatenate([ks_ref[pl.ds(k0, NSA_TK), lanes_of[g]], onehot], axis=1)
            s = _dot_nt(k, q_slc[g])
            if diagonal:
                pos = k0 + lax.broadcasted_iota(jnp.int32, (NSA_TK, 1), 0)
                s = jnp.where(t_lanes >= pos, s, NEG_BIG)
            out.append(_flash_update(carry[g], s, v_t))
        return tuple(out)

    n_off = q0 // NSA_TK
    carry = lax.fori_loop(0, n_off, functools.partial(slc_tile, diagonal=False), init)
    slc = slc_tile(n_off, carry, True)

    span = WINDOW + tq
    w0 = pl.multiple_of(jnp.maximum(q0 - WINDOW, 0), tq)
    pos = w0 + lax.broadcasted_iota(jnp.int32, (span, 1), 0)
    visible = (t_lanes >= pos) & (pos > t_lanes - WINDOW)
    v_t = vwt_ref[0, :, pl.ds(w0, span)]
    win = [_flash_update(init[g], jnp.where(
        visible, _dot_nt(kw_ref[pl.ds(w0, span), lanes_of[g]], q_st[g]), NEG_BIG), v_t)
        for g in range(G)]

    for h in range(NSA_HEADS):
        g, r = divmod(h, R)
        rows = slice(r * tq, (r + 1) * tq)
        gate = lambda j: gate_t[h * 3 + j:h * 3 + j + 1, :]
        out_t = (gate(0) * o_cmp[g][:, rows]
                 + gate(1) * (slc[g][2][:, rows] / slc[g][1][:, rows])
                 + gate(2) * (win[g][2][:, rows] / win[g][1][:, rows]))
        o_ref[:, h * LANES:(h + 1) * LANES] = out_t.T.astype(BF16)


def _nsa_consts(t_len, n_cmp_pad):
    n_cmp = (t_len - CMP_LEN) // CMP_STRIDE + 1
    n_sel = t_len // SEL_LEN
    cmp_start = np.arange(n_cmp) * CMP_STRIDE
    sel_start = np.arange(n_sel) * SEL_LEN
    ovl = np.clip(np.minimum(cmp_start[:, None] + CMP_LEN, sel_start[None, :] + SEL_LEN)
                  - np.maximum(cmp_start[:, None], sel_start[None, :]), 0, None) / CMP_LEN
    ovl_t = np.zeros((n_sel, n_cmp_pad), np.float32)
    ovl_t[:, :n_cmp] = ovl.T
    onehot = np.zeros((t_len, LANES), np.float32)
    onehot[np.arange(t_len), np.arange(t_len) // SEL_LEN] = 1.0
    return jnp.asarray(ovl_t, BF16), jnp.asarray(onehot, BF16)


def _nsa_attention(oa, ob, vs_t, vw_t, cmp_k, cmp_vt, b, t_len):
    assert NSA_TK % NSA_TQ == 0 and t_len % NSA_TK == 0 and WINDOW % NSA_TQ == 0
    assert t_len >= WINDOW + NSA_TQ and t_len // SEL_LEN <= LANES
    n_cmp_pad = cmp_k.shape[1]
    ovl_t, onehot = _nsa_consts(t_len, n_cmp_pad)
    nq = t_len // NSA_TQ
    k_col0 = Q_EXP // K_AUG
    kspec = lambda c: pl.BlockSpec((t_len, K_AUG), lambda bi, i: (bi, k_col0 + c))
    vspec = pl.BlockSpec((1, D_KV, t_len), lambda bi, i: (bi, 0, 0))
    const2 = lambda a: pl.BlockSpec(a.shape, lambda bi, i: (0, 0))
    return pl.pallas_call(
        _nsa_kernel,
        grid=(b, nq),
        in_specs=[
            pl.BlockSpec((NSA_TQ, Q_EXP), lambda bi, i: (bi * nq + i, 0)),
            kspec(0), kspec(1), vspec, vspec,
            pl.BlockSpec((1, n_cmp_pad, K_AUG), lambda bi, i: (bi, 0, 0)),
            pl.BlockSpec((1, D_KV, n_cmp_pad), lambda bi, i: (bi, 0, 0)),
            const2(ovl_t), const2(onehot),
            pl.BlockSpec((NSA_TQ, LANES), lambda bi, i: (bi * nq + i, 2)),
        ],
        out_specs=pl.BlockSpec((NSA_TQ, Q_EXP), lambda bi, i: (bi * nq + i, 0)),
        out_shape=jax.ShapeDtypeStruct((b * t_len, Q_EXP), BF16),
        compiler_params=_cparams(("parallel", "arbitrary")),
        name="nsa_attention",
    )(oa, oa, oa, vs_t, vw_t, cmp_k, cmp_vt, ovl_t, onehot, ob)


def _dot_x3(x, e):
    hi = x.astype(BF16)
    r1 = x - hi.astype(F32)
    mid = r1.astype(BF16)
    lo = (r1 - mid.astype(F32)).astype(BF16)
    return _dot(hi, e) + _dot(mid, e) + _dot(lo, e)


def _head_sum_matrix():
    ids = np.arange(D_RWKV) // HEAD_DIM
    return jnp.asarray(ids[:, None] == ids[None, :], BF16)


def _rwkv_pre_kernel(p_ref, hp_ref, mu_ref, w0_ref, w2_ref, a0_ref, a2_ref, g2_ref, kk_ref,
                     ka_ref, rk_ref, bd_ref, r_o, lw_o, k_o, kkn_o, a_o, vt_o, bv_o, g_o, *,
                     tiles_per_seq):
    i = pl.program_id(0)
    p = p_ref[...]
    tm = p.shape[0]
    keep = jnp.where(i % tiles_per_seq == 0, 0.0, 1.0)
    halo = hp_ref[SUBLANES - 1:SUBLANES, :] * keep
    prev = pltpu.roll(p, 1, axis=0)
    row0 = lax.broadcasted_iota(jnp.int32, (tm, 1), 0) == 0
    prev = jnp.where(row0, halo, prev)
    ps = p + (prev - p) * mu_ref[...]
    d = D_RWKV
    r = ps[:, 0:d]
    k = ps[:, d:2 * d]
    v = ps[:, 2 * d:3 * d]
    xw = ps[:, 3 * d:3 * d + RANK_W]
    xa = ps[:, 3 * d + RANK_W:3 * d + RANK_W + RANK_A]
    xg = ps[:, 3 * d + RANK_W + RANK_A:]
    z = -(w0_ref[...] + _dot(jnp.tanh(xw).astype(BF16), w2_ref[...]))
    softplus = jnp.maximum(z, 0.0) + jnp.log(1.0 + jnp.exp(-jnp.abs(z)))
    w = -softplus - 0.5
    a = _sigmoid(a0_ref[...] + _dot(xa.astype(BF16), a2_ref[...]))
    g = _dot(_sigmoid(xg).astype(BF16), g2_ref[...])
    kk = k * kk_ref[...]
    ss = _dot_x3(kk * kk, bd_ref[...])
    kk = kk / jnp.maximum(jnp.sqrt(ss), 1e-12)
    k_mod = k * (1.0 + (a - 1.0) * ka_ref[...])
    for o_ref, val in ((r_o, r), (lw_o, -jnp.exp(w)), (k_o, k_mod), (kkn_o, kk), (a_o, a)):
        for hd in range(RWKV_HEADS):
            o_ref[0, hd] = val[:, hd * HEAD_DIM:(hd + 1) * HEAD_DIM]
    vt_o[0] = v.T.reshape(RWKV_HEADS, HEAD_DIM, tm)
    bv_o[...] = _dot_x3(r * k_mod * rk_ref[...], bd_ref[...]) * v
    g_o[...] = g


def _rwkv_pre(oc, b, t_len, mu, w0, w2, a0, a2, g2, k_k, k_a, r_k, bd, tm=256):
    n = oc.shape[0]
    tps = t_len // tm
    halo_blocks = tm // SUBLANES
    full = lambda a: pl.BlockSpec(a.shape, lambda i: (0, 0))
    tok = pl.BlockSpec((tm, D_RWKV), lambda i: (i, 0))
    hm = pl.BlockSpec((1, RWKV_HEADS, tm, HEAD_DIM), lambda i: (i // tps, 0, i % tps, 0))
    hm_t = pl.BlockSpec((1, RWKV_HEADS, HEAD_DIM, tm), lambda i: (i // tps, 0, 0, i % tps))
    params = (mu, w0, w2, a0, a2, g2, k_k, k_a, r_k, bd)
    hm_shape = jax.ShapeDtypeStruct((b, RWKV_HEADS, t_len, HEAD_DIM), F32)
    return pl.pallas_call(
        functools.partial(_rwkv_pre_kernel, tiles_per_seq=tps),
        grid=(n // tm,),
        in_specs=[
            pl.BlockSpec((tm, D_RWKV_IN), lambda i: (i, 0)),
            pl.BlockSpec((SUBLANES, D_RWKV_IN),
                         lambda i: (jnp.maximum(i * halo_blocks - 1, 0), 0)),
        ] + [full(a) for a in params],
        out_specs=[hm] * 5 + [hm_t, tok, tok],
        out_shape=[hm_shape] * 5
        + [jax.ShapeDtypeStruct((b, RWKV_HEADS, HEAD_DIM, t_len), F32)]
        + [jax.ShapeDtypeStruct((n, D_RWKV), F32)] * 2,
        compiler_params=_cparams(("parallel",)),
        name="rwkv_pre",
    )(oc, oc, *params)


RWKV_C = 64
RWKV_CB = 256
RWKV_PASSES = 1


def _bmm(eq, a, b):
    ein = lambda x, y: jnp.einsum(eq, x, y, preferred_element_type=F32)
    if RWKV_PASSES == 1:
        return ein(a.astype(BF16), b.astype(BF16))
    ah, al = _split2(a)
    bh, bl = _split2(b)
    return ein(ah, bh) + ein(ah, bl) + ein(al, bh)


def _cumsum_rows(x2d, seg):
    rows = lax.broadcasted_iota(jnp.int32, (x2d.shape[0], 1), 0) % seg
    step = 1
    while step < seg:
        shifted = pltpu.roll(x2d, step, axis=0)
        x2d = x2d + jnp.where(rows >= step, shifted, 0.0)
        step *= 2
    return x2d


def _rwkv_rec_kernel(r_ref, lw_ref, k_ref, kk_ref, a_ref, vt_ref, ot_ref, s_ref):
    H = r_ref.shape[1]
    C = RWKV_C

    @pl.when(pl.program_id(1) == 0)
    def _():
        s_ref[...] = jnp.zeros_like(s_ref)

    ri = lax.broadcasted_iota(jnp.int32, (C, C), 0)
    ci = lax.broadcasted_iota(jnp.int32, (C, C), 1)
    strict = (ri > ci)[None]
    incl = (ri >= ci)[None]
    eye = (ri == ci).astype(F32)[None]

    for sub in range(RWKV_CB // C):
        rows = slice(sub * C, (sub + 1) * C)
        r = r_ref[0, :, rows, :]
        lw = lw_ref[0, :, rows, :]
        k = k_ref[0, :, rows, :]
        kk = kk_ref[0, :, rows, :]
        a = a_ref[0, :, rows, :]
        vt = vt_ref[0, :, :, rows]
        s0 = s_ref[...]

        cum = _cumsum_rows(lw.reshape(H * C, HEAD_DIM), C).reshape(H, C, HEAD_DIM)
        cum_last = cum[:, C - 1:C, :]
        p_inv = jnp.exp(-cum)
        w_last = jnp.exp(cum_last - cum)
        bm = kk * a
        at = -kk * jnp.exp(cum - lw)
        rt = r * jnp.exp(cum)
        bt = bm * p_inv
        kt = k * p_inv

        nt = 'hik,hjk->hij'
        ar = jnp.concatenate([at, rt], axis=1)
        ar_b = _bmm(nt, ar, bt)
        ar_k = _bmm(nt, ar, kt)
        m_ab = jnp.where(strict, ar_b[:, :C], 0.0)
        n_rb = jnp.where(incl, ar_b[:, C:], 0.0)
        m_ak = jnp.where(strict, ar_k[:, :C], 0.0)
        n_rk = jnp.where(incl, ar_k[:, C:], 0.0)

        tinv = eye + m_ab
        mp = _bmm('hij,hjk->hik', m_ab, m_ab)
        n = 2
        while n < C:
            both = _bmm('hij,hjk->hik', jnp.concatenate([mp, tinv], axis=1), mp)
            mp = both[:, :C]
            tinv = tinv + both[:, C:]
            n *= 2

        rhs_t = _bmm('hvk,hik->hvi', s0, at) + _bmm('hvj,hij->hvi', vt, m_ak)
        ut = _bmm('hvj,hij->hvi', rhs_t, tinv)
        ot = (_bmm('hvk,hik->hvi', s0, rt) + _bmm('hvj,hij->hvi', ut, n_rb)
              + _bmm('hvj,hij->hvi', vt, n_rk))
        ot_ref[0, :, :, rows] = ot
        s_ref[...] = (s0 * jnp.exp(cum_last) + _bmm('hvj,hjk->hvk', ut, bm * w_last)
                      + _bmm('hvj,hjk->hvk', vt, k * w_last))


def _rwkv_recurrence(r, lw, k, kk, a, vt):
    b, h, t_len, d = r.shape
    tok = pl.BlockSpec((1, h, RWKV_CB, d), lambda bi, c: (bi, 0, c, 0))
    tr = pl.BlockSpec((1, h, d, RWKV_CB), lambda bi, c: (bi, 0, 0, c))
    return pl.pallas_call(
        _rwkv_rec_kernel,
        grid=(b, t_len // RWKV_CB),
        in_specs=[tok] * 5 + [tr],
        out_specs=tr,
        out_shape=jax.ShapeDtypeStruct((b, h, d, t_len), F32),
        scratch_shapes=[pltpu.VMEM((h, d, d), F32)],
        compiler_params=_cparams(("parallel", "arbitrary")),
        name="rwkv_recurrence",
    )(r, lw, k, kk, a, vt)


def _rwkv_post_kernel(ot_ref, bv_ref, g_ref, lnw_ref, lnb_ref, bd_ref, y_ref):
    tm = bv_ref.shape[0]
    o = ot_ref[0].reshape(D_RWKV, tm).T
    bd = bd_ref[...]
    inv = 1.0 / HEAD_DIM
    mean = _dot_x3(o, bd) * inv
    d = o - mean
    var = _dot_x3(d * d, bd) * inv
    on = d * lax.rsqrt(var + GN_EPS) * lnw_ref[...] + lnb_ref[...]
    y_ref[...] = ((on + bv_ref[...]) * g_ref[...]).astype(BF16)


def _rwkv_post(ot, bv, g, ln_w, ln_b, bd, tm=512):
    n = bv.shape[0]
    tps = ot.shape[-1] // tm
    tile = pl.BlockSpec((tm, D_RWKV), lambda i: (i, 0))
    hm_t = pl.BlockSpec((1, RWKV_HEADS, HEAD_DIM, tm), lambda i: (i // tps, 0, 0, i % tps))
    full = lambda a: pl.BlockSpec(a.shape, lambda i: (0, 0))
    return pl.pallas_call(
        _rwkv_post_kernel,
        grid=(n // tm,),
        in_specs=[hm_t, tile, tile, full(ln_w), full(ln_b), full(bd)],
        out_specs=tile,
        out_shape=jax.ShapeDtypeStruct((n, D_RWKV), BF16),
        compiler_params=_cparams(("parallel",)),
        name="rwkv_post",
    )(ot, bv, g, ln_w, ln_b, bd)


def _rwkv_mixer(oc, b, t_len, mu, w0, w2, a0, a2, g2, k_k, k_a, r_k, ln_w, ln_b):
    bd = _head_sum_matrix()
    row = lambda z: z.reshape(1, -1)
    r, lw, k, kk, a, vt, bv, g = _rwkv_pre(
        oc, b, t_len, row(mu), row(w0), w2.astype(BF16), row(a0), a2.astype(BF16),
        g2.astype(BF16), row(k_k), row(k_a), row(r_k), bd)
    ot = _rwkv_recurrence(r, lw, k, kk, a, vt)
    return _rwkv_post(ot, bv, g, row(ln_w), row(ln_b), bd)


def _out_proj_kernel(x_ref, yn_ref, yr_ref, wn_ref, wr_ref, g_ref, x1_ref, h2_ref):
    x1 = x_ref[...] + _dot(yn_ref[...], wn_ref[...]) + _dot(yr_ref[...], wr_ref[...])
    x1_ref[...] = x1
    h2_ref[...] = x1 * lax.rsqrt(jnp.mean(x1 * x1, axis=-1, keepdims=True) + EPS) * g_ref[...]


def _expand_w_out(w_out):
    wn = w_out[:D_NSA].reshape(NSA_HEADS, 1, HEAD_DIM, D_MODEL)
    onehot = (np.arange(NSA_HEADS)[:, None] // NSA_GROUP == np.arange(NSA_KV_HEADS)[None, :])
    wn = wn * jnp.asarray(onehot, F32)[:, :, None, None]
    return wn.reshape(Q_EXP, D_MODEL).astype(BF16), w_out[D_NSA:].astype(BF16)


def _out_proj(x2d, y_nsa, y_rwkv, wn, wr, g, tm=512):
    n = x2d.shape[0]
    tile = lambda a: pl.BlockSpec((tm, a.shape[1]), lambda i: (i, 0))
    full = lambda a: pl.BlockSpec(a.shape, lambda i: (0, 0))
    return pl.pallas_call(
        _out_proj_kernel,
        grid=(n // tm,),
        in_specs=[tile(x2d), tile(y_nsa), tile(y_rwkv), full(wn), full(wr), full(g)],
        out_specs=[tile(x2d), tile(x2d)],
        out_shape=[jax.ShapeDtypeStruct((n, D_MODEL), F32)] * 2,
        compiler_params=_cparams(("parallel",)),
        name="out_proj",
    )(x2d, y_nsa, y_rwkv, wn, wr, g)


def _topk_rows(s, k, rid):
    vals, ids = [], []
    for _ in range(k):
        m = jnp.max(s, axis=0, keepdims=True)
        first = jnp.min(jnp.where(s == m, rid, jnp.inf), axis=0, keepdims=True)
        vals.append(m)
        ids.append(first)
        s = jnp.where(rid == first, -jnp.inf, s)
    return jnp.concatenate(vals, axis=0), jnp.concatenate(ids, axis=0)


def _take_rows(table, idx):
    rows = lax.broadcasted_iota(jnp.int32, (table.shape[0], 1), 0).astype(F32)
    out = [jnp.sum(jnp.where(rows == idx[r:r + 1], table, 0.0), axis=0, keepdims=True)
           for r in range(idx.shape[0])]
    return jnp.concatenate(out, axis=0)


def _pair_candidates():
    k = PEER_TOPK
    flat = [0 * k + j for j in range(k)]
    for i in range(1, SUBLANES):
        flat += [i * k + j for j in range(SUBLANES)]
    flat += [i * k for i in range(SUBLANES, k)]
    return np.asarray(flat, np.float32).reshape(-1, 1)


def _peer_route_kernel(h_ref, wq_ref, sk_ref, flat_ref, e_ref, g_ref):
    K = PEER_TOPK
    half = PEER_D_QUERY // 2
    q = _dot(h_ref[...].astype(BF16), wq_ref[...]).astype(BF16)
    key_ids = lax.broadcasted_iota(jnp.int32, (PEER_N_KEYS, 1), 0).astype(F32)
    flat = flat_ref[...]
    rows_e, rows_g = [], []
    for h in range(PEER_HEADS):
        top = []
        for c in range(2):
            col = (h * 2 + c) * half
            s_t = _dot_nt(sk_ref[h, c], q[:, col:col + half])
            top.append(_topk_rows(s_t, K, key_ids))
        (v0, i0), (v1, i1) = top
        cand = jnp.concatenate(
            [v0[0:1] + v1]
            + [v0[i:i + 1] + v1[:SUBLANES] for i in range(1, SUBLANES)]
            + [v0[SUBLANES:] + v1[0:1]], axis=0)
        best, pair = _topk_rows(cand, K, flat)
        pi = jnp.floor(pair * (1.0 / K))
        pj = pair - pi * K
        experts = _take_rows(i0, pi) * float(PEER_N_KEYS) + _take_rows(i1, pj)
        p = jnp.exp(best - jnp.max(best, axis=0, keepdims=True))
        rows_e.append(experts * float(HALF_ROWS))
        rows_g.append(p / jnp.sum(p, axis=0, keepdims=True))
    e_ref[...] = jnp.concatenate(rows_e, axis=0).T.astype(jnp.int32)
    g_ref[...] = jnp.concatenate(rows_g, axis=0).T


def _peer_route(h2, wq, sk, tm=256):
    n = h2.shape[0]
    hk = PEER_HEADS * PEER_TOPK
    flat = jnp.asarray(_pair_candidates())
    out = pl.BlockSpec((tm, hk), lambda i: (i, 0))
    return pl.pallas_call(
        _peer_route_kernel,
        grid=(n // tm,),
        in_specs=[
            pl.BlockSpec((tm, D_MODEL), lambda i: (i, 0)),
            pl.BlockSpec(wq.shape, lambda i: (0, 0)),
            pl.BlockSpec(sk.shape, lambda i: (0, 0, 0, 0)),
            pl.BlockSpec(flat.shape, lambda i: (0, 0)),
        ],
        out_specs=[out, out],
        out_shape=[jax.ShapeDtypeStruct((n, hk), jnp.int32), jax.ShapeDtypeStruct((n, hk), F32)],
        compiler_params=_cparams(("parallel",)),
        name="peer_route",
    )(h2, wq, sk, flat)


PEER_TT = 128
HALF_ROWS = SUBLANES // 2
SUB_ORDER = (0, 4, 2, 6, 1, 5, 3, 7)
HIGH_MASK = 0xFFFF0000


def _pack_table(w):
    bits = lax.bitcast_convert_type(w.astype(BF16), jnp.uint16).astype(jnp.uint32)
    half = w.shape[1] // 2
    packed = bits[:, :half] | (bits[:, half:] << 16)
    return packed.reshape(w.shape[0] * HALF_ROWS, LANES)


def _table_row(tbl_ref, row0):
    return tbl_ref[pl.ds(pl.multiple_of(row0, HALF_ROWS), HALF_ROWS), :]


def _unpack_words(word):
    lo = pltpu.bitcast(word << 16, F32)
    hi = pltpu.bitcast(word & jnp.uint32(HIGH_MASK), F32)
    return lo, hi


def _sublane_tree(c):
    sub = lax.broadcasted_iota(jnp.int32, (SUBLANES, LANES), 0)
    m2 = (sub % 4) < 2
    d = []
    for x, y in zip(c[0::2], c[1::2]):
        d.append(jnp.where(m2, x + pltpu.roll(x, 6, axis=0), y + pltpu.roll(y, 2, axis=0)))
    m1 = (sub % 2) == 0
    x, y = d
    return jnp.where(m1, x + pltpu.roll(x, 7, axis=0), y + pltpu.roll(y, 1, axis=0))


def _peer_act_kernel(e_ref, h_ref, tbl_ref, gate_ref, ones_ref, w_ref, part_ref):
    hk = PEER_HEADS * PEER_TOPK
    tt = h_ref.shape[0]

    def token(t, carry):
        hrow = h_ref[t]
        h_lo = jnp.concatenate([hrow[:HALF_ROWS]] * 2, axis=0)
        h_hi = jnp.concatenate([hrow[HALF_ROWS:]] * 2, axis=0)
        for m in range(hk // SUBLANES):
            es = [e_ref[t, m * SUBLANES + j] for j in SUB_ORDER]
            c = []
            for ea, eb in zip(es[0::2], es[1::2]):
                lo, hi = _unpack_words(jnp.concatenate(
                    [_table_row(tbl_ref, ea), _table_row(tbl_ref, eb)], axis=0))
                c.append(lo * h_lo + hi * h_hi)
            row0 = pl.multiple_of((t * (hk // SUBLANES) + m) * SUBLANES, SUBLANES)
            part_ref[pl.ds(row0, SUBLANES), :] = _sublane_tree(c)
        return carry

    lax.fori_loop(0, tt, token, 0)
    half = (tt // 2) * hk
    hi, lo = _split2(part_ref[0:half, :])
    sums_a = _dot(jnp.concatenate([hi, lo], axis=1), ones_ref[...])
    sums_b = jnp.sum(part_ref[half:, :], axis=-1, keepdims=True)
    sums_b = jnp.broadcast_to(sums_b, (tt * hk - half, LANES))
    sums = jnp.concatenate([sums_a, sums_b], axis=0).reshape(tt, hk, LANES)
    eye = (lax.broadcasted_iota(jnp.int32, (hk, LANES), 0)
           == lax.broadcasted_iota(jnp.int32, (hk, LANES), 1))
    act = jnp.sum(jnp.where(eye[None], sums, 0.0), axis=1)
    w_ref[...] = _gelu(act) * gate_ref[...]


def _table_spec(tbl):
    return pl.BlockSpec(tbl.shape, lambda i: (0, 0), pipeline_mode=pl.Buffered(1))


def _peer_act(experts, h2_tiles, tbl, gates):
    n, hk = experts.shape
    ones = jnp.ones((2 * LANES, LANES), BF16)
    return pl.pallas_call(
        _peer_act_kernel,
        grid=(n // PEER_TT,),
        in_specs=[
            pl.BlockSpec((PEER_TT, hk), lambda i: (i, 0), memory_space=pltpu.SMEM),
            pl.BlockSpec((PEER_TT, SUBLANES, LANES), lambda i: (i, 0, 0)),
            _table_spec(tbl),
            pl.BlockSpec((PEER_TT, hk), lambda i: (i, 0)),
            pl.BlockSpec(ones.shape, lambda i: (0, 0)),
        ],
        out_specs=pl.BlockSpec((PEER_TT, hk), lambda i: (i, 0)),
        out_shape=jax.ShapeDtypeStruct((n, hk), F32),
        scratch_shapes=[pltpu.VMEM((PEER_TT * hk, LANES), F32)],
        compiler_params=_cparams(("arbitrary",)),
        name="peer_act",
    )(experts, h2_tiles, tbl, gates, ones)


PEER_ACCS = 4
PEER_GROUP = 64


def _peer_out_kernel(e_ref, w_ref, x_ref, tbl_ref, g_ref, ones_ref, o_ref, wrep_ref, *,
                     final_norm):
    hk = PEER_HEADS * PEER_TOPK
    tt = x_ref.shape[0]

    eye = (lax.broadcasted_iota(jnp.int32, (hk, LANES), 0)
           == lax.broadcasted_iota(jnp.int32, (hk, LANES), 1))
    diag = jnp.where(eye[None], w_ref[...][:, None, :], 0.0).reshape(tt * hk, LANES)
    wrep_ref[...] = _dot(diag.astype(BF16), ones_ref[...])

    def token(t, carry):
        def group(gi, accs):
            acc_lo, acc_hi = list(accs[0]), list(accs[1])
            base = t * hk + gi * PEER_GROUP
            for j in range(PEER_GROUP):
                lo, hi = _unpack_words(_table_row(tbl_ref, e_ref[base + j]))
                wk = wrep_ref[pl.ds(base + j, 1), :]
                acc_lo[j % PEER_ACCS] = acc_lo[j % PEER_ACCS] + wk * lo
                acc_hi[j % PEER_ACCS] = acc_hi[j % PEER_ACCS] + wk * hi
            return tuple(acc_lo), tuple(acc_hi)

        zeros = tuple(jnp.zeros((HALF_ROWS, LANES), F32) for _ in range(PEER_ACCS))
        acc_lo, acc_hi = lax.fori_loop(0, hk // PEER_GROUP, group, (zeros, zeros))
        tree = lambda a: (a[0] + a[1]) + (a[2] + a[3])
        o_ref[t] = x_ref[t] + jnp.concatenate([tree(acc_lo), tree(acc_hi)], axis=0)
        return carry

    lax.fori_loop(0, tt, token, 0)
    if final_norm:
        x2 = o_ref[...]
        ms = jnp.sum(jnp.sum(x2 * x2, axis=2, keepdims=True), axis=1, keepdims=True) / D_MODEL
        o_ref[...] = x2 * lax.rsqrt(ms + EPS) * g_ref[...]


def _peer_out(experts, w, x1_tiles, tbl, final_g, final_norm):
    n, hk = experts.shape
    ones = jnp.ones((LANES, LANES), BF16)
    smem = pl.BlockSpec((PEER_TT * hk,), lambda i: (i,), memory_space=pltpu.SMEM)
    tile = pl.BlockSpec((PEER_TT, SUBLANES, LANES), lambda i: (i, 0, 0))
    return pl.pallas_call(
        functools.partial(_peer_out_kernel, final_norm=final_norm),
        grid=(n // PEER_TT,),
        in_specs=[smem, pl.BlockSpec((PEER_TT, hk), lambda i: (i, 0)), tile, _table_spec(tbl),
                  pl.BlockSpec(final_g.shape, lambda i: (0, 0, 0)),
                  pl.BlockSpec(ones.shape, lambda i: (0, 0))],
        out_specs=tile,
        out_shape=jax.ShapeDtypeStruct(x1_tiles.shape, F32),
        scratch_shapes=[pltpu.VMEM((PEER_TT * hk, LANES), F32)],
        compiler_params=_cparams(("arbitrary",)),
        name="peer_out",
    )(experts.reshape(-1), w, x1_tiles, tbl, final_g, ones)


def _peer_ffn_residual(x1, h2, wq, sk, tbl_u, tbl_v, final_g, final_norm):
    n = x1.shape[0]
    experts, gates = _peer_route(h2, wq, sk)
    tiles = lambda z: z.reshape(n, SUBLANES, LANES)
    w = _peer_act(experts, tiles(h2), tbl_u, gates)
    g_tile = final_g.reshape(1, SUBLANES, LANES)
    return _peer_out(experts, w, tiles(x1), tbl_v, g_tile, final_norm).reshape(n, D_MODEL)


def kernel(x, norm_mix_g, w_in, cmp_k_pe, cmp_k_w1, cmp_k_w2, cmp_v_pe, cmp_v_w1, cmp_v_w2,
           rwkv_mu, rwkv_w0, rwkv_w2, rwkv_a0, rwkv_a2, rwkv_g2, rwkv_k_k, rwkv_k_a, rwkv_r_k,
           rwkv_ln_w, rwkv_ln_b, w_out, norm_ffn_g, peer_w_q, peer_sub_keys, peer_u, peer_v,
           norm_final_g):
    b, t_len, _ = x.shape
    n = b * t_len
    row = lambda z: z.reshape(1, -1)
    x2d = x.reshape(n, D_MODEL)
    depth = w_in.shape[0]
    for l in range(depth):
        oa, ob, oc = _in_proj(x2d, row(norm_mix_g[l]), _pad_w_in(w_in[l]), t_len)
        nb = t_len // CMP_STRIDE
        rk = ob[:, 0:D_KV].reshape(b, nb, CMP_STRIDE * D_KV)
        rv = ob[:, D_KV:2 * D_KV].reshape(b, nb, CMP_STRIDE * D_KV)
        cmp_k, cmp_vt = _compress(
            rk, rv, _expand_cmp_weights(cmp_k_pe[l], cmp_k_w1[l], cmp_k_w2[l], LANES),
            _expand_cmp_weights(cmp_v_pe[l], cmp_v_w1[l], cmp_v_w2[l], HEAD_DIM))
        v_col0 = Q_EXP + 2 * K_AUG
        v_t = lambda c: oa[:, v_col0 + c * D_KV:v_col0 + (c + 1) * D_KV].reshape(
            b, t_len, D_KV).transpose(0, 2, 1)
        y_nsa = _nsa_attention(oa, ob, v_t(0), v_t(1), cmp_k, cmp_vt, b, t_len)
        y_rwkv = _rwkv_mixer(oc, b, t_len, rwkv_mu[l], rwkv_w0[l], rwkv_w2[l], rwkv_a0[l],
                             rwkv_a2[l], rwkv_g2[l], rwkv_k_k[l], rwkv_k_a[l], rwkv_r_k[l],
                             rwkv_ln_w[l], rwkv_ln_b[l])
        wn, wr = _expand_w_out(w_out[l])
        x1, h2 = _out_proj(x2d, y_nsa, y_rwkv, wn, wr, row(norm_ffn_g[l]))
        x2d = _peer_ffn_residual(x1, h2, peer_w_q[l].astype(BF16), peer_sub_keys[l].astype(BF16),
                                 _pack_table(peer_u[l]), _pack_table(peer_v[l]),
                                 norm_final_g, final_norm=(l == depth - 1))
    return x2d.reshape(b, t_len, D_MODEL)
```

```python
import functools

import numpy as np
import jax
import jax.numpy as jnp
from jax import lax
from jax.experimental import pallas as pl
from jax.experimental.pallas import tpu as pltpu

F32 = jnp.float32
BF16 = jnp.bfloat16

D_MODEL = 1024
HEAD_DIM = 64
NSA_HEADS = 8
NSA_KV_HEADS = 2
NSA_GROUP = NSA_HEADS // NSA_KV_HEADS
RWKV_HEADS = 8
D_NSA = NSA_HEADS * HEAD_DIM
D_RWKV = RWKV_HEADS * HEAD_DIM
D_KV = NSA_KV_HEADS * HEAD_DIM
CMP_LEN = 32
CMP_STRIDE = 16
CMP_HIDDEN = 128
SEL_LEN = 64
SEL_TOPN = 16
WINDOW = 512
RANK_W = 64
RANK_A = 64
RANK_G = 128
D_RWKV_IN = 3 * D_RWKV + RANK_W + RANK_A + RANK_G
PEER_HEADS = 8
PEER_N_KEYS = 128
PEER_D_QUERY = 256
PEER_TOPK = 16
EPS = 1e-6
GN_EPS = 64e-5

LANES = 128
SUBLANES = 8
VMEM_LIMIT = 56 * 1024 * 1024

NEG_BIG = -1e30
Q_EXP = NSA_HEADS * LANES
K_AUG = NSA_KV_HEADS * LANES
NA_COLS = Q_EXP + 2 * K_AUG + 2 * D_KV
NB_COLS = 3 * LANES
FEAT0 = HEAD_DIM
POS_SPLIT = 64


def _cparams(sem):
    return pltpu.CompilerParams(dimension_semantics=sem, vmem_limit_bytes=VMEM_LIMIT)


def _dot(a, b):
    return jnp.dot(a, b, preferred_element_type=F32)


def _dot_nt(a, b):
    return lax.dot_general(a, b, (((1,), (1,)), ((), ())), preferred_element_type=F32)


def _split2(x):
    hi = x.astype(BF16)
    lo = (x - hi.astype(F32)).astype(BF16)
    return hi, lo


def _dot_x2(x, e):
    hi, lo = _split2(x)
    return _dot(hi, e) + _dot(lo, e)


def _gelu(x):
    return 0.5 * x * (1.0 + jnp.tanh(0.7978845608028654 * (x + 0.044715 * (x * x * x))))


def _sigmoid(x):
    return 1.0 / (1.0 + jnp.exp(-x))


def _key_features(pos, lane):
    hi = (pos // POS_SPLIT).astype(F32)
    lo = (pos % POS_SPLIT).astype(F32)
    return jnp.where(lane == FEAT0, hi, jnp.where(lane == FEAT0 + 1, lo,
                     jnp.where((lane == FEAT0 + 2) | (lane == FEAT0 + 3), 1.0, 0.0)))


def _query_features(t, lane, slope):
    hi = (t // POS_SPLIT).astype(F32)
    lo = (t % POS_SPLIT).astype(F32)
    return jnp.where(lane == FEAT0, slope * POS_SPLIT, jnp.where(
        lane == FEAT0 + 1, slope, jnp.where(
            lane == FEAT0 + 2, -slope * POS_SPLIT * hi, jnp.where(
                lane == FEAT0 + 3, -slope * lo, 0.0))))


def _in_proj_kernel(x_ref, g_ref, w_ref, oa_ref, ob_ref, oc_ref, *, t_len):
    x = x_ref[...]
    tm = x.shape[0]
    h = x * lax.rsqrt(jnp.mean(x * x, axis=-1, keepdims=True) + EPS) * g_ref[...]
    hb = h.astype(BF16)
    t = (pl.program_id(0) * tm + lax.broadcasted_iota(jnp.int32, (tm, 1), 0)) % t_len
    col = lax.broadcasted_iota(jnp.int32, (1, Q_EXP), 1)
    slope = jnp.zeros((1, Q_EXP), F32)
    for hd in range(NSA_HEADS):
        slope = jnp.where(col // LANES == hd, 2.0 ** -(hd + 1), slope)
    q = _dot(hb, w_ref[:, :Q_EXP]) * (HEAD_DIM ** -0.5) + _query_features(t, col % LANES, slope)
    oa_ref[:, :Q_EXP] = q.astype(BF16)
    colk = lax.broadcasted_iota(jnp.int32, (1, 2 * K_AUG), 1)
    k = _dot(hb, w_ref[:, Q_EXP:Q_EXP + 2 * K_AUG]) + _key_features(t, colk % LANES)
    oa_ref[:, Q_EXP:Q_EXP + 2 * K_AUG] = k.astype(BF16)
    oa_ref[:, Q_EXP + 2 * K_AUG:] = _dot(hb, w_ref[:, Q_EXP + 2 * K_AUG:NA_COLS]).astype(BF16)
    ob_ref[...] = _dot(hb, w_ref[:, NA_COLS:NA_COLS + NB_COLS])
    oc_ref[...] = _dot(hb, w_ref[:, NA_COLS + NB_COLS:])


def _in_proj(x2d, g, w_pad, t_len, tm=256):
    n = x2d.shape[0]
    ncols = w_pad.shape[1]
    nc = ncols - NA_COLS - NB_COLS
    return pl.pallas_call(
        functools.partial(_in_proj_kernel, t_len=t_len),
        grid=(n // tm,),
        in_specs=[
            pl.BlockSpec((tm, D_MODEL), lambda i: (i, 0)),
            pl.BlockSpec((1, D_MODEL), lambda i: (0, 0)),
            pl.BlockSpec((D_MODEL, ncols), lambda i: (0, 0)),
        ],
        out_specs=[
            pl.BlockSpec((tm, NA_COLS), lambda i: (i, 0)),
            pl.BlockSpec((tm, NB_COLS), lambda i: (i, 0)),
            pl.BlockSpec((tm, nc), lambda i: (i, 0)),
        ],
        out_shape=[
            jax.ShapeDtypeStruct((n, NA_COLS), BF16),
            jax.ShapeDtypeStruct((n, NB_COLS), F32),
            jax.ShapeDtypeStruct((n, nc), F32),
        ],
        compiler_params=_cparams(("parallel",)),
        name="in_proj",
    )(x2d, g, w_pad)


def _pad_w_in(w_in):
    sizes = (D_NSA, D_KV, D_KV, D_KV, D_KV, D_KV, D_KV, 3 * NSA_HEADS, D_RWKV_IN)
    offs = np.cumsum((0,) + sizes)
    q, kc, vc, ks, vs, kw, vw, gl, rw = (w_in[:, offs[i]:offs[i + 1]] for i in range(9))
    def lane_pad(w, groups):
        w = w.reshape(D_MODEL, groups, HEAD_DIM)
        return jnp.pad(w, ((0, 0), (0, 0), (0, LANES - HEAD_DIM))).reshape(D_MODEL, groups * LANES)

    glp = jnp.pad(gl, ((0, 0), (0, LANES - 3 * NSA_HEADS)))
    return jnp.concatenate(
        [lane_pad(q, NSA_HEADS), lane_pad(ks, NSA_KV_HEADS), lane_pad(kw, NSA_KV_HEADS),
         vs, vw, kc, vc, glp, rw], axis=1).astype(BF16)


def _compress_kernel(rk_ref, rv_ref, pek_ref, pev_ref, w1k_ref, w1v_ref, w2k_ref, w2v_ref,
                     ok_ref, ov_ref):
    for r_ref, pe_ref, w1_ref, w2_ref, o_ref in (
            (rk_ref, pek_ref, w1k_ref, w2k_ref, ok_ref),
            (rv_ref, pev_ref, w1v_ref, w2v_ref, ov_ref)):
        rows = r_ref[0]
        nxt = pltpu.roll(rows, rows.shape[0] - 1, axis=0)
        a = (rows + pe_ref[0:1, :]).astype(BF16)
        b = (nxt + pe_ref[1:2, :]).astype(BF16)
        hid = _dot(a, w1_ref[0]) + _dot(b, w1_ref[1])
        out = _dot(_gelu(hid).astype(BF16), w2_ref[...])
        if o_ref is ok_ref:
            nb = rows.shape[0]
            end = lax.broadcasted_iota(jnp.int32, (nb, 1), 0) * CMP_STRIDE + (CMP_LEN - 1)
            lane = lax.broadcasted_iota(jnp.int32, (1, K_AUG), 1) % LANES
            out = out + _key_features(end, lane)
        else:
            out = out.T
        o_ref[0] = out.astype(BF16)


def _expand_cmp_weights(pe, w1, w2, out_lanes):
    half = CMP_LEN // 2
    eye = jnp.eye(NSA_KV_HEADS, dtype=F32)
    w1r = w1.reshape(2, half, HEAD_DIM, CMP_HIDDEN)
    w1e = w1r[:, :, None, :, None, :] * eye[None, None, :, None, :, None]
    w1e = w1e.reshape(2, half * D_KV, NSA_KV_HEADS * CMP_HIDDEN).astype(BF16)
    pee = jnp.broadcast_to(pe.reshape(2, half, 1, HEAD_DIM), (2, half, NSA_KV_HEADS, HEAD_DIM))
    pee = pee.reshape(2, half * D_KV)
    w2p = jnp.pad(w2, ((0, 0), (0, out_lanes - HEAD_DIM)))
    w2e = (eye[:, None, :, None] * w2p[None, :, None, :]).reshape(
        NSA_KV_HEADS * CMP_HIDDEN, NSA_KV_HEADS * out_lanes).astype(BF16)
    return pee, w1e, w2e


def _compress(rk, rv, wk, wv):
    b, nb, width = rk.shape
    full2 = lambda a: pl.BlockSpec(a.shape, lambda i: (0, 0))
    full3 = lambda a: pl.BlockSpec(a.shape, lambda i: (0, 0, 0))
    row = pl.BlockSpec((1, nb, width), lambda i: (i, 0, 0))
    out = lambda rows, lanes: pl.BlockSpec((1, rows, lanes), lambda i: (i, 0, 0))
    return pl.pallas_call(
        _compress_kernel,
        grid=(b,),
        in_specs=[row, row, full2(wk[0]), full2(wv[0]), full3(wk[1]), full3(wv[1]),
                  full2(wk[2]), full2(wv[2])],
        out_specs=[out(nb, K_AUG), out(D_KV, nb)],
        out_shape=[jax.ShapeDtypeStruct((b, nb, K_AUG), BF16),
                   jax.ShapeDtypeStruct((b, D_KV, nb), BF16)],
        compiler_params=_cparams(("parallel",)),
        name="nsa_compress",
    )(rk, rv, wk[0], wv[0], wk[1], wv[1], wk[2], wv[2])


NSA_TQ = 128
NSA_TK = 1024


def _masked_softmax(s, mask):
    s = jnp.where(mask, s, NEG_BIG)
    s = s - jnp.max(s, axis=-1, keepdims=True)
    p = jnp.where(mask, jnp.exp(s), 0.0)
    return p / jnp.maximum(jnp.sum(p, axis=-1, keepdims=True), 1e-30)


def _flash_update(carry, s, v_t):
    m, l, acc = carry
    m_new = jnp.maximum(m, jnp.max(s, axis=0, keepdims=True))
    alpha = jnp.exp(m - m_new)
    p = jnp.exp(s - m_new)
    l = alpha * l + jnp.sum(p, axis=0, keepdims=True)
    acc = alpha * acc + _dot(v_t, p.astype(BF16))
    return m_new, l, acc


def _nsa_kernel(q_ref, ks_ref, kw_ref, vst_ref, vwt_ref, ck_ref, cvt_ref, ovl_ref, oh_ref, gl_ref,
                o_ref):
    tq = NSA_TQ
    R = NSA_GROUP
    i = pl.program_id(1)
    q0 = i * tq
    t_lane = q0 + lax.broadcasted_iota(jnp.int32, (1, tq), 1)
    t_lanes = jnp.concatenate([t_lane] * R, axis=1)
    n_cmp = ck_ref.shape[1]
    n_sel = ovl_ref.shape[0]

    gate_t = _sigmoid(gl_ref[...]).T

    cmp_end = lax.broadcasted_iota(jnp.int32, (n_cmp, 1), 0) * CMP_STRIDE + (CMP_LEN - 1)
    mask_c = t_lanes >= cmp_end
    ids = lax.broadcasted_iota(jnp.int32, (n_sel, 1), 0)
    idsf = ids.astype(F32)
    cur = t_lane // SEL_LEN
    forced = (ids == 0) | (ids == cur) | (ids == cur - 1)
    valid = ids * SEL_LEN <= t_lane

    G = NSA_KV_HEADS
    lanes_of = [slice(g * LANES, (g + 1) * LANES) for g in range(G)]
    q_st, q_slc, o_cmp = [], [], []
    for g in range(G):
        qh = [q_ref[:, h * LANES:(h + 1) * LANES] for h in range(g * R, (g + 1) * R)]
        q_st.append(jnp.concatenate(qh, axis=0))

        s = jnp.where(mask_c, _dot_nt(ck_ref[0, :, lanes_of[g]], q_st[g]), NEG_BIG)
        s = s - jnp.max(s, axis=0, keepdims=True)
        p = jnp.where(mask_c, jnp.exp(s), 0.0)
        p = p / jnp.maximum(jnp.sum(p, axis=0, keepdims=True), 1e-30)
        o_cmp.append(_dot(cvt_ref[0], p.astype(BF16)))
        p_hi, p_lo = _split2(p)
        imp4 = _dot(ovl_ref[...], p_hi) + _dot(ovl_ref[...], p_lo)
        imp = imp4[:, 0:tq]
        for r in range(1, R):
            imp = imp + imp4[:, r * tq:(r + 1) * tq]
        imp = jnp.where(forced, 1e6, jnp.where(valid, imp, -1.0))

        sel = jnp.zeros((n_sel, tq), F32)
        for _ in range(min(SEL_TOPN, n_sel)):
            m = jnp.max(imp, axis=0, keepdims=True)
            first = jnp.min(jnp.where(imp == m, idsf, float(n_sel)), axis=0, keepdims=True)
            hit = idsf == first
            sel = jnp.where(hit, 1.0, sel)
            imp = jnp.where(hit, -3e38, imp)
        bias_t = jnp.where(sel > 0.5, 0.0, NEG_BIG)
        if n_sel < LANES:
            bias_t = jnp.concatenate([bias_t, jnp.zeros((LANES - n_sel, tq), F32)], axis=0)
        bias = bias_t.T.astype(BF16)

        q_slc.append(jnp.concatenate([q_st[g], jnp.concatenate([bias] * R, axis=0)], axis=1))

    init = tuple((jnp.full((1, R * tq), -3e38, F32), jnp.zeros((1, R * tq), F32),
                  jnp.zeros((LANES, R * tq), F32)) for _ in range(G))

    def slc_tile(j, carry, diagonal):
        k0 = pl.multiple_of(j * NSA_TK, NSA_TK)
        onehot = oh_ref[pl.ds(k0, NSA_TK), :]
        v_t = vst_ref[0, :, pl.ds(k0, NSA_TK)]
        out = []
        for g in range(G):
            k = jnp.concatenate([ks_ref[pl.ds(k0, NSA_TK), lanes_of[g]], onehot], axis=1)
            s = _dot_nt(k, q_slc[g])
            if diagonal:
                pos = k0 + lax.broadcasted_iota(jnp.int32, (NSA_TK, 1), 0)
                s = jnp.where(t_lanes >= pos, s, NEG_BIG)
            out.append(_flash_update(carry[g], s, v_t))
        return tuple(out)

    n_off = q0 // NSA_TK
    carry = lax.fori_loop(0, n_off, functools.partial(slc_tile, diagonal=False), init)
    slc = slc_tile(n_off, carry, True)

    span = WINDOW + tq
    w0 = pl.multiple_of(jnp.maximum(q0 - WINDOW, 0), tq)
    pos = w0 + lax.broadcasted_iota(jnp.int32, (span, 1), 0)
    visible = (t_lanes >= pos) & (pos > t_lanes - WINDOW)
    v_t = vwt_ref[0, :, pl.ds(w0, span)]
    win = [_flash_update(init[g], jnp.where(
        visible, _dot_nt(kw_ref[pl.ds(w0, span), lanes_of[g]], q_st[g]), NEG_BIG), v_t)
        for g in range(G)]

    for h in range(NSA_HEADS):
        g, r = divmod(h, R)
        rows = slice(r * tq, (r + 1) * tq)
        gate = lambda j: gate_t[h * 3 + j:h * 3 + j + 1, :]
        out_t = (gate(0) * o_cmp[g][:, rows]
                 + gate(1) * (slc[g][2][:, rows] / slc[g][1][:, rows])
                 + gate(2) * (win[g][2][:, rows] / win[g][1][:, rows]))
        o_ref[:, h * LANES:(h + 1) * LANES] = out_t.T.astype(BF16)


def _nsa_consts(t_len, n_cmp_pad):
    n_cmp = (t_len - CMP_LEN) // CMP_STRIDE + 1
    n_sel = t_len // SEL_LEN
    cmp_start = np.arange(n_cmp) * CMP_STRIDE
    sel_start = np.arange(n_sel) * SEL_LEN
    ovl = np.clip(np.minimum(cmp_start[:, None] + CMP_LEN, sel_start[None, :] + SEL_LEN)
                  - np.maximum(cmp_start[:, None], sel_start[None, :]), 0, None) / CMP_LEN
    ovl_t = np.zeros((n_sel, n_cmp_pad), np.float32)
    ovl_t[:, :n_cmp] = ovl.T
    onehot = np.zeros((t_len, LANES), np.float32)
    onehot[np.arange(t_len), np.arange(t_len) // SEL_LEN] = 1.0
    return jnp.asarray(ovl_t, BF16), jnp.asarray(onehot, BF16)


def _nsa_attention(oa, ob, vs_t, vw_t, cmp_k, cmp_vt, b, t_len):
    assert NSA_TK % NSA_TQ == 0 and t_len % NSA_TK == 0 and WINDOW % NSA_TQ == 0
    assert t_len >= WINDOW + NSA_TQ and t_len // SEL_LEN <= LANES
    n_cmp_pad = cmp_k.shape[1]
    ovl_t, onehot = _nsa_consts(t_len, n_cmp_pad)
    nq = t_len // NSA_TQ
    k_col0 = Q_EXP // K_AUG
    kspec = lambda c: pl.BlockSpec((t_len, K_AUG), lambda bi, i: (bi, k_col0 + c))
    vspec = pl.BlockSpec((1, D_KV, t_len), lambda bi, i: (bi, 0, 0))
    const2 = lambda a: pl.BlockSpec(a.shape, lambda bi, i: (0, 0))
    return pl.pallas_call(
        _nsa_kernel,
        grid=(b, nq),
        in_specs=[
            pl.BlockSpec((NSA_TQ, Q_EXP), lambda bi, i: (bi * nq + i, 0)),
            kspec(0), kspec(1), vspec, vspec,
            pl.BlockSpec((1, n_cmp_pad, K_AUG), lambda bi, i: (bi, 0, 0)),
            pl.BlockSpec((1, D_KV, n_cmp_pad), lambda bi, i: (bi, 0, 0)),
            const2(ovl_t), const2(onehot),
            pl.BlockSpec((NSA_TQ, LANES), lambda bi, i: (bi * nq + i, 2)),
        ],
        out_specs=pl.BlockSpec((NSA_TQ, Q_EXP), lambda bi, i: (bi * nq + i, 0)),
        out_shape=jax.ShapeDtypeStruct((b * t_len, Q_EXP), BF16),
        compiler_params=_cparams(("parallel", "arbitrary")),
        name="nsa_attention",
    )(oa, oa, oa, vs_t, vw_t, cmp_k, cmp_vt, ovl_t, onehot, ob)


def _dot_x3(x, e):
    hi = x.astype(BF16)
    r1 = x - hi.astype(F32)
    mid = r1.astype(BF16)
    lo = (r1 - mid.astype(F32)).astype(BF16)
    return _dot(hi, e) + _dot(mid, e) + _dot(lo, e)


def _head_sum_matrix():
    ids = np.arange(D_RWKV) // HEAD_DIM
    return jnp.asarray(ids[:, None] == ids[None, :], BF16)


def _rwkv_pre_kernel(p_ref, hp_ref, mu_ref, w0_ref, w2_ref, a0_ref, a2_ref, g2_ref, kk_ref,
                     ka_ref, rk_ref, bd_ref, r_o, lw_o, k_o, kkn_o, a_o, vt_o, bv_o, g_o, *,
                     tiles_per_seq):
    i = pl.program_id(0)
    p = p_ref[...]
    tm = p.shape[0]
    keep = jnp.where(i % tiles_per_seq == 0, 0.0, 1.0)
    halo = hp_ref[SUBLANES - 1:SUBLANES, :] * keep
    prev = pltpu.roll(p, 1, axis=0)
    row0 = lax.broadcasted_iota(jnp.int32, (tm, 1), 0) == 0
    prev = jnp.where(row0, halo, prev)
    ps = p + (prev - p) * mu_ref[...]
    d = D_RWKV
    r = ps[:, 0:d]
    k = ps[:, d:2 * d]
    v = ps[:, 2 * d:3 * d]
    xw = ps[:, 3 * d:3 * d + RANK_W]
    xa = ps[:, 3 * d + RANK_W:3 * d + RANK_W + RANK_A]
    xg = ps[:, 3 * d + RANK_W + RANK_A:]
    z = -(w0_ref[...] + _dot(jnp.tanh(xw).astype(BF16), w2_ref[...]))
    softplus = jnp.maximum(z, 0.0) + jnp.log(1.0 + jnp.exp(-jnp.abs(z)))
    w = -softplus - 0.5
    a = _sigmoid(a0_ref[...] + _dot(xa.astype(BF16), a2_ref[...]))
    g = _dot(_sigmoid(xg).astype(BF16), g2_ref[...])
    kk = k * kk_ref[...]
    ss = _dot_x3(kk * kk, bd_ref[...])
    kk = kk / jnp.maximum(jnp.sqrt(ss), 1e-12)
    k_mod = k * (1.0 + (a - 1.0) * ka_ref[...])
    for o_ref, val in ((r_o, r), (lw_o, -jnp.exp(w)), (k_o, k_mod), (kkn_o, kk), (a_o, a)):
        for hd in range(RWKV_HEADS):
            o_ref[0, hd] = val[:, hd * HEAD_DIM:(hd + 1) * HEAD_DIM]
    vt_o[0] = v.T.reshape(RWKV_HEADS, HEAD_DIM, tm)
    bv_o[...] = _dot_x3(r * k_mod * rk_ref[...], bd_ref[...]) * v
    g_o[...] = g


def _rwkv_pre(oc, b, t_len, mu, w0, w2, a0, a2, g2, k_k, k_a, r_k, bd, tm=256):
    n = oc.shape[0]
    tps = t_len // tm
    halo_blocks = tm // SUBLANES
    full = lambda a: pl.BlockSpec(a.shape, lambda i: (0, 0))
    tok = pl.BlockSpec((tm, D_RWKV), lambda i: (i, 0))
    hm = pl.BlockSpec((1, RWKV_HEADS, tm, HEAD_DIM), lambda i: (i // tps, 0, i % tps, 0))
    hm_t = pl.BlockSpec((1, RWKV_HEADS, HEAD_DIM, tm), lambda i: (i // tps, 0, 0, i % tps))
    params = (mu, w0, w2, a0, a2, g2, k_k, k_a, r_k, bd)
    hm_shape = jax.ShapeDtypeStruct((b, RWKV_HEADS, t_len, HEAD_DIM), F32)
    return pl.pallas_call(
        functools.partial(_rwkv_pre_kernel, tiles_per_seq=tps),
        grid=(n // tm,),
        in_specs=[
            pl.BlockSpec((tm, D_RWKV_IN), lambda i: (i, 0)),
            pl.BlockSpec((SUBLANES, D_RWKV_IN),
                         lambda i: (jnp.maximum(i * halo_blocks - 1, 0), 0)),
        ] + [full(a) for a in params],
        out_specs=[hm] * 5 + [hm_t, tok, tok],
        out_shape=[hm_shape] * 5
        + [jax.ShapeDtypeStruct((b, RWKV_HEADS, HEAD_DIM, t_len), F32)]
        + [jax.ShapeDtypeStruct((n, D_RWKV), F32)] * 2,
        compiler_params=_cparams(("parallel",)),
        name="rwkv_pre",
    )(oc, oc, *params)


RWKV_C = 64
RWKV_CB = 256
RWKV_PASSES = 1


def _bmm(eq, a, b):
    ein = lambda x, y: jnp.einsum(eq, x, y, preferred_element_type=F32)
    if RWKV_PASSES == 1:
        return ein(a.astype(BF16), b.astype(BF16))
    ah, al = _split2(a)
    bh, bl = _split2(b)
    return ein(ah, bh) + ein(ah, bl) + ein(al, bh)


def _cumsum_rows(x2d, seg):
    rows = lax.broadcasted_iota(jnp.int32, (x2d.shape[0], 1), 0) % seg
    step = 1
    while step < seg:
        shifted = pltpu.roll(x2d, step, axis=0)
        x2d = x2d + jnp.where(rows >= step, shifted, 0.0)
        step *= 2
    return x2d


def _rwkv_rec_kernel(r_ref, lw_ref, k_ref, kk_ref, a_ref, vt_ref, ot_ref, s_ref):
    H = r_ref.shape[1]
    C = RWKV_C

    @pl.when(pl.program_id(1) == 0)
    def _():
        s_ref[...] = jnp.zeros_like(s_ref)

    ri = lax.broadcasted_iota(jnp.int32, (C, C), 0)
    ci = lax.broadcasted_iota(jnp.int32, (C, C), 1)
    strict = (ri > ci)[None]
    incl = (ri >= ci)[None]
    eye = (ri == ci).astype(F32)[None]

    for sub in range(RWKV_CB // C):
        rows = slice(sub * C, (sub + 1) * C)
        r = r_ref[0, :, rows, :]
        lw = lw_ref[0, :, rows, :]
        k = k_ref[0, :, rows, :]
        kk = kk_ref[0, :, rows, :]
        a = a_ref[0, :, rows, :]
        vt = vt_ref[0, :, :, rows]
        s0 = s_ref[...]

        cum = _cumsum_rows(lw.reshape(H * C, HEAD_DIM), C).reshape(H, C, HEAD_DIM)
        cum_last = cum[:, C - 1:C, :]
        p_inv = jnp.exp(-cum)
        w_last = jnp.exp(cum_last - cum)
        bm = kk * a
        at = -kk * jnp.exp(cum - lw)
        rt = r * jnp.exp(cum)
        bt = bm * p_inv
        kt = k * p_inv

        nt = 'hik,hjk->hij'
        ar = jnp.concatenate([at, rt], axis=1)
        ar_b = _bmm(nt, ar, bt)
        ar_k = _bmm(nt, ar, kt)
        m_ab = jnp.where(strict, ar_b[:, :C], 0.0)
        n_rb = jnp.where(incl, ar_b[:, C:], 0.0)
        m_ak = jnp.where(strict, ar_k[:, :C], 0.0)
        n_rk = jnp.where(incl, ar_k[:, C:], 0.0)

        tinv = eye + m_ab
        mp = _bmm('hij,hjk->hik', m_ab, m_ab)
        n = 2
        while n < C:
            both = _bmm('hij,hjk->hik', jnp.concatenate([mp, tinv], axis=1), mp)
            mp = both[:, :C]
            tinv = tinv + both[:, C:]
            n *= 2

        rhs_t = _bmm('hvk,hik->hvi', s0, at) + _bmm('hvj,hij->hvi', vt, m_ak)
        ut = _bmm('hvj,hij->hvi', rhs_t, tinv)
        ot = (_bmm('hvk,hik->hvi', s0, rt) + _bmm('hvj,hij->hvi', ut, n_rb)
              + _bmm('hvj,hij->hvi', vt, n_rk))
        ot_ref[0, :, :, rows] = ot
        s_ref[...] = (s0 * jnp.exp(cum_last) + _bmm('hvj,hjk->hvk', ut, bm * w_last)
                      + _bmm('hvj,hjk->hvk', vt, k * w_last))


def _rwkv_recurrence(r, lw, k, kk, a, vt):
    b, h, t_len, d = r.shape
    tok = pl.BlockSpec((1, h, RWKV_CB, d), lambda bi, c: (bi, 0, c, 0))
    tr = pl.BlockSpec((1, h, d, RWKV_CB), lambda bi, c: (bi, 0, 0, c))
    return pl.pallas_call(
        _rwkv_rec_kernel,
        grid=(b, t_len // RWKV_CB),
        in_specs=[tok] * 5 + [tr],
        out_specs=tr,
        out_shape=jax.ShapeDtypeStruct((b, h, d, t_len), F32),
        scratch_shapes=[pltpu.VMEM((h, d, d), F32)],
        compiler_params=_cparams(("parallel", "arbitrary")),
        name="rwkv_recurrence",
    )(r, lw, k, kk, a, vt)


def _rwkv_post_kernel(ot_ref, bv_ref, g_ref, lnw_ref, lnb_ref, bd_ref, y_ref):
    tm = bv_ref.shape[0]
    o = ot_ref[0].reshape(D_RWKV, tm).T
    bd = bd_ref[...]
    inv = 1.0 / HEAD_DIM
    mean = _dot_x3(o, bd) * inv
    d = o - mean
    var = _dot_x3(d * d, bd) * inv
    on = d * lax.rsqrt(var + GN_EPS) * lnw_ref[...] + lnb_ref[...]
    y_ref[...] = ((on + bv_ref[...]) * g_ref[...]).astype(BF16)


def _rwkv_post(ot, bv, g, ln_w, ln_b, bd, tm=512):
    n = bv.shape[0]
    tps = ot.shape[-1] // tm
    tile = pl.BlockSpec((tm, D_RWKV), lambda i: (i, 0))
    hm_t = pl.BlockSpec((1, RWKV_HEADS, HEAD_DIM, tm), lambda i: (i // tps, 0, 0, i % tps))
    full = lambda a: pl.BlockSpec(a.shape, lambda i: (0, 0))
    return pl.pallas_call(
        _rwkv_post_kernel,
        grid=(n // tm,),
        in_specs=[hm_t, tile, tile, full(ln_w), full(ln_b), full(bd)],
        out_specs=tile,
        out_shape=jax.ShapeDtypeStruct((n, D_RWKV), BF16),
        compiler_params=_cparams(("parallel",)),
        name="rwkv_post",
    )(ot, bv, g, ln_w, ln_b, bd)


def _rwkv_mixer(oc, b, t_len, mu, w0, w2, a0, a2, g2, k_k, k_a, r_k, ln_w, ln_b):
    bd = _head_sum_matrix()
    row = lambda z: z.reshape(1, -1)
    r, lw, k, kk, a, vt, bv, g = _rwkv_pre(
        oc, b, t_len, row(mu), row(w0), w2.astype(BF16), row(a0), a2.astype(BF16),
        g2.astype(BF16), row(k_k), row(k_a), row(r_k), bd)
    ot = _rwkv_recurrence(r, lw, k, kk, a, vt)
    return _rwkv_post(ot, bv, g, row(ln_w), row(ln_b), bd)


def _out_proj_kernel(x_ref, yn_ref, yr_ref, wn_ref, wr_ref, g_ref, x1_ref, h2_ref):
    x1 = x_ref[...] + _dot(yn_ref[...], wn_ref[...]) + _dot(yr_ref[...], wr_ref[...])
    x1_ref[...] = x1
    h2_ref[...] = x1 * lax.rsqrt(jnp.mean(x1 * x1, axis=-1, keepdims=True) + EPS) * g_ref[...]


def _expand_w_out(w_out):
    wn = w_out[:D_NSA].reshape(NSA_HEADS, 1, HEAD_DIM, D_MODEL)
    onehot = (np.arange(NSA_HEADS)[:, None] // NSA_GROUP == np.arange(NSA_KV_HEADS)[None, :])
    wn = wn * jnp.asarray(onehot, F32)[:, :, None, None]
    return wn.reshape(Q_EXP, D_MODEL).astype(BF16), w_out[D_NSA:].astype(BF16)


def _out_proj(x2d, y_nsa, y_rwkv, wn, wr, g, tm=512):
    n = x2d.shape[0]
    tile = lambda a: pl.BlockSpec((tm, a.shape[1]), lambda i: (i, 0))
    full = lambda a: pl.BlockSpec(a.shape, lambda i: (0, 0))
    return pl.pallas_call(
        _out_proj_kernel,
        grid=(n // tm,),
        in_specs=[tile(x2d), tile(y_nsa), tile(y_rwkv), full(wn), full(wr), full(g)],
        out_specs=[tile(x2d), tile(x2d)],
        out_shape=[jax.ShapeDtypeStruct((n, D_MODEL), F32)] * 2,
        compiler_params=_cparams(("parallel",)),
        name="out_proj",
    )(x2d, y_nsa, y_rwkv, wn, wr, g)


def _topk_rows(s, k, rid):
    vals, ids = [], []
    for _ in range(k):
        m = jnp.max(s, axis=0, keepdims=True)
        first = jnp.min(jnp.where(s == m, rid, jnp.inf), axis=0, keepdims=True)
        vals.append(m)
        ids.append(first)
        s = jnp.where(rid == first, -jnp.inf, s)
    return jnp.concatenate(vals, axis=0), jnp.concatenate(ids, axis=0)


def _take_rows(table, idx):
    rows = lax.broadcasted_iota(jnp.int32, (table.shape[0], 1), 0).astype(F32)
    out = [jnp.sum(jnp.where(rows == idx[r:r + 1], table, 0.0), axis=0, keepdims=True)
           for r in range(idx.shape[0])]
    return jnp.concatenate(out, axis=0)


def _pair_candidates():
    k = PEER_TOPK
    flat = [0 * k + j for j in range(k)]
    for i in range(1, SUBLANES):
        flat += [i * k + j for j in range(SUBLANES)]
    flat += [i * k for i in range(SUBLANES, k)]
    return np.asarray(flat, np.float32).reshape(-1, 1)


def _peer_route_kernel(h_ref, wq_ref, sk_ref, flat_ref, e_ref, g_ref):
    K = PEER_TOPK
    half = PEER_D_QUERY // 2
    q = _dot(h_ref[...].astype(BF16), wq_ref[...]).astype(BF16)
    key_ids = lax.broadcasted_iota(jnp.int32, (PEER_N_KEYS, 1), 0).astype(F32)
    flat = flat_ref[...]
    rows_e, rows_g = [], []
    for h in range(PEER_HEADS):
        top = []
        for c in range(2):
            col = (h * 2 + c) * half
            s_t = _dot_nt(sk_ref[h, c], q[:, col:col + half])
            top.append(_topk_rows(s_t, K, key_ids))
        (v0, i0), (v1, i1) = top
        cand = jnp.concatenate(
            [v0[0:1] + v1]
            + [v0[i:i + 1] + v1[:SUBLANES] for i in range(1, SUBLANES)]
            + [v0[SUBLANES:] + v1[0:1]], axis=0)
        best, pair = _topk_rows(cand, K, flat)
        pi = jnp.floor(pair * (1.0 / K))
        pj = pair - pi * K
        experts = _take_rows(i0, pi) * float(PEER_N_KEYS) + _take_rows(i1, pj)
        p = jnp.exp(best - jnp.max(best, axis=0, keepdims=True))
        rows_e.append(experts * float(HALF_ROWS))
        rows_g.append(p / jnp.sum(p, axis=0, keepdims=True))
    e_ref[...] = jnp.concatenate(rows_e, axis=0).T.astype(jnp.int32)
    g_ref[...] = jnp.concatenate(rows_g, axis=0).T


def _peer_route(h2, wq, sk, tm=256):
    n = h2.shape[0]
    hk = PEER_HEADS * PEER_TOPK
    flat = jnp.asarray(_pair_candidates())
    out = pl.BlockSpec((tm, hk), lambda i: (i, 0))
    return pl.pallas_call(
        _peer_route_kernel,
        grid=(n // tm,),
        in_specs=[
            pl.BlockSpec((tm, D_MODEL), lambda i: (i, 0)),
            pl.BlockSpec(wq.shape, lambda i: (0, 0)),
            pl.BlockSpec(sk.shape, lambda i: (0, 0, 0, 0)),
            pl.BlockSpec(flat.shape, lambda i: (0, 0)),
        ],
        out_specs=[out, out],
        out_shape=[jax.ShapeDtypeStruct((n, hk), jnp.int32), jax.ShapeDtypeStruct((n, hk), F32)],
        compiler_params=_cparams(("parallel",)),
        name="peer_route",
    )(h2, wq, sk, flat)


PEER_TT = 128
HALF_ROWS = SUBLANES // 2
SUB_ORDER = (0, 4, 2, 6, 1, 5, 3, 7)
HIGH_MASK = 0xFFFF0000


def _pack_table(w):
    bits = lax.bitcast_convert_type(w.astype(BF16), jnp.uint16).astype(jnp.uint32)
    half = w.shape[1] // 2
    packed = bits[:, :half] | (bits[:, half:] << 16)
    return packed.reshape(w.shape[0] * HALF_ROWS, LANES)


def _table_row(tbl_ref, row0):
    return tbl_ref[pl.ds(pl.multiple_of(row0, HALF_ROWS), HALF_ROWS), :]


def _unpack_words(word):
    lo = pltpu.bitcast(word << 16, F32)
    hi = pltpu.bitcast(word & jnp.uint32(HIGH_MASK), F32)
    return lo, hi


def _sublane_tree(c):
    sub = lax.broadcasted_iota(jnp.int32, (SUBLANES, LANES), 0)
    m2 = (sub % 4) < 2
    d = []
    for x, y in zip(c[0::2], c[1::2]):
        d.append(jnp.where(m2, x + pltpu.roll(x, 6, axis=0), y + pltpu.roll(y, 2, axis=0)))
    m1 = (sub % 2) == 0
    x, y = d
    return jnp.where(m1, x + pltpu.roll(x, 7, axis=0), y + pltpu.roll(y, 1, axis=0))


def _peer_act_kernel(e_ref, h_ref, tbl_ref, gate_ref, ones_ref, w_ref, part_ref):
    hk = PEER_HEADS * PEER_TOPK
    tt = h_ref.shape[0]

    def token(t):
        hrow = h_ref[t]
        h_lo = jnp.concatenate([hrow[:HALF_ROWS]] * 2, axis=0)
        h_hi = jnp.concatenate([hrow[HALF_ROWS:]] * 2, axis=0)
        for m in range(hk // SUBLANES):
            es = [e_ref[t, m * SUBLANES + j] for j in SUB_ORDER]
            c = []
            for ea, eb in zip(es[0::2], es[1::2]):
                lo, hi = _unpack_words(jnp.concatenate(
                    [_table_row(tbl_ref, ea), _table_row(tbl_ref, eb)], axis=0))
                c.append(lo * h_lo + hi * h_hi)
            row0 = pl.multiple_of((t * (hk // SUBLANES) + m) * SUBLANES, SUBLANES)
            part_ref[pl.ds(row0, SUBLANES), :] = _sublane_tree(c)

    def token_pair(i, carry):
        token(2 * i)
        token(2 * i + 1)
        return carry

    lax.fori_loop(0, tt // 2, token_pair, 0)
    half = (tt // 2) * hk
    hi, lo = _split2(part_ref[0:half, :])
    sums_a = _dot(jnp.concatenate([hi, lo], axis=1), ones_ref[...])
    sums_b = jnp.sum(part_ref[half:, :], axis=-1, keepdims=True)
    sums_b = jnp.broadcast_to(sums_b, (tt * hk - half, LANES))
    sums = jnp.concatenate([sums_a, sums_b], axis=0).reshape(tt, hk, LANES)
    eye = (lax.broadcasted_iota(jnp.int32, (hk, LANES), 0)
           == lax.broadcasted_iota(jnp.int32, (hk, LANES), 1))
    act = jnp.sum(jnp.where(eye[None], sums, 0.0), axis=1)
    w_ref[...] = _gelu(act) * gate_ref[...]


def _table_spec(tbl):
    return pl.BlockSpec(tbl.shape, lambda i: (0, 0), pipeline_mode=pl.Buffered(1))


def _peer_act(experts, h2_tiles, tbl, gates):
    n, hk = experts.shape
    ones = jnp.ones((2 * LANES, LANES), BF16)
    return pl.pallas_call(
        _peer_act_kernel,
        grid=(n // PEER_TT,),
        in_specs=[
            pl.BlockSpec((PEER_TT, hk), lambda i: (i, 0), memory_space=pltpu.SMEM),
            pl.BlockSpec((PEER_TT, SUBLANES, LANES), lambda i: (i, 0, 0)),
            _table_spec(tbl),
            pl.BlockSpec((PEER_TT, hk), lambda i: (i, 0)),
            pl.BlockSpec(ones.shape, lambda i: (0, 0)),
        ],
        out_specs=pl.BlockSpec((PEER_TT, hk), lambda i: (i, 0)),
        out_shape=jax.ShapeDtypeStruct((n, hk), F32),
        scratch_shapes=[pltpu.VMEM((PEER_TT * hk, LANES), F32)],
        compiler_params=_cparams(("arbitrary",)),
        name="peer_act",
    )(experts, h2_tiles, tbl, gates, ones)


PEER_ACCS = 4


def _peer_out_kernel(e_ref, w_ref, x_ref, tbl_ref, g_ref, ones_ref, o_ref, wrep_ref, *,
                     final_norm):
    hk = PEER_HEADS * PEER_TOPK
    tt = x_ref.shape[0]

    eye = (lax.broadcasted_iota(jnp.int32, (hk, LANES), 0)
           == lax.broadcasted_iota(jnp.int32, (hk, LANES), 1))
    diag = jnp.where(eye[None], w_ref[...][:, None, :], 0.0).reshape(tt * hk, LANES)
    wrep_ref[...] = _dot(diag.astype(BF16), ones_ref[...])

    def token(t):
        acc_lo = [jnp.zeros((HALF_ROWS, LANES), F32) for _ in range(PEER_ACCS)]
        acc_hi = [jnp.zeros((HALF_ROWS, LANES), F32) for _ in range(PEER_ACCS)]
        base = t * hk
        for j in range(hk):
            lo, hi = _unpack_words(_table_row(tbl_ref, e_ref[base + j]))
            wk = wrep_ref[pl.ds(base + j, 1), :]
            acc_lo[j % PEER_ACCS] = acc_lo[j % PEER_ACCS] + wk * lo
            acc_hi[j % PEER_ACCS] = acc_hi[j % PEER_ACCS] + wk * hi
        tree = lambda a: (a[0] + a[1]) + (a[2] + a[3])
        o_ref[t] = x_ref[t] + jnp.concatenate([tree(acc_lo), tree(acc_hi)], axis=0)

    def token_pair(i, carry):
        token(2 * i)
        token(2 * i + 1)
        return carry

    lax.fori_loop(0, tt // 2, token_pair, 0)
    if final_norm:
        x2 = o_ref[...]
        ms = jnp.sum(jnp.sum(x2 * x2, axis=2, keepdims=True), axis=1, keepdims=True) / D_MODEL
        o_ref[...] = x2 * lax.rsqrt(ms + EPS) * g_ref[...]


def _peer_out(experts, w, x1_tiles, tbl, final_g, final_norm):
    n, hk = experts.shape
    ones = jnp.ones((LANES, LANES), BF16)
    smem = pl.BlockSpec((PEER_TT * hk,), lambda i: (i,), memory_space=pltpu.SMEM)
    tile = pl.BlockSpec((PEER_TT, SUBLANES, LANES), lambda i: (i, 0, 0))
    return pl.pallas_call(
        functools.partial(_peer_out_kernel, final_norm=final_norm),
        grid=(n // PEER_TT,),
        in_specs=[smem, pl.BlockSpec((PEER_TT, hk), lambda i: (i, 0)), tile, _table_spec(tbl),
                  pl.BlockSpec(final_g.shape, lambda i: (0, 0, 0)),
                  pl.BlockSpec(ones.shape, lambda i: (0, 0))],
        out_specs=tile,
        out_shape=jax.ShapeDtypeStruct(x1_tiles.shape, F32),
        scratch_shapes=[pltpu.VMEM((PEER_TT * hk, LANES), F32)],
        compiler_params=_cparams(("arbitrary",)),
        name="peer_out",
    )(experts.reshape(-1), w, x1_tiles, tbl, final_g, ones)


def _peer_ffn_residual(x1, h2, wq, sk, tbl_u, tbl_v, final_g, final_norm):
    n = x1.shape[0]
    experts, gates = _peer_route(h2, wq, sk)
    tiles = lambda z: z.reshape(n, SUBLANES, LANES)
    w = _peer_act(experts, tiles(h2), tbl_u, gates)
    g_tile = final_g.reshape(1, SUBLANES, LANES)
    return _peer_out(experts, w, tiles(x1), tbl_v, g_tile, final_norm).reshape(n, D_MODEL)


def kernel(x, norm_mix_g, w_in, cmp_k_pe, cmp_k_w1, cmp_k_w2, cmp_v_pe, cmp_v_w1, cmp_v_w2,
           rwkv_mu, rwkv_w0, rwkv_w2, rwkv_a0, rwkv_a2, rwkv_g2, rwkv_k_k, rwkv_k_a, rwkv_r_k,
           rwkv_ln_w, rwkv_ln_b, w_out, norm_ffn_g, peer_w_q, peer_sub_keys, peer_u, peer_v,
           norm_final_g):
    b, t_len, _ = x.shape
    n = b * t_len
    row = lambda z: z.reshape(1, -1)
    x2d = x.reshape(n, D_MODEL)
    depth = w_in.shape[0]
    for l in range(depth):
        oa, ob, oc = _in_proj(x2d, row(norm_mix_g[l]), _pad_w_in(w_in[l]), t_len)
        nb = t_len // CMP_STRIDE
        rk = ob[:, 0:D_KV].reshape(b, nb, CMP_STRIDE * D_KV)
        rv = ob[:, D_KV:2 * D_KV].reshape(b, nb, CMP_STRIDE * D_KV)
        cmp_k, cmp_vt = _compress(
            rk, rv, _expand_cmp_weights(cmp_k_pe[l], cmp_k_w1[l], cmp_k_w2[l], LANES),
            _expand_cmp_weights(cmp_v_pe[l], cmp_v_w1[l], cmp_v_w2[l], HEAD_DIM))
        v_col0 = Q_EXP + 2 * K_AUG
        v_t = lambda c: oa[:, v_col0 + c * D_KV:v_col0 + (c + 1) * D_KV].reshape(
            b, t_len, D_KV).transpose(0, 2, 1)
        y_nsa = _nsa_attention(oa, ob, v_t(0), v_t(1), cmp_k, cmp_vt, b, t_len)
        y_rwkv = _rwkv_mixer(oc, b, t_len, rwkv_mu[l], rwkv_w0[l], rwkv_w2[l], rwkv_a0[l],
                             rwkv_a2[l], rwkv_g2[l], rwkv_k_k[l], rwkv_k_a[l], rwkv_r_k[l],
                             rwkv_ln_w[l], rwkv_ln_b[l])
        wn, wr = _expand_w_out(w_out[l])
        x1, h2 = _out_proj(x2d, y_nsa, y_rwkv, wn, wr, row(norm_ffn_g[l]))
        x2d = _peer_ffn_residual(x1, h2, peer_w_q[l].astype(BF16), peer_sub_keys[l].astype(BF16),
                                 _pack_table(peer_u[l]), _pack_table(peer_v[l]),
                                 norm_final_g, final_norm=(l == depth - 1))
    return x2d.reshape(b, t_len, D_MODEL)
```

```python
import functools

import numpy as np
import jax
import jax.numpy as jnp
from jax import lax
from jax.experimental import pallas as pl
from jax.experimental.pallas import tpu as pltpu

F32 = jnp.float32
BF16 = jnp.bfloat16

D_MODEL = 1024
HEAD_DIM = 64
NSA_HEADS = 8
NSA_KV_HEADS = 2
NSA_GROUP = NSA_HEADS // NSA_KV_HEADS
RWKV_HEADS = 8
D_NSA = NSA_HEADS * HEAD_DIM
D_RWKV = RWKV_HEADS * HEAD_DIM
D_KV = NSA_KV_HEADS * HEAD_DIM
CMP_LEN = 32
CMP_STRIDE = 16
CMP_HIDDEN = 128
SEL_LEN = 64
SEL_TOPN = 16
WINDOW = 512
RANK_W = 64
RANK_A = 64
RANK_G = 128
D_RWKV_IN = 3 * D_RWKV + RANK_W + RANK_A + RANK_G
PEER_HEADS = 8
PEER_N_KEYS = 128
PEER_D_QUERY = 256
PEER_TOPK = 16
EPS = 1e-6
GN_EPS = 64e-5

LANES = 128
SUBLANES = 8
VMEM_LIMIT = 56 * 1024 * 1024

NEG_BIG = -1e30
Q_EXP = NSA_HEADS * LANES
K_AUG = NSA_KV_HEADS * LANES
NA_COLS = Q_EXP + 2 * K_AUG + 2 * D_KV
NB_COLS = 3 * LANES
FEAT0 = HEAD_DIM
POS_SPLIT = 64


def _cparams(sem):
    return pltpu.CompilerParams(dimension_semantics=sem, vmem_limit_bytes=VMEM_LIMIT)


def _dot(a, b):
    return jnp.dot(a, b, preferred_element_type=F32)


def _dot_nt(a, b):
    return lax.dot_general(a, b, (((1,), (1,)), ((), ())), preferred_element_type=F32)


def _split2(x):
    hi = x.astype(BF16)
    lo = (x - hi.astype(F32)).astype(BF16)
    return hi, lo


def _dot_x2(x, e):
    hi, lo = _split2(x)
    return _dot(hi, e) + _dot(lo, e)


def _gelu(x):
    return 0.5 * x * (1.0 + jnp.tanh(0.7978845608028654 * (x + 0.044715 * (x * x * x))))


def _sigmoid(x):
    return 1.0 / (1.0 + jnp.exp(-x))


def _key_features(pos, lane):
    hi = (pos // POS_SPLIT).astype(F32)
    lo = (pos % POS_SPLIT).astype(F32)
    return jnp.where(lane == FEAT0, hi, jnp.where(lane == FEAT0 + 1, lo,
                     jnp.where((lane == FEAT0 + 2) | (lane == FEAT0 + 3), 1.0, 0.0)))


def _query_features(t, lane, slope):
    hi = (t // POS_SPLIT).astype(F32)
    lo = (t % POS_SPLIT).astype(F32)
    return jnp.where(lane == FEAT0, slope * POS_SPLIT, jnp.where(
        lane == FEAT0 + 1, slope, jnp.where(
            lane == FEAT0 + 2, -slope * POS_SPLIT * hi, jnp.where(
                lane == FEAT0 + 3, -slope * lo, 0.0))))


def _in_proj_kernel(x_ref, g_ref, w_ref, oa_ref, ob_ref, oc_ref, *, t_len):
    x = x_ref[...]
    tm = x.shape[0]
    h = x * lax.rsqrt(jnp.mean(x * x, axis=-1, keepdims=True) + EPS) * g_ref[...]
    hb = h.astype(BF16)
    t = (pl.program_id(0) * tm + lax.broadcasted_iota(jnp.int32, (tm, 1), 0)) % t_len
    col = lax.broadcasted_iota(jnp.int32, (1, Q_EXP), 1)
    slope = jnp.zeros((1, Q_EXP), F32)
    for hd in range(NSA_HEADS):
        slope = jnp.where(col // LANES == hd, 2.0 ** -(hd + 1), slope)
    q = _dot(hb, w_ref[:, :Q_EXP]) * (HEAD_DIM ** -0.5) + _query_features(t, col % LANES, slope)
    oa_ref[:, :Q_EXP] = q.astype(BF16)
    colk = lax.broadcasted_iota(jnp.int32, (1, 2 * K_AUG), 1)
    k = _dot(hb, w_ref[:, Q_EXP:Q_EXP + 2 * K_AUG]) + _key_features(t, colk % LANES)
    oa_ref[:, Q_EXP:Q_EXP + 2 * K_AUG] = k.astype(BF16)
    oa_ref[:, Q_EXP + 2 * K_AUG:] = _dot(hb, w_ref[:, Q_EXP + 2 * K_AUG:NA_COLS]).astype(BF16)
    ob_ref[...] = _dot(hb, w_ref[:, NA_COLS:NA_COLS + NB_COLS])
    oc_ref[...] = _dot(hb, w_ref[:, NA_COLS + NB_COLS:])


def _in_proj(x2d, g, w_pad, t_len, tm=256):
    n = x2d.shape[0]
    ncols = w_pad.shape[1]
    nc = ncols - NA_COLS - NB_COLS
    return pl.pallas_call(
        functools.partial(_in_proj_kernel, t_len=t_len),
        grid=(n // tm,),
        in_specs=[
            pl.BlockSpec((tm, D_MODEL), lambda i: (i, 0)),
            pl.BlockSpec((1, D_MODEL), lambda i: (0, 0)),
            pl.BlockSpec((D_MODEL, ncols), lambda i: (0, 0)),
        ],
        out_specs=[
            pl.BlockSpec((tm, NA_COLS), lambda i: (i, 0)),
            pl.BlockSpec((tm, NB_COLS), lambda i: (i, 0)),
            pl.BlockSpec((tm, nc), lambda i: (i, 0)),
        ],
        out_shape=[
            jax.ShapeDtypeStruct((n, NA_COLS), BF16),
            jax.ShapeDtypeStruct((n, NB_COLS), F32),
            jax.ShapeDtypeStruct((n, nc), F32),
        ],
        compiler_params=_cparams(("parallel",)),
        name="in_proj",
    )(x2d, g, w_pad)


def _pad_w_in(w_in):
    sizes = (D_NSA, D_KV, D_KV, D_KV, D_KV, D_KV, D_KV, 3 * NSA_HEADS, D_RWKV_IN)
    offs = np.cumsum((0,) + sizes)
    q, kc, vc, ks, vs, kw, vw, gl, rw = (w_in[:, offs[i]:offs[i + 1]] for i in range(9))
    def lane_pad(w, groups):
        w = w.reshape(D_MODEL, groups, HEAD_DIM)
        return jnp.pad(w, ((0, 0), (0, 0), (0, LANES - HEAD_DIM))).reshape(D_MODEL, groups * LANES)

    glp = jnp.pad(gl, ((0, 0), (0, LANES - 3 * NSA_HEADS)))
    return jnp.concatenate(
        [lane_pad(q, NSA_HEADS), lane_pad(ks, NSA_KV_HEADS), lane_pad(kw, NSA_KV_HEADS),
         vs, vw, kc, vc, glp, rw], axis=1).astype(BF16)


def _compress_kernel(rk_ref, rv_ref, pek_ref, pev_ref, w1k_ref, w1v_ref, w2k_ref, w2v_ref,
                     ok_ref, ov_ref):
    for r_ref, pe_ref, w1_ref, w2_ref, o_ref in (
            (rk_ref, pek_ref, w1k_ref, w2k_ref, ok_ref),
            (rv_ref, pev_ref, w1v_ref, w2v_ref, ov_ref)):
        rows = r_ref[0]
        nxt = pltpu.roll(rows, rows.shape[0] - 1, axis=0)
        a = (rows + pe_ref[0:1, :]).astype(BF16)
        b = (nxt + pe_ref[1:2, :]).astype(BF16)
        hid = _dot(a, w1_ref[0]) + _dot(b, w1_ref[1])
        out = _dot(_gelu(hid).astype(BF16), w2_ref[...])
        if o_ref is ok_ref:
            nb = rows.shape[0]
            end = lax.broadcasted_iota(jnp.int32, (nb, 1), 0) * CMP_STRIDE + (CMP_LEN - 1)
            lane = lax.broadcasted_iota(jnp.int32, (1, K_AUG), 1) % LANES
            out = out + _key_features(end, lane)
        else:
            out = out.T
        o_ref[0] = out.astype(BF16)


def _expand_cmp_weights(pe, w1, w2, out_lanes):
    half = CMP_LEN // 2
    eye = jnp.eye(NSA_KV_HEADS, dtype=F32)
    w1r = w1.reshape(2, half, HEAD_DIM, CMP_HIDDEN)
    w1e = w1r[:, :, None, :, None, :] * eye[None, None, :, None, :, None]
    w1e = w1e.reshape(2, half * D_KV, NSA_KV_HEADS * CMP_HIDDEN).astype(BF16)
    pee = jnp.broadcast_to(pe.reshape(2, half, 1, HEAD_DIM), (2, half, NSA_KV_HEADS, HEAD_DIM))
    pee = pee.reshape(2, half * D_KV)
    w2p = jnp.pad(w2, ((0, 0), (0, out_lanes - HEAD_DIM)))
    w2e = (eye[:, None, :, None] * w2p[None, :, None, :]).reshape(
        NSA_KV_HEADS * CMP_HIDDEN, NSA_KV_HEADS * out_lanes).astype(BF16)
    return pee, w1e, w2e


def _compress(rk, rv, wk, wv):
    b, nb, width = rk.shape
    full2 = lambda a: pl.BlockSpec(a.shape, lambda i: (0, 0))
    full3 = lambda a: pl.BlockSpec(a.shape, lambda i: (0, 0, 0))
    row = pl.BlockSpec((1, nb, width), lambda i: (i, 0, 0))
    out = lambda rows, lanes: pl.BlockSpec((1, rows, lanes), lambda i: (i, 0, 0))
    return pl.pallas_call(
        _compress_kernel,
        grid=(b,),
        in_specs=[row, row, full2(wk[0]), full2(wv[0]), full3(wk[1]), full3(wv[1]),
                  full2(wk[2]), full2(wv[2])],
        out_specs=[out(nb, K_AUG), out(D_KV, nb)],
        out_shape=[jax.ShapeDtypeStruct((b, nb, K_AUG), BF16),
                   jax.ShapeDtypeStruct((b, D_KV, nb), BF16)],
        compiler_params=_cparams(("parallel",)),
        name="nsa_compress",
    )(rk, rv, wk[0], wv[0], wk[1], wv[1], wk[2], wv[2])


NSA_TQ = 128
NSA_TK = 1024


def _masked_softmax(s, mask):
    s = jnp.where(mask, s, NEG_BIG)
    s = s - jnp.max(s, axis=-1, keepdims=True)
    p = jnp.where(mask, jnp.exp(s), 0.0)
    return p / jnp.maximum(jnp.sum(p, axis=-1, keepdims=True), 1e-30)


def _flash_update(carry, s, v_t):
    m, l, acc = carry
    m_new = jnp.maximum(m, jnp.max(s, axis=0, keepdims=True))
    alpha = jnp.exp(m - m_new)
    p = jnp.exp(s - m_new)
    l = alpha * l + jnp.sum(p, axis=0, keepdims=True)
    acc = alpha * acc + _dot(v_t, p.astype(BF16))
    return m_new, l, acc


def _nsa_kernel(q_ref, ks_ref, kw_ref, vst_ref, vwt_ref, ck_ref, cvt_ref, ovl_ref, oh_ref, gl_ref,
                o_ref):
    tq = NSA_TQ
    R = NSA_GROUP
    i = pl.program_id(1)
    q0 = i * tq
    t_lane = q0 + lax.broadcasted_iota(jnp.int32, (1, tq), 1)
    t_lanes = jnp.concatenate([t_lane] * R, axis=1)
    n_cmp = ck_ref.shape[1]
    n_sel = ovl_ref.shape[0]

    gate_t = _sigmoid(gl_ref[...]).T

    cmp_end = lax.broadcasted_iota(jnp.int32, (n_cmp, 1), 0) * CMP_STRIDE + (CMP_LEN - 1)
    mask_c = t_lanes >= cmp_end
    ids = lax.broadcasted_iota(jnp.int32, (n_sel, 1), 0)
    idsf = ids.astype(F32)
    cur = t_lane // SEL_LEN
    forced = (ids == 0) | (ids == cur) | (ids == cur - 1)
    valid = ids * SEL_LEN <= t_lane

    G = NSA_KV_HEADS
    lanes_of = [slice(g * LANES, (g + 1) * LANES) for g in range(G)]
    q_st, q_slc, o_cmp = [], [], []
    for g in range(G):
        qh = [q_ref[:, h * LANES:(h + 1) * LANES] for h in range(g * R, (g + 1) * R)]
        q_st.append(jnp.concatenate(qh, axis=0))

        s = jnp.where(mask_c, _dot_nt(ck_ref[0, :, lanes_of[g]], q_st[g]), NEG_BIG)
        s = s - jnp.max(s, axis=0, keepdims=True)
        p = jnp.where(mask_c, jnp.exp(s), 0.0)
        p = p / jnp.maximum(jnp.sum(p, axis=0, keepdims=True), 1e-30)
        o_cmp.append(_dot(cvt_ref[0], p.astype(BF16)))
        p_hi, p_lo = _split2(p)
        imp4 = _dot(ovl_ref[...], p_hi) + _dot(ovl_ref[...], p_lo)
        imp = imp4[:, 0:tq]
        for r in range(1, R):
            imp = imp + imp4[:, r * tq:(r + 1) * tq]
        imp = jnp.where(forced, 1e6, jnp.where(valid, imp, -1.0))

        sel = jnp.zeros((n_sel, tq), F32)
        for _ in range(min(SEL_TOPN, n_sel)):
            m = jnp.max(imp, axis=0, keepdims=True)
            first = jnp.min(jnp.where(imp == m, idsf, float(n_sel)), axis=0, keepdims=True)
            hit = idsf == first
            sel = jnp.where(hit, 1.0, sel)
            imp = jnp.where(hit, -3e38, imp)
        bias_t = jnp.where(sel > 0.5, 0.0, NEG_BIG)
        if n_sel < LANES:
            bias_t = jnp.concatenate([bias_t, jnp.zeros((LANES - n_sel, tq), F32)], axis=0)
        bias = bias_t.T.astype(BF16)

        q_slc.append(jnp.concatenate([q_st[g], jnp.concatenate([bias] * R, axis=0)], axis=1))

    init = tuple((jnp.full((1, R * tq), -3e38, F32), jnp.zeros((1, R * tq), F32),
                  jnp.zeros((LANES, R * tq), F32)) for _ in range(G))

    def slc_tile(j, carry, diagonal):
        k0 = pl.multiple_of(j * NSA_TK, NSA_TK)
        onehot = oh_ref[pl.ds(k0, NSA_TK), :]
        v_t = vst_ref[0, :, pl.ds(k0, NSA_TK)]
        out = []
        for g in range(G):
            k = jnp.concatenate([ks_ref[pl.ds(k0, NSA_TK), lanes_of[g]], onehot], axis=1)
            s = _dot_nt(k, q_slc[g])
            if diagonal:
                pos = k0 + lax.broadcasted_iota(jnp.int32, (NSA_TK, 1), 0)
                s = jnp.where(t_lanes >= pos, s, NEG_BIG)
            out.append(_flash_update(carry[g], s, v_t))
        return tuple(out)

    n_off = q0 // NSA_TK
    carry = lax.fori_loop(0, n_off, functools.partial(slc_tile, diagonal=False), init)
    slc = slc_tile(n_off, carry, True)

    span = WINDOW + tq
    w0 = pl.multiple_of(jnp.maximum(q0 - WINDOW, 0), tq)
    pos = w0 + lax.broadcasted_iota(jnp.int32, (span, 1), 0)
    visible = (t_lanes >= pos) & (pos > t_lanes - WINDOW)
    v_t = vwt_ref[0, :, pl.ds(w0, span)]
    win = [_flash_update(init[g], jnp.where(
        visible, _dot_nt(kw_ref[pl.ds(w0, span), lanes_of[g]], q_st[g]), NEG_BIG), v_t)
        for g in range(G)]

    for h in range(NSA_HEADS):
        g, r = divmod(h, R)
        rows = slice(r * tq, (r + 1) * tq)
        gate = lambda j: gate_t[h * 3 + j:h * 3 + j + 1, :]
        out_t = (gate(0) * o_cmp[g][:, rows]
                 + gate(1) * (slc[g][2][:, rows] / slc[g][1][:, rows])
                 + gate(2) * (win[g][2][:, rows] / win[g][1][:, rows]))
        o_ref[:, h * LANES:(h + 1) * LANES] = out_t.T.astype(BF16)


def _nsa_consts(t_len, n_cmp_pad):
    n_cmp = (t_len - CMP_LEN) // CMP_STRIDE + 1
    n_sel = t_len // SEL_LEN
    cmp_start = np.arange(n_cmp) * CMP_STRIDE
    sel_start = np.arange(n_sel) * SEL_LEN
    ovl = np.clip(np.minimum(cmp_start[:, None] + CMP_LEN, sel_start[None, :] + SEL_LEN)
                  - np.maximum(cmp_start[:, None], sel_start[None, :]), 0, None) / CMP_LEN
    ovl_t = np.zeros((n_sel, n_cmp_pad), np.float32)
    ovl_t[:, :n_cmp] = ovl.T
    onehot = np.zeros((t_len, LANES), np.float32)
    onehot[np.arange(t_len), np.arange(t_len) // SEL_LEN] = 1.0
    return jnp.asarray(ovl_t, BF16), jnp.asarray(onehot, BF16)


def _nsa_attention(oa, ob, vs_t, vw_t, cmp_k, cmp_vt, b, t_len):
    assert NSA_TK % NSA_TQ == 0 and t_len % NSA_TK == 0 and WINDOW % NSA_TQ == 0
    assert t_len >= WINDOW + NSA_TQ and t_len // SEL_LEN <= LANES
    n_cmp_pad = cmp_k.shape[1]
    ovl_t, onehot = _nsa_consts(t_len, n_cmp_pad)
    nq = t_len // NSA_TQ
    k_col0 = Q_EXP // K_AUG
    kspec = lambda c: pl.BlockSpec((t_len, K_AUG), lambda bi, i: (bi, k_col0 + c))
    vspec = pl.BlockSpec((1, D_KV, t_len), lambda bi, i: (bi, 0, 0))
    const2 = lambda a: pl.BlockSpec(a.shape, lambda bi, i: (0, 0))
    return pl.pallas_call(
        _nsa_kernel,
        grid=(b, nq),
        in_specs=[
            pl.BlockSpec((NSA_TQ, Q_EXP), lambda bi, i: (bi * nq + i, 0)),
            kspec(0), kspec(1), vspec, vspec,
            pl.BlockSpec((1, n_cmp_pad, K_AUG), lambda bi, i: (bi, 0, 0)),
            pl.BlockSpec((1, D_KV, n_cmp_pad), lambda bi, i: (bi, 0, 0)),
            const2(ovl_t), const2(onehot),
            pl.BlockSpec((NSA_TQ, LANES), lambda bi, i: (bi * nq + i, 2)),
        ],
        out_specs=pl.BlockSpec((NSA_TQ, Q_EXP), lambda bi, i: (bi * nq + i, 0)),
        out_shape=jax.ShapeDtypeStruct((b * t_len, Q_EXP), BF16),
        compiler_params=_cparams(("parallel", "arbitrary")),
        name="nsa_attention",
    )(oa, oa, oa, vs_t, vw_t, cmp_k, cmp_vt, ovl_t, onehot, ob)


def _dot_x3(x, e):
    hi = x.astype(BF16)
    r1 = x - hi.astype(F32)
    mid = r1.astype(BF16)
    lo = (r1 - mid.astype(F32)).astype(BF16)
    return _dot(hi, e) + _dot(mid, e) + _dot(lo, e)


def _head_sum_matrix():
    ids = np.arange(D_RWKV) // HEAD_DIM
    return jnp.asarray(ids[:, None] == ids[None, :], BF16)


def _rwkv_pre_kernel(p_ref, hp_ref, mu_ref, w0_ref, w2_ref, a0_ref, a2_ref, g2_ref, kk_ref,
                     ka_ref, rk_ref, bd_ref, r_o, lw_o, k_o, kkn_o, a_o, vt_o, bv_o, g_o, *,
                     tiles_per_seq):
    i = pl.program_id(0)
    p = p_ref[...]
    tm = p.shape[0]
    keep = jnp.where(i % tiles_per_seq == 0, 0.0, 1.0)
    halo = hp_ref[SUBLANES - 1:SUBLANES, :] * keep
    prev = pltpu.roll(p, 1, axis=0)
    row0 = lax.broadcasted_iota(jnp.int32, (tm, 1), 0) == 0
    prev = jnp.where(row0, halo, prev)
    ps = p + (prev - p) * mu_ref[...]
    d = D_RWKV
    r = ps[:, 0:d]
    k = ps[:, d:2 * d]
    v = ps[:, 2 * d:3 * d]
    xw = ps[:, 3 * d:3 * d + RANK_W]
    xa = ps[:, 3 * d + RANK_W:3 * d + RANK_W + RANK_A]
    xg = ps[:, 3 * d + RANK_W + RANK_A:]
    z = -(w0_ref[...] + _dot(jnp.tanh(xw).astype(BF16), w2_ref[...]))
    softplus = jnp.maximum(z, 0.0) + jnp.log(1.0 + jnp.exp(-jnp.abs(z)))
    w = -softplus - 0.5
    a = _sigmoid(a0_ref[...] + _dot(xa.astype(BF16), a2_ref[...]))
    g = _dot(_sigmoid(xg).astype(BF16), g2_ref[...])
    kk = k * kk_ref[...]
    ss = _dot_x3(kk * kk, bd_ref[...])
    kk = kk / jnp.maximum(jnp.sqrt(ss), 1e-12)
    k_mod = k * (1.0 + (a - 1.0) * ka_ref[...])
    for o_ref, val in ((r_o, r), (lw_o, -jnp.exp(w)), (k_o, k_mod), (kkn_o, kk), (a_o, a)):
        for hd in range(RWKV_HEADS):
            o_ref[0, hd] = val[:, hd * HEAD_DIM:(hd + 1) * HEAD_DIM]
    vt_o[0] = v.T.reshape(RWKV_HEADS, HEAD_DIM, tm)
    bv_o[...] = _dot_x3(r * k_mod * rk_ref[...], bd_ref[...]) * v
    g_o[...] = g


def _rwkv_pre(oc, b, t_len, mu, w0, w2, a0, a2, g2, k_k, k_a, r_k, bd, tm=256):
    n = oc.shape[0]
    tps = t_len // tm
    halo_blocks = tm // SUBLANES
    full = lambda a: pl.BlockSpec(a.shape, lambda i: (0, 0))
    tok = pl.BlockSpec((tm, D_RWKV), lambda i: (i, 0))
    hm = pl.BlockSpec((1, RWKV_HEADS, tm, HEAD_DIM), lambda i: (i // tps, 0, i % tps, 0))
    hm_t = pl.BlockSpec((1, RWKV_HEADS, HEAD_DIM, tm), lambda i: (i // tps, 0, 0, i % tps))
    params = (mu, w0, w2, a0, a2, g2, k_k, k_a, r_k, bd)
    hm_shape = jax.ShapeDtypeStruct((b, RWKV_HEADS, t_len, HEAD_DIM), F32)
    return pl.pallas_call(
        functools.partial(_rwkv_pre_kernel, tiles_per_seq=tps),
        grid=(n // tm,),
        in_specs=[
            pl.BlockSpec((tm, D_RWKV_IN), lambda i: (i, 0)),
            pl.BlockSpec((SUBLANES, D_RWKV_IN),
                         lambda i: (jnp.maximum(i * halo_blocks - 1, 0), 0)),
        ] + [full(a) for a in params],
        out_specs=[hm] * 5 + [hm_t, tok, tok],
        out_shape=[hm_shape] * 5
        + [jax.ShapeDtypeStruct((b, RWKV_HEADS, HEAD_DIM, t_len), F32)]
        + [jax.ShapeDtypeStruct((n, D_RWKV), F32)] * 2,
        compiler_params=_cparams(("parallel",)),
        name="rwkv_pre",
    )(oc, oc, *params)


RWKV_C = 64
RWKV_CB = 256
RWKV_PASSES = 1


def _bmm(eq, a, b):
    ein = lambda x, y: jnp.einsum(eq, x, y, preferred_element_type=F32)
    if RWKV_PASSES == 1:
        return ein(a.astype(BF16), b.astype(BF16))
    ah, al = _split2(a)
    bh, bl = _split2(b)
    return ein(ah, bh) + ein(ah, bl) + ein(al, bh)


def _cumsum_rows(x2d, seg):
    rows = lax.broadcasted_iota(jnp.int32, (x2d.shape[0], 1), 0) % seg
    step = 1
    while step < seg:
        shifted = pltpu.roll(x2d, step, axis=0)
        x2d = x2d + jnp.where(rows >= step, shifted, 0.0)
        step *= 2
    return x2d


def _rwkv_rec_kernel(r_ref, lw_ref, k_ref, kk_ref, a_ref, vt_ref, ot_ref, s_ref):
    H = r_ref.shape[1]
    C = RWKV_C

    @pl.when(pl.program_id(1) == 0)
    def _():
        s_ref[...] = jnp.zeros_like(s_ref)

    ri = lax.broadcasted_iota(jnp.int32, (C, C), 0)
    ci = lax.broadcasted_iota(jnp.int32, (C, C), 1)
    strict = (ri > ci)[None]
    incl = (ri >= ci)[None]
    eye = (ri == ci).astype(F32)[None]

    for sub in range(RWKV_CB // C):
        rows = slice(sub * C, (sub + 1) * C)
        r = r_ref[0, :, rows, :]
        lw = lw_ref[0, :, rows, :]
        k = k_ref[0, :, rows, :]
        kk = kk_ref[0, :, rows, :]
        a = a_ref[0, :, rows, :]
        vt = vt_ref[0, :, :, rows]
        s0 = s_ref[...]

        cum = _cumsum_rows(lw.reshape(H * C, HEAD_DIM), C).reshape(H, C, HEAD_DIM)
        cum_last = cum[:, C - 1:C, :]
        p_inv = jnp.exp(-cum)
        w_last = jnp.exp(cum_last - cum)
        bm = kk * a
        at = -kk * jnp.exp(cum - lw)
        rt = r * jnp.exp(cum)
        bt = bm * p_inv
        kt = k * p_inv

        nt = 'hik,hjk->hij'
        ar = jnp.concatenate([at, rt], axis=1)
        ar_b = _bmm(nt, ar, bt)
        ar_k = _bmm(nt, ar, kt)
        m_ab = jnp.where(strict, ar_b[:, :C], 0.0)
        n_rb = jnp.where(incl, ar_b[:, C:], 0.0)
        m_ak = jnp.where(strict, ar_k[:, :C], 0.0)
        n_rk = jnp.where(incl, ar_k[:, C:], 0.0)

        tinv = eye + m_ab
        mp = _bmm('hij,hjk->hik', m_ab, m_ab)
        n = 2
        while n < C:
            both = _bmm('hij,hjk->hik', jnp.concatenate([mp, tinv], axis=1), mp)
            mp = both[:, :C]
            tinv = tinv + both[:, C:]
            n *= 2

        rhs_t = _bmm('hvk,hik->hvi', s0, at) + _bmm('hvj,hij->hvi', vt, m_ak)
        ut = _bmm('hvj,hij->hvi', rhs_t, tinv)
        ot = (_bmm('hvk,hik->hvi', s0, rt) + _bmm('hvj,hij->hvi', ut, n_rb)
              + _bmm('hvj,hij->hvi', vt, n_rk))
        ot_ref[0, :, :, rows] = ot
        s_ref[...] = (s0 * jnp.exp(cum_last) + _bmm('hvj,hjk->hvk', ut, bm * w_last)
                      + _bmm('hvj,hjk->hvk', vt, k * w_last))


def _rwkv_recurrence(r, lw, k, kk, a, vt):
    b, h, t_len, d = r.shape
    tok = pl.BlockSpec((1, h, RWKV_CB, d), lambda bi, c: (bi, 0, c, 0))
    tr = pl.BlockSpec((1, h, d, RWKV_CB), lambda bi, c: (bi, 0, 0, c))
    return pl.pallas_call(
        _rwkv_rec_kernel,
        grid=(b, t_len // RWKV_CB),
        in_specs=[tok] * 5 + [tr],
        out_specs=tr,
        out_shape=jax.ShapeDtypeStruct((b, h, d, t_len), F32),
        scratch_shapes=[pltpu.VMEM((h, d, d), F32)],
        compiler_params=_cparams(("parallel", "arbitrary")),
        name="rwkv_recurrence",
    )(r, lw, k, kk, a, vt)


def _rwkv_post_kernel(ot_ref, bv_ref, g_ref, lnw_ref, lnb_ref, bd_ref, y_ref):
    tm = bv_ref.shape[0]
    o = ot_ref[0].reshape(D_RWKV, tm).T
    bd = bd_ref[...]
    inv = 1.0 / HEAD_DIM
    mean = _dot_x3(o, bd) * inv
    d = o - mean
    var = _dot_x3(d * d, bd) * inv
    on = d * lax.rsqrt(var + GN_EPS) * lnw_ref[...] + lnb_ref[...]
    y_ref[...] = ((on + bv_ref[...]) * g_ref[...]).astype(BF16)


def _rwkv_post(ot, bv, g, ln_w, ln_b, bd, tm=512):
    n = bv.shape[0]
    tps = ot.shape[-1] // tm
    tile = pl.BlockSpec((tm, D_RWKV), lambda i: (i, 0))
    hm_t = pl.BlockSpec((1, RWKV_HEADS, HEAD_DIM, tm), lambda i: (i // tps, 0, 0, i % tps))
    full = lambda a: pl.BlockSpec(a.shape, lambda i: (0, 0))
    return pl.pallas_call(
        _rwkv_post_kernel,
        grid=(n // tm,),
        in_specs=[hm_t, tile, tile, full(ln_w), full(ln_b), full(bd)],
        out_specs=tile,
        out_shape=jax.ShapeDtypeStruct((n, D_RWKV), BF16),
        compiler_params=_cparams(("parallel",)),
        name="rwkv_post",
    )(ot, bv, g, ln_w, ln_b, bd)


def _rwkv_mixer(oc, b, t_len, mu, w0, w2, a0, a2, g2, k_k, k_a, r_k, ln_w, ln_b):
    bd = _head_sum_matrix()
    row = lambda z: z.reshape(1, -1)
    r, lw, k, kk, a, vt, bv, g = _rwkv_pre(
        oc, b, t_len, row(mu), row(w0), w2.astype(BF16), row(a0), a2.astype(BF16),
        g2.astype(BF16), row(k_k), row(k_a), row(r_k), bd)
    ot = _rwkv_recurrence(r, lw, k, kk, a, vt)
    return _rwkv_post(ot, bv, g, row(ln_w), row(ln_b), bd)


def _out_proj_kernel(x_ref, yn_ref, yr_ref, wn_ref, wr_ref, g_ref, x1_ref, h2_ref):
    x1 = x_ref[...] + _dot(yn_ref[...], wn_ref[...]) + _dot(yr_ref[...], wr_ref[...])
    x1_ref[...] = x1
    h2_ref[...] = x1 * lax.rsqrt(jnp.mean(x1 * x1, axis=-1, keepdims=True) + EPS) * g_ref[...]


def _expand_w_out(w_out):
    wn = w_out[:D_NSA].reshape(NSA_HEADS, 1, HEAD_DIM, D_MODEL)
    onehot = (np.arange(NSA_HEADS)[:, None] // NSA_GROUP == np.arange(NSA_KV_HEADS)[None, :])
    wn = wn * jnp.asarray(onehot, F32)[:, :, None, None]
    return wn.reshape(Q_EXP, D_MODEL).astype(BF16), w_out[D_NSA:].astype(BF16)


def _out_proj(x2d, y_nsa, y_rwkv, wn, wr, g, tm=512):
    n = x2d.shape[0]
    tile = lambda a: pl.BlockSpec((tm, a.shape[1]), lambda i: (i, 0))
    full = lambda a: pl.BlockSpec(a.shape, lambda i: (0, 0))
    return pl.pallas_call(
        _out_proj_kernel,
        grid=(n // tm,),
        in_specs=[tile(x2d), tile(y_nsa), tile(y_rwkv), full(wn), full(wr), full(g)],
        out_specs=[tile(x2d), tile(x2d)],
        out_shape=[jax.ShapeDtypeStruct((n, D_MODEL), F32)] * 2,
        compiler_params=_cparams(("parallel",)),
        name="out_proj",
    )(x2d, y_nsa, y_rwkv, wn, wr, g)


def _topk_rows(s, k, rid):
    vals, ids = [], []
    for _ in range(k):
        m = jnp.max(s, axis=0, keepdims=True)
        first = jnp.min(jnp.where(s == m, rid, jnp.inf), axis=0, keepdims=True)
        vals.append(m)
        ids.append(first)
        s = jnp.where(rid == first, -jnp.inf, s)
    return jnp.concatenate(vals, axis=0), jnp.concatenate(ids, axis=0)


def _take_rows(table, idx):
    rows = lax.broadcasted_iota(jnp.int32, (table.shape[0], 1), 0).astype(F32)
    out = [jnp.sum(jnp.where(rows == idx[r:r + 1], table, 0.0), axis=0, keepdims=True)
           for r in range(idx.shape[0])]
    return jnp.concatenate(out, axis=0)


def _pair_candidates():
    k = PEER_TOPK
    flat = [0 * k + j for j in range(k)]
    for i in range(1, SUBLANES):
        flat += [i * k + j for j in range(SUBLANES)]
    flat += [i * k for i in range(SUBLANES, k)]
    return np.asarray(flat, np.float32).reshape(-1, 1)


def _peer_route_kernel(h_ref, wq_ref, sk_ref, flat_ref, e_ref, g_ref):
    K = PEER_TOPK
    half = PEER_D_QUERY // 2
    q = _dot(h_ref[...].astype(BF16), wq_ref[...]).astype(BF16)
    key_ids = lax.broadcasted_iota(jnp.int32, (PEER_N_KEYS, 1), 0).astype(F32)
    flat = flat_ref[...]
    rows_e, rows_g = [], []
    for h in range(PEER_HEADS):
        top = []
        for c in range(2):
            col = (h * 2 + c) * half
            s_t = _dot_nt(sk_ref[h, c], q[:, col:col + half])
            top.append(_topk_rows(s_t, K, key_ids))
        (v0, i0), (v1, i1) = top
        cand = jnp.concatenate(
            [v0[0:1] + v1]
            + [v0[i:i + 1] + v1[:SUBLANES] for i in range(1, SUBLANES)]
            + [v0[SUBLANES:] + v1[0:1]], axis=0)
        best, pair = _topk_rows(cand, K, flat)
        pi = jnp.floor(pair * (1.0 / K))
        pj = pair - pi * K
        experts = _take_rows(i0, pi) * float(PEER_N_KEYS) + _take_rows(i1, pj)
        p = jnp.exp(best - jnp.max(best, axis=0, keepdims=True))
        rows_e.append(experts * float(HALF_ROWS))
        rows_g.append(p / jnp.sum(p, axis=0, keepdims=True))
    e_ref[...] = jnp.concatenate(rows_e, axis=0).T.astype(jnp.int32)
    g_ref[...] = jnp.concatenate(rows_g, axis=0).T


def _peer_route(h2, wq, sk, tm=256):
    n = h2.shape[0]
    hk = PEER_HEADS * PEER_TOPK
    flat = jnp.asarray(_pair_candidates())
    out = pl.BlockSpec((tm, hk), lambda i: (i, 0))
    return pl.pallas_call(
        _peer_route_kernel,
        grid=(n // tm,),
        in_specs=[
            pl.BlockSpec((tm, D_MODEL), lambda i: (i, 0)),
            pl.BlockSpec(wq.shape, lambda i: (0, 0)),
            pl.BlockSpec(sk.shape, lambda i: (0, 0, 0, 0)),
            pl.BlockSpec(flat.shape, lambda i: (0, 0)),
        ],
        out_specs=[out, out],
        out_shape=[jax.ShapeDtypeStruct((n, hk), jnp.int32), jax.ShapeDtypeStruct((n, hk), F32)],
        compiler_params=_cparams(("parallel",)),
        name="peer_route",
    )(h2, wq, sk, flat)


PEER_TT = 128
PEER_UNROLL = 4
HALF_ROWS = SUBLANES // 2
SUB_ORDER = (0, 4, 2, 6, 1, 5, 3, 7)
HIGH_MASK = 0xFFFF0000


def _pack_table(w):
    bits = lax.bitcast_convert_type(w.astype(BF16), jnp.uint16).astype(jnp.uint32)
    half = w.shape[1] // 2
    packed = bits[:, :half] | (bits[:, half:] << 16)
    return packed.reshape(w.shape[0] * HALF_ROWS, LANES)


def _table_row(tbl_ref, row0):
    return tbl_ref[pl.ds(pl.multiple_of(row0, HALF_ROWS), HALF_ROWS), :]


def _unpack_words(word):
    lo = pltpu.bitcast(word << 16, F32)
    hi = pltpu.bitcast(word & jnp.uint32(HIGH_MASK), F32)
    return lo, hi


def _sublane_tree(c):
    sub = lax.broadcasted_iota(jnp.int32, (SUBLANES, LANES), 0)
    m2 = (sub % 4) < 2
    d = []
    for x, y in zip(c[0::2], c[1::2]):
        d.append(jnp.where(m2, x + pltpu.roll(x, 6, axis=0), y + pltpu.roll(y, 2, axis=0)))
    m1 = (sub % 2) == 0
    x, y = d
    return jnp.where(m1, x + pltpu.roll(x, 7, axis=0), y + pltpu.roll(y, 1, axis=0))


def _peer_act_kernel(e_ref, h_ref, tbl_ref, gate_ref, ones_ref, w_ref, part_ref):
    hk = PEER_HEADS * PEER_TOPK
    tt = h_ref.shape[0]

    def token(t):
        hrow = h_ref[t]
        h_lo = jnp.concatenate([hrow[:HALF_ROWS]] * 2, axis=0)
        h_hi = jnp.concatenate([hrow[HALF_ROWS:]] * 2, axis=0)
        for m in range(hk // SUBLANES):
            es = [e_ref[t, m * SUBLANES + j] for j in SUB_ORDER]
            c = []
            for ea, eb in zip(es[0::2], es[1::2]):
                lo, hi = _unpack_words(jnp.concatenate(
                    [_table_row(tbl_ref, ea), _table_row(tbl_ref, eb)], axis=0))
                c.append(lo * h_lo + hi * h_hi)
            row0 = pl.multiple_of((t * (hk // SUBLANES) + m) * SUBLANES, SUBLANES)
            part_ref[pl.ds(row0, SUBLANES), :] = _sublane_tree(c)

    def token_group(i, carry):
        for u in range(PEER_UNROLL):
            token(PEER_UNROLL * i + u)
        return carry

    lax.fori_loop(0, tt // PEER_UNROLL, token_group, 0)
    half = (tt // 2) * hk
    hi, lo = _split2(part_ref[0:half, :])
    sums_a = _dot(jnp.concatenate([hi, lo], axis=1), ones_ref[...])
    sums_b = jnp.sum(part_ref[half:, :], axis=-1, keepdims=True)
    sums_b = jnp.broadcast_to(sums_b, (tt * hk - half, LANES))
    sums = jnp.concatenate([sums_a, sums_b], axis=0).reshape(tt, hk, LANES)
    eye = (lax.broadcasted_iota(jnp.int32, (hk, LANES), 0)
           == lax.broadcasted_iota(jnp.int32, (hk, LANES), 1))
    act = jnp.sum(jnp.where(eye[None], sums, 0.0), axis=1)
    w_ref[...] = _gelu(act) * gate_ref[...]


def _table_spec(tbl):
    return pl.BlockSpec(tbl.shape, lambda i: (0, 0), pipeline_mode=pl.Buffered(1))


def _peer_act(experts, h2_tiles, tbl, gates):
    n, hk = experts.shape
    ones = jnp.ones((2 * LANES, LANES), BF16)
    return pl.pallas_call(
        _peer_act_kernel,
        grid=(n // PEER_TT,),
        in_specs=[
            pl.BlockSpec((PEER_TT, hk), lambda i: (i, 0), memory_space=pltpu.SMEM),
            pl.BlockSpec((PEER_TT, SUBLANES, LANES), lambda i: (i, 0, 0)),
            _table_spec(tbl),
            pl.BlockSpec((PEER_TT, hk), lambda i: (i, 0)),
            pl.BlockSpec(ones.shape, lambda i: (0, 0)),
        ],
        out_specs=pl.BlockSpec((PEER_TT, hk), lambda i: (i, 0)),
        out_shape=jax.ShapeDtypeStruct((n, hk), F32),
        scratch_shapes=[pltpu.VMEM((PEER_TT * hk, LANES), F32)],
        compiler_params=_cparams(("arbitrary",)),
        name="peer_act",
    )(experts, h2_tiles, tbl, gates, ones)


PEER_ACCS = 4


def _peer_out_kernel(e_ref, w_ref, x_ref, tbl_ref, g_ref, ones_ref, o_ref, wrep_ref, *,
                     final_norm):
    hk = PEER_HEADS * PEER_TOPK
    tt = x_ref.shape[0]

    eye = (lax.broadcasted_iota(jnp.int32, (hk, LANES), 0)
           == lax.broadcasted_iota(jnp.int32, (hk, LANES), 1))
    diag = jnp.where(eye[None], w_ref[...][:, None, :], 0.0).reshape(tt * hk, LANES)
    wrep_ref[...] = _dot(diag.astype(BF16), ones_ref[...])

    def token(t):
        acc_lo = [jnp.zeros((HALF_ROWS, LANES), F32) for _ in range(PEER_ACCS)]
        acc_hi = [jnp.zeros((HALF_ROWS, LANES), F32) for _ in range(PEER_ACCS)]
        base = t * hk
        for j in range(hk):
            lo, hi = _unpack_words(_table_row(tbl_ref, e_ref[base + j]))
            wk = wrep_ref[pl.ds(base + j, 1), :]
            acc_lo[j % PEER_ACCS] = acc_lo[j % PEER_ACCS] + wk * lo
            acc_hi[j % PEER_ACCS] = acc_hi[j % PEER_ACCS] + wk * hi
        tree = lambda a: (a[0] + a[1]) + (a[2] + a[3])
        o_ref[t] = x_ref[t] + jnp.concatenate([tree(acc_lo), tree(acc_hi)], axis=0)

    def token_group(i, carry):
        for u in range(PEER_UNROLL):
            token(PEER_UNROLL * i + u)
        return carry

    lax.fori_loop(0, tt // PEER_UNROLL, token_group, 0)
    if final_norm:
        x2 = o_ref[...]
        ms = jnp.sum(jnp.sum(x2 * x2, axis=2, keepdims=True), axis=1, keepdims=True) / D_MODEL
        o_ref[...] = x2 * lax.rsqrt(ms + EPS) * g_ref[...]


def _peer_out(experts, w, x1_tiles, tbl, final_g, final_norm):
    n, hk = experts.shape
    ones = jnp.ones((LANES, LANES), BF16)
    smem = pl.BlockSpec((PEER_TT * hk,), lambda i: (i,), memory_space=pltpu.SMEM)
    tile = pl.BlockSpec((PEER_TT, SUBLANES, LANES), lambda i: (i, 0, 0))
    return pl.pallas_call(
        functools.partial(_peer_out_kernel, final_norm=final_norm),
        grid=(n // PEER_TT,),
        in_specs=[smem, pl.BlockSpec((PEER_TT, hk), lambda i: (i, 0)), tile, _table_spec(tbl),
                  pl.BlockSpec(final_g.shape, lambda i: (0, 0, 0)),
                  pl.BlockSpec(ones.shape, lambda i: (0, 0))],
        out_specs=tile,
        out_shape=jax.ShapeDtypeStruct(x1_tiles.shape, F32),
        scratch_shapes=[pltpu.VMEM((PEER_TT * hk, LANES), F32)],
        compiler_params=_cparams(("arbitrary",)),
        name="peer_out",
    )(experts.reshape(-1), w, x1_tiles, tbl, final_g, ones)


def _peer_ffn_residual(x1, h2, wq, sk, tbl_u, tbl_v, final_g, final_norm):
    n = x1.shape[0]
    experts, gates = _peer_route(h2, wq, sk)
    tiles = lambda z: z.reshape(n, SUBLANES, LANES)
    w = _peer_act(experts, tiles(h2), tbl_u, gates)
    g_tile = final_g.reshape(1, SUBLANES, LANES)
    return _peer_out(experts, w, tiles(x1), tbl_v, g_tile, final_norm).reshape(n, D_MODEL)


def kernel(x, norm_mix_g, w_in, cmp_k_pe, cmp_k_w1, cmp_k_w2, cmp_v_pe, cmp_v_w1, cmp_v_w2,
           rwkv_mu, rwkv_w0, rwkv_w2, rwkv_a0, rwkv_a2, rwkv_g2, rwkv_k_k, rwkv_k_a, rwkv_r_k,
           rwkv_ln_w, rwkv_ln_b, w_out, norm_ffn_g, peer_w_q, peer_sub_keys, peer_u, peer_v,
           norm_final_g):
    b, t_len, _ = x.shape
    n = b * t_len
    row = lambda z: z.reshape(1, -1)
    x2d = x.reshape(n, D_MODEL)
    depth = w_in.shape[0]
    for l in range(depth):
        oa, ob, oc = _in_proj(x2d, row(norm_mix_g[l]), _pad_w_in(w_in[l]), t_len)
        nb = t_len // CMP_STRIDE
        rk = ob[:, 0:D_KV].reshape(b, nb, CMP_STRIDE * D_KV)
        rv = ob[:, D_KV:2 * D_KV].reshape(b, nb, CMP_STRIDE * D_KV)
        cmp_k, cmp_vt = _compress(
            rk, rv, _expand_cmp_weights(cmp_k_pe[l], cmp_k_w1[l], cmp_k_w2[l], LANES),
            _expand_cmp_weights(cmp_v_pe[l], cmp_v_w1[l], cmp_v_w2[l], HEAD_DIM))
        v_col0 = Q_EXP + 2 * K_AUG
        v_t = lambda c: oa[:, v_col0 + c * D_KV:v_col0 + (c + 1) * D_KV].reshape(
            b, t_len, D_KV).transpose(0, 2, 1)
        y_nsa = _nsa_attention(oa, ob, v_t(0), v_t(1), cmp_k, cmp_vt, b, t_len)
        y_rwkv = _rwkv_mixer(oc, b, t_len, rwkv_mu[l], rwkv_w0[l], rwkv_w2[l], rwkv_a0[l],
                             rwkv_a2[l], rwkv_g2[l], rwkv_k_k[l], rwkv_k_a[l], rwkv_r_k[l],
                             rwkv_ln_w[l], rwkv_ln_b[l])
        wn, wr = _expand_w_out(w_out[l])
        x1, h2 = _out_proj(x2d, y_nsa, y_rwkv, wn, wr, row(norm_ffn_g[l]))
        x2d = _peer_ffn_residual(x1, h2, peer_w_q[l].astype(BF16), peer_sub_keys[l].astype(BF16),
                                 _pack_table(peer_u[l]), _pack_table(peer_v[l]),
                                 norm_final_g, final_norm=(l == depth - 1))
    return x2d.reshape(b, t_len, D_MODEL)
```

```python
import functools

import numpy as np
import jax
import jax.numpy as jnp
from jax import lax
from jax.experimental import pallas as pl
from jax.experimental.pallas import tpu as pltpu

F32 = jnp.float32
BF16 = jnp.bfloat16

D_MODEL = 1024
HEAD_DIM = 64
NSA_HEADS = 8
NSA_KV_HEADS = 2
NSA_GROUP = NSA_HEADS // NSA_KV_HEADS
RWKV_HEADS = 8
D_NSA = NSA_HEADS * HEAD_DIM
D_RWKV = RWKV_HEADS * HEAD_DIM
D_KV = NSA_KV_HEADS * HEAD_DIM
CMP_LEN = 32
CMP_STRIDE = 16
CMP_HIDDEN = 128
SEL_LEN = 64
SEL_TOPN = 16
WINDOW = 512
RANK_W = 64
RANK_A = 64
RANK_G = 128
D_RWKV_IN = 3 * D_RWKV + RANK_W + RANK_A + RANK_G
PEER_HEADS = 8
PEER_N_KEYS = 128
PEER_D_QUERY = 256
PEER_TOPK = 16
EPS = 1e-6
GN_EPS = 64e-5

LANES = 128
SUBLANES = 8
VMEM_LIMIT = 56 * 1024 * 1024

NEG_BIG = -1e30
Q_EXP = NSA_HEADS * LANES
K_AUG = NSA_KV_HEADS * LANES
NA_COLS = Q_EXP + 2 * K_AUG + 2 * D_KV
NB_COLS = 3 * LANES
FEAT0 = HEAD_DIM
POS_SPLIT = 64


def _cparams(sem):
    return pltpu.CompilerParams(dimension_semantics=sem, vmem_limit_bytes=VMEM_LIMIT)


def _dot(a, b):
    return jnp.dot(a, b, preferred_element_type=F32)


def _dot_nt(a, b):
    return lax.dot_general(a, b, (((1,), (1,)), ((), ())), preferred_element_type=F32)


def _split2(x):
    hi = x.astype(BF16)
    lo = (x - hi.astype(F32)).astype(BF16)
    return hi, lo


def _dot_x2(x, e):
    hi, lo = _split2(x)
    return _dot(hi, e) + _dot(lo, e)


def _gelu(x):
    return 0.5 * x * (1.0 + jnp.tanh(0.7978845608028654 * (x + 0.044715 * (x * x * x))))


def _sigmoid(x):
    return 1.0 / (1.0 + jnp.exp(-x))


def _key_features(pos, lane):
    hi = (pos // POS_SPLIT).astype(F32)
    lo = (pos % POS_SPLIT).astype(F32)
    return jnp.where(lane == FEAT0, hi, jnp.where(lane == FEAT0 + 1, lo,
                     jnp.where((lane == FEAT0 + 2) | (lane == FEAT0 + 3), 1.0, 0.0)))


def _query_features(t, lane, slope):
    hi = (t // POS_SPLIT).astype(F32)
    lo = (t % POS_SPLIT).astype(F32)
    return jnp.where(lane == FEAT0, slope * POS_SPLIT, jnp.where(
        lane == FEAT0 + 1, slope, jnp.where(
            lane == FEAT0 + 2, -slope * POS_SPLIT * hi, jnp.where(
                lane == FEAT0 + 3, -slope * lo, 0.0))))


def _in_proj_kernel(x_ref, g_ref, w_ref, oa_ref, ob_ref, oc_ref, *, t_len):
    x = x_ref[...]
    tm = x.shape[0]
    h = x * lax.rsqrt(jnp.mean(x * x, axis=-1, keepdims=True) + EPS) * g_ref[...]
    hb = h.astype(BF16)
    t = (pl.program_id(0) * tm + lax.broadcasted_iota(jnp.int32, (tm, 1), 0)) % t_len
    col = lax.broadcasted_iota(jnp.int32, (1, Q_EXP), 1)
    slope = jnp.zeros((1, Q_EXP), F32)
    for hd in range(NSA_HEADS):
        slope = jnp.where(col // LANES == hd, 2.0 ** -(hd + 1), slope)
    q = _dot(hb, w_ref[:, :Q_EXP]) * (HEAD_DIM ** -0.5) + _query_features(t, col % LANES, slope)
    oa_ref[:, :Q_EXP] = q.astype(BF16)
    colk = lax.broadcasted_iota(jnp.int32, (1, 2 * K_AUG), 1)
    k = _dot(hb, w_ref[:, Q_EXP:Q_EXP + 2 * K_AUG]) + _key_features(t, colk % LANES)
    oa_ref[:, Q_EXP:Q_EXP + 2 * K_AUG] = k.astype(BF16)
    oa_ref[:, Q_EXP + 2 * K_AUG:] = _dot(hb, w_ref[:, Q_EXP + 2 * K_AUG:NA_COLS]).astype(BF16)
    ob_ref[...] = _dot(hb, w_ref[:, NA_COLS:NA_COLS + NB_COLS])
    oc_ref[...] = _dot(hb, w_ref[:, NA_COLS + NB_COLS:])


def _in_proj(x2d, g, w_pad, t_len, tm=256):
    n = x2d.shape[0]
    ncols = w_pad.shape[1]
    nc = ncols - NA_COLS - NB_COLS
    return pl.pallas_call(
        functools.partial(_in_proj_kernel, t_len=t_len),
        grid=(n // tm,),
        in_specs=[
            pl.BlockSpec((tm, D_MODEL), lambda i: (i, 0)),
            pl.BlockSpec((1, D_MODEL), lambda i: (0, 0)),
            pl.BlockSpec((D_MODEL, ncols), lambda i: (0, 0)),
        ],
        out_specs=[
            pl.BlockSpec((tm, NA_COLS), lambda i: (i, 0)),
            pl.BlockSpec((tm, NB_COLS), lambda i: (i, 0)),
            pl.BlockSpec((tm, nc), lambda i: (i, 0)),
        ],
        out_shape=[
            jax.ShapeDtypeStruct((n, NA_COLS), BF16),
            jax.ShapeDtypeStruct((n, NB_COLS), F32),
            jax.ShapeDtypeStruct((n, nc), F32),
        ],
        compiler_params=_cparams(("parallel",)),
        name="in_proj",
    )(x2d, g, w_pad)


def _pad_w_in(w_in):
    sizes = (D_NSA, D_KV, D_KV, D_KV, D_KV, D_KV, D_KV, 3 * NSA_HEADS, D_RWKV_IN)
    offs = np.cumsum((0,) + sizes)
    q, kc, vc, ks, vs, kw, vw, gl, rw = (w_in[:, offs[i]:offs[i + 1]] for i in range(9))
    def lane_pad(w, groups):
        w = w.reshape(D_MODEL, groups, HEAD_DIM)
        return jnp.pad(w, ((0, 0), (0, 0), (0, LANES - HEAD_DIM))).reshape(D_MODEL, groups * LANES)

    glp = jnp.pad(gl, ((0, 0), (0, LANES - 3 * NSA_HEADS)))
    return jnp.concatenate(
        [lane_pad(q, NSA_HEADS), lane_pad(ks, NSA_KV_HEADS), lane_pad(kw, NSA_KV_HEADS),
         vs, vw, kc, vc, glp, rw], axis=1).astype(BF16)


def _compress_kernel(rk_ref, rv_ref, pek_ref, pev_ref, w1k_ref, w1v_ref, w2k_ref, w2v_ref,
                     ok_ref, ov_ref):
    for r_ref, pe_ref, w1_ref, w2_ref, o_ref in (
            (rk_ref, pek_ref, w1k_ref, w2k_ref, ok_ref),
            (rv_ref, pev_ref, w1v_ref, w2v_ref, ov_ref)):
        rows = r_ref[0]
        nxt = pltpu.roll(rows, rows.shape[0] - 1, axis=0)
        a = (rows + pe_ref[0:1, :]).astype(BF16)
        b = (nxt + pe_ref[1:2, :]).astype(BF16)
        hid = _dot(a, w1_ref[0]) + _dot(b, w1_ref[1])
        out = _dot(_gelu(hid).astype(BF16), w2_ref[...])
        if o_ref is ok_ref:
            nb = rows.shape[0]
            end = lax.broadcasted_iota(jnp.int32, (nb, 1), 0) * CMP_STRIDE + (CMP_LEN - 1)
            lane = lax.broadcasted_iota(jnp.int32, (1, K_AUG), 1) % LANES
            out = out + _key_features(end, lane)
        else:
            out = out.T
        o_ref[0] = out.astype(BF16)


def _expand_cmp_weights(pe, w1, w2, out_lanes):
    half = CMP_LEN // 2
    eye = jnp.eye(NSA_KV_HEADS, dtype=F32)
    w1r = w1.reshape(2, half, HEAD_DIM, CMP_HIDDEN)
    w1e = w1r[:, :, None, :, None, :] * eye[None, None, :, None, :, None]
    w1e = w1e.reshape(2, half * D_KV, NSA_KV_HEADS * CMP_HIDDEN).astype(BF16)
    pee = jnp.broadcast_to(pe.reshape(2, half, 1, HEAD_DIM), (2, half, NSA_KV_HEADS, HEAD_DIM))
    pee = pee.reshape(2, half * D_KV)
    w2p = jnp.pad(w2, ((0, 0), (0, out_lanes - HEAD_DIM)))
    w2e = (eye[:, None, :, None] * w2p[None, :, None, :]).reshape(
        NSA_KV_HEADS * CMP_HIDDEN, NSA_KV_HEADS * out_lanes).astype(BF16)
    return pee, w1e, w2e


def _compress(rk, rv, wk, wv):
    b, nb, width = rk.shape
    full2 = lambda a: pl.BlockSpec(a.shape, lambda i: (0, 0))
    full3 = lambda a: pl.BlockSpec(a.shape, lambda i: (0, 0, 0))
    row = pl.BlockSpec((1, nb, width), lambda i: (i, 0, 0))
    out = lambda rows, lanes: pl.BlockSpec((1, rows, lanes), lambda i: (i, 0, 0))
    return pl.pallas_call(
        _compress_kernel,
        grid=(b,),
        in_specs=[row, row, full2(wk[0]), full2(wv[0]), full3(wk[1]), full3(wv[1]),
                  full2(wk[2]), full2(wv[2])],
        out_specs=[out(nb, K_AUG), out(D_KV, nb)],
        out_shape=[jax.ShapeDtypeStruct((b, nb, K_AUG), BF16),
                   jax.ShapeDtypeStruct((b, D_KV, nb), BF16)],
        compiler_params=_cparams(("parallel",)),
        name="nsa_compress",
    )(rk, rv, wk[0], wv[0], wk[1], wv[1], wk[2], wv[2])


NSA_TQ = 256
NSA_TK = 1024


def _masked_softmax(s, mask):
    s = jnp.where(mask, s, NEG_BIG)
    s = s - jnp.max(s, axis=-1, keepdims=True)
    p = jnp.where(mask, jnp.exp(s), 0.0)
    return p / jnp.maximum(jnp.sum(p, axis=-1, keepdims=True), 1e-30)


def _flash_update(carry, s, v_t):
    m, l, acc = carry
    m_new = jnp.maximum(m, jnp.max(s, axis=0, keepdims=True))
    alpha = jnp.exp(m - m_new)
    p = jnp.exp(s - m_new)
    l = alpha * l + jnp.sum(p, axis=0, keepdims=True)
    acc = alpha * acc + _dot(v_t, p.astype(BF16))
    return m_new, l, acc


def _nsa_kernel(q_ref, ks_ref, kw_ref, vst_ref, vwt_ref, ck_ref, cvt_ref, ovl_ref, oh_ref, gl_ref,
                o_ref):
    tq = NSA_TQ
    R = NSA_GROUP
    i = pl.program_id(1)
    q0 = i * tq
    t_lane = q0 + lax.broadcasted_iota(jnp.int32, (1, tq), 1)
    t_lanes = jnp.concatenate([t_lane] * R, axis=1)
    n_cmp = ck_ref.shape[1]
    n_sel = ovl_ref.shape[0]

    gate_t = _sigmoid(gl_ref[...]).T

    cmp_end = lax.broadcasted_iota(jnp.int32, (n_cmp, 1), 0) * CMP_STRIDE + (CMP_LEN - 1)
    mask_c = t_lanes >= cmp_end
    ids = lax.broadcasted_iota(jnp.int32, (n_sel, 1), 0)
    idsf = ids.astype(F32)
    cur = t_lane // SEL_LEN
    forced = (ids == 0) | (ids == cur) | (ids == cur - 1)
    valid = ids * SEL_LEN <= t_lane

    G = NSA_KV_HEADS
    lanes_of = [slice(g * LANES, (g + 1) * LANES) for g in range(G)]
    q_st, q_slc, o_cmp = [], [], []
    for g in range(G):
        qh = [q_ref[:, h * LANES:(h + 1) * LANES] for h in range(g * R, (g + 1) * R)]
        q_st.append(jnp.concatenate(qh, axis=0))

        s = jnp.where(mask_c, _dot_nt(ck_ref[0, :, lanes_of[g]], q_st[g]), NEG_BIG)
        s = s - jnp.max(s, axis=0, keepdims=True)
        p = jnp.where(mask_c, jnp.exp(s), 0.0)
        p = p / jnp.maximum(jnp.sum(p, axis=0, keepdims=True), 1e-30)
        o_cmp.append(_dot(cvt_ref[0], p.astype(BF16)))
        p_hi, p_lo = _split2(p)
        imp4 = _dot(ovl_ref[...], p_hi) + _dot(ovl_ref[...], p_lo)
        imp = imp4[:, 0:tq]
        for r in range(1, R):
            imp = imp + imp4[:, r * tq:(r + 1) * tq]
        imp = jnp.where(forced, 1e6, jnp.where(valid, imp, -1.0))

        sel = jnp.zeros((n_sel, tq), F32)
        for _ in range(min(SEL_TOPN, n_sel)):
            m = jnp.max(imp, axis=0, keepdims=True)
            first = jnp.min(jnp.where(imp == m, idsf, float(n_sel)), axis=0, keepdims=True)
            hit = idsf == first
            sel = jnp.where(hit, 1.0, sel)
            imp = jnp.where(hit, -3e38, imp)
        bias_t = jnp.where(sel > 0.5, 0.0, NEG_BIG)
        if n_sel < LANES:
            bias_t = jnp.concatenate([bias_t, jnp.zeros((LANES - n_sel, tq), F32)], axis=0)
        bias = bias_t.T.astype(BF16)

        q_slc.append(jnp.concatenate([q_st[g], jnp.concatenate([bias] * R, axis=0)], axis=1))

    init = tuple((jnp.full((1, R * tq), -3e38, F32), jnp.zeros((1, R * tq), F32),
                  jnp.zeros((LANES, R * tq), F32)) for _ in range(G))

    def slc_tile(j, carry, diagonal):
        k0 = pl.multiple_of(j * NSA_TK, NSA_TK)
        onehot = oh_ref[pl.ds(k0, NSA_TK), :]
        v_t = vst_ref[0, :, pl.ds(k0, NSA_TK)]
        out = []
        for g in range(G):
            k = jnp.concatenate([ks_ref[pl.ds(k0, NSA_TK), lanes_of[g]], onehot], axis=1)
            s = _dot_nt(k, q_slc[g])
            if diagonal:
                pos = k0 + lax.broadcasted_iota(jnp.int32, (NSA_TK, 1), 0)
                s = jnp.where(t_lanes >= pos, s, NEG_BIG)
            out.append(_flash_update(carry[g], s, v_t))
        return tuple(out)

    n_off = q0 // NSA_TK
    carry = lax.fori_loop(0, n_off, functools.partial(slc_tile, diagonal=False), init)
    slc = slc_tile(n_off, carry, True)

    span = WINDOW + tq
    w0 = pl.multiple_of(jnp.maximum(q0 - WINDOW, 0), tq)
    pos = w0 + lax.broadcasted_iota(jnp.int32, (span, 1), 0)
    visible = (t_lanes >= pos) & (pos > t_lanes - WINDOW)
    v_t = vwt_ref[0, :, pl.ds(w0, span)]
    win = [_flash_update(init[g], jnp.where(
        visible, _dot_nt(kw_ref[pl.ds(w0, span), lanes_of[g]], q_st[g]), NEG_BIG), v_t)
        for g in range(G)]

    for h in range(NSA_HEADS):
        g, r = divmod(h, R)
        rows = slice(r * tq, (r + 1) * tq)
        gate = lambda j: gate_t[h * 3 + j:h * 3 + j + 1, :]
        out_t = (gate(0) * o_cmp[g][:, rows]
                 + gate(1) * (slc[g][2][:, rows] / slc[g][1][:, rows])
                 + gate(2) * (win[g][2][:, rows] / win[g][1][:, rows]))
        o_ref[:, h * LANES:(h + 1) * LANES] = out_t.T.astype(BF16)


def _nsa_consts(t_len, n_cmp_pad):
    n_cmp = (t_len - CMP_LEN) // CMP_STRIDE + 1
    n_sel = t_len // SEL_LEN
    cmp_start = np.arange(n_cmp) * CMP_STRIDE
    sel_start = np.arange(n_sel) * SEL_LEN
    ovl = np.clip(np.minimum(cmp_start[:, None] + CMP_LEN, sel_start[None, :] + SEL_LEN)
                  - np.maximum(cmp_start[:, None], sel_start[None, :]), 0, None) / CMP_LEN
    ovl_t = np.zeros((n_sel, n_cmp_pad), np.float32)
    ovl_t[:, :n_cmp] = ovl.T
    onehot = np.zeros((t_len, LANES), np.float32)
    onehot[np.arange(t_len), np.arange(t_len) // SEL_LEN] = 1.0
    return jnp.asarray(ovl_t, BF16), jnp.asarray(onehot, BF16)


def _nsa_attention(oa, ob, vs_t, vw_t, cmp_k, cmp_vt, b, t_len):
    assert NSA_TK % NSA_TQ == 0 and t_len % NSA_TK == 0 and WINDOW % NSA_TQ == 0
    assert t_len >= WINDOW + NSA_TQ and t_len // SEL_LEN <= LANES
    n_cmp_pad = cmp_k.shape[1]
    ovl_t, onehot = _nsa_consts(t_len, n_cmp_pad)
    nq = t_len // NSA_TQ
    k_col0 = Q_EXP // K_AUG
    kspec = lambda c: pl.BlockSpec((t_len, K_AUG), lambda bi, i: (bi, k_col0 + c))
    vspec = pl.BlockSpec((1, D_KV, t_len), lambda bi, i: (bi, 0, 0))
    const2 = lambda a: pl.BlockSpec(a.shape, lambda bi, i: (0, 0))
    return pl.pallas_call(
        _nsa_kernel,
        grid=(b, nq),
        in_specs=[
            pl.BlockSpec((NSA_TQ, Q_EXP), lambda bi, i: (bi * nq + i, 0)),
            kspec(0), kspec(1), vspec, vspec,
            pl.BlockSpec((1, n_cmp_pad, K_AUG), lambda bi, i: (bi, 0, 0)),
            pl.BlockSpec((1, D_KV, n_cmp_pad), lambda bi, i: (bi, 0, 0)),
            const2(ovl_t), const2(onehot),
            pl.BlockSpec((NSA_TQ, LANES), lambda bi, i: (bi * nq + i, 2)),
        ],
        out_specs=pl.BlockSpec((NSA_TQ, Q_EXP), lambda bi, i: (bi * nq + i, 0)),
        out_shape=jax.ShapeDtypeStruct((b * t_len, Q_EXP), BF16),
        compiler_params=_cparams(("parallel", "arbitrary")),
        name="nsa_attention",
    )(oa, oa, oa, vs_t, vw_t, cmp_k, cmp_vt, ovl_t, onehot, ob)


def _dot_x3(x, e):
    hi = x.astype(BF16)
    r1 = x - hi.astype(F32)
    mid = r1.astype(BF16)
    lo = (r1 - mid.astype(F32)).astype(BF16)
    return _dot(hi, e) + _dot(mid, e) + _dot(lo, e)


def _head_sum_matrix():
    ids = np.arange(D_RWKV) // HEAD_DIM
    return jnp.asarray(ids[:, None] == ids[None, :], BF16)


def _rwkv_pre_kernel(p_ref, hp_ref, mu_ref, w0_ref, w2_ref, a0_ref, a2_ref, g2_ref, kk_ref,
                     ka_ref, rk_ref, bd_ref, r_o, lw_o, k_o, kkn_o, a_o, vt_o, bv_o, g_o, *,
                     tiles_per_seq):
    i = pl.program_id(0)
    p = p_ref[...]
    tm = p.shape[0]
    keep = jnp.where(i % tiles_per_seq == 0, 0.0, 1.0)
    halo = hp_ref[SUBLANES - 1:SUBLANES, :] * keep
    prev = pltpu.roll(p, 1, axis=0)
    row0 = lax.broadcasted_iota(jnp.int32, (tm, 1), 0) == 0
    prev = jnp.where(row0, halo, prev)
    ps = p + (prev - p) * mu_ref[...]
    d = D_RWKV
    r = ps[:, 0:d]
    k = ps[:, d:2 * d]
    v = ps[:, 2 * d:3 * d]
    xw = ps[:, 3 * d:3 * d + RANK_W]
    xa = ps[:, 3 * d + RANK_W:3 * d + RANK_W + RANK_A]
    xg = ps[:, 3 * d + RANK_W + RANK_A:]
    z = -(w0_ref[...] + _dot(jnp.tanh(xw).astype(BF16), w2_ref[...]))
    softplus = jnp.maximum(z, 0.0) + jnp.log(1.0 + jnp.exp(-jnp.abs(z)))
    w = -softplus - 0.5
    a = _sigmoid(a0_ref[...] + _dot(xa.astype(BF16), a2_ref[...]))
    g = _dot(_sigmoid(xg).astype(BF16), g2_ref[...])
    kk = k * kk_ref[...]
    ss = _dot_x3(kk * kk, bd_ref[...])
    kk = kk / jnp.maximum(jnp.sqrt(ss), 1e-12)
    k_mod = k * (1.0 + (a - 1.0) * ka_ref[...])
    for o_ref, val in ((r_o, r), (lw_o, -jnp.exp(w)), (k_o, k_mod), (kkn_o, kk), (a_o, a)):
        for hd in range(RWKV_HEADS):
            o_ref[0, hd] = val[:, hd * HEAD_DIM:(hd + 1) * HEAD_DIM]
    vt_o[0] = v.T.reshape(RWKV_HEADS, HEAD_DIM, tm)
    bv_o[...] = _dot_x3(r * k_mod * rk_ref[...], bd_ref[...]) * v
    g_o[...] = g


def _rwkv_pre(oc, b, t_len, mu, w0, w2, a0, a2, g2, k_k, k_a, r_k, bd, tm=256):
    n = oc.shape[0]
    tps = t_len // tm
    halo_blocks = tm // SUBLANES
    full = lambda a: pl.BlockSpec(a.shape, lambda i: (0, 0))
    tok = pl.BlockSpec((tm, D_RWKV), lambda i: (i, 0))
    hm = pl.BlockSpec((1, RWKV_HEADS, tm, HEAD_DIM), lambda i: (i // tps, 0, i % tps, 0))
    hm_t = pl.BlockSpec((1, RWKV_HEADS, HEAD_DIM, tm), lambda i: (i // tps, 0, 0, i % tps))
    params = (mu, w0, w2, a0, a2, g2, k_k, k_a, r_k, bd)
    hm_shape = jax.ShapeDtypeStruct((b, RWKV_HEADS, t_len, HEAD_DIM), F32)
    return pl.pallas_call(
        functools.partial(_rwkv_pre_kernel, tiles_per_seq=tps),
        grid=(n // tm,),
        in_specs=[
            pl.BlockSpec((tm, D_RWKV_IN), lambda i: (i, 0)),
            pl.BlockSpec((SUBLANES, D_RWKV_IN),
                         lambda i: (jnp.maximum(i * halo_blocks - 1, 0), 0)),
        ] + [full(a) for a in params],
        out_specs=[hm] * 5 + [hm_t, tok, tok],
        out_shape=[hm_shape] * 5
        + [jax.ShapeDtypeStruct((b, RWKV_HEADS, HEAD_DIM, t_len), F32)]
        + [jax.ShapeDtypeStruct((n, D_RWKV), F32)] * 2,
        compiler_params=_cparams(("parallel",)),
        name="rwkv_pre",
    )(oc, oc, *params)


RWKV_C = 64
RWKV_CB = 256
RWKV_PASSES = 1


def _bmm(eq, a, b):
    ein = lambda x, y: jnp.einsum(eq, x, y, preferred_element_type=F32)
    if RWKV_PASSES == 1:
        return ein(a.astype(BF16), b.astype(BF16))
    ah, al = _split2(a)
    bh, bl = _split2(b)
    return ein(ah, bh) + ein(ah, bl) + ein(al, bh)


def _cumsum_rows(x2d, seg):
    rows = lax.broadcasted_iota(jnp.int32, (x2d.shape[0], 1), 0) % seg
    step = 1
    while step < seg:
        shifted = pltpu.roll(x2d, step, axis=0)
        x2d = x2d + jnp.where(rows >= step, shifted, 0.0)
        step *= 2
    return x2d


def _rwkv_rec_kernel(r_ref, lw_ref, k_ref, kk_ref, a_ref, vt_ref, ot_ref, s_ref):
    H = r_ref.shape[1]
    C = RWKV_C

    @pl.when(pl.program_id(1) == 0)
    def _():
        s_ref[...] = jnp.zeros_like(s_ref)

    ri = lax.broadcasted_iota(jnp.int32, (C, C), 0)
    ci = lax.broadcasted_iota(jnp.int32, (C, C), 1)
    strict = (ri > ci)[None]
    incl = (ri >= ci)[None]
    eye = (ri == ci).astype(F32)[None]

    for sub in range(RWKV_CB // C):
        rows = slice(sub * C, (sub + 1) * C)
        r = r_ref[0, :, rows, :]
        lw = lw_ref[0, :, rows, :]
        k = k_ref[0, :, rows, :]
        kk = kk_ref[0, :, rows, :]
        a = a_ref[0, :, rows, :]
        vt = vt_ref[0, :, :, rows]
        s0 = s_ref[...]

        cum = _cumsum_rows(lw.reshape(H * C, HEAD_DIM), C).reshape(H, C, HEAD_DIM)
        cum_last = cum[:, C - 1:C, :]
        p_inv = jnp.exp(-cum)
        w_last = jnp.exp(cum_last - cum)
        bm = kk * a
        at = -kk * jnp.exp(cum - lw)
        rt = r * jnp.exp(cum)
        bt = bm * p_inv
        kt = k * p_inv

        nt = 'hik,hjk->hij'
        ar = jnp.concatenate([at, rt], axis=1)
        ar_b = _bmm(nt, ar, bt)
        ar_k = _bmm(nt, ar, kt)
        m_ab = jnp.where(strict, ar_b[:, :C], 0.0)
        n_rb = jnp.where(incl, ar_b[:, C:], 0.0)
        m_ak = jnp.where(strict, ar_k[:, :C], 0.0)
        n_rk = jnp.where(incl, ar_k[:, C:], 0.0)

        tinv = eye + m_ab
        mp = _bmm('hij,hjk->hik', m_ab, m_ab)
        n = 2
        while n < C:
            both = _bmm('hij,hjk->hik', jnp.concatenate([mp, tinv], axis=1), mp)
            mp = both[:, :C]
            tinv = tinv + both[:, C:]
            n *= 2

        rhs_t = _bmm('hvk,hik->hvi', s0, at) + _bmm('hvj,hij->hvi', vt, m_ak)
        ut = _bmm('hvj,hij->hvi', rhs_t, tinv)
        ot = (_bmm('hvk,hik->hvi', s0, rt) + _bmm('hvj,hij->hvi', ut, n_rb)
              + _bmm('hvj,hij->hvi', vt, n_rk))
        ot_ref[0, :, :, rows] = ot
        s_ref[...] = (s0 * jnp.exp(cum_last) + _bmm('hvj,hjk->hvk', ut, bm * w_last)
                      + _bmm('hvj,hjk->hvk', vt, k * w_last))


def _rwkv_recurrence(r, lw, k, kk, a, vt):
    b, h, t_len, d = r.shape
    tok = pl.BlockSpec((1, h, RWKV_CB, d), lambda bi, c: (bi, 0, c, 0))
    tr = pl.BlockSpec((1, h, d, RWKV_CB), lambda bi, c: (bi, 0, 0, c))
    return pl.pallas_call(
        _rwkv_rec_kernel,
        grid=(b, t_len // RWKV_CB),
        in_specs=[tok] * 5 + [tr],
        out_specs=tr,
        out_shape=jax.ShapeDtypeStruct((b, h, d, t_len), F32),
        scratch_shapes=[pltpu.VMEM((h, d, d), F32)],
        compiler_params=_cparams(("parallel", "arbitrary")),
        name="rwkv_recurrence",
    )(r, lw, k, kk, a, vt)


def _rwkv_post_kernel(ot_ref, bv_ref, g_ref, lnw_ref, lnb_ref, bd_ref, y_ref):
    tm = bv_ref.shape[0]
    o = ot_ref[0].reshape(D_RWKV, tm).T
    bd = bd_ref[...]
    inv = 1.0 / HEAD_DIM
    mean = _dot_x3(o, bd) * inv
    d = o - mean
    var = _dot_x3(d * d, bd) * inv
    on = d * lax.rsqrt(var + GN_EPS) * lnw_ref[...] + lnb_ref[...]
    y_ref[...] = ((on + bv_ref[...]) * g_ref[...]).astype(BF16)


def _rwkv_post(ot, bv, g, ln_w, ln_b, bd, tm=512):
    n = bv.shape[0]
    tps = ot.shape[-1] // tm
    tile = pl.BlockSpec((tm, D_RWKV), lambda i: (i, 0))
    hm_t = pl.BlockSpec((1, RWKV_HEADS, HEAD_DIM, tm), lambda i: (i // tps, 0, 0, i % tps))
    full = lambda a: pl.BlockSpec(a.shape, lambda i: (0, 0))
    return pl.pallas_call(
        _rwkv_post_kernel,
        grid=(n // tm,),
        in_specs=[hm_t, tile, tile, full(ln_w), full(ln_b), full(bd)],
        out_specs=tile,
        out_shape=jax.ShapeDtypeStruct((n, D_RWKV), BF16),
        compiler_params=_cparams(("parallel",)),
        name="rwkv_post",
    )(ot, bv, g, ln_w, ln_b, bd)


def _rwkv_mixer(oc, b, t_len, mu, w0, w2, a0, a2, g2, k_k, k_a, r_k, ln_w, ln_b):
    bd = _head_sum_matrix()
    row = lambda z: z.reshape(1, -1)
    r, lw, k, kk, a, vt, bv, g = _rwkv_pre(
        oc, b, t_len, row(mu), row(w0), w2.astype(BF16), row(a0), a2.astype(BF16),
        g2.astype(BF16), row(k_k), row(k_a), row(r_k), bd)
    ot = _rwkv_recurrence(r, lw, k, kk, a, vt)
    return _rwkv_post(ot, bv, g, row(ln_w), row(ln_b), bd)


def _out_proj_kernel(x_ref, yn_ref, yr_ref, wn_ref, wr_ref, g_ref, x1_ref, h2_ref):
    x1 = x_ref[...] + _dot(yn_ref[...], wn_ref[...]) + _dot(yr_ref[...], wr_ref[...])
    x1_ref[...] = x1
    h2_ref[...] = x1 * lax.rsqrt(jnp.mean(x1 * x1, axis=-1, keepdims=True) + EPS) * g_ref[...]


def _expand_w_out(w_out):
    wn = w_out[:D_NSA].reshape(NSA_HEADS, 1, HEAD_DIM, D_MODEL)
    onehot = (np.arange(NSA_HEADS)[:, None] // NSA_GROUP == np.arange(NSA_KV_HEADS)[None, :])
    wn = wn * jnp.asarray(onehot, F32)[:, :, None, None]
    return wn.reshape(Q_EXP, D_MODEL).astype(BF16), w_out[D_NSA:].astype(BF16)


def _out_proj(x2d, y_nsa, y_rwkv, wn, wr, g, tm=512):
    n = x2d.shape[0]
    tile = lambda a: pl.BlockSpec((tm, a.shape[1]), lambda i: (i, 0))
    full = lambda a: pl.BlockSpec(a.shape, lambda i: (0, 0))
    return pl.pallas_call(
        _out_proj_kernel,
        grid=(n // tm,),
        in_specs=[tile(x2d), tile(y_nsa), tile(y_rwkv), full(wn), full(wr), full(g)],
        out_specs=[tile(x2d), tile(x2d)],
        out_shape=[jax.ShapeDtypeStruct((n, D_MODEL), F32)] * 2,
        compiler_params=_cparams(("parallel",)),
        name="out_proj",
    )(x2d, y_nsa, y_rwkv, wn, wr, g)


def _topk_rows(s, k, rid):
    vals, ids = [], []
    for _ in range(k):
        m = jnp.max(s, axis=0, keepdims=True)
        first = jnp.min(jnp.where(s == m, rid, jnp.inf), axis=0, keepdims=True)
        vals.append(m)
        ids.append(first)
        s = jnp.where(rid == first, -jnp.inf, s)
    return jnp.concatenate(vals, axis=0), jnp.concatenate(ids, axis=0)


def _take_rows(table, idx):
    rows = lax.broadcasted_iota(jnp.int32, (table.shape[0], 1), 0).astype(F32)
    out = [jnp.sum(jnp.where(rows == idx[r:r + 1], table, 0.0), axis=0, keepdims=True)
           for r in range(idx.shape[0])]
    return jnp.concatenate(out, axis=0)


def _pair_candidates():
    k = PEER_TOPK
    flat = [0 * k + j for j in range(k)]
    for i in range(1, SUBLANES):
        flat += [i * k + j for j in range(SUBLANES)]
    flat += [i * k for i in range(SUBLANES, k)]
    return np.asarray(flat, np.float32).reshape(-1, 1)


def _peer_route_kernel(h_ref, wq_ref, sk_ref, flat_ref, e_ref, g_ref):
    K = PEER_TOPK
    half = PEER_D_QUERY // 2
    q = _dot(h_ref[...].astype(BF16), wq_ref[...]).astype(BF16)
    key_ids = lax.broadcasted_iota(jnp.int32, (PEER_N_KEYS, 1), 0).astype(F32)
    flat = flat_ref[...]
    rows_e, rows_g = [], []
    for h in range(PEER_HEADS):
        top = []
        for c in range(2):
            col = (h * 2 + c) * half
            s_t = _dot_nt(sk_ref[h, c], q[:, col:col + half])
            top.append(_topk_rows(s_t, K, key_ids))
        (v0, i0), (v1, i1) = top
        cand = jnp.concatenate(
            [v0[0:1] + v1]
            + [v0[i:i + 1] + v1[:SUBLANES] for i in range(1, SUBLANES)]
            + [v0[SUBLANES:] + v1[0:1]], axis=0)
        best, pair = _topk_rows(cand, K, flat)
        pi = jnp.floor(pair * (1.0 / K))
        pj = pair - pi * K
        experts = _take_rows(i0, pi) * float(PEER_N_KEYS) + _take_rows(i1, pj)
        p = jnp.exp(best - jnp.max(best, axis=0, keepdims=True))
        rows_e.append(experts * float(HALF_ROWS))
        rows_g.append(p / jnp.sum(p, axis=0, keepdims=True))
    e_ref[...] = jnp.concatenate(rows_e, axis=0).T.astype(jnp.int32)
    g_ref[...] = jnp.concatenate(rows_g, axis=0).T


def _peer_route(h2, wq, sk, tm=256):
    n = h2.shape[0]
    hk = PEER_HEADS * PEER_TOPK
    flat = jnp.asarray(_pair_candidates())
    out = pl.BlockSpec((tm, hk), lambda i: (i, 0))
    return pl.pallas_call(
        _peer_route_kernel,
        grid=(n // tm,),
        in_specs=[
            pl.BlockSpec((tm, D_MODEL), lambda i: (i, 0)),
            pl.BlockSpec(wq.shape, lambda i: (0, 0)),
            pl.BlockSpec(sk.shape, lambda i: (0, 0, 0, 0)),
            pl.BlockSpec(flat.shape, lambda i: (0, 0)),
        ],
        out_specs=[out, out],
        out_shape=[jax.ShapeDtypeStruct((n, hk), jnp.int32), jax.ShapeDtypeStruct((n, hk), F32)],
        compiler_params=_cparams(("parallel",)),
        name="peer_route",
    )(h2, wq, sk, flat)


PEER_TT = 128
PEER_UNROLL = 4
HALF_ROWS = SUBLANES // 2
SUB_ORDER = (0, 4, 2, 6, 1, 5, 3, 7)
HIGH_MASK = 0xFFFF0000


def _pack_table(w):
    bits = lax.bitcast_convert_type(w.astype(BF16), jnp.uint16).astype(jnp.uint32)
    half = w.shape[1] // 2
    packed = bits[:, :half] | (bits[:, half:] << 16)
    return packed.reshape(w.shape[0] * HALF_ROWS, LANES)


def _table_row(tbl_ref, row0):
    return tbl_ref[pl.ds(pl.multiple_of(row0, HALF_ROWS), HALF_ROWS), :]


def _unpack_words(word):
    lo = pltpu.bitcast(word << 16, F32)
    hi = pltpu.bitcast(word & jnp.uint32(HIGH_MASK), F32)
    return lo, hi


def _sublane_tree(c):
    sub = lax.broadcasted_iota(jnp.int32, (SUBLANES, LANES), 0)
    m2 = (sub % 4) < 2
    d = []
    for x, y in zip(c[0::2], c[1::2]):
        d.append(jnp.where(m2, x + pltpu.roll(x, 6, axis=0), y + pltpu.roll(y, 2, axis=0)))
    m1 = (sub % 2) == 0
    x, y = d
    return jnp.where(m1, x + pltpu.roll(x, 7, axis=0), y + pltpu.roll(y, 1, axis=0))


def _peer_act_kernel(e_ref, h_ref, tbl_ref, gate_ref, ones_ref, w_ref, part_ref):
    hk = PEER_HEADS * PEER_TOPK
    tt = h_ref.shape[0]

    def token(t):
        hrow = h_ref[t]
        h_lo = jnp.concatenate([hrow[:HALF_ROWS]] * 2, axis=0)
        h_hi = jnp.concatenate([hrow[HALF_ROWS:]] * 2, axis=0)
        for m in range(hk // SUBLANES):
            es = [e_ref[t, m * SUBLANES + j] for j in SUB_ORDER]
            c = []
            for ea, eb in zip(es[0::2], es[1::2]):
                lo, hi = _unpack_words(jnp.concatenate(
                    [_table_row(tbl_ref, ea), _table_row(tbl_ref, eb)], axis=0))
                c.append(lo * h_lo + hi * h_hi)
            row0 = pl.multiple_of((t * (hk // SUBLANES) + m) * SUBLANES, SUBLANES)
            part_ref[pl.ds(row0, SUBLANES), :] = _sublane_tree(c)

    def token_group(i, carry):
        for u in range(PEER_UNROLL):
            token(PEER_UNROLL * i + u)
        return carry

    lax.fori_loop(0, tt // PEER_UNROLL, token_group, 0)
    half = (tt // 2) * hk
    hi, lo = _split2(part_ref[0:half, :])
    sums_a = _dot(jnp.concatenate([hi, lo], axis=1), ones_ref[...])
    sums_b = jnp.sum(part_ref[half:, :], axis=-1, keepdims=True)
    sums_b = jnp.broadcast_to(sums_b, (tt * hk - half, LANES))
    sums = jnp.concatenate([sums_a, sums_b], axis=0).reshape(tt, hk, LANES)
    eye = (lax.broadcasted_iota(jnp.int32, (hk, LANES), 0)
           == lax.broadcasted_iota(jnp.int32, (hk, LANES), 1))
    act = jnp.sum(jnp.where(eye[None], sums, 0.0), axis=1)
    w_ref[...] = _gelu(act) * gate_ref[...]


def _table_spec(tbl):
    return pl.BlockSpec(tbl.shape, lambda i: (0, 0), pipeline_mode=pl.Buffered(1))


def _peer_act(experts, h2_tiles, tbl, gates):
    n, hk = experts.shape
    ones = jnp.ones((2 * LANES, LANES), BF16)
    return pl.pallas_call(
        _peer_act_kernel,
        grid=(n // PEER_TT,),
        in_specs=[
            pl.BlockSpec((PEER_TT, hk), lambda i: (i, 0), memory_space=pltpu.SMEM),
            pl.BlockSpec((PEER_TT, SUBLANES, LANES), lambda i: (i, 0, 0)),
            _table_spec(tbl),
            pl.BlockSpec((PEER_TT, hk), lambda i: (i, 0)),
            pl.BlockSpec(ones.shape, lambda i: (0, 0)),
        ],
        out_specs=pl.BlockSpec((PEER_TT, hk), lambda i: (i, 0)),
        out_shape=jax.ShapeDtypeStruct((n, hk), F32),
        scratch_shapes=[pltpu.VMEM((PEER_TT * hk, LANES), F32)],
        compiler_params=_cparams(("arbitrary",)),
        name="peer_act",
    )(experts, h2_tiles, tbl, gates, ones)


PEER_ACCS = 4


def _peer_out_kernel(e_ref, w_ref, x_ref, tbl_ref, g_ref, ones_ref, o_ref, wrep_ref, *,
                     final_norm):
    hk = PEER_HEADS * PEER_TOPK
    tt = x_ref.shape[0]

    eye = (lax.broadcasted_iota(jnp.int32, (hk, LANES), 0)
           == lax.broadcasted_iota(jnp.int32, (hk, LANES), 1))
    diag = jnp.where(eye[None], w_ref[...][:, None, :], 0.0).reshape(tt * hk, LANES)
    wrep_ref[...] = _dot(diag.astype(BF16), ones_ref[...])

    def token(t):
        acc_lo = [jnp.zeros((HALF_ROWS, LANES), F32) for _ in range(PEER_ACCS)]
        acc_hi = [jnp.zeros((HALF_ROWS, LANES), F32) for _ in range(PEER_ACCS)]
        base = t * hk
        for j in range(hk):
            lo, hi = _unpack_words(_table_row(tbl_ref, e_ref[base + j]))
            wk = wrep_ref[pl.ds(base + j, 1), :]
            acc_lo[j % PEER_ACCS] = acc_lo[j % PEER_ACCS] + wk * lo
            acc_hi[j % PEER_ACCS] = acc_hi[j % PEER_ACCS] + wk * hi
        tree = lambda a: (a[0] + a[1]) + (a[2] + a[3])
        o_ref[t] = x_ref[t] + jnp.concatenate([tree(acc_lo), tree(acc_hi)], axis=0)

    def token_group(i, carry):
        for u in range(PEER_UNROLL):
            token(PEER_UNROLL * i + u)
        return carry

    lax.fori_loop(0, tt // PEER_UNROLL, token_group, 0)
    if final_norm:
        x2 = o_ref[...]
        ms = jnp.sum(jnp.sum(x2 * x2, axis=2, keepdims=True), axis=1, keepdims=True) / D_MODEL
        o_ref[...] = x2 * lax.rsqrt(ms + EPS) * g_ref[...]


def _peer_out(experts, w, x1_tiles, tbl, final_g, final_norm):
    n, hk = experts.shape
    ones = jnp.ones((LANES, LANES), BF16)
    smem = pl.BlockSpec((PEER_TT * hk,), lambda i: (i,), memory_space=pltpu.SMEM)
    tile = pl.BlockSpec((PEER_TT, SUBLANES, LANES), lambda i: (i, 0, 0))
    return pl.pallas_call(
        functools.partial(_peer_out_kernel, final_norm=final_norm),
        grid=(n // PEER_TT,),
        in_specs=[smem, pl.BlockSpec((PEER_TT, hk), lambda i: (i, 0)), tile, _table_spec(tbl),
                  pl.BlockSpec(final_g.shape, lambda i: (0, 0, 0)),
                  pl.BlockSpec(ones.shape, lambda i: (0, 0))],
        out_specs=tile,
        out_shape=jax.ShapeDtypeStruct(x1_tiles.shape, F32),
        scratch_shapes=[pltpu.VMEM((PEER_TT * hk, LANES), F32)],
        compiler_params=_cparams(("arbitrary",)),
        name="peer_out",
    )(experts.reshape(-1), w, x1_tiles, tbl, final_g, ones)


def _peer_ffn_residual(x1, h2, wq, sk, tbl_u, tbl_v, final_g, final_norm):
    n = x1.shape[0]
    experts, gates = _peer_route(h2, wq, sk)
    tiles = lambda z: z.reshape(n, SUBLANES, LANES)
    w = _peer_act(experts, tiles(h2), tbl_u, gates)
    g_tile = final_g.reshape(1, SUBLANES, LANES)
    return _peer_out(experts, w, tiles(x1), tbl_v, g_tile, final_norm).reshape(n, D_MODEL)


def kernel(x, norm_mix_g, w_in, cmp_k_pe, cmp_k_w1, cmp_k_w2, cmp_v_pe, cmp_v_w1, cmp_v_w2,
           rwkv_mu, rwkv_w0, rwkv_w2, rwkv_a0, rwkv_a2, rwkv_g2, rwkv_k_k, rwkv_k_a, rwkv_r_k,
           rwkv_ln_w, rwkv_ln_b, w_out, norm_ffn_g, peer_w_q, peer_sub_keys, peer_u, peer_v,
           norm_final_g):
    b, t_len, _ = x.shape
    n = b * t_len
    row = lambda z: z.reshape(1, -1)
    x2d = x.reshape(n, D_MODEL)
    depth = w_in.shape[0]
    for l in range(depth):
        oa, ob, oc = _in_proj(x2d, row(norm_mix_g[l]), _pad_w_in(w_in[l]), t_len)
        nb = t_len // CMP_STRIDE
        rk = ob[:, 0:D_KV].reshape(b, nb, CMP_STRIDE * D_KV)
        rv = ob[:, D_KV:2 * D_KV].reshape(b, nb, CMP_STRIDE * D_KV)
        cmp_k, cmp_vt = _compress(
            rk, rv, _expand_cmp_weights(cmp_k_pe[l], cmp_k_w1[l], cmp_k_w2[l], LANES),
            _expand_cmp_weights(cmp_v_pe[l], cmp_v_w1[l], cmp_v_w2[l], HEAD_DIM))
        v_col0 = Q_EXP + 2 * K_AUG
        v_t = lambda c: oa[:, v_col0 + c * D_KV:v_col0 + (c + 1) * D_KV].reshape(
            b, t_len, D_KV).transpose(0, 2, 1)
        y_nsa = _nsa_attention(oa, ob, v_t(0), v_t(1), cmp_k, cmp_vt, b, t_len)
        y_rwkv = _rwkv_mixer(oc, b, t_len, rwkv_mu[l], rwkv_w0[l], rwkv_w2[l], rwkv_a0[l],
                             rwkv_a2[l], rwkv_g2[l], rwkv_k_k[l], rwkv_k_a[l], rwkv_r_k[l],
                             rwkv_ln_w[l], rwkv_ln_b[l])
        wn, wr = _expand_w_out(w_out[l])
        x1, h2 = _out_proj(x2d, y_nsa, y_rwkv, wn, wr, row(norm_ffn_g[l]))
        x2d = _peer_ffn_residual(x1, h2, peer_w_q[l].astype(BF16), peer_sub_keys[l].astype(BF16),
                                 _pack_table(peer_u[l]), _pack_table(peer_v[l]),
                                 norm_final_g, final_norm=(l == depth - 1))
    return x2d.reshape(b, t_len, D_MODEL)
```

```python
import functools

import numpy as np
import jax
import jax.numpy as jnp
from jax import lax
from jax.experimental import pallas as pl
from jax.experimental.pallas import tpu as pltpu

F32 = jnp.float32
BF16 = jnp.bfloat16

D_MODEL = 1024
HEAD_DIM = 64
NSA_HEADS = 8
NSA_KV_HEADS = 2
NSA_GROUP = NSA_HEADS // NSA_KV_HEADS
RWKV_HEADS = 8
D_NSA = NSA_HEADS * HEAD_DIM
D_RWKV = RWKV_HEADS * HEAD_DIM
D_KV = NSA_KV_HEADS * HEAD_DIM
CMP_LEN = 32
CMP_STRIDE = 16
CMP_HIDDEN = 128
SEL_LEN = 64
SEL_TOPN = 16
WINDOW = 512
RANK_W = 64
RANK_A = 64
RANK_G = 128
D_RWKV_IN = 3 * D_RWKV + RANK_W + RANK_A + RANK_G
PEER_HEADS = 8
PEER_N_KEYS = 128
PEER_D_QUERY = 256
PEER_TOPK = 16
EPS = 1e-6
GN_EPS = 64e-5

LANES = 128
SUBLANES = 8
VMEM_LIMIT = 56 * 1024 * 1024

NEG_BIG = -1e30
LOWEST = -3e38
Q_EXP = NSA_HEADS * LANES
K_AUG = NSA_KV_HEADS * LANES
NA_COLS = Q_EXP + 2 * K_AUG + 2 * D_KV
NB_COLS = 3 * LANES
FEAT0 = HEAD_DIM
POS_SPLIT = 64


def _cparams(sem):
    return pltpu.CompilerParams(dimension_semantics=sem, vmem_limit_bytes=VMEM_LIMIT)


def _dot(a, b):
    return jnp.dot(a, b, preferred_element_type=F32)


def _dot_nt(a, b):
    return lax.dot_general(a, b, (((1,), (1,)), ((), ())), preferred_element_type=F32)


def _split2(x):
    hi = x.astype(BF16)
    lo = (x - hi.astype(F32)).astype(BF16)
    return hi, lo


def _gelu(x):
    return 0.5 * x * (1.0 + jnp.tanh(0.7978845608028654 * (x + 0.044715 * (x * x * x))))


def _sigmoid(x):
    return 1.0 / (1.0 + jnp.exp(-x))


def _key_features(pos, lane):
    hi = (pos // POS_SPLIT).astype(F32)
    lo = (pos % POS_SPLIT).astype(F32)
    return jnp.where(lane == FEAT0, hi, jnp.where(lane == FEAT0 + 1, lo,
                     jnp.where((lane == FEAT0 + 2) | (lane == FEAT0 + 3), 1.0, 0.0)))


def _query_features(t, lane, slope):
    hi = (t // POS_SPLIT).astype(F32)
    lo = (t % POS_SPLIT).astype(F32)
    return jnp.where(lane == FEAT0, slope * POS_SPLIT, jnp.where(
        lane == FEAT0 + 1, slope, jnp.where(
            lane == FEAT0 + 2, -slope * POS_SPLIT * hi, jnp.where(
                lane == FEAT0 + 3, -slope * lo, 0.0))))


def _in_proj_kernel(x_ref, g_ref, w_ref, oa_ref, ob_ref, oc_ref, *, t_len):
    x = x_ref[...]
    tm = x.shape[0]
    h = x * lax.rsqrt(jnp.mean(x * x, axis=-1, keepdims=True) + EPS) * g_ref[...]
    hb = h.astype(BF16)
    t = (pl.program_id(0) * tm + lax.broadcasted_iota(jnp.int32, (tm, 1), 0)) % t_len
    col = lax.broadcasted_iota(jnp.int32, (1, Q_EXP), 1)
    slope = jnp.zeros((1, Q_EXP), F32)
    for hd in range(NSA_HEADS):
        slope = jnp.where(col // LANES == hd, 2.0 ** -(hd + 1), slope)
    q = _dot(hb, w_ref[:, :Q_EXP]) * (HEAD_DIM ** -0.5) + _query_features(t, col % LANES, slope)
    oa_ref[:, :Q_EXP] = q.astype(BF16)
    colk = lax.broadcasted_iota(jnp.int32, (1, 2 * K_AUG), 1)
    k = _dot(hb, w_ref[:, Q_EXP:Q_EXP + 2 * K_AUG]) + _key_features(t, colk % LANES)
    oa_ref[:, Q_EXP:Q_EXP + 2 * K_AUG] = k.astype(BF16)
    oa_ref[:, Q_EXP + 2 * K_AUG:] = _dot(hb, w_ref[:, Q_EXP + 2 * K_AUG:NA_COLS]).astype(BF16)
    ob_ref[...] = _dot(hb, w_ref[:, NA_COLS:NA_COLS + NB_COLS])
    oc_ref[...] = _dot(hb, w_ref[:, NA_COLS + NB_COLS:])


def _in_proj(x2d, g, w_pad, t_len, tm=256):
    n = x2d.shape[0]
    ncols = w_pad.shape[1]
    nc = ncols - NA_COLS - NB_COLS
    return pl.pallas_call(
        functools.partial(_in_proj_kernel, t_len=t_len),
        grid=(n // tm,),
        in_specs=[
            pl.BlockSpec((tm, D_MODEL), lambda i: (i, 0)),
            pl.BlockSpec((1, D_MODEL), lambda i: (0, 0)),
            pl.BlockSpec((D_MODEL, ncols), lambda i: (0, 0)),
        ],
        out_specs=[
            pl.BlockSpec((tm, NA_COLS), lambda i: (i, 0)),
            pl.BlockSpec((tm, NB_COLS), lambda i: (i, 0)),
            pl.BlockSpec((tm, nc), lambda i: (i, 0)),
        ],
        out_shape=[
            jax.ShapeDtypeStruct((n, NA_COLS), BF16),
            jax.ShapeDtypeStruct((n, NB_COLS), F32),
            jax.ShapeDtypeStruct((n, nc), F32),
        ],
        compiler_params=_cparams(("parallel",)),
        name="in_proj",
    )(x2d, g, w_pad)


def _pad_w_in(w_in):
    sizes = (D_NSA, D_KV, D_KV, D_KV, D_KV, D_KV, D_KV, 3 * NSA_HEADS, D_RWKV_IN)
    offs = np.cumsum((0,) + sizes)
    q, kc, vc, ks, vs, kw, vw, gl, rw = (w_in[:, offs[i]:offs[i + 1]] for i in range(9))
    def lane_pad(w, groups):
        w = w.reshape(D_MODEL, groups, HEAD_DIM)
        return jnp.pad(w, ((0, 0), (0, 0), (0, LANES - HEAD_DIM))).reshape(D_MODEL, groups * LANES)

    glp = jnp.pad(gl, ((0, 0), (0, LANES - 3 * NSA_HEADS)))
    return jnp.concatenate(
        [lane_pad(q, NSA_HEADS), lane_pad(ks, NSA_KV_HEADS), lane_pad(kw, NSA_KV_HEADS),
         vs, vw, kc, vc, glp, rw], axis=1).astype(BF16)


def _compress_kernel(rk_ref, rv_ref, pek_ref, pev_ref, w1k_ref, w1v_ref, w2k_ref, w2v_ref,
                     ok_ref, ov_ref):
    for r_ref, pe_ref, w1_ref, w2_ref, o_ref in (
            (rk_ref, pek_ref, w1k_ref, w2k_ref, ok_ref),
            (rv_ref, pev_ref, w1v_ref, w2v_ref, ov_ref)):
        rows = r_ref[0]
        nxt = pltpu.roll(rows, rows.shape[0] - 1, axis=0)
        a = (rows + pe_ref[0:1, :]).astype(BF16)
        b = (nxt + pe_ref[1:2, :]).astype(BF16)
        hid = _dot(a, w1_ref[0]) + _dot(b, w1_ref[1])
        out = _dot(_gelu(hid).astype(BF16), w2_ref[...])
        if o_ref is ok_ref:
            nb = rows.shape[0]
            end = lax.broadcasted_iota(jnp.int32, (nb, 1), 0) * CMP_STRIDE + (CMP_LEN - 1)
            lane = lax.broadcasted_iota(jnp.int32, (1, K_AUG), 1) % LANES
            out = out + _key_features(end, lane)
        else:
            out = out.T
        o_ref[0] = out.astype(BF16)


def _expand_cmp_weights(pe, w1, w2, out_lanes):
    half = CMP_LEN // 2
    eye = jnp.eye(NSA_KV_HEADS, dtype=F32)
    w1r = w1.reshape(2, half, HEAD_DIM, CMP_HIDDEN)
    w1e = w1r[:, :, None, :, None, :] * eye[None, None, :, None, :, None]
    w1e = w1e.reshape(2, half * D_KV, NSA_KV_HEADS * CMP_HIDDEN).astype(BF16)
    pee = jnp.broadcast_to(pe.reshape(2, half, 1, HEAD_DIM), (2, half, NSA_KV_HEADS, HEAD_DIM))
    pee = pee.reshape(2, half * D_KV)
    w2p = jnp.pad(w2, ((0, 0), (0, out_lanes - HEAD_DIM)))
    w2e = (eye[:, None, :, None] * w2p[None, :, None, :]).reshape(
        NSA_KV_HEADS * CMP_HIDDEN, NSA_KV_HEADS * out_lanes).astype(BF16)
    return pee, w1e, w2e


def _compress(rk, rv, wk, wv):
    b, nb, width = rk.shape
    full2 = lambda a: pl.BlockSpec(a.shape, lambda i: (0, 0))
    full3 = lambda a: pl.BlockSpec(a.shape, lambda i: (0, 0, 0))
    row = pl.BlockSpec((1, nb, width), lambda i: (i, 0, 0))
    out = lambda rows, lanes: pl.BlockSpec((1, rows, lanes), lambda i: (i, 0, 0))
    return pl.pallas_call(
        _compress_kernel,
        grid=(b,),
        in_specs=[row, row, full2(wk[0]), full2(wv[0]), full3(wk[1]), full3(wv[1]),
                  full2(wk[2]), full2(wv[2])],
        out_specs=[out(nb, K_AUG), out(D_KV, nb)],
        out_shape=[jax.ShapeDtypeStruct((b, nb, K_AUG), BF16),
                   jax.ShapeDtypeStruct((b, D_KV, nb), BF16)],
        compiler_params=_cparams(("parallel",)),
        name="nsa_compress",
    )(rk, rv, wk[0], wv[0], wk[1], wv[1], wk[2], wv[2])


NSA_TQ = 256
NSA_TK = 1024


def _flash_update(carry, s, v_t):
    m, l, acc = carry
    m_new = jnp.maximum(m, jnp.max(s, axis=0, keepdims=True))
    alpha = jnp.exp(m - m_new)
    p = jnp.exp(s - m_new)
    l = alpha * l + jnp.sum(p, axis=0, keepdims=True)
    acc = alpha * acc + _dot(v_t, p.astype(BF16))
    return m_new, l, acc


def _nsa_kernel(q_ref, ks_ref, kw_ref, vst_ref, vwt_ref, ck_ref, cvt_ref, ovl_ref, oh_ref, gl_ref,
                o_ref):
    tq = NSA_TQ
    R = NSA_GROUP
    i = pl.program_id(1)
    q0 = i * tq
    t_lane = q0 + lax.broadcasted_iota(jnp.int32, (1, tq), 1)
    t_lanes = jnp.concatenate([t_lane] * R, axis=1)
    n_cmp = ck_ref.shape[1]
    n_sel = ovl_ref.shape[0]

    gate_t = _sigmoid(gl_ref[...]).T

    cmp_end = lax.broadcasted_iota(jnp.int32, (n_cmp, 1), 0) * CMP_STRIDE + (CMP_LEN - 1)
    mask_c = t_lanes >= cmp_end
    ids = lax.broadcasted_iota(jnp.int32, (n_sel, 1), 0)
    idsf = ids.astype(F32)
    cur = t_lane // SEL_LEN
    forced = (ids == 0) | (ids == cur) | (ids == cur - 1)
    valid = ids * SEL_LEN <= t_lane

    G = NSA_KV_HEADS
    lanes_of = [slice(g * LANES, (g + 1) * LANES) for g in range(G)]
    q_st, q_slc, o_cmp = [], [], []
    for g in range(G):
        qh = [q_ref[:, h * LANES:(h + 1) * LANES] for h in range(g * R, (g + 1) * R)]
        q_st.append(jnp.concatenate(qh, axis=0))

        s = jnp.where(mask_c, _dot_nt(ck_ref[0, :, lanes_of[g]], q_st[g]), NEG_BIG)
        s = s - jnp.max(s, axis=0, keepdims=True)
        p = jnp.where(mask_c, jnp.exp(s), 0.0)
        p = p / jnp.maximum(jnp.sum(p, axis=0, keepdims=True), 1e-30)
        o_cmp.append(_dot(cvt_ref[0], p.astype(BF16)))
        p_hi, p_lo = _split2(p)
        imp4 = _dot(ovl_ref[...], p_hi) + _dot(ovl_ref[...], p_lo)
        imp = imp4[:, 0:tq]
        for r in range(1, R):
            imp = imp + imp4[:, r * tq:(r + 1) * tq]
        imp = jnp.where(forced, 1e6, jnp.where(valid, imp, -1.0))

        sel = jnp.zeros((n_sel, tq), F32)
        for _ in range(min(SEL_TOPN, n_sel)):
            m = jnp.max(imp, axis=0, keepdims=True)
            first = jnp.min(jnp.where(imp == m, idsf, float(n_sel)), axis=0, keepdims=True)
            hit = idsf == first
            sel = jnp.where(hit, 1.0, sel)
            imp = jnp.where(hit, LOWEST, imp)
        bias_t = jnp.where(sel > 0.5, 0.0, NEG_BIG)
        if n_sel < LANES:
            bias_t = jnp.concatenate([bias_t, jnp.zeros((LANES - n_sel, tq), F32)], axis=0)
        bias = bias_t.T.astype(BF16)

        q_slc.append(jnp.concatenate([q_st[g], jnp.concatenate([bias] * R, axis=0)], axis=1))

    init = tuple((jnp.full((1, R * tq), LOWEST, F32), jnp.zeros((1, R * tq), F32),
                  jnp.zeros((LANES, R * tq), F32)) for _ in range(G))

    def slc_tile(j, carry, diagonal):
        k0 = pl.multiple_of(j * NSA_TK, NSA_TK)
        onehot = oh_ref[pl.ds(k0, NSA_TK), :]
        v_t = vst_ref[0, :, pl.ds(k0, NSA_TK)]
        out = []
        for g in range(G):
            k = jnp.concatenate([ks_ref[pl.ds(k0, NSA_TK), lanes_of[g]], onehot], axis=1)
            s = _dot_nt(k, q_slc[g])
            if diagonal:
                pos = k0 + lax.broadcasted_iota(jnp.int32, (NSA_TK, 1), 0)
                s = jnp.where(t_lanes >= pos, s, NEG_BIG)
            out.append(_flash_update(carry[g], s, v_t))
        return tuple(out)

    n_off = q0 // NSA_TK
    carry = lax.fori_loop(0, n_off, functools.partial(slc_tile, diagonal=False), init)
    slc = slc_tile(n_off, carry, True)

    span = WINDOW + tq
    w0 = pl.multiple_of(jnp.maximum(q0 - WINDOW, 0), tq)
    pos = w0 + lax.broadcasted_iota(jnp.int32, (span, 1), 0)
    visible = (t_lanes >= pos) & (pos > t_lanes - WINDOW)
    v_t = vwt_ref[0, :, pl.ds(w0, span)]
    win = [_flash_update(init[g], jnp.where(
        visible, _dot_nt(kw_ref[pl.ds(w0, span), lanes_of[g]], q_st[g]), NEG_BIG), v_t)
        for g in range(G)]

    for h in range(NSA_HEADS):
        g, r = divmod(h, R)
        rows = slice(r * tq, (r + 1) * tq)
        gate = lambda j: gate_t[h * 3 + j:h * 3 + j + 1, :]
        out_t = (gate(0) * o_cmp[g][:, rows]
                 + gate(1) * (slc[g][2][:, rows] / slc[g][1][:, rows])
                 + gate(2) * (win[g][2][:, rows] / win[g][1][:, rows]))
        o_ref[:, h * LANES:(h + 1) * LANES] = out_t.T.astype(BF16)


def _nsa_consts(t_len, n_cmp_pad):
    n_cmp = (t_len - CMP_LEN) // CMP_STRIDE + 1
    n_sel = t_len // SEL_LEN
    cmp_start = np.arange(n_cmp) * CMP_STRIDE
    sel_start = np.arange(n_sel) * SEL_LEN
    ovl = np.clip(np.minimum(cmp_start[:, None] + CMP_LEN, sel_start[None, :] + SEL_LEN)
                  - np.maximum(cmp_start[:, None], sel_start[None, :]), 0, None) / CMP_LEN
    ovl_t = np.zeros((n_sel, n_cmp_pad), np.float32)
    ovl_t[:, :n_cmp] = ovl.T
    onehot = np.zeros((t_len, LANES), np.float32)
    onehot[np.arange(t_len), np.arange(t_len) // SEL_LEN] = 1.0
    return jnp.asarray(ovl_t, BF16), jnp.asarray(onehot, BF16)


def _nsa_attention(oa, ob, vs_t, vw_t, cmp_k, cmp_vt, b, t_len):
    assert NSA_TK % NSA_TQ == 0 and t_len % NSA_TK == 0 and WINDOW % NSA_TQ == 0
    assert t_len >= WINDOW + NSA_TQ and t_len // SEL_LEN <= LANES
    n_cmp_pad = cmp_k.shape[1]
    ovl_t, onehot = _nsa_consts(t_len, n_cmp_pad)
    nq = t_len // NSA_TQ
    k_col0 = Q_EXP // K_AUG
    kspec = lambda c: pl.BlockSpec((t_len, K_AUG), lambda bi, i: (bi, k_col0 + c))
    vspec = pl.BlockSpec((1, D_KV, t_len), lambda bi, i: (bi, 0, 0))
    const2 = lambda a: pl.BlockSpec(a.shape, lambda bi, i: (0, 0))
    return pl.pallas_call(
        _nsa_kernel,
        grid=(b, nq),
        in_specs=[
            pl.BlockSpec((NSA_TQ, Q_EXP), lambda bi, i: (bi * nq + i, 0)),
            kspec(0), kspec(1), vspec, vspec,
            pl.BlockSpec((1, n_cmp_pad, K_AUG), lambda bi, i: (bi, 0, 0)),
            pl.BlockSpec((1, D_KV, n_cmp_pad), lambda bi, i: (bi, 0, 0)),
            const2(ovl_t), const2(onehot),
            pl.BlockSpec((NSA_TQ, LANES), lambda bi, i: (bi * nq + i, 2)),
        ],
        out_specs=pl.BlockSpec((NSA_TQ, Q_EXP), lambda bi, i: (bi * nq + i, 0)),
        out_shape=jax.ShapeDtypeStruct((b * t_len, Q_EXP), BF16),
        compiler_params=_cparams(("parallel", "arbitrary")),
        name="nsa_attention",
    )(oa, oa, oa, vs_t, vw_t, cmp_k, cmp_vt, ovl_t, onehot, ob)


def _dot_x3(x, e):
    hi = x.astype(BF16)
    r1 = x - hi.astype(F32)
    mid = r1.astype(BF16)
    lo = (r1 - mid.astype(F32)).astype(BF16)
    return _dot(hi, e) + _dot(mid, e) + _dot(lo, e)


def _head_sum_matrix():
    ids = np.arange(D_RWKV) // HEAD_DIM
    return jnp.asarray(ids[:, None] == ids[None, :], BF16)


def _rwkv_pre_kernel(p_ref, hp_ref, mu_ref, w0_ref, w2_ref, a0_ref, a2_ref, g2_ref, kk_ref,
                     ka_ref, rk_ref, bd_ref, r_o, lw_o, k_o, kkn_o, a_o, vt_o, bv_o, g_o, *,
                     tiles_per_seq):
    i = pl.program_id(0)
    p = p_ref[...]
    tm = p.shape[0]
    keep = jnp.where(i % tiles_per_seq == 0, 0.0, 1.0)
    halo = hp_ref[SUBLANES - 1:SUBLANES, :] * keep
    prev = pltpu.roll(p, 1, axis=0)
    row0 = lax.broadcasted_iota(jnp.int32, (tm, 1), 0) == 0
    prev = jnp.where(row0, halo, prev)
    ps = p + (prev - p) * mu_ref[...]
    d = D_RWKV
    r = ps[:, 0:d]
    k = ps[:, d:2 * d]
    v = ps[:, 2 * d:3 * d]
    xw = ps[:, 3 * d:3 * d + RANK_W]
    xa = ps[:, 3 * d + RANK_W:3 * d + RANK_W + RANK_A]
    xg = ps[:, 3 * d + RANK_W + RANK_A:]
    z = -(w0_ref[...] + _dot(jnp.tanh(xw).astype(BF16), w2_ref[...]))
    softplus = jnp.maximum(z, 0.0) + jnp.log(1.0 + jnp.exp(-jnp.abs(z)))
    w = -softplus - 0.5
    a = _sigmoid(a0_ref[...] + _dot(xa.astype(BF16), a2_ref[...]))
    g = _dot(_sigmoid(xg).astype(BF16), g2_ref[...])
    kk = k * kk_ref[...]
    ss = _dot_x3(kk * kk, bd_ref[...])
    kk = kk / jnp.maximum(jnp.sqrt(ss), 1e-12)
    k_mod = k * (1.0 + (a - 1.0) * ka_ref[...])
    for o_ref, val in ((r_o, r), (lw_o, -jnp.exp(w)), (k_o, k_mod), (kkn_o, kk), (a_o, a)):
        for hd in range(RWKV_HEADS):
            o_ref[0, hd] = val[:, hd * HEAD_DIM:(hd + 1) * HEAD_DIM]
    vt_o[0] = v.T.reshape(RWKV_HEADS, HEAD_DIM, tm)
    bv_o[...] = _dot_x3(r * k_mod * rk_ref[...], bd_ref[...]) * v
    g_o[...] = g


def _rwkv_pre(oc, b, t_len, mu, w0, w2, a0, a2, g2, k_k, k_a, r_k, bd, tm=256):
    n = oc.shape[0]
    tps = t_len // tm
    halo_blocks = tm // SUBLANES
    full = lambda a: pl.BlockSpec(a.shape, lambda i: (0, 0))
    tok = pl.BlockSpec((tm, D_RWKV), lambda i: (i, 0))
    hm = pl.BlockSpec((1, RWKV_HEADS, tm, HEAD_DIM), lambda i: (i // tps, 0, i % tps, 0))
    hm_t = pl.BlockSpec((1, RWKV_HEADS, HEAD_DIM, tm), lambda i: (i // tps, 0, 0, i % tps))
    params = (mu, w0, w2, a0, a2, g2, k_k, k_a, r_k, bd)
    hm_shape = jax.ShapeDtypeStruct((b, RWKV_HEADS, t_len, HEAD_DIM), F32)
    return pl.pallas_call(
        functools.partial(_rwkv_pre_kernel, tiles_per_seq=tps),
        grid=(n // tm,),
        in_specs=[
            pl.BlockSpec((tm, D_RWKV_IN), lambda i: (i, 0)),
            pl.BlockSpec((SUBLANES, D_RWKV_IN),
                         lambda i: (jnp.maximum(i * halo_blocks - 1, 0), 0)),
        ] + [full(a) for a in params],
        out_specs=[hm] * 5 + [hm_t, tok, tok],
        out_shape=[hm_shape] * 5
        + [jax.ShapeDtypeStruct((b, RWKV_HEADS, HEAD_DIM, t_len), F32)]
        + [jax.ShapeDtypeStruct((n, D_RWKV), F32)] * 2,
        compiler_params=_cparams(("parallel",)),
        name="rwkv_pre",
    )(oc, oc, *params)


RWKV_C = 64
RWKV_CB = 256
RWKV_PASSES = 1


def _bmm(eq, a, b):
    ein = lambda x, y: jnp.einsum(eq, x, y, preferred_element_type=F32)
    if RWKV_PASSES == 1:
        return ein(a.astype(BF16), b.astype(BF16))
    ah, al = _split2(a)
    bh, bl = _split2(b)
    return ein(ah, bh) + ein(ah, bl) + ein(al, bh)


def _cumsum_rows(x2d, seg):
    rows = lax.broadcasted_iota(jnp.int32, (x2d.shape[0], 1), 0) % seg
    step = 1
    while step < seg:
        shifted = pltpu.roll(x2d, step, axis=0)
        x2d = x2d + jnp.where(rows >= step, shifted, 0.0)
        step *= 2
    return x2d


def _rwkv_rec_kernel(r_ref, lw_ref, k_ref, kk_ref, a_ref, vt_ref, ot_ref, s_ref):
    H = r_ref.shape[1]
    C = RWKV_C

    @pl.when(pl.program_id(1) == 0)
    def _():
        s_ref[...] = jnp.zeros_like(s_ref)

    ri = lax.broadcasted_iota(jnp.int32, (C, C), 0)
    ci = lax.broadcasted_iota(jnp.int32, (C, C), 1)
    strict = (ri > ci)[None]
    incl = (ri >= ci)[None]
    eye = (ri == ci).astype(F32)[None]

    for sub in range(RWKV_CB // C):
        rows = slice(sub * C, (sub + 1) * C)
        r = r_ref[0, :, rows, :]
        lw = lw_ref[0, :, rows, :]
        k = k_ref[0, :, rows, :]
        kk = kk_ref[0, :, rows, :]
        a = a_ref[0, :, rows, :]
        vt = vt_ref[0, :, :, rows]
        s0 = s_ref[...]

        cum = _cumsum_rows(lw.reshape(H * C, HEAD_DIM), C).reshape(H, C, HEAD_DIM)
        cum_last = cum[:, C - 1:C, :]
        p_inv = jnp.exp(-cum)
        w_last = jnp.exp(cum_last - cum)
        bm = kk * a
        at = -kk * jnp.exp(cum - lw)
        rt = r * jnp.exp(cum)
        bt = bm * p_inv
        kt = k * p_inv

        nt = 'hik,hjk->hij'
        ar = jnp.concatenate([at, rt], axis=1)
        ar_b = _bmm(nt, ar, bt)
        ar_k = _bmm(nt, ar, kt)
        m_ab = jnp.where(strict, ar_b[:, :C], 0.0)
        n_rb = jnp.where(incl, ar_b[:, C:], 0.0)
        m_ak = jnp.where(strict, ar_k[:, :C], 0.0)
        n_rk = jnp.where(incl, ar_k[:, C:], 0.0)

        tinv = eye + m_ab
        mp = _bmm('hij,hjk->hik', m_ab, m_ab)
        n = 2
        while n < C:
            both = _bmm('hij,hjk->hik', jnp.concatenate([mp, tinv], axis=1), mp)
            mp = both[:, :C]
            tinv = tinv + both[:, C:]
            n *= 2

        rhs_t = _bmm('hvk,hik->hvi', s0, at) + _bmm('hvj,hij->hvi', vt, m_ak)
        ut = _bmm('hvj,hij->hvi', rhs_t, tinv)
        ot = (_bmm('hvk,hik->hvi', s0, rt) + _bmm('hvj,hij->hvi', ut, n_rb)
              + _bmm('hvj,hij->hvi', vt, n_rk))
        ot_ref[0, :, :, rows] = ot
        s_ref[...] = (s0 * jnp.exp(cum_last) + _bmm('hvj,hjk->hvk', ut, bm * w_last)
                      + _bmm('hvj,hjk->hvk', vt, k * w_last))


def _rwkv_recurrence(r, lw, k, kk, a, vt):
    b, h, t_len, d = r.shape
    tok = pl.BlockSpec((1, h, RWKV_CB, d), lambda bi, c: (bi, 0, c, 0))
    tr = pl.BlockSpec((1, h, d, RWKV_CB), lambda bi, c: (bi, 0, 0, c))
    return pl.pallas_call(
        _rwkv_rec_kernel,
        grid=(b, t_len // RWKV_CB),
        in_specs=[tok] * 5 + [tr],
        out_specs=tr,
        out_shape=jax.ShapeDtypeStruct((b, h, d, t_len), F32),
        scratch_shapes=[pltpu.VMEM((h, d, d), F32)],
        compiler_params=_cparams(("parallel", "arbitrary")),
        name="rwkv_recurrence",
    )(r, lw, k, kk, a, vt)


def _rwkv_post_kernel(ot_ref, bv_ref, g_ref, lnw_ref, lnb_ref, bd_ref, y_ref):
    tm = bv_ref.shape[0]
    o = ot_ref[0].reshape(D_RWKV, tm).T
    bd = bd_ref[...]
    inv = 1.0 / HEAD_DIM
    mean = _dot_x3(o, bd) * inv
    d = o - mean
    var = _dot_x3(d * d, bd) * inv
    on = d * lax.rsqrt(var + GN_EPS) * lnw_ref[...] + lnb_ref[...]
    y_ref[...] = ((on + bv_ref[...]) * g_ref[...]).astype(BF16)


def _rwkv_post(ot, bv, g, ln_w, ln_b, bd, tm=512):
    n = bv.shape[0]
    tps = ot.shape[-1] // tm
    tile = pl.BlockSpec((tm, D_RWKV), lambda i: (i, 0))
    hm_t = pl.BlockSpec((1, RWKV_HEADS, HEAD_DIM, tm), lambda i: (i // tps, 0, 0, i % tps))
    full = lambda a: pl.BlockSpec(a.shape, lambda i: (0, 0))
    return pl.pallas_call(
        _rwkv_post_kernel,
        grid=(n // tm,),
        in_specs=[hm_t, tile, tile, full(ln_w), full(ln_b), full(bd)],
        out_specs=tile,
        out_shape=jax.ShapeDtypeStruct((n, D_RWKV), BF16),
        compiler_params=_cparams(("parallel",)),
        name="rwkv_post",
    )(ot, bv, g, ln_w, ln_b, bd)


def _rwkv_mixer(oc, b, t_len, mu, w0, w2, a0, a2, g2, k_k, k_a, r_k, ln_w, ln_b):
    bd = _head_sum_matrix()
    row = lambda z: z.reshape(1, -1)
    r, lw, k, kk, a, vt, bv, g = _rwkv_pre(
        oc, b, t_len, row(mu), row(w0), w2.astype(BF16), row(a0), a2.astype(BF16),
        g2.astype(BF16), row(k_k), row(k_a), row(r_k), bd)
    ot = _rwkv_recurrence(r, lw, k, kk, a, vt)
    return _rwkv_post(ot, bv, g, row(ln_w), row(ln_b), bd)


def _out_proj_kernel(x_ref, yn_ref, yr_ref, wn_ref, wr_ref, g_ref, x1_ref, h2_ref):
    x1 = x_ref[...] + _dot(yn_ref[...], wn_ref[...]) + _dot(yr_ref[...], wr_ref[...])
    x1_ref[...] = x1
    h2_ref[...] = x1 * lax.rsqrt(jnp.mean(x1 * x1, axis=-1, keepdims=True) + EPS) * g_ref[...]


def _expand_w_out(w_out):
    wn = w_out[:D_NSA].reshape(NSA_HEADS, 1, HEAD_DIM, D_MODEL)
    onehot = (np.arange(NSA_HEADS)[:, None] // NSA_GROUP == np.arange(NSA_KV_HEADS)[None, :])
    wn = wn * jnp.asarray(onehot, F32)[:, :, None, None]
    return wn.reshape(Q_EXP, D_MODEL).astype(BF16), w_out[D_NSA:].astype(BF16)


def _out_proj(x2d, y_nsa, y_rwkv, wn, wr, g, tm=512):
    n = x2d.shape[0]
    tile = lambda a: pl.BlockSpec((tm, a.shape[1]), lambda i: (i, 0))
    full = lambda a: pl.BlockSpec(a.shape, lambda i: (0, 0))
    return pl.pallas_call(
        _out_proj_kernel,
        grid=(n // tm,),
        in_specs=[tile(x2d), tile(y_nsa), tile(y_rwkv), full(wn), full(wr), full(g)],
        out_specs=[tile(x2d), tile(x2d)],
        out_shape=[jax.ShapeDtypeStruct((n, D_MODEL), F32)] * 2,
        compiler_params=_cparams(("parallel",)),
        name="out_proj",
    )(x2d, y_nsa, y_rwkv, wn, wr, g)


def _topk_rows(s, k, rid):
    vals, ids = [], []
    for _ in range(k):
        m = jnp.max(s, axis=0, keepdims=True)
        first = jnp.min(jnp.where(s == m, rid, jnp.inf), axis=0, keepdims=True)
        vals.append(m)
        ids.append(first)
        s = jnp.where(rid == first, -jnp.inf, s)
    return jnp.concatenate(vals, axis=0), jnp.concatenate(ids, axis=0)


def _take_rows(table, idx):
    rows = lax.broadcasted_iota(jnp.int32, (table.shape[0], 1), 0).astype(F32)
    out = [jnp.sum(jnp.where(rows == idx[r:r + 1], table, 0.0), axis=0, keepdims=True)
           for r in range(idx.shape[0])]
    return jnp.concatenate(out, axis=0)


def _pair_candidates():
    k = PEER_TOPK
    flat = [0 * k + j for j in range(k)]
    for i in range(1, SUBLANES):
        flat += [i * k + j for j in range(SUBLANES)]
    flat += [i * k for i in range(SUBLANES, k)]
    return np.asarray(flat, np.float32).reshape(-1, 1)


def _peer_route_kernel(h_ref, wq_ref, sk_ref, flat_ref, e_ref, g_ref):
    K = PEER_TOPK
    half = PEER_D_QUERY // 2
    q = _dot(h_ref[...].astype(BF16), wq_ref[...]).astype(BF16)
    key_ids = lax.broadcasted_iota(jnp.int32, (PEER_N_KEYS, 1), 0).astype(F32)
    flat = flat_ref[...]
    rows_e, rows_g = [], []
    for h in range(PEER_HEADS):
        top = []
        for c in range(2):
            col = (h * 2 + c) * half
            s_t = _dot_nt(sk_ref[h, c], q[:, col:col + half])
            top.append(_topk_rows(s_t, K, key_ids))
        (v0, i0), (v1, i1) = top
        cand = jnp.concatenate(
            [v0[0:1] + v1]
            + [v0[i:i + 1] + v1[:SUBLANES] for i in range(1, SUBLANES)]
            + [v0[SUBLANES:] + v1[0:1]], axis=0)
        best, pair = _topk_rows(cand, K, flat)
        pi = jnp.floor(pair * (1.0 / K))
        pj = pair - pi * K
        experts = _take_rows(i0, pi) * float(PEER_N_KEYS) + _take_rows(i1, pj)
        p = jnp.exp(best - jnp.max(best, axis=0, keepdims=True))
        rows_e.append(experts)
        rows_g.append(p / jnp.sum(p, axis=0, keepdims=True))
    e_ref[...] = jnp.concatenate(rows_e, axis=0).T.astype(jnp.int32)
    g_ref[...] = jnp.concatenate(rows_g, axis=0).T


def _peer_route(h2, wq, sk, tm=256):
    n = h2.shape[0]
    hk = PEER_HEADS * PEER_TOPK
    flat = jnp.asarray(_pair_candidates())
    out = pl.BlockSpec((tm, hk), lambda i: (i, 0))
    return pl.pallas_call(
        _peer_route_kernel,
        grid=(n // tm,),
        in_specs=[
            pl.BlockSpec((tm, D_MODEL), lambda i: (i, 0)),
            pl.BlockSpec(wq.shape, lambda i: (0, 0)),
            pl.BlockSpec(sk.shape, lambda i: (0, 0, 0, 0)),
            pl.BlockSpec(flat.shape, lambda i: (0, 0)),
        ],
        out_specs=[out, out],
        out_shape=[jax.ShapeDtypeStruct((n, hk), jnp.int32), jax.ShapeDtypeStruct((n, hk), F32)],
        compiler_params=_cparams(("parallel",)),
        name="peer_route",
    )(h2, wq, sk, flat)


PEER_TT = 128
PEER_UNROLL = 4
PEER_MXU_EIGHTHS = 4
SUB_ORDER = (0, 4, 2, 6, 1, 5, 3, 7)


def _pack_table(w):
    return w.astype(BF16).reshape(w.shape[0], SUBLANES, LANES)


def _table_row(tbl_ref, e):
    return tbl_ref[e].astype(F32)


def _sublane_tree(ps):
    sub = lax.broadcasted_iota(jnp.int32, (SUBLANES, LANES), 0)
    m4 = sub < 4
    c = []
    for a, b in zip(ps[0::2], ps[1::2]):
        c.append(jnp.where(m4, a, b) + pltpu.roll(jnp.where(m4, b, a), 4, axis=0))
    m2 = (sub % 4) < 2
    d = []
    for x, y in zip(c[0::2], c[1::2]):
        d.append(jnp.where(m2, x + pltpu.roll(x, 6, axis=0), y + pltpu.roll(y, 2, axis=0)))
    m1 = (sub % 2) == 0
    x, y = d
    return jnp.where(m1, x + pltpu.roll(x, 7, axis=0), y + pltpu.roll(y, 1, axis=0))


def _peer_act_kernel(e_ref, h_ref, tbl_ref, gate_ref, ones_ref, w_ref, part_ref):
    hk = PEER_HEADS * PEER_TOPK
    tt = h_ref.shape[0]

    def token(t):
        hrow = h_ref[t]
        for m in range(hk // SUBLANES):
            ps = [_table_row(tbl_ref, e_ref[t, m * SUBLANES + j]) * hrow for j in SUB_ORDER]
            row0 = pl.multiple_of((t * (hk // SUBLANES) + m) * SUBLANES, SUBLANES)
            part_ref[pl.ds(row0, SUBLANES), :] = _sublane_tree(ps)

    def token_group(i, carry):
        for u in range(PEER_UNROLL):
            token(PEER_UNROLL * i + u)
        return carry

    lax.fori_loop(0, tt // PEER_UNROLL, token_group, 0)
    t_mxu = tt * PEER_MXU_EIGHTHS // 8
    hi, lo = _split2(part_ref[0:t_mxu * hk, :])
    sums_a = _dot(jnp.concatenate([hi, lo], axis=1), ones_ref[...])
    eye = (lax.broadcasted_iota(jnp.int32, (hk, LANES), 0)
           == lax.broadcasted_iota(jnp.int32, (hk, LANES), 1))
    act_a = jnp.sum(jnp.where(eye[None], sums_a.reshape(t_mxu, hk, LANES), 0.0), axis=1)
    act_b = jnp.sum(part_ref[t_mxu * hk:, :].reshape(tt - t_mxu, hk, LANES), axis=-1)
    act = jnp.concatenate([act_a, act_b], axis=0)
    w_ref[...] = _gelu(act) * gate_ref[...]


def _table_spec(tbl):
    return pl.BlockSpec(tbl.shape, lambda i: (0, 0, 0), pipeline_mode=pl.Buffered(1))


def _peer_act(experts, h2_tiles, tbl, gates):
    n, hk = experts.shape
    ones = jnp.ones((2 * LANES, LANES), BF16)
    return pl.pallas_call(
        _peer_act_kernel,
        grid=(n // PEER_TT,),
        in_specs=[
            pl.BlockSpec((PEER_TT, hk), lambda i: (i, 0), memory_space=pltpu.SMEM),
            pl.BlockSpec((PEER_TT, SUBLANES, LANES), lambda i: (i, 0, 0)),
            _table_spec(tbl),
            pl.BlockSpec((PEER_TT, hk), lambda i: (i, 0)),
            pl.BlockSpec(ones.shape, lambda i: (0, 0)),
        ],
        out_specs=pl.BlockSpec((PEER_TT, hk), lambda i: (i, 0)),
        out_shape=jax.ShapeDtypeStruct((n, hk), F32),
        scratch_shapes=[pltpu.VMEM((PEER_TT * hk, LANES), F32)],
        compiler_params=_cparams(("arbitrary",)),
        name="peer_act",
    )(experts, h2_tiles, tbl, gates, ones)


PEER_ACCS = 4


def _peer_out_kernel(e_ref, w_ref, x_ref, tbl_ref, g_ref, ones_ref, o_ref, wrep_ref, *,
                     final_norm):
    hk = PEER_HEADS * PEER_TOPK
    tt = x_ref.shape[0]

    eye = (lax.broadcasted_iota(jnp.int32, (hk, LANES), 0)
           == lax.broadcasted_iota(jnp.int32, (hk, LANES), 1))
    diag = jnp.where(eye[None], w_ref[...][:, None, :], 0.0).reshape(tt * hk, LANES)
    wrep_ref[...] = _dot(diag.astype(BF16), ones_ref[...])

    def token(t):
        acc = [jnp.zeros((SUBLANES, LANES), F32) for _ in range(PEER_ACCS)]
        base = t * hk
        for j in range(hk):
            wk = wrep_ref[pl.ds(base + j, 1), :]
            acc[j % PEER_ACCS] = acc[j % PEER_ACCS] + wk * _table_row(tbl_ref, e_ref[base + j])
        o_ref[t] = x_ref[t] + ((acc[0] + acc[1]) + (acc[2] + acc[3]))

    def token_group(i, carry):
        for u in range(PEER_UNROLL):
            token(PEER_UNROLL * i + u)
        return carry

    lax.fori_loop(0, tt // PEER_UNROLL, token_group, 0)
    if final_norm:
        x2 = o_ref[...]
        ms = jnp.sum(jnp.sum(x2 * x2, axis=2, keepdims=True), axis=1, keepdims=True) / D_MODEL
        o_ref[...] = x2 * lax.rsqrt(ms + EPS) * g_ref[...]


def _peer_out(experts, w, x1_tiles, tbl, final_g, final_norm):
    n, hk = experts.shape
    ones = jnp.ones((LANES, LANES), BF16)
    smem = pl.BlockSpec((PEER_TT * hk,), lambda i: (i,), memory_space=pltpu.SMEM)
    tile = pl.BlockSpec((PEER_TT, SUBLANES, LANES), lambda i: (i, 0, 0))
    return pl.pallas_call(
        functools.partial(_peer_out_kernel, final_norm=final_norm),
        grid=(n // PEER_TT,),
        in_specs=[smem, pl.BlockSpec((PEER_TT, hk), lambda i: (i, 0)), tile, _table_spec(tbl),
                  pl.BlockSpec(final_g.shape, lambda i: (0, 0, 0)),
                  pl.BlockSpec(ones.shape, lambda i: (0, 0))],
        out_specs=tile,
        out_shape=jax.ShapeDtypeStruct(x1_tiles.shape, F32),
        scratch_shapes=[pltpu.VMEM((PEER_TT * hk, LANES), F32)],
        compiler_params=_cparams(("arbitrary",)),
        name="peer_out",
    )(experts.reshape(-1), w, x1_tiles, tbl, final_g, ones)


def _peer_ffn_residual(x1, h2, wq, sk, tbl_u, tbl_v, final_g, final_norm):
    n = x1.shape[0]
    experts, gates = _peer_route(h2, wq, sk)
    tiles = lambda z: z.reshape(n, SUBLANES, LANES)
    w = _peer_act(experts, tiles(h2), tbl_u, gates)
    g_tile = final_g.reshape(1, SUBLANES, LANES)
    return _peer_out(experts, w, tiles(x1), tbl_v, g_tile, final_norm).reshape(n, D_MODEL)


def kernel(x, norm_mix_g, w_in, cmp_k_pe, cmp_k_w1, cmp_k_w2, cmp_v_pe, cmp_v_w1, cmp_v_w2,
           rwkv_mu, rwkv_w0, rwkv_w2, rwkv_a0, rwkv_a2, rwkv_g2, rwkv_k_k, rwkv_k_a, rwkv_r_k,
           rwkv_ln_w, rwkv_ln_b, w_out, norm_ffn_g, peer_w_q, peer_sub_keys, peer_u, peer_v,
           norm_final_g):
    b, t_len, _ = x.shape
    n = b * t_len
    row = lambda z: z.reshape(1, -1)
    x2d = x.reshape(n, D_MODEL)
    depth = w_in.shape[0]
    for l in range(depth):
        oa, ob, oc = _in_proj(x2d, row(norm_mix_g[l]), _pad_w_in(w_in[l]), t_len)
        nb = t_len // CMP_STRIDE
        rk = ob[:, 0:D_KV].reshape(b, nb, CMP_STRIDE * D_KV)
        rv = ob[:, D_KV:2 * D_KV].reshape(b, nb, CMP_STRIDE * D_KV)
        cmp_k, cmp_vt = _compress(
            rk, rv, _expand_cmp_weights(cmp_k_pe[l], cmp_k_w1[l], cmp_k_w2[l], LANES),
            _expand_cmp_weights(cmp_v_pe[l], cmp_v_w1[l], cmp_v_w2[l], HEAD_DIM))
        v_col0 = Q_EXP + 2 * K_AUG
        v_t = lambda c: oa[:, v_col0 + c * D_KV:v_col0 + (c + 1) * D_KV].reshape(
            b, t_len, D_KV).transpose(0, 2, 1)
        y_nsa = _nsa_attention(oa, ob, v_t(0), v_t(1), cmp_k, cmp_vt, b, t_len)
        y_rwkv = _rwkv_mixer(oc, b, t_len, rwkv_mu[l], rwkv_w0[l], rwkv_w2[l], rwkv_a0[l],
                             rwkv_a2[l], rwkv_g2[l], rwkv_k_k[l], rwkv_k_a[l], rwkv_r_k[l],
                             rwkv_ln_w[l], rwkv_ln_b[l])
        wn, wr = _expand_w_out(w_out[l])
        x1, h2 = _out_proj(x2d, y_nsa, y_rwkv, wn, wr, row(norm_ffn_g[l]))
        x2d = _peer_ffn_residual(x1, h2, peer_w_q[l].astype(BF16), peer_sub_keys[l].astype(BF16),
                                 _pack_table(peer_u[l]), _pack_table(peer_v[l]),
                                 norm_final_g, final_norm=(l == depth - 1))
    return x2d.reshape(b, t_len, D_MODEL)
```

```python
import functools

import numpy as np
import jax
import jax.numpy as jnp
from jax import lax
from jax.experimental import pallas as pl
from jax.experimental.pallas import tpu as pltpu

F32 = jnp.float32
BF16 = jnp.bfloat16

D_MODEL = 1024
HEAD_DIM = 64
NSA_HEADS = 8
NSA_KV_HEADS = 2
NSA_GROUP = NSA_HEADS // NSA_KV_HEADS
RWKV_HEADS = 8
D_NSA = NSA_HEADS * HEAD_DIM
D_RWKV = RWKV_HEADS * HEAD_DIM
D_KV = NSA_KV_HEADS * HEAD_DIM
CMP_LEN = 32
CMP_STRIDE = 16
CMP_HIDDEN = 128
SEL_LEN = 64
SEL_TOPN = 16
WINDOW = 512
RANK_W = 64
RANK_A = 64
RANK_G = 128
D_RWKV_IN = 3 * D_RWKV + RANK_W + RANK_A + RANK_G
PEER_HEADS = 8
PEER_N_KEYS = 128
PEER_D_QUERY = 256
PEER_TOPK = 16
EPS = 1e-6
GN_EPS = 64e-5

LANES = 128
SUBLANES = 8
VMEM_LIMIT = 56 * 1024 * 1024

NEG_BIG = -1e30
LOWEST = -3e38
Q_EXP = NSA_HEADS * LANES
K_AUG = NSA_KV_HEADS * LANES
NA_COLS = Q_EXP + 2 * K_AUG + 2 * D_KV
NB_COLS = 3 * LANES
FEAT0 = HEAD_DIM
POS_SPLIT = 64


def _cparams(sem):
    return pltpu.CompilerParams(dimension_semantics=sem, vmem_limit_bytes=VMEM_LIMIT)


def _dot(a, b):
    return jnp.dot(a, b, preferred_element_type=F32)


def _dot_nt(a, b):
    return lax.dot_general(a, b, (((1,), (1,)), ((), ())), preferred_element_type=F32)


def _split2(x):
    hi = x.astype(BF16)
    lo = (x - hi.astype(F32)).astype(BF16)
    return hi, lo


def _gelu(x):
    return 0.5 * x * (1.0 + jnp.tanh(0.7978845608028654 * (x + 0.044715 * (x * x * x))))


def _sigmoid(x):
    return 1.0 / (1.0 + jnp.exp(-x))


def _key_features(pos, lane):
    hi = (pos // POS_SPLIT).astype(F32)
    lo = (pos % POS_SPLIT).astype(F32)
    return jnp.where(lane == FEAT0, hi, jnp.where(lane == FEAT0 + 1, lo,
                     jnp.where((lane == FEAT0 + 2) | (lane == FEAT0 + 3), 1.0, 0.0)))


def _query_features(t, lane, slope):
    hi = (t // POS_SPLIT).astype(F32)
    lo = (t % POS_SPLIT).astype(F32)
    return jnp.where(lane == FEAT0, slope * POS_SPLIT, jnp.where(
        lane == FEAT0 + 1, slope, jnp.where(
            lane == FEAT0 + 2, -slope * POS_SPLIT * hi, jnp.where(
                lane == FEAT0 + 3, -slope * lo, 0.0))))


def _in_proj_kernel(x_ref, g_ref, w_ref, oa_ref, ob_ref, oc_ref, *, t_len):
    x = x_ref[...]
    tm = x.shape[0]
    h = x * lax.rsqrt(jnp.mean(x * x, axis=-1, keepdims=True) + EPS) * g_ref[...]
    hb = h.astype(BF16)
    t = (pl.program_id(0) * tm + lax.broadcasted_iota(jnp.int32, (tm, 1), 0)) % t_len
    col = lax.broadcasted_iota(jnp.int32, (1, Q_EXP), 1)
    slope = jnp.zeros((1, Q_EXP), F32)
    for hd in range(NSA_HEADS):
        slope = jnp.where(col // LANES == hd, 2.0 ** -(hd + 1), slope)
    q = _dot(hb, w_ref[:, :Q_EXP]) * (HEAD_DIM ** -0.5) + _query_features(t, col % LANES, slope)
    oa_ref[:, :Q_EXP] = q.astype(BF16)
    colk = lax.broadcasted_iota(jnp.int32, (1, 2 * K_AUG), 1)
    k = _dot(hb, w_ref[:, Q_EXP:Q_EXP + 2 * K_AUG]) + _key_features(t, colk % LANES)
    oa_ref[:, Q_EXP:Q_EXP + 2 * K_AUG] = k.astype(BF16)
    oa_ref[:, Q_EXP + 2 * K_AUG:] = _dot(hb, w_ref[:, Q_EXP + 2 * K_AUG:NA_COLS]).astype(BF16)
    ob_ref[...] = _dot(hb, w_ref[:, NA_COLS:NA_COLS + NB_COLS])
    oc_ref[...] = _dot(hb, w_ref[:, NA_COLS + NB_COLS:])


def _in_proj(x2d, g, w_pad, t_len, tm=256):
    n = x2d.shape[0]
    ncols = w_pad.shape[1]
    nc = ncols - NA_COLS - NB_COLS
    return pl.pallas_call(
        functools.partial(_in_proj_kernel, t_len=t_len),
        grid=(n // tm,),
        in_specs=[
            pl.BlockSpec((tm, D_MODEL), lambda i: (i, 0)),
            pl.BlockSpec((1, D_MODEL), lambda i: (0, 0)),
            pl.BlockSpec((D_MODEL, ncols), lambda i: (0, 0)),
        ],
        out_specs=[
            pl.BlockSpec((tm, NA_COLS), lambda i: (i, 0)),
            pl.BlockSpec((tm, NB_COLS), lambda i: (i, 0)),
            pl.BlockSpec((tm, nc), lambda i: (i, 0)),
        ],
        out_shape=[
            jax.ShapeDtypeStruct((n, NA_COLS), BF16),
            jax.ShapeDtypeStruct((n, NB_COLS), F32),
            jax.ShapeDtypeStruct((n, nc), F32),
        ],
        compiler_params=_cparams(("parallel",)),
        name="in_proj",
    )(x2d, g, w_pad)


def _pad_w_in(w_in):
    sizes = (D_NSA, D_KV, D_KV, D_KV, D_KV, D_KV, D_KV, 3 * NSA_HEADS, D_RWKV_IN)
    offs = np.cumsum((0,) + sizes)
    q, kc, vc, ks, vs, kw, vw, gl, rw = (w_in[:, offs[i]:offs[i + 1]] for i in range(9))
    def lane_pad(w, groups):
        w = w.reshape(D_MODEL, groups, HEAD_DIM)
        return jnp.pad(w, ((0, 0), (0, 0), (0, LANES - HEAD_DIM))).reshape(D_MODEL, groups * LANES)

    glp = jnp.pad(gl, ((0, 0), (0, LANES - 3 * NSA_HEADS)))
    return jnp.concatenate(
        [lane_pad(q, NSA_HEADS), lane_pad(ks, NSA_KV_HEADS), lane_pad(kw, NSA_KV_HEADS),
         vs, vw, kc, vc, glp, rw], axis=1).astype(BF16)


def _compress_kernel(rk_ref, rv_ref, pek_ref, pev_ref, w1k_ref, w1v_ref, w2k_ref, w2v_ref,
                     ok_ref, ov_ref):
    for r_ref, pe_ref, w1_ref, w2_ref, o_ref in (
            (rk_ref, pek_ref, w1k_ref, w2k_ref, ok_ref),
            (rv_ref, pev_ref, w1v_ref, w2v_ref, ov_ref)):
        rows = r_ref[0]
        nxt = pltpu.roll(rows, rows.shape[0] - 1, axis=0)
        a = (rows + pe_ref[0:1, :]).astype(BF16)
        b = (nxt + pe_ref[1:2, :]).astype(BF16)
        hid = _dot(a, w1_ref[0]) + _dot(b, w1_ref[1])
        out = _dot(_gelu(hid).astype(BF16), w2_ref[...])
        if o_ref is ok_ref:
            nb = rows.shape[0]
            end = lax.broadcasted_iota(jnp.int32, (nb, 1), 0) * CMP_STRIDE + (CMP_LEN - 1)
            lane = lax.broadcasted_iota(jnp.int32, (1, K_AUG), 1) % LANES
            out = out + _key_features(end, lane)
        else:
            out = out.T
        o_ref[0] = out.astype(BF16)


def _expand_cmp_weights(pe, w1, w2, out_lanes):
    half = CMP_LEN // 2
    eye = jnp.eye(NSA_KV_HEADS, dtype=F32)
    w1r = w1.reshape(2, half, HEAD_DIM, CMP_HIDDEN)
    w1e = w1r[:, :, None, :, None, :] * eye[None, None, :, None, :, None]
    w1e = w1e.reshape(2, half * D_KV, NSA_KV_HEADS * CMP_HIDDEN).astype(BF16)
    pee = jnp.broadcast_to(pe.reshape(2, half, 1, HEAD_DIM), (2, half, NSA_KV_HEADS, HEAD_DIM))
    pee = pee.reshape(2, half * D_KV)
    w2p = jnp.pad(w2, ((0, 0), (0, out_lanes - HEAD_DIM)))
    w2e = (eye[:, None, :, None] * w2p[None, :, None, :]).reshape(
        NSA_KV_HEADS * CMP_HIDDEN, NSA_KV_HEADS * out_lanes).astype(BF16)
    return pee, w1e, w2e


def _compress(rk, rv, wk, wv):
    b, nb, width = rk.shape
    full2 = lambda a: pl.BlockSpec(a.shape, lambda i: (0, 0))
    full3 = lambda a: pl.BlockSpec(a.shape, lambda i: (0, 0, 0))
    row = pl.BlockSpec((1, nb, width), lambda i: (i, 0, 0))
    out = lambda rows, lanes: pl.BlockSpec((1, rows, lanes), lambda i: (i, 0, 0))
    return pl.pallas_call(
        _compress_kernel,
        grid=(b,),
        in_specs=[row, row, full2(wk[0]), full2(wv[0]), full3(wk[1]), full3(wv[1]),
                  full2(wk[2]), full2(wv[2])],
        out_specs=[out(nb, K_AUG), out(D_KV, nb)],
        out_shape=[jax.ShapeDtypeStruct((b, nb, K_AUG), BF16),
                   jax.ShapeDtypeStruct((b, D_KV, nb), BF16)],
        compiler_params=_cparams(("parallel",)),
        name="nsa_compress",
    )(rk, rv, wk[0], wv[0], wk[1], wv[1], wk[2], wv[2])


NSA_TQ = 256
NSA_TK = 1024


def _flash_update(carry, s, v_t):
    m, l, acc = carry
    m_new = jnp.maximum(m, jnp.max(s, axis=0, keepdims=True))
    alpha = jnp.exp(m - m_new)
    p = jnp.exp(s - m_new)
    l = alpha * l + jnp.sum(p, axis=0, keepdims=True)
    acc = alpha * acc + _dot(v_t, p.astype(BF16))
    return m_new, l, acc


def _nsa_kernel(q_ref, ks_ref, kw_ref, vst_ref, vwt_ref, ck_ref, cvt_ref, ovl_ref, oh_ref, gl_ref,
                o_ref):
    tq = NSA_TQ
    R = NSA_GROUP
    i = pl.program_id(1)
    q0 = i * tq
    t_lane = q0 + lax.broadcasted_iota(jnp.int32, (1, tq), 1)
    t_lanes = jnp.concatenate([t_lane] * R, axis=1)
    n_cmp = ck_ref.shape[1]
    n_sel = ovl_ref.shape[0]

    gate_t = _sigmoid(gl_ref[...]).T

    cmp_end = lax.broadcasted_iota(jnp.int32, (n_cmp, 1), 0) * CMP_STRIDE + (CMP_LEN - 1)
    mask_c = t_lanes >= cmp_end
    ids = lax.broadcasted_iota(jnp.int32, (n_sel, 1), 0)
    idsf = ids.astype(F32)
    cur = t_lane // SEL_LEN
    forced = (ids == 0) | (ids == cur) | (ids == cur - 1)
    valid = ids * SEL_LEN <= t_lane

    G = NSA_KV_HEADS
    lanes_of = [slice(g * LANES, (g + 1) * LANES) for g in range(G)]
    q_st, q_slc, o_cmp = [], [], []
    for g in range(G):
        qh = [q_ref[:, h * LANES:(h + 1) * LANES] for h in range(g * R, (g + 1) * R)]
        q_st.append(jnp.concatenate(qh, axis=0))

        s = jnp.where(mask_c, _dot_nt(ck_ref[0, :, lanes_of[g]], q_st[g]), NEG_BIG)
        s = s - jnp.max(s, axis=0, keepdims=True)
        p = jnp.where(mask_c, jnp.exp(s), 0.0)
        p = p / jnp.maximum(jnp.sum(p, axis=0, keepdims=True), 1e-30)
        o_cmp.append(_dot(cvt_ref[0], p.astype(BF16)))
        p_hi, p_lo = _split2(p)
        imp4 = _dot(ovl_ref[...], p_hi) + _dot(ovl_ref[...], p_lo)
        imp = imp4[:, 0:tq]
        for r in range(1, R):
            imp = imp + imp4[:, r * tq:(r + 1) * tq]
        imp = jnp.where(forced, 1e6, jnp.where(valid, imp, -1.0))

        sel = jnp.zeros((n_sel, tq), F32)
        for _ in range(min(SEL_TOPN, n_sel)):
            m = jnp.max(imp, axis=0, keepdims=True)
            first = jnp.min(jnp.where(imp == m, idsf, float(n_sel)), axis=0, keepdims=True)
            hit = idsf == first
            sel = jnp.where(hit, 1.0, sel)
            imp = jnp.where(hit, LOWEST, imp)
        bias_t = jnp.where(sel > 0.5, 0.0, NEG_BIG)
        if n_sel < LANES:
            bias_t = jnp.concatenate([bias_t, jnp.zeros((LANES - n_sel, tq), F32)], axis=0)
        bias = bias_t.T.astype(BF16)

        q_slc.append(jnp.concatenate([q_st[g], jnp.concatenate([bias] * R, axis=0)], axis=1))

    init = tuple((jnp.full((1, R * tq), LOWEST, F32), jnp.zeros((1, R * tq), F32),
                  jnp.zeros((LANES, R * tq), F32)) for _ in range(G))

    def slc_tile(j, carry, diagonal):
        k0 = pl.multiple_of(j * NSA_TK, NSA_TK)
        onehot = oh_ref[pl.ds(k0, NSA_TK), :]
        v_t = vst_ref[0, :, pl.ds(k0, NSA_TK)]
        out = []
        for g in range(G):
            k = jnp.concatenate([ks_ref[pl.ds(k0, NSA_TK), lanes_of[g]], onehot], axis=1)
            s = _dot_nt(k, q_slc[g])
            if diagonal:
                pos = k0 + lax.broadcasted_iota(jnp.int32, (NSA_TK, 1), 0)
                s = jnp.where(t_lanes >= pos, s, NEG_BIG)
            out.append(_flash_update(carry[g], s, v_t))
        return tuple(out)

    n_off = q0 // NSA_TK
    carry = lax.fori_loop(0, n_off, functools.partial(slc_tile, diagonal=False), init)
    slc = slc_tile(n_off, carry, True)

    span = WINDOW + tq
    w0 = pl.multiple_of(jnp.maximum(q0 - WINDOW, 0), tq)
    pos = w0 + lax.broadcasted_iota(jnp.int32, (span, 1), 0)
    visible = (t_lanes >= pos) & (pos > t_lanes - WINDOW)
    v_t = vwt_ref[0, :, pl.ds(w0, span)]
    win = [_flash_update(init[g], jnp.where(
        visible, _dot_nt(kw_ref[pl.ds(w0, span), lanes_of[g]], q_st[g]), NEG_BIG), v_t)
        for g in range(G)]

    for h in range(NSA_HEADS):
        g, r = divmod(h, R)
        rows = slice(r * tq, (r + 1) * tq)
        gate = lambda j: gate_t[h * 3 + j:h * 3 + j + 1, :]
        out_t = (gate(0) * o_cmp[g][:, rows]
                 + gate(1) * (slc[g][2][:, rows] / slc[g][1][:, rows])
                 + gate(2) * (win[g][2][:, rows] / win[g][1][:, rows]))
        o_ref[:, h * LANES:(h + 1) * LANES] = out_t.T.astype(BF16)


def _nsa_consts(t_len, n_cmp_pad):
    n_cmp = (t_len - CMP_LEN) // CMP_STRIDE + 1
    n_sel = t_len // SEL_LEN
    cmp_start = np.arange(n_cmp) * CMP_STRIDE
    sel_start = np.arange(n_sel) * SEL_LEN
    ovl = np.clip(np.minimum(cmp_start[:, None] + CMP_LEN, sel_start[None, :] + SEL_LEN)
                  - np.maximum(cmp_start[:, None], sel_start[None, :]), 0, None) / CMP_LEN
    ovl_t = np.zeros((n_sel, n_cmp_pad), np.float32)
    ovl_t[:, :n_cmp] = ovl.T
    onehot = np.zeros((t_len, LANES), np.float32)
    onehot[np.arange(t_len), np.arange(t_len) // SEL_LEN] = 1.0
    return jnp.asarray(ovl_t, BF16), jnp.asarray(onehot, BF16)


def _nsa_attention(oa, ob, vs_t, vw_t, cmp_k, cmp_vt, b, t_len):
    assert NSA_TK % NSA_TQ == 0 and t_len % NSA_TK == 0 and WINDOW % NSA_TQ == 0
    assert t_len >= WINDOW + NSA_TQ and t_len // SEL_LEN <= LANES
    n_cmp_pad = cmp_k.shape[1]
    ovl_t, onehot = _nsa_consts(t_len, n_cmp_pad)
    nq = t_len // NSA_TQ
    k_col0 = Q_EXP // K_AUG
    kspec = lambda c: pl.BlockSpec((t_len, K_AUG), lambda bi, i: (bi, k_col0 + c))
    vspec = pl.BlockSpec((1, D_KV, t_len), lambda bi, i: (bi, 0, 0))
    const2 = lambda a: pl.BlockSpec(a.shape, lambda bi, i: (0, 0))
    return pl.pallas_call(
        _nsa_kernel,
        grid=(b, nq),
        in_specs=[
            pl.BlockSpec((NSA_TQ, Q_EXP), lambda bi, i: (bi * nq + i, 0)),
            kspec(0), kspec(1), vspec, vspec,
            pl.BlockSpec((1, n_cmp_pad, K_AUG), lambda bi, i: (bi, 0, 0)),
            pl.BlockSpec((1, D_KV, n_cmp_pad), lambda bi, i: (bi, 0, 0)),
            const2(ovl_t), const2(onehot),
            pl.BlockSpec((NSA_TQ, LANES), lambda bi, i: (bi * nq + i, 2)),
        ],
        out_specs=pl.BlockSpec((NSA_TQ, Q_EXP), lambda bi, i: (bi * nq + i, 0)),
        out_shape=jax.ShapeDtypeStruct((b * t_len, Q_EXP), BF16),
        compiler_params=_cparams(("parallel", "arbitrary")),
        name="nsa_attention",
    )(oa, oa, oa, vs_t, vw_t, cmp_k, cmp_vt, ovl_t, onehot, ob)


def _dot_x3(x, e):
    hi = x.astype(BF16)
    r1 = x - hi.astype(F32)
    mid = r1.astype(BF16)
    lo = (r1 - mid.astype(F32)).astype(BF16)
    return _dot(hi, e) + _dot(mid, e) + _dot(lo, e)


def _head_sum_matrix():
    ids = np.arange(D_RWKV) // HEAD_DIM
    return jnp.asarray(ids[:, None] == ids[None, :], BF16)


def _rwkv_pre_kernel(p_ref, hp_ref, mu_ref, w0_ref, w2_ref, a0_ref, a2_ref, g2_ref, kk_ref,
                     ka_ref, rk_ref, bd_ref, r_o, lw_o, k_o, kkn_o, a_o, vt_o, bv_o, g_o, *,
                     tiles_per_seq):
    i = pl.program_id(0)
    p = p_ref[...]
    tm = p.shape[0]
    keep = jnp.where(i % tiles_per_seq == 0, 0.0, 1.0)
    halo = hp_ref[SUBLANES - 1:SUBLANES, :] * keep
    prev = pltpu.roll(p, 1, axis=0)
    row0 = lax.broadcasted_iota(jnp.int32, (tm, 1), 0) == 0
    prev = jnp.where(row0, halo, prev)
    ps = p + (prev - p) * mu_ref[...]
    d = D_RWKV
    r = ps[:, 0:d]
    k = ps[:, d:2 * d]
    v = ps[:, 2 * d:3 * d]
    xw = ps[:, 3 * d:3 * d + RANK_W]
    xa = ps[:, 3 * d + RANK_W:3 * d + RANK_W + RANK_A]
    xg = ps[:, 3 * d + RANK_W + RANK_A:]
    z = -(w0_ref[...] + _dot(jnp.tanh(xw).astype(BF16), w2_ref[...]))
    softplus = jnp.maximum(z, 0.0) + jnp.log(1.0 + jnp.exp(-jnp.abs(z)))
    w = -softplus - 0.5
    a = _sigmoid(a0_ref[...] + _dot(xa.astype(BF16), a2_ref[...]))
    g = _dot(_sigmoid(xg).astype(BF16), g2_ref[...])
    kk = k * kk_ref[...]
    ss = _dot_x3(kk * kk, bd_ref[...])
    kk = kk / jnp.maximum(jnp.sqrt(ss), 1e-12)
    k_mod = k * (1.0 + (a - 1.0) * ka_ref[...])
    for o_ref, val in ((r_o, r), (lw_o, -jnp.exp(w)), (k_o, k_mod), (kkn_o, kk), (a_o, a)):
        for hd in range(RWKV_HEADS):
            o_ref[0, hd] = val[:, hd * HEAD_DIM:(hd + 1) * HEAD_DIM]
    vt_o[0] = v.T.reshape(RWKV_HEADS, HEAD_DIM, tm)
    bv_o[...] = _dot_x3(r * k_mod * rk_ref[...], bd_ref[...]) * v
    g_o[...] = g


def _rwkv_pre(oc, b, t_len, mu, w0, w2, a0, a2, g2, k_k, k_a, r_k, bd, tm=256):
    n = oc.shape[0]
    tps = t_len // tm
    halo_blocks = tm // SUBLANES
    full = lambda a: pl.BlockSpec(a.shape, lambda i: (0, 0))
    tok = pl.BlockSpec((tm, D_RWKV), lambda i: (i, 0))
    hm = pl.BlockSpec((1, RWKV_HEADS, tm, HEAD_DIM), lambda i: (i // tps, 0, i % tps, 0))
    hm_t = pl.BlockSpec((1, RWKV_HEADS, HEAD_DIM, tm), lambda i: (i // tps, 0, 0, i % tps))
    params = (mu, w0, w2, a0, a2, g2, k_k, k_a, r_k, bd)
    hm_shape = jax.ShapeDtypeStruct((b, RWKV_HEADS, t_len, HEAD_DIM), F32)
    return pl.pallas_call(
        functools.partial(_rwkv_pre_kernel, tiles_per_seq=tps),
        grid=(n // tm,),
        in_specs=[
            pl.BlockSpec((tm, D_RWKV_IN), lambda i: (i, 0)),
            pl.BlockSpec((SUBLANES, D_RWKV_IN),
                         lambda i: (jnp.maximum(i * halo_blocks - 1, 0), 0)),
        ] + [full(a) for a in params],
        out_specs=[hm] * 5 + [hm_t, tok, tok],
        out_shape=[hm_shape] * 5
        + [jax.ShapeDtypeStruct((b, RWKV_HEADS, HEAD_DIM, t_len), F32)]
        + [jax.ShapeDtypeStruct((n, D_RWKV), F32)] * 2,
        compiler_params=_cparams(("parallel",)),
        name="rwkv_pre",
    )(oc, oc, *params)


RWKV_C = 64
RWKV_CB = 256
RWKV_PASSES = 1


def _bmm(eq, a, b):
    ein = lambda x, y: jnp.einsum(eq, x, y, preferred_element_type=F32)
    if RWKV_PASSES == 1:
        return ein(a.astype(BF16), b.astype(BF16))
    ah, al = _split2(a)
    bh, bl = _split2(b)
    return ein(ah, bh) + ein(ah, bl) + ein(al, bh)


def _cumsum_rows(x2d, seg):
    rows = lax.broadcasted_iota(jnp.int32, (x2d.shape[0], 1), 0) % seg
    step = 1
    while step < seg:
        shifted = pltpu.roll(x2d, step, axis=0)
        x2d = x2d + jnp.where(rows >= step, shifted, 0.0)
        step *= 2
    return x2d


def _rwkv_rec_kernel(r_ref, lw_ref, k_ref, kk_ref, a_ref, vt_ref, ot_ref, s_ref):
    H = r_ref.shape[1]
    C = RWKV_C

    @pl.when(pl.program_id(1) == 0)
    def _():
        s_ref[...] = jnp.zeros_like(s_ref)

    ri = lax.broadcasted_iota(jnp.int32, (C, C), 0)
    ci = lax.broadcasted_iota(jnp.int32, (C, C), 1)
    strict = (ri > ci)[None]
    incl = (ri >= ci)[None]
    eye = (ri == ci).astype(F32)[None]

    for sub in range(RWKV_CB // C):
        rows = slice(sub * C, (sub + 1) * C)
        r = r_ref[0, :, rows, :]
        lw = lw_ref[0, :, rows, :]
        k = k_ref[0, :, rows, :]
        kk = kk_ref[0, :, rows, :]
        a = a_ref[0, :, rows, :]
        vt = vt_ref[0, :, :, rows]
        s0 = s_ref[...]

        cum = _cumsum_rows(lw.reshape(H * C, HEAD_DIM), C).reshape(H, C, HEAD_DIM)
        cum_last = cum[:, C - 1:C, :]
        p_inv = jnp.exp(-cum)
        w_last = jnp.exp(cum_last - cum)
        bm = kk * a
        at = -kk * jnp.exp(cum - lw)
        rt = r * jnp.exp(cum)
        bt = bm * p_inv
        kt = k * p_inv

        nt = 'hik,hjk->hij'
        ar = jnp.concatenate([at, rt], axis=1)
        ar_b = _bmm(nt, ar, bt)
        ar_k = _bmm(nt, ar, kt)
        m_ab = jnp.where(strict, ar_b[:, :C], 0.0)
        n_rb = jnp.where(incl, ar_b[:, C:], 0.0)
        m_ak = jnp.where(strict, ar_k[:, :C], 0.0)
        n_rk = jnp.where(incl, ar_k[:, C:], 0.0)

        tinv = eye + m_ab
        mp = _bmm('hij,hjk->hik', m_ab, m_ab)
        n = 2
        while n < C:
            both = _bmm('hij,hjk->hik', jnp.concatenate([mp, tinv], axis=1), mp)
            mp = both[:, :C]
            tinv = tinv + both[:, C:]
            n *= 2

        rhs_t = _bmm('hvk,hik->hvi', s0, at) + _bmm('hvj,hij->hvi', vt, m_ak)
        ut = _bmm('hvj,hij->hvi', rhs_t, tinv)
        ot = (_bmm('hvk,hik->hvi', s0, rt) + _bmm('hvj,hij->hvi', ut, n_rb)
              + _bmm('hvj,hij->hvi', vt, n_rk))
        ot_ref[0, :, :, rows] = ot
        s_ref[...] = (s0 * jnp.exp(cum_last) + _bmm('hvj,hjk->hvk', ut, bm * w_last)
                      + _bmm('hvj,hjk->hvk', vt, k * w_last))


def _rwkv_recurrence(r, lw, k, kk, a, vt):
    b, h, t_len, d = r.shape
    tok = pl.BlockSpec((1, h, RWKV_CB, d), lambda bi, c: (bi, 0, c, 0))
    tr = pl.BlockSpec((1, h, d, RWKV_CB), lambda bi, c: (bi, 0, 0, c))
    return pl.pallas_call(
        _rwkv_rec_kernel,
        grid=(b, t_len // RWKV_CB),
        in_specs=[tok] * 5 + [tr],
        out_specs=tr,
        out_shape=jax.ShapeDtypeStruct((b, h, d, t_len), F32),
        scratch_shapes=[pltpu.VMEM((h, d, d), F32)],
        compiler_params=_cparams(("parallel", "arbitrary")),
        name="rwkv_recurrence",
    )(r, lw, k, kk, a, vt)


def _rwkv_post_kernel(ot_ref, bv_ref, g_ref, lnw_ref, lnb_ref, bd_ref, y_ref):
    tm = bv_ref.shape[0]
    o = ot_ref[0].reshape(D_RWKV, tm).T
    bd = bd_ref[...]
    inv = 1.0 / HEAD_DIM
    mean = _dot_x3(o, bd) * inv
    d = o - mean
    var = _dot_x3(d * d, bd) * inv
    on = d * lax.rsqrt(var + GN_EPS) * lnw_ref[...] + lnb_ref[...]
    y_ref[...] = ((on + bv_ref[...]) * g_ref[...]).astype(BF16)


def _rwkv_post(ot, bv, g, ln_w, ln_b, bd, tm=512):
    n = bv.shape[0]
    tps = ot.shape[-1] // tm
    tile = pl.BlockSpec((tm, D_RWKV), lambda i: (i, 0))
    hm_t = pl.BlockSpec((1, RWKV_HEADS, HEAD_DIM, tm), lambda i: (i // tps, 0, 0, i % tps))
    full = lambda a: pl.BlockSpec(a.shape, lambda i: (0, 0))
    return pl.pallas_call(
        _rwkv_post_kernel,
        grid=(n // tm,),
        in_specs=[hm_t, tile, tile, full(ln_w), full(ln_b), full(bd)],
        out_specs=tile,
        out_shape=jax.ShapeDtypeStruct((n, D_RWKV), BF16),
        compiler_params=_cparams(("parallel",)),
        name="rwkv_post",
    )(ot, bv, g, ln_w, ln_b, bd)


def _rwkv_mixer(oc, b, t_len, mu, w0, w2, a0, a2, g2, k_k, k_a, r_k, ln_w, ln_b):
    bd = _head_sum_matrix()
    row = lambda z: z.reshape(1, -1)
    r, lw, k, kk, a, vt, bv, g = _rwkv_pre(
        oc, b, t_len, row(mu), row(w0), w2.astype(BF16), row(a0), a2.astype(BF16),
        g2.astype(BF16), row(k_k), row(k_a), row(r_k), bd)
    ot = _rwkv_recurrence(r, lw, k, kk, a, vt)
    return _rwkv_post(ot, bv, g, row(ln_w), row(ln_b), bd)


def _out_proj_kernel(x_ref, yn_ref, yr_ref, wn_ref, wr_ref, g_ref, x1_ref, h2_ref):
    x1 = x_ref[...] + _dot(yn_ref[...], wn_ref[...]) + _dot(yr_ref[...], wr_ref[...])
    x1_ref[...] = x1
    h2_ref[...] = x1 * lax.rsqrt(jnp.mean(x1 * x1, axis=-1, keepdims=True) + EPS) * g_ref[...]


def _expand_w_out(w_out):
    wn = w_out[:D_NSA].reshape(NSA_HEADS, 1, HEAD_DIM, D_MODEL)
    onehot = (np.arange(NSA_HEADS)[:, None] // NSA_GROUP == np.arange(NSA_KV_HEADS)[None, :])
    wn = wn * jnp.asarray(onehot, F32)[:, :, None, None]
    return wn.reshape(Q_EXP, D_MODEL).astype(BF16), w_out[D_NSA:].astype(BF16)


def _out_proj(x2d, y_nsa, y_rwkv, wn, wr, g, tm=512):
    n = x2d.shape[0]
    tile = lambda a: pl.BlockSpec((tm, a.shape[1]), lambda i: (i, 0))
    full = lambda a: pl.BlockSpec(a.shape, lambda i: (0, 0))
    return pl.pallas_call(
        _out_proj_kernel,
        grid=(n // tm,),
        in_specs=[tile(x2d), tile(y_nsa), tile(y_rwkv), full(wn), full(wr), full(g)],
        out_specs=[tile(x2d), tile(x2d)],
        out_shape=[jax.ShapeDtypeStruct((n, D_MODEL), F32)] * 2,
        compiler_params=_cparams(("parallel",)),
        name="out_proj",
    )(x2d, y_nsa, y_rwkv, wn, wr, g)


def _topk_rows(s, k, rid):
    vals, ids = [], []
    for _ in range(k):
        m = jnp.max(s, axis=0, keepdims=True)
        first = jnp.min(jnp.where(s == m, rid, jnp.inf), axis=0, keepdims=True)
        vals.append(m)
        ids.append(first)
        s = jnp.where(rid == first, -jnp.inf, s)
    return jnp.concatenate(vals, axis=0), jnp.concatenate(ids, axis=0)


def _take_rows(table, idx):
    rows = lax.broadcasted_iota(jnp.int32, (table.shape[0], 1), 0).astype(F32)
    out = [jnp.sum(jnp.where(rows == idx[r:r + 1], table, 0.0), axis=0, keepdims=True)
           for r in range(idx.shape[0])]
    return jnp.concatenate(out, axis=0)


def _pair_candidates():
    k = PEER_TOPK
    flat = [0 * k + j for j in range(k)]
    for i in range(1, SUBLANES):
        flat += [i * k + j for j in range(SUBLANES)]
    flat += [i * k for i in range(SUBLANES, k)]
    return np.asarray(flat, np.float32).reshape(-1, 1)


def _peer_route_kernel(h_ref, wq_ref, sk_ref, flat_ref, e_ref, g_ref):
    K = PEER_TOPK
    half = PEER_D_QUERY // 2
    q = _dot(h_ref[...].astype(BF16), wq_ref[...]).astype(BF16)
    key_ids = lax.broadcasted_iota(jnp.int32, (PEER_N_KEYS, 1), 0).astype(F32)
    flat = flat_ref[...]
    rows_e, rows_g = [], []
    for h in range(PEER_HEADS):
        top = []
        for c in range(2):
            col = (h * 2 + c) * half
            s_t = _dot_nt(sk_ref[h, c], q[:, col:col + half])
            top.append(_topk_rows(s_t, K, key_ids))
        (v0, i0), (v1, i1) = top
        cand = jnp.concatenate(
            [v0[0:1] + v1]
            + [v0[i:i + 1] + v1[:SUBLANES] for i in range(1, SUBLANES)]
            + [v0[SUBLANES:] + v1[0:1]], axis=0)
        best, pair = _topk_rows(cand, K, flat)
        pi = jnp.floor(pair * (1.0 / K))
        pj = pair - pi * K
        experts = _take_rows(i0, pi) * float(PEER_N_KEYS) + _take_rows(i1, pj)
        p = jnp.exp(best - jnp.max(best, axis=0, keepdims=True))
        rows_e.append(experts)
        rows_g.append(p / jnp.sum(p, axis=0, keepdims=True))
    e_ref[...] = jnp.concatenate(rows_e, axis=0).T.astype(jnp.int32)
    g_ref[...] = jnp.concatenate(rows_g, axis=0).T


def _peer_route(h2, wq, sk, tm=256):
    n = h2.shape[0]
    hk = PEER_HEADS * PEER_TOPK
    flat = jnp.asarray(_pair_candidates())
    out = pl.BlockSpec((tm, hk), lambda i: (i, 0))
    return pl.pallas_call(
        _peer_route_kernel,
        grid=(n // tm,),
        in_specs=[
            pl.BlockSpec((tm, D_MODEL), lambda i: (i, 0)),
            pl.BlockSpec(wq.shape, lambda i: (0, 0)),
            pl.BlockSpec(sk.shape, lambda i: (0, 0, 0, 0)),
            pl.BlockSpec(flat.shape, lambda i: (0, 0)),
        ],
        out_specs=[out, out],
        out_shape=[jax.ShapeDtypeStruct((n, hk), jnp.int32), jax.ShapeDtypeStruct((n, hk), F32)],
        compiler_params=_cparams(("parallel",)),
        name="peer_route",
    )(h2, wq, sk, flat)


PEER_TT = 128
PEER_UNROLL = 4
PEER_MXU_EIGHTHS = 4
SUB_ORDER = (0, 4, 2, 6, 1, 5, 3, 7)


def _pack_table(w):
    return w.astype(BF16).reshape(w.shape[0], SUBLANES, LANES)


def _table_row(tbl_ref, e):
    return tbl_ref[e].astype(F32)


def _sublane_tree(ps):
    sub = lax.broadcasted_iota(jnp.int32, (SUBLANES, LANES), 0)
    m4 = sub < 4
    c = []
    for a, b in zip(ps[0::2], ps[1::2]):
        c.append(jnp.where(m4, a, b) + pltpu.roll(jnp.where(m4, b, a), 4, axis=0))
    m2 = (sub % 4) < 2
    d = []
    for x, y in zip(c[0::2], c[1::2]):
        d.append(jnp.where(m2, x + pltpu.roll(x, 6, axis=0), y + pltpu.roll(y, 2, axis=0)))
    m1 = (sub % 2) == 0
    x, y = d
    return jnp.where(m1, x + pltpu.roll(x, 7, axis=0), y + pltpu.roll(y, 1, axis=0))


def _peer_act_kernel(e_ref, h_ref, tbl_ref, gate_ref, ones_ref, w_ref, part_ref):
    hk = PEER_HEADS * PEER_TOPK
    tt = h_ref.shape[0]

    def token(t):
        hrow = h_ref[t]
        for m in range(hk // SUBLANES):
            ps = [_table_row(tbl_ref, e_ref[t, m * SUBLANES + j]) * hrow for j in SUB_ORDER]
            row0 = pl.multiple_of((t * (hk // SUBLANES) + m) * SUBLANES, SUBLANES)
            part_ref[pl.ds(row0, SUBLANES), :] = _sublane_tree(ps)

    def token_group(i, carry):
        for u in range(PEER_UNROLL):
            token(PEER_UNROLL * i + u)
        return carry

    lax.fori_loop(0, tt // PEER_UNROLL, token_group, 0)
    t_mxu = tt * PEER_MXU_EIGHTHS // 8
    hi, lo = _split2(part_ref[0:t_mxu * hk, :])
    sums_a = _dot(jnp.concatenate([hi, lo], axis=1), ones_ref[...])
    eye = (lax.broadcasted_iota(jnp.int32, (hk, LANES), 0)
           == lax.broadcasted_iota(jnp.int32, (hk, LANES), 1))
    act_a = jnp.sum(jnp.where(eye[None], sums_a.reshape(t_mxu, hk, LANES), 0.0), axis=1)
    act_b = jnp.sum(part_ref[t_mxu * hk:, :].reshape(tt - t_mxu, hk, LANES), axis=-1)
    act = jnp.concatenate([act_a, act_b], axis=0)
    w_ref[...] = _gelu(act) * gate_ref[...]


def _table_spec(tbl):
    return pl.BlockSpec(tbl.shape, lambda i: (0, 0, 0), pipeline_mode=pl.Buffered(1))


def _peer_act(experts, h2_tiles, tbl, gates):
    n, hk = experts.shape
    ones = jnp.ones((2 * LANES, LANES), BF16)
    return pl.pallas_call(
        _peer_act_kernel,
        grid=(n // PEER_TT,),
        in_specs=[
            pl.BlockSpec((PEER_TT, hk), lambda i: (i, 0), memory_space=pltpu.SMEM),
            pl.BlockSpec((PEER_TT, SUBLANES, LANES), lambda i: (i, 0, 0)),
            _table_spec(tbl),
            pl.BlockSpec((PEER_TT, hk), lambda i: (i, 0)),
            pl.BlockSpec(ones.shape, lambda i: (0, 0)),
        ],
        out_specs=pl.BlockSpec((PEER_TT, hk), lambda i: (i, 0)),
        out_shape=jax.ShapeDtypeStruct((n, hk), F32),
        scratch_shapes=[pltpu.VMEM((PEER_TT * hk, LANES), F32)],
        compiler_params=_cparams(("arbitrary",)),
        name="peer_act",
    )(experts, h2_tiles, tbl, gates, ones)


PEER_ACCS = 4
PEER_CHUNK = 32


def _peer_out_kernel(e_ref, w_ref, x_ref, tbl_ref, g_ref, ones_ref, o_ref, wrep_ref, *,
                     final_norm):
    hk = PEER_HEADS * PEER_TOPK
    tt = x_ref.shape[0]

    eye = (lax.broadcasted_iota(jnp.int32, (hk, LANES), 0)
           == lax.broadcasted_iota(jnp.int32, (hk, LANES), 1))
    diag = jnp.where(eye[None], w_ref[...][:, None, :], 0.0).reshape(tt * hk, LANES)
    wrep_ref[...] = _dot(diag.astype(BF16), ones_ref[...])

    def token(t, carry):
        def chunk(ci, acc):
            acc = list(acc)
            base = t * hk + ci * PEER_CHUNK
            for j in range(PEER_CHUNK):
                wk = wrep_ref[pl.ds(base + j, 1), :]
                acc[j % PEER_ACCS] = acc[j % PEER_ACCS] + wk * _table_row(
                    tbl_ref, e_ref[base + j])
            return tuple(acc)

        zeros = tuple(jnp.zeros((SUBLANES, LANES), F32) for _ in range(PEER_ACCS))
        acc = lax.fori_loop(0, hk // PEER_CHUNK, chunk, zeros)
        o_ref[t] = x_ref[t] + ((acc[0] + acc[1]) + (acc[2] + acc[3]))
        return carry

    lax.fori_loop(0, tt, token, 0)
    if final_norm:
        x2 = o_ref[...]
        ms = jnp.sum(jnp.sum(x2 * x2, axis=2, keepdims=True), axis=1, keepdims=True) / D_MODEL
        o_ref[...] = x2 * lax.rsqrt(ms + EPS) * g_ref[...]


def _peer_out(experts, w, x1_tiles, tbl, final_g, final_norm):
    n, hk = experts.shape
    ones = jnp.ones((LANES, LANES), BF16)
    smem = pl.BlockSpec((PEER_TT * hk,), lambda i: (i,), memory_space=pltpu.SMEM)
    tile = pl.BlockSpec((PEER_TT, SUBLANES, LANES), lambda i: (i, 0, 0))
    return pl.pallas_call(
        functools.partial(_peer_out_kernel, final_norm=final_norm),
        grid=(n // PEER_TT,),
        in_specs=[smem, pl.BlockSpec((PEER_TT, hk), lambda i: (i, 0)), tile, _table_spec(tbl),
                  pl.BlockSpec(final_g.shape, lambda i: (0, 0, 0)),
                  pl.BlockSpec(ones.shape, lambda i: (0, 0))],
        out_specs=tile,
        out_shape=jax.ShapeDtypeStruct(x1_tiles.shape, F32),
        scratch_shapes=[pltpu.VMEM((PEER_TT * hk, LANES), F32)],
        compiler_params=_cparams(("arbitrary",)),
        name="peer_out",
    )(experts.reshape(-1), w, x1_tiles, tbl, final_g, ones)


def _peer_ffn_residual(x1, h2, wq, sk, tbl_u, tbl_v, final_g, final_norm):
    n = x1.shape[0]
    experts, gates = _peer_route(h2, wq, sk)
    tiles = lambda z: z.reshape(n, SUBLANES, LANES)
    w = _peer_act(experts, tiles(h2), tbl_u, gates)
    g_tile = final_g.reshape(1, SUBLANES, LANES)
    return _peer_out(experts, w, tiles(x1), tbl_v, g_tile, final_norm).reshape(n, D_MODEL)


def kernel(x, norm_mix_g, w_in, cmp_k_pe, cmp_k_w1, cmp_k_w2, cmp_v_pe, cmp_v_w1, cmp_v_w2,
           rwkv_mu, rwkv_w0, rwkv_w2, rwkv_a0, rwkv_a2, rwkv_g2, rwkv_k_k, rwkv_k_a, rwkv_r_k,
           rwkv_ln_w, rwkv_ln_b, w_out, norm_ffn_g, peer_w_q, peer_sub_keys, peer_u, peer_v,
           norm_final_g):
    b, t_len, _ = x.shape
    n = b * t_len
    row = lambda z: z.reshape(1, -1)
    x2d = x.reshape(n, D_MODEL)
    depth = w_in.shape[0]
    for l in range(depth):
        oa, ob, oc = _in_proj(x2d, row(norm_mix_g[l]), _pad_w_in(w_in[l]), t_len)
        nb = t_len // CMP_STRIDE
        rk = ob[:, 0:D_KV].reshape(b, nb, CMP_STRIDE * D_KV)
        rv = ob[:, D_KV:2 * D_KV].reshape(b, nb, CMP_STRIDE * D_KV)
        cmp_k, cmp_vt = _compress(
            rk, rv, _expand_cmp_weights(cmp_k_pe[l], cmp_k_w1[l], cmp_k_w2[l], LANES),
            _expand_cmp_weights(cmp_v_pe[l], cmp_v_w1[l], cmp_v_w2[l], HEAD_DIM))
        v_col0 = Q_EXP + 2 * K_AUG
        v_t = lambda c: oa[:, v_col0 + c * D_KV:v_col0 + (c + 1) * D_KV].reshape(
            b, t_len, D_KV).transpose(0, 2, 1)
        y_nsa = _nsa_attention(oa, ob, v_t(0), v_t(1), cmp_k, cmp_vt, b, t_len)
        y_rwkv = _rwkv_mixer(oc, b, t_len, rwkv_mu[l], rwkv_w0[l], rwkv_w2[l], rwkv_a0[l],
                             rwkv_a2[l], rwkv_g2[l], rwkv_k_k[l], rwkv_k_a[l], rwkv_r_k[l],
                             rwkv_ln_w[l], rwkv_ln_b[l])
        wn, wr = _expand_w_out(w_out[l])
        x1, h2 = _out_proj(x2d, y_nsa, y_rwkv, wn, wr, row(norm_ffn_g[l]))
        x2d = _peer_ffn_residual(x1, h2, peer_w_q[l].astype(BF16), peer_sub_keys[l].astype(BF16),
                                 _pack_table(peer_u[l]), _pack_table(peer_v[l]),
                                 norm_final_g, final_norm=(l == depth - 1))
    return x2d.reshape(b, t_len, D_MODEL)
```
